```python
import math
import jax, jax.numpy as jnp
from jax import lax
import numpy as np

D_MODEL = 2048
BATCH = 4
SEQ = 4096
DEPTH = 2

HEAD_DIM = 64
ATTN_WIDTH = D_MODEL // 2
N_Q_HEADS = ATTN_WIDTH // HEAD_DIM
N_KV_HEADS = N_Q_HEADS // 4
GQA_GROUP = N_Q_HEADS // N_KV_HEADS
N_BRANCH = 3
CMP_LEN = 32
CMP_STRIDE = 16
CMP_OVERLAP = CMP_LEN // CMP_STRIDE
SLC_LEN = 64
SLC_TOPK = 16
WINDOW = 512
Q_BLOCK = 64
CONV_CHANNELS = D_MODEL - ATTN_WIDTH
CONV_TAPS = 31
D_MIX = ATTN_WIDTH + CONV_CHANNELS
D_FF = ((8 * D_MODEL // 3 + 255) // 256) * 256
FFN_CONV_TAPS = 3
N_BUCKETS = 32
MAX_DISTANCE = 128
NORM_EPS = 1e-6
NEG_INF = -1e30

Q_COLS = ATTN_WIDTH
KV_COLS = N_BRANCH * 2 * N_KV_HEADS * HEAD_DIM
GATE_COLS = N_BRANCH * N_Q_HEADS
CONV_IN_COLS = 2 * CONV_CHANNELS
IN_COLS = Q_COLS + KV_COLS + GATE_COLS + CONV_IN_COLS

kernel_name = "hymba_nsa_conformer_convffn"


def rms_norm(x, g):
    xf = x.astype(jnp.float32)
    y = xf * lax.rsqrt(jnp.mean(xf * xf, axis=-1, keepdims=True) + NORM_EPS)
    return (y * g.astype(jnp.float32)).astype(x.dtype)


def layer_norm(x, g, b):
    xf = x.astype(jnp.float32)
    mu = jnp.mean(xf, axis=-1, keepdims=True)
    xc = xf - mu
    var = jnp.mean(xc * xc, axis=-1, keepdims=True)
    y = xc * lax.rsqrt(var + NORM_EPS) * g.astype(jnp.float32) + b.astype(jnp.float32)
    return y.astype(x.dtype)


def t5_bucket(dist):
    n = jnp.maximum(dist, 0)
    max_exact = N_BUCKETS // 2
    nf = jnp.maximum(n, 1).astype(jnp.float32)
    large = max_exact + (jnp.log(nf / max_exact) / math.log(MAX_DISTANCE / max_exact)
                         * (N_BUCKETS - max_exact)).astype(jnp.int32)
    large = jnp.minimum(large, N_BUCKETS - 1)
    return jnp.where(n < max_exact, n, large)


def masked_softmax(s, mask):
    s = jnp.where(mask, s.astype(jnp.float32), NEG_INF)
    p = jax.nn.softmax(s, axis=-1)
    return jnp.where(mask, p, 0.0)


def causal_depthwise_conv(x, w, b):
    taps, c = w.shape
    y = lax.conv_general_dilated(x, w[:, None, :].astype(x.dtype), window_strides=(1,),
                                 padding=[(taps - 1, 0)],
                                 dimension_numbers=('NWC', 'WIO', 'NWC'),
                                 feature_group_count=c)
    return y + b.astype(x.dtype)


def compress_tokens(tok, pos, w1, w2):
    b, t, h, d = tok.shape
    n_chunk = t // CMP_STRIDE
    n_cmp = n_chunk - CMP_OVERLAP + 1
    chunks = tok.reshape(b, n_chunk, CMP_STRIDE, h, d)
    blocks = jnp.concatenate([chunks[:, o:o + n_cmp] for o in range(CMP_OVERLAP)], axis=2)
    blocks = blocks + pos[None, None, :, None, :].astype(tok.dtype)
    flat = blocks.transpose(0, 1, 3, 2, 4).reshape(b, n_cmp, h, CMP_LEN * d)
    return jax.nn.silu(flat @ w1) @ w2


def nsa_attention(q, kv, gates, rel_bias, cmp_pos, cmp_w1, cmp_w2):
    b, t_len, hkv, g, dh = q.shape
    scale = HEAD_DIM ** -0.5
    k_cmp = compress_tokens(kv[:, :, 0, 0], cmp_pos[0], cmp_w1[0], cmp_w2[0])
    v_cmp = compress_tokens(kv[:, :, 0, 1], cmp_pos[1], cmp_w1[1], cmp_w2[1])
    n_cmp = k_cmp.shape[1]
    n_slc = t_len // SLC_LEN
    k_sel = min(SLC_TOPK, n_slc)
    k_slc = kv[:, :, 1, 0].reshape(b, n_slc, SLC_LEN, hkv, dh).transpose(0, 3, 1, 2, 4)
    v_slc = kv[:, :, 1, 1].reshape(b, n_slc, SLC_LEN, hkv, dh).transpose(0, 3, 1, 2, 4)
    pad = ((0, 0), (WINDOW, 0), (0, 0), (0, 0))
    k_win = jnp.pad(kv[:, :, 2, 0], pad)
    v_win = jnp.pad(kv[:, :, 2, 1], pad)
    ci_ = jnp.arange(n_cmp)[:, None]
    sj_ = jnp.arange(n_slc)[None, :]
    sel_map = ((ci_ * CMP_STRIDE < (sj_ + 1) * SLC_LEN) &
               (ci_ * CMP_STRIDE + CMP_LEN > sj_ * SLC_LEN)).astype(jnp.float32)
    cmp_end = jnp.arange(n_cmp, dtype=jnp.int32) * CMP_STRIDE + CMP_LEN - 1
    blk_ids = jnp.arange(n_slc, dtype=jnp.int32)[None, :]
    tab_t = rel_bias.reshape(N_BUCKETS, hkv, g).transpose(1, 0, 2)
    h_ar = jnp.arange(hkv)[None, :, None, None, None]
    gather_blocks = jax.vmap(jax.vmap(lambda blk, ii: blk[ii]))

    n_chunks = t_len // Q_BLOCK
    q_ch = q.reshape(b, n_chunks, Q_BLOCK, hkv, g, dh).transpose(1, 0, 3, 2, 4, 5)
    g_ch = gates.reshape(b, n_chunks, Q_BLOCK, hkv, g, N_BRANCH).transpose(1, 0, 3, 2, 4, 5)

    def chunk_fn(args):
        ci, qc, gc = args
        t = ci * Q_BLOCK + jnp.arange(Q_BLOCK, dtype=jnp.int32)
        qs = qc * scale
        s_cmp = jnp.einsum('bhqgd,bnhd->bhqgn', qs, k_cmp).astype(jnp.float32)
        bias_c = rel_bias[t5_bucket(t[:, None] - cmp_end[None, :])]
        bias_c = bias_c.reshape(Q_BLOCK, n_cmp, hkv, g).transpose(2, 0, 3, 1)
        valid_c = (cmp_end[None, :] <= t[:, None])[None, None, :, None, :]
        p_cmp = masked_softmax(s_cmp + bias_c.astype(jnp.float32), valid_c)
        o_cmp = jnp.einsum('bhqgn,bnhd->bhqgd', p_cmp.astype(v_cmp.dtype), v_cmp)
        imp = jnp.einsum('bhqgn,nj->bhqj', p_cmp, sel_map)
        cur = (t // SLC_LEN)[:, None]
        forced = (blk_ids == 0) | (blk_ids == cur) | (blk_ids == cur - 1)
        causal_blk = blk_ids * SLC_LEN <= t[:, None]
        imp = jnp.where(forced, jnp.inf, imp)
        imp = jnp.where(causal_blk, imp, -jnp.inf)
        _, idx = lax.top_k(imp, k_sel)
        kg = gather_blocks(k_slc, idx)
        vg = gather_blocks(v_slc, idx)
        pos = idx[..., None] * SLC_LEN + jnp.arange(SLC_LEN, dtype=jnp.int32)
        dist_s = t[None, None, :, None, None] - pos
        bias_s = jnp.moveaxis(tab_t[h_ar, t5_bucket(dist_s)], -1, 3)
        s_sel = jnp.einsum('bhqgd,bhqkld->bhqgkl', qs, kg).astype(jnp.float32)
        s_sel = (s_sel + bias_s.astype(jnp.float32)).reshape(b, hkv, Q_BLOCK, g, k_sel * SLC_LEN)
        valid_s = (dist_s >= 0).reshape(b, hkv, Q_BLOCK, 1, k_sel * SLC_LEN)
        p_sel = masked_softmax(s_sel, valid_s).reshape(b, hkv, Q_BLOCK, g, k_sel, SLC_LEN)
        o_sel = jnp.einsum('bhqgkl,bhqkld->bhqgd', p_sel.astype(vg.dtype), vg)
        kw = lax.dynamic_slice_in_dim(k_win, ci * Q_BLOCK, Q_BLOCK + WINDOW, axis=1)
        vw = lax.dynamic_slice_in_dim(v_win, ci * Q_BLOCK, Q_BLOCK + WINDOW, axis=1)
        kpos = ci * Q_BLOCK - WINDOW + jnp.arange(Q_BLOCK + WINDOW, dtype=jnp.int32)
        dist_w = t[:, None] - kpos[None, :]
        valid_w = ((dist_w >= 0) & (dist_w < WINDOW) & (kpos[None, :] >= 0))[None, None, :, None, :]
        bias_w = rel_bias[t5_bucket(dist_w)].reshape(Q_BLOCK, Q_BLOCK + WINDOW, hkv, g).transpose(2, 0, 3, 1)
        s_win = jnp.einsum('bhqgd,bshd->bhqgs', qs, kw).astype(jnp.float32)
        p_win = masked_softmax(s_win + bias_w.astype(jnp.float32), valid_w)
        o_win = jnp.einsum('bhqgs,bshd->bhqgd', p_win.astype(vw.dtype), vw)
        return gc[..., 0:1] * o_cmp + gc[..., 1:2] * o_sel + gc[..., 2:3] * o_win

    out = lax.map(chunk_fn, (jnp.arange(n_chunks, dtype=jnp.int32), q_ch, g_ch))
    out = out.transpose(1, 0, 3, 2, 4, 5)
    return out.reshape(b, t_len, ATTN_WIDTH)


def conformer_conv(h2, conv_w, conv_b, ln_g, ln_b):
    a, gate = jnp.split(h2, 2, axis=-1)
    u = a * jax.nn.sigmoid(gate)
    u = causal_depthwise_conv(u, conv_w, conv_b)
    u = layer_norm(u, ln_g, ln_b)
    return jax.nn.silu(u)


def conv_glu_ffn(h, w_up, cw, cb, w_down):
    u = causal_depthwise_conv(h @ w_up, cw, cb)
    a, gate = jnp.split(u, 2, axis=-1)
    return (jax.nn.silu(gate) * a) @ w_down


def setup_inputs(seed: int = 0) -> dict:
    key = jax.random.key(seed)
    ks = jax.random.split(key, 18)
    f32 = jnp.float32
    nrm = lambda k, shape, s: jax.random.normal(k, shape, f32) * s
    return {
        "x": nrm(ks[0], (BATCH, SEQ, D_MODEL), 1.0),
        "rel_bias": nrm(ks[1], (N_BUCKETS, N_Q_HEADS), 0.5),
        "mix_norm_g": 1.0 + nrm(ks[2], (DEPTH, D_MODEL), 0.02),
        "w_in": nrm(ks[3], (DEPTH, D_MODEL, IN_COLS), D_MODEL ** -0.5),
        "cmp_pos": nrm(ks[4], (DEPTH, 2, CMP_LEN, HEAD_DIM), 0.1),
        "cmp_w1": nrm(ks[5], (DEPTH, 2, CMP_LEN * HEAD_DIM, HEAD_DIM), (CMP_LEN * HEAD_DIM) ** -0.5),
        "cmp_w2": nrm(ks[6], (DEPTH, 2, HEAD_DIM, HEAD_DIM), HEAD_DIM ** -0.5),
        "conv_w": nrm(ks[7], (DEPTH, CONV_TAPS, CONV_CHANNELS), CONV_TAPS ** -0.5),
        "conv_b": nrm(ks[8], (DEPTH, CONV_CHANNELS), 0.01),
        "conv_ln_g": 1.0 + nrm(ks[9], (DEPTH, CONV_CHANNELS), 0.02),
        "conv_ln_b": nrm(ks[10], (DEPTH, CONV_CHANNELS), 0.01),
        "w_out": nrm(ks[11], (DEPTH, D_MIX, D_MODEL), D_MIX ** -0.5),
        "ffn_norm_g": 1.0 + nrm(ks[12], (DEPTH, D_MODEL), 0.02),
        "w_up": nrm(ks[13], (DEPTH, D_MODEL, 2 * D_FF), D_MODEL ** -0.5),
        "ffn_conv_w": nrm(ks[14], (DEPTH, FFN_CONV_TAPS, 2 * D_FF), FFN_CONV_TAPS ** -0.5),
        "ffn_conv_b": nrm(ks[15], (DEPTH, 2 * D_FF), 0.01),
        "w_down": nrm(ks[16], (DEPTH, D_FF, D_MODEL), D_FF ** -0.5),
        "final_norm_g": 1.0 + nrm(ks[17], (D_MODEL,), 0.02),
    }


def reference(x, rel_bias, mix_norm_g, w_in, cmp_pos, cmp_w1, cmp_w2, conv_w, conv_b,
              conv_ln_g, conv_ln_b, w_out, ffn_norm_g, w_up, ffn_conv_w, ffn_conv_b,
              w_down, final_norm_g):
    b, t_len, _ = x.shape
    splits = [Q_COLS, Q_COLS + KV_COLS, Q_COLS + KV_COLS + GATE_COLS]
    for l in range(DEPTH):
        h = rms_norm(x, mix_norm_g[l])
        proj = h @ w_in[l]
        q, kv, gate_logits, conv_in = jnp.split(proj, splits, axis=-1)
        q = q.reshape(b, t_len, N_KV_HEADS, GQA_GROUP, HEAD_DIM)
        kv = kv.reshape(b, t_len, N_BRANCH, 2, N_KV_HEADS, HEAD_DIM)
        gates = jax.nn.sigmoid(gate_logits).reshape(b, t_len, N_KV_HEADS, GQA_GROUP, N_BRANCH)
        attn = nsa_attention(q, kv, gates, rel_bias, cmp_pos[l], cmp_w1[l], cmp_w2[l])
        conv = conformer_conv(conv_in, conv_w[l], conv_b[l], conv_ln_g[l], conv_ln_b[l])
        x = x + jnp.concatenate([attn, conv], axis=-1) @ w_out[l]
        h = rms_norm(x, ffn_norm_g[l])
        x = x + conv_glu_ffn(h, w_up[l], ffn_conv_w[l], ffn_conv_b[l], w_down[l])
    return rms_norm(x, final_norm_g)
```

```python
import functools
import math

import numpy as np
import jax
import jax.numpy as jnp
from jax import lax
from jax.experimental import pallas as pl
from jax.experimental.pallas import tpu as pltpu

F32 = jnp.float32
BF16 = jnp.bfloat16

D_MODEL = 2048
HEAD_DIM = 64
N_KV_HEADS = 4
GQA_GROUP = 4
N_Q_HEADS = N_KV_HEADS * GQA_GROUP
N_BRANCH = 3
ATTN_WIDTH = N_Q_HEADS * HEAD_DIM
CMP_LEN = 32
CMP_STRIDE = 16
SLC_LEN = 64
SLC_TOPK = 16
WINDOW = 512
CONV_CHANNELS = D_MODEL - ATTN_WIDTH
CONV_TAPS = 31
D_FF = 5632
FFN_CONV_TAPS = 3
N_BUCKETS = 32
MAX_DISTANCE = 128
NORM_EPS = 1e-6

KV_PAIR = 2 * HEAD_DIM
KV_COLS = N_BRANCH * N_KV_HEADS * KV_PAIR
GATE_COLS = N_BRANCH * N_Q_HEADS
COL_Q = 0
COL_CONV_A = ATTN_WIDTH
COL_CONV_G = COL_CONV_A + CONV_CHANNELS
COL_KV = COL_CONV_G + CONV_CHANNELS
PROJ_COLS = COL_KV + KV_COLS

MASK_VALUE = -1e30
BLOCK_PENALTY = -1e9
KEY_BIG = 1e30

LANES = 128
VMEM_LIMIT = 56 * 1024 * 1024
CONV_HALO = 32
FFN_HALO = 16


def _t5_bucket_last_distance():
    n = np.arange(0, 4 * MAX_DISTANCE, dtype=np.int64)
    max_exact = N_BUCKETS // 2
    nf = np.maximum(n, 1).astype(np.float64)
    large = max_exact + np.floor(np.log(nf / max_exact) / math.log(MAX_DISTANCE / max_exact)
                                 * (N_BUCKETS - max_exact)).astype(np.int64)
    large = np.minimum(large, N_BUCKETS - 1)
    bucket = np.where(n < max_exact, n, large)
    last = []
    for b in range(N_BUCKETS - 1):
        idx = np.nonzero(bucket == b)[0]
        last.append(int(idx.max()) if idx.size else None)
    return last


_BUCKET_LAST = _t5_bucket_last_distance()


def _params(*sem):
    return pltpu.CompilerParams(dimension_semantics=sem, vmem_limit_bytes=VMEM_LIMIT)


def _sigmoid(x):
    return jax.nn.sigmoid(x)


def _rms_norm_rows(x, g):
    ms = jnp.mean(x * x, axis=-1, keepdims=True)
    return x * lax.rsqrt(ms + NORM_EPS) * g


def _bias_from_distance(dist, tab_ref, head):
    c_far = tab_ref[N_BUCKETS - 1, head]
    val = jnp.zeros(dist.shape, F32)
    for b in range(N_BUCKETS - 2, -1, -1):
        if _BUCKET_LAST[b] is None:
            continue
        val = jnp.where(dist <= _BUCKET_LAST[b], tab_ref[b, head] - c_far, val)
    return val


def _band_bias_body(tab_ref, o_ref, *, tq):
    head = pl.program_id(0)
    shape = (tq, WINDOW + tq)
    dist = (lax.broadcasted_iota(jnp.int32, shape, 0) + WINDOW
            - lax.broadcasted_iota(jnp.int32, shape, 1))
    val = _bias_from_distance(dist, tab_ref, head)
    val = jnp.where(dist >= 0, jnp.where(dist < WINDOW, val, MASK_VALUE), MASK_VALUE)
    o_ref[0] = val


def _cmp_bias_body(tab_ref, o_ref, *, tq, nc):
    head = pl.program_id(0)
    t0 = pl.program_id(1) * tq
    shape = (tq, nc)
    t = t0 + lax.broadcasted_iota(jnp.int32, shape, 0)
    cmp_end = lax.broadcasted_iota(jnp.int32, shape, 1) * CMP_STRIDE + (CMP_LEN - 1)
    dist = t - cmp_end
    val = _bias_from_distance(dist, tab_ref, head)
    o_ref[0] = jnp.where(dist >= 0, val, MASK_VALUE)


def _bias_tiles(rel_bias, t_len, tq):
    nc = t_len // CMP_STRIDE
    smem = pl.BlockSpec(memory_space=pltpu.SMEM)
    band = pl.pallas_call(
        functools.partial(_band_bias_body, tq=tq),
        grid=(N_Q_HEADS,),
        in_specs=[smem],
        out_specs=pl.BlockSpec((1, tq, WINDOW + tq), lambda h: (h, 0, 0)),
        out_shape=jax.ShapeDtypeStruct((N_Q_HEADS, tq, WINDOW + tq), F32),
        compiler_params=_params("arbitrary"),
        name="band_bias",
    )(rel_bias)
    cmpb = pl.pallas_call(
        functools.partial(_cmp_bias_body, tq=tq, nc=nc),
        grid=(N_Q_HEADS, t_len // tq),
        in_specs=[smem],
        out_specs=pl.BlockSpec((1, tq, nc), lambda h, i: (h, i, 0)),
        out_shape=jax.ShapeDtypeStruct((N_Q_HEADS, t_len, nc), F32),
        compiler_params=_params("arbitrary", "arbitrary"),
        name="cmp_bias",
    )(rel_bias)
    return band, cmpb


def _inproj_body(x_ref, g_ref, w_ref, wg_ref, o_ref, gate_ref, h_ref):
    @pl.when(pl.program_id(1) == 0)
    def _():
        h = _rms_norm_rows(x_ref[...], g_ref[...]).astype(BF16)
        h_ref[...] = h
        gate_ref[...] = _sigmoid(jnp.dot(h, wg_ref[...], preferred_element_type=F32))

    o_ref[...] = jnp.dot(h_ref[...], w_ref[...], preferred_element_type=F32).astype(o_ref.dtype)


def _in_projection(x2, g, w, wg, *, tm, tn):
    n = x2.shape[0]
    gcols = wg.shape[1]
    return pl.pallas_call(
        _inproj_body,
        grid=(n // tm, PROJ_COLS // tn),
        in_specs=[
            pl.BlockSpec((tm, D_MODEL), lambda i, j: (i, 0)),
            pl.BlockSpec((1, D_MODEL), lambda i, j: (0, 0)),
            pl.BlockSpec((D_MODEL, tn), lambda i, j: (0, j)),
            pl.BlockSpec((D_MODEL, gcols), lambda i, j: (0, 0)),
        ],
        out_specs=[
            pl.BlockSpec((tm, tn), lambda i, j: (i, j)),
            pl.BlockSpec((tm, gcols), lambda i, j: (i, 0)),
        ],
        out_shape=[
            jax.ShapeDtypeStruct((n, PROJ_COLS), BF16),
            jax.ShapeDtypeStruct((n, gcols), F32),
        ],
        scratch_shapes=[pltpu.VMEM((tm, D_MODEL), BF16)],
        compiler_params=_params("arbitrary", "arbitrary"),
        name="in_projection",
    )(x2, g, w, wg)


def _compress_body(tok_ref, pos_ref, w1_ref, w2_ref, o_ref, *, nc):
    outs = []
    for kv in range(2):
        c = tok_ref[0, 0, kv].astype(F32)
        top = jnp.dot((c + pos_ref[kv, 0:1, :]).astype(BF16), w1_ref[kv, 0], preferred_element_type=F32)
        bot = jnp.dot((c + pos_ref[kv, 1:2, :]).astype(BF16), w1_ref[kv, 1], preferred_element_type=F32)
        pre = top + pltpu.roll(bot, nc - 1, axis=0)
        act = pre * _sigmoid(pre)
        outs.append(jnp.dot(act.astype(BF16), w2_ref[kv], preferred_element_type=F32))
    o_ref[0, 0] = jnp.concatenate(outs, axis=1).astype(o_ref.dtype)


def _compress(tokc, pos, w1, w2):
    b, _, _, nc, width = tokc.shape
    return pl.pallas_call(
        functools.partial(_compress_body, nc=nc),
        grid=(b, N_KV_HEADS),
        in_specs=[
            pl.BlockSpec((1, 1, 2, nc, width), lambda i, h: (i, h, 0, 0, 0)),
            pl.BlockSpec((2, 2, width), lambda i, h: (0, 0, 0)),
            pl.BlockSpec((2, 2, width, HEAD_DIM), lambda i, h: (0, 0, 0, 0)),
            pl.BlockSpec((2, HEAD_DIM, HEAD_DIM), lambda i, h: (0, 0, 0)),
        ],
        out_specs=pl.BlockSpec((1, 1, nc, KV_PAIR), lambda i, h: (i, h, 0, 0)),
        out_shape=jax.ShapeDtypeStruct((b, N_KV_HEADS, nc, KV_PAIR), BF16),
        compiler_params=_params("arbitrary", "arbitrary"),
        name="compress",
    )(tokc, pos, w1, w2)


_NT = (((1,), (1,)), ((), ()))


def _attn_body(q_ref, kvc_ref, kvs_ref, kvw_ref, bt_ref, bc_ref, gate_ref, o_ref,
               kp_ref, m_ref, l_ref, acc_ref, *, tq, t_len):
    qi = pl.program_id(2)
    nc = t_len // CMP_STRIDE
    ns = t_len // SLC_LEN
    rows = GQA_GROUP * tq
    n_win_tiles = WINDOW // tq

    @pl.when(qi == 0)
    def _():
        k = kvs_ref[0][:, :HEAD_DIM]
        blk = lax.broadcasted_iota(jnp.int32, (t_len, HEAD_DIM), 0) // SLC_LEN
        col = lax.broadcasted_iota(jnp.int32, (t_len, HEAD_DIM), 1)
        onehot = jnp.where(blk == col, 1.0, 0.0).astype(BF16)
        kp_ref[...] = jnp.concatenate([k, onehot], axis=1)

    q = q_ref[0]
    q4 = jnp.concatenate([q[:, g * HEAD_DIM:(g + 1) * HEAD_DIM] for g in range(GQA_GROUP)], axis=0)
    q4 = (q4.astype(F32) * (HEAD_DIM ** -0.5)).astype(BF16)

    kvc = kvc_ref[0, 0]
    s = lax.dot_general(q4, kvc[:, :HEAD_DIM], _NT, preferred_element_type=F32)
    s = s + bc_ref[...].reshape(rows, nc)
    m = jnp.max(s, axis=1, keepdims=True)
    p = jnp.exp(s - m)
    l = jnp.sum(p, axis=1, keepdims=True)
    pn = p * jnp.where(m > 0.5 * MASK_VALUE, 1.0 / l, 0.0)
    o_cmp = jnp.dot(pn.astype(BF16), kvc, preferred_element_type=F32)[:, HEAD_DIM:]

    ps = pn[0:tq] + pn[tq:2 * tq] + pn[2 * tq:3 * tq] + pn[3 * tq:4 * tq]
    ci = lax.broadcasted_iota(jnp.int32, (nc, HEAD_DIM), 0) * CMP_STRIDE
    sj = lax.broadcasted_iota(jnp.int32, (nc, HEAD_DIM), 1) * SLC_LEN
    overlap = jnp.where(ci < sj + SLC_LEN, jnp.where(ci + CMP_LEN > sj, 1.0, 0.0), 0.0).astype(BF16)
    p_hi = ps.astype(BF16)
    r_hi = ps - p_hi.astype(F32)
    p_md = r_hi.astype(BF16)
    p_lo = (r_hi - p_md.astype(F32)).astype(BF16)
    imp = (jnp.dot(p_hi, overlap, preferred_element_type=F32)
           + jnp.dot(p_md, overlap, preferred_element_type=F32)
           + jnp.dot(p_lo, overlap, preferred_element_type=F32))

    t = qi * tq + lax.broadcasted_iota(jnp.int32, (tq, HEAD_DIM), 0)
    blk = lax.broadcasted_iota(jnp.int32, (tq, HEAD_DIM), 1)
    cur = t // SLC_LEN
    key = jnp.where(blk == 0, KEY_BIG, jnp.where(blk == cur, KEY_BIG, jnp.where(blk == cur - 1, KEY_BIG, imp)))
    key = jnp.where(blk * SLC_LEN <= t, key, -KEY_BIG)
    rank = jnp.zeros((tq, HEAD_DIM), F32)
    for j in range(min(ns, HEAD_DIM)):
        col = key[:, j:j + 1]
        gt = jnp.where(col > key, 1.0, 0.0)
        ge = jnp.where(col >= key, 1.0, 0.0)
        rank = rank + jnp.where(blk > j, ge, gt)
    pen = jnp.where(rank < float(min(SLC_TOPK, ns)), 0.0, BLOCK_PENALTY).astype(BF16)
    q_sel = jnp.concatenate([q4, jnp.concatenate([pen] * GQA_GROUP, axis=0)], axis=1)
    q_win = jnp.concatenate([q4, jnp.zeros((rows, HEAD_DIM), BF16)], axis=1)

    def tile(qx, k_ref, v_ref, kj, bias_off, first):
        start = pl.multiple_of(kj * tq, tq)
        sc = lax.dot_general(qx, k_ref[pl.ds(start, tq), :], _NT, preferred_element_type=F32)
        if bias_off is not None:
            sc = sc + bt_ref[:, :, bias_off:bias_off + tq].reshape(rows, tq)
        v = v_ref[pl.ds(start, tq), :]
        if first:
            m_new = jnp.max(sc, axis=1, keepdims=True)
            pr = jnp.exp(sc - m_new)
            l_ref[...] = jnp.sum(pr, axis=1, keepdims=True)
            acc_ref[...] = jnp.dot(pr.astype(BF16), v, preferred_element_type=F32)
        else:
            m_old = m_ref[...]
            m_new = jnp.maximum(m_old, jnp.max(sc, axis=1, keepdims=True))
            alpha = jnp.exp(m_old - m_new)
            pr = jnp.exp(sc - m_new)
            l_ref[...] = alpha * l_ref[...] + jnp.sum(pr, axis=1, keepdims=True)
            acc_ref[...] = alpha * acc_ref[...] + jnp.dot(pr.astype(BF16), v, preferred_element_type=F32)
        m_ref[...] = m_new

    def finish():
        return acc_ref[:, HEAD_DIM:] / l_ref[...]

    kw_ref = kvw_ref.at[0]
    tile(q_win, kw_ref, kw_ref, qi, WINDOW, True)
    for d in range(1, n_win_tiles + 1):
        @pl.when(qi >= d)
        def _(d=d):
            tile(q_win, kw_ref, kw_ref, qi - d, WINDOW - d * tq, False)
    o_win = finish()

    ks_ref = kvs_ref.at[0]
    tile(q_sel, kp_ref, ks_ref, qi, WINDOW, True)

    @pl.when(qi >= 1)
    def _():
        tile(q_sel, kp_ref, ks_ref, qi - 1, WINDOW - tq, False)

    def far_tile(kj, carry):
        tile(q_sel, kp_ref, ks_ref, kj, None, False)
        return carry

    lax.fori_loop(0, jnp.maximum(qi - 1, 0), far_tile, 0)
    o_sel = finish()

    gates = gate_ref[0]
    outs = []
    for g in range(GQA_GROUP):
        sl = slice(g * tq, (g + 1) * tq)
        outs.append(gates[:, g:g + 1] * o_cmp[sl]
                    + gates[:, GQA_GROUP + g:GQA_GROUP + g + 1] * o_sel[sl]
                    + gates[:, 2 * GQA_GROUP + g:2 * GQA_GROUP + g + 1] * o_win[sl])
    o_ref[0] = jnp.concatenate(outs, axis=1).astype(o_ref.dtype)


def _attention(proj3, kvc, band, cmpb, gates3, *, tq):
    b, t_len, _ = proj3.shape
    nc = t_len // CMP_STRIDE
    rows = GQA_GROUP * tq
    qw = GQA_GROUP * HEAD_DIM
    slc_blk = COL_KV // KV_PAIR + N_KV_HEADS
    win_blk = COL_KV // KV_PAIR + 2 * N_KV_HEADS
    return pl.pallas_call(
        functools.partial(_attn_body, tq=tq, t_len=t_len),
        grid=(b, N_KV_HEADS, t_len // tq),
        in_specs=[
            pl.BlockSpec((1, tq, qw), lambda i, h, q: (i, q, h)),
            pl.BlockSpec((1, 1, nc, KV_PAIR), lambda i, h, q: (i, h, 0, 0)),
            pl.BlockSpec((1, t_len, KV_PAIR), lambda i, h, q: (i, 0, slc_blk + h)),
            pl.BlockSpec((1, t_len, KV_PAIR), lambda i, h, q: (i, 0, win_blk + h)),
            pl.BlockSpec((GQA_GROUP, tq, WINDOW + tq), lambda i, h, q: (h, 0, 0)),
            pl.BlockSpec((GQA_GROUP, tq, nc), lambda i, h, q: (h, q, 0)),
            pl.BlockSpec((1, tq, LANES), lambda i, h, q: (i, q, h)),
        ],
        out_specs=pl.BlockSpec((1, tq, qw), lambda i, h, q: (i, q, h)),
        out_shape=jax.ShapeDtypeStruct((b, t_len, ATTN_WIDTH), BF16),
        scratch_shapes=[
            pltpu.VMEM((t_len, KV_PAIR), BF16),
            pltpu.VMEM((rows, 1), F32),
            pltpu.VMEM((rows, 1), F32),
            pltpu.VMEM((rows, KV_PAIR), F32),
        ],
        compiler_params=_params("arbitrary", "arbitrary", "arbitrary"),
        name="nsa_attention",
    )(proj3, kvc, proj3, proj3, band, cmpb, gates3)


def _conformer_body(a_ref, g_ref, ah_ref, gh_ref, w_ref, b_ref, lg_ref, lb_ref, o_ref, u_ref, *, tt):
    ti = pl.program_id(1)
    u_ref[CONV_HALO:, :] = a_ref[0].astype(F32) * _sigmoid(g_ref[0].astype(F32))
    halo = ah_ref[0].astype(F32) * _sigmoid(gh_ref[0].astype(F32))
    u_ref[0:CONV_HALO, :] = jnp.where(ti > 0, halo, 0.0)
    acc = jnp.zeros((tt, CONV_CHANNELS), F32)
    base = CONV_HALO - (CONV_TAPS - 1)
    for k in range(CONV_TAPS):
        acc = acc + w_ref[k:k + 1, :] * u_ref[pl.ds(base + k, tt), :]
    acc = acc + b_ref[...]
    mu = jnp.mean(acc, axis=-1, keepdims=True)
    xc = acc - mu
    var = jnp.mean(xc * xc, axis=-1, keepdims=True)
    y = xc * lax.rsqrt(var + NORM_EPS) * lg_ref[...] + lb_ref[...]
    o_ref[0] = (y * _sigmoid(y)).astype(o_ref.dtype)


def _conformer(proj3, w, b, lg, lb, *, tt):
    bsz, t_len, _ = proj3.shape
    a_blk = COL_CONV_A // CONV_CHANNELS
    g_blk = COL_CONV_G // CONV_CHANNELS
    hpt = tt // CONV_HALO
    halo_idx = lambda i, t: jnp.maximum(t * hpt - 1, 0)
    vec = pl.BlockSpec((1, CONV_CHANNELS), lambda i, t: (0, 0))
    return pl.pallas_call(
        functools.partial(_conformer_body, tt=tt),
        grid=(bsz, t_len // tt),
        in_specs=[
            pl.BlockSpec((1, tt, CONV_CHANNELS), lambda i, t: (i, t, a_blk)),
            pl.BlockSpec((1, tt, CONV_CHANNELS), lambda i, t: (i, t, g_blk)),
            pl.BlockSpec((1, CONV_HALO, CONV_CHANNELS), lambda i, t: (i, halo_idx(i, t), a_blk)),
            pl.BlockSpec((1, CONV_HALO, CONV_CHANNELS), lambda i, t: (i, halo_idx(i, t), g_blk)),
            pl.BlockSpec((CONV_TAPS, CONV_CHANNELS), lambda i, t: (0, 0)),
            vec, vec, vec,
        ],
        out_specs=pl.BlockSpec((1, tt, CONV_CHANNELS), lambda i, t: (i, t, 0)),
        out_shape=jax.ShapeDtypeStruct((bsz, t_len, CONV_CHANNELS), BF16),
        scratch_shapes=[pltpu.VMEM((CONV_HALO + tt, CONV_CHANNELS), F32)],
        compiler_params=_params("arbitrary", "arbitrary"),
        name="conformer_conv",
    )(proj3, proj3, proj3, proj3, w, b, lg, lb)


def _outproj_body(x_ref, a_ref, c_ref, wa_ref, wc_ref, o_ref):
    o_ref[...] = (x_ref[...]
                  + jnp.dot(a_ref[...], wa_ref[...], preferred_element_type=F32)
                  + jnp.dot(c_ref[...], wc_ref[...], preferred_element_type=F32))


def _out_projection(x2, attn2, conv2, wa, wc, *, tm, tn):
    n = x2.shape[0]
    return pl.pallas_call(
        _outproj_body,
        grid=(n // tm, D_MODEL // tn),
        in_specs=[
            pl.BlockSpec((tm, tn), lambda i, j: (i, j)),
            pl.BlockSpec((tm, ATTN_WIDTH), lambda i, j: (i, 0)),
            pl.BlockSpec((tm, CONV_CHANNELS), lambda i, j: (i, 0)),
            pl.BlockSpec((ATTN_WIDTH, tn), lambda i, j: (0, j)),
            pl.BlockSpec((CONV_CHANNELS, tn), lambda i, j: (0, j)),
        ],
        out_specs=pl.BlockSpec((tm, tn), lambda i, j: (i, j)),
        out_shape=jax.ShapeDtypeStruct((n, D_MODEL), F32),
        compiler_params=_params("arbitrary", "arbitrary"),
        name="out_projection",
    )(x2, attn2, conv2, wa, wc)


def _ffn_up_body(x_ref, xh_ref, g_ref, wa_ref, wg_ref, cwa_ref, cwg_ref, cba_ref, cbg_ref, o_ref,
                 h_ref, ua_ref, ug_ref, *, tm, tiles_per_seq):
    i = pl.program_id(0)

    @pl.when(pl.program_id(1) == 0)
    def _():
        h_ref[FFN_HALO:, :] = _rms_norm_rows(x_ref[...], g_ref[...]).astype(BF16)
        hh = _rms_norm_rows(xh_ref[...], g_ref[...])
        h_ref[0:FFN_HALO, :] = jnp.where(i % tiles_per_seq != 0, hh, 0.0).astype(BF16)

    h = h_ref[...]
    ua_ref[...] = jnp.dot(h, wa_ref[...], preferred_element_type=F32)
    ug_ref[...] = jnp.dot(h, wg_ref[...], preferred_element_type=F32)

    def conv(u_ref, cw_ref, cb_ref):
        base = FFN_HALO - (FFN_CONV_TAPS - 1)
        y = cw_ref[0:1, :] * u_ref[pl.ds(base, tm), :]
        for k in range(1, FFN_CONV_TAPS):
            y = y + cw_ref[k:k + 1, :] * u_ref[pl.ds(base + k, tm), :]
        return y + cb_ref[...]

    a = conv(ua_ref, cwa_ref, cba_ref)
    gate = conv(ug_ref, cwg_ref, cbg_ref)
    o_ref[...] = (gate * _sigmoid(gate) * a).astype(o_ref.dtype)


def _ffn_up(x2, g, w_up, cw, cb, *, tm, tn, t_len):
    n = x2.shape[0]
    nj = D_FF // tn
    hpt = tm // FFN_HALO
    return pl.pallas_call(
        functools.partial(_ffn_up_body, tm=tm, tiles_per_seq=t_len // tm),
        grid=(n // tm, nj),
        in_specs=[
            pl.BlockSpec((tm, D_MODEL), lambda i, j: (i, 0)),
            pl.BlockSpec((FFN_HALO, D_MODEL), lambda i, j: (jnp.maximum(i * hpt - 1, 0), 0)),
            pl.BlockSpec((1, D_MODEL), lambda i, j: (0, 0)),
            pl.BlockSpec((D_MODEL, tn), lambda i, j: (0, j)),
            pl.BlockSpec((D_MODEL, tn), lambda i, j: (0, j + nj)),
            pl.BlockSpec((FFN_CONV_TAPS, tn), lambda i, j: (0, j)),
            pl.BlockSpec((FFN_CONV_TAPS, tn), lambda i, j: (0, j + nj)),
            pl.BlockSpec((1, tn), lambda i, j: (0, j)),
            pl.BlockSpec((1, tn), lambda i, j: (0, j + nj)),
        ],
        out_specs=pl.BlockSpec((tm, tn), lambda i, j: (i, j)),
        out_shape=jax.ShapeDtypeStruct((n, D_FF), BF16),
        scratch_shapes=[
            pltpu.VMEM((FFN_HALO + tm, D_MODEL), BF16),
            pltpu.VMEM((FFN_HALO + tm, tn), F32),
            pltpu.VMEM((FFN_HALO + tm, tn), F32),
        ],
        compiler_params=_params("arbitrary", "arbitrary"),
        name="ffn_up",
    )(x2, x2, g, w_up, w_up, cw, cw, cb, cb)


def _ffn_down_body(act_ref, w_ref, x_ref, gn_ref, o_ref, acc_ref, *, final_norm):
    k = pl.program_id(1)

    @pl.when(k == 0)
    def _():
        acc_ref[...] = jnp.zeros_like(acc_ref)

    acc_ref[...] += jnp.dot(act_ref[...], w_ref[...], preferred_element_type=F32)

    @pl.when(k == pl.num_programs(1) - 1)
    def _():
        y = x_ref[...] + acc_ref[...]
        if final_norm:
            y = _rms_norm_rows(y, gn_ref[...])
        o_ref[...] = y


def _ffn_down(act, w_down, x2, gn, *, tm, tk, final_norm):
    n = x2.shape[0]
    return pl.pallas_call(
        functools.partial(_ffn_down_body, final_norm=final_norm),
        grid=(n // tm, D_FF // tk),
        in_specs=[
            pl.BlockSpec((tm, tk), lambda i, k: (i, k)),
            pl.BlockSpec((tk, D_MODEL), lambda i, k: (k, 0)),
            pl.BlockSpec((tm, D_MODEL), lambda i, k: (i, 0)),
            pl.BlockSpec((1, D_MODEL), lambda i, k: (0, 0)),
        ],
        out_specs=pl.BlockSpec((tm, D_MODEL), lambda i, k: (i, 0)),
        out_shape=jax.ShapeDtypeStruct((n, D_MODEL), F32),
        scratch_shapes=[pltpu.VMEM((tm, D_MODEL), F32)],
        compiler_params=_params("arbitrary", "arbitrary"),
        name="ffn_down",
    )(act, w_down, x2, gn)


def _in_weight_layout(w_in_l):
    q_cols = np.arange(ATTN_WIDTH)
    kv0 = ATTN_WIDTH
    gate0 = kv0 + KV_COLS
    conv0 = gate0 + GATE_COLS
    conv_cols = conv0 + np.arange(2 * CONV_CHANNELS)
    kv_cols = []
    for br in range(N_BRANCH):
        for h in range(N_KV_HEADS):
            for kv in range(2):
                kv_cols.append(kv0 + ((br * 2 + kv) * N_KV_HEADS + h) * HEAD_DIM + np.arange(HEAD_DIM))
    perm = np.concatenate([q_cols, conv_cols] + kv_cols)
    w_main = jnp.take(w_in_l, jnp.asarray(perm, jnp.int32), axis=1).astype(BF16)
    gsrc = np.zeros((N_KV_HEADS, N_BRANCH * GQA_GROUP), np.int32)
    for h in range(N_KV_HEADS):
        for br in range(N_BRANCH):
            for g in range(GQA_GROUP):
                gsrc[h, br * GQA_GROUP + g] = gate0 + (h * GQA_GROUP + g) * N_BRANCH + br
    wg = jnp.take(w_in_l, jnp.asarray(gsrc.reshape(-1)), axis=1).reshape(D_MODEL, N_KV_HEADS, -1)
    wg = jnp.pad(wg, ((0, 0), (0, 0), (0, LANES - N_BRANCH * GQA_GROUP)))
    return w_main, wg.reshape(D_MODEL, N_KV_HEADS * LANES).astype(BF16)


def _pick(n, prefs):
    for p in prefs:
        if n % p == 0:
            return p
    return n


def kernel(x, rel_bias, mix_norm_g, w_in, cmp_pos, cmp_w1, cmp_w2, conv_w, conv_b, conv_ln_g, conv_ln_b,
           w_out, ffn_norm_g, w_up, ffn_conv_w, ffn_conv_b, w_down, final_norm_g):
    bsz, t_len, _ = x.shape
    depth = w_in.shape[0]
    n = bsz * t_len
    assert t_len % 256 == 0 and t_len // SLC_LEN <= HEAD_DIM
    tq = 256
    tm = _pick(t_len, (512, 256))
    nc = t_len // CMP_STRIDE

    band, cmpb = _bias_tiles(rel_bias, t_len, tq)
    x2 = x.reshape(n, D_MODEL)
    half = CMP_STRIDE * HEAD_DIM
    for l in range(depth):
        w_main, wg = _in_weight_layout(w_in[l])
        proj, gates = _in_projection(x2, mix_norm_g[l][None, :], w_main, wg, tm=tm, tn=768)
        proj3 = proj.reshape(bsz, t_len, PROJ_COLS)
        gates3 = gates.reshape(bsz, t_len, N_KV_HEADS * LANES)

        tokc = proj3[:, :, COL_KV:COL_KV + N_KV_HEADS * KV_PAIR]
        tokc = tokc.reshape(bsz, nc, CMP_STRIDE, N_KV_HEADS, 2, HEAD_DIM).transpose(0, 3, 4, 1, 2, 5)
        tokc = tokc.reshape(bsz, N_KV_HEADS, 2, nc, half)
        pos = cmp_pos[l].reshape(2, 2, half)
        w1 = cmp_w1[l].reshape(2, 2, half, HEAD_DIM).astype(BF16)
        kvc = _compress(tokc, pos, w1, cmp_w2[l].astype(BF16))

        attn = _attention(proj3, kvc, band, cmpb, gates3, tq=tq)
        conv = _conformer(proj3, conv_w[l], conv_b[l][None, :], conv_ln_g[l][None, :],
                          conv_ln_b[l][None, :], tt=256)
        w_o = w_out[l].astype(BF16)
        x2 = _out_projection(x2, attn.reshape(n, ATTN_WIDTH), conv.reshape(n, CONV_CHANNELS),
                             w_o[:ATTN_WIDTH], w_o[ATTN_WIDTH:], tm=tm, tn=1024)

        act = _ffn_up(x2, ffn_norm_g[l][None, :], w_up[l].astype(BF16), ffn_conv_w[l],
                      ffn_conv_b[l][None, :], tm=tm, tn=512, t_len=t_len)
        x2 = _ffn_down(act, w_down[l].astype(BF16), x2, final_norm_g[None, :], tm=tm, tk=512,
                       final_norm=(l == depth - 1))
    return x2.reshape(bsz, t_len, D_MODEL)
```

```python
import functools
import math

import numpy as np
import jax
import jax.numpy as jnp
from jax import lax
from jax.experimental import pallas as pl
from jax.experimental.pallas import tpu as pltpu

F32 = jnp.float32
BF16 = jnp.bfloat16

D_MODEL = 2048
HEAD_DIM = 64
N_KV_HEADS = 4
GQA_GROUP = 4
N_Q_HEADS = N_KV_HEADS * GQA_GROUP
N_BRANCH = 3
ATTN_WIDTH = N_Q_HEADS * HEAD_DIM
CMP_LEN = 32
CMP_STRIDE = 16
SLC_LEN = 64
SLC_TOPK = 16
WINDOW = 512
CONV_CHANNELS = D_MODEL - ATTN_WIDTH
CONV_TAPS = 31
D_FF = 5632
FFN_CONV_TAPS = 3
N_BUCKETS = 32
MAX_DISTANCE = 128
NORM_EPS = 1e-6

KV_PAIR = 2 * HEAD_DIM
KV_COLS = N_BRANCH * N_KV_HEADS * KV_PAIR
GATE_COLS = N_BRANCH * N_Q_HEADS
COL_Q = 0
COL_CONV_A = ATTN_WIDTH
COL_CONV_G = COL_CONV_A + CONV_CHANNELS
COL_KV = COL_CONV_G + CONV_CHANNELS
PROJ_COLS = COL_KV + KV_COLS

MASK_VALUE = -1e30
BLOCK_PENALTY = -1e9
KEY_BIG = 1e30

LANES = 128
VMEM_LIMIT = 56 * 1024 * 1024
CONV_HALO = 32
FFN_HALO = 16


def _t5_bucket_last_distance():
    n = np.arange(0, 4 * MAX_DISTANCE, dtype=np.int64)
    max_exact = N_BUCKETS // 2
    nf = np.maximum(n, 1).astype(np.float64)
    large = max_exact + np.floor(np.log(nf / max_exact) / math.log(MAX_DISTANCE / max_exact)
                                 * (N_BUCKETS - max_exact)).astype(np.int64)
    large = np.minimum(large, N_BUCKETS - 1)
    bucket = np.where(n < max_exact, n, large)
    last = []
    for b in range(N_BUCKETS - 1):
        idx = np.nonzero(bucket == b)[0]
        last.append(int(idx.max()) if idx.size else None)
    return last


_BUCKET_LAST = _t5_bucket_last_distance()


def _params(*sem):
    return pltpu.CompilerParams(dimension_semantics=sem, vmem_limit_bytes=VMEM_LIMIT)


def _sigmoid(x):
    return jax.nn.sigmoid(x)


def _rms_norm_rows(x, g):
    ms = jnp.mean(x * x, axis=-1, keepdims=True)
    return x * lax.rsqrt(ms + NORM_EPS) * g


def _bias_from_distance(dist, tab_ref, head):
    c_far = tab_ref[N_BUCKETS - 1, head]
    val = jnp.zeros(dist.shape, F32)
    for b in range(N_BUCKETS - 2, -1, -1):
        if _BUCKET_LAST[b] is None:
            continue
        val = jnp.where(dist <= _BUCKET_LAST[b], tab_ref[b, head] - c_far, val)
    return val


def _band_bias_body(tab_ref, o_ref, *, tq):
    head = pl.program_id(0) * GQA_GROUP + pl.program_id(1)
    shape = (WINDOW + tq, tq)
    dist = (lax.broadcasted_iota(jnp.int32, shape, 1) + WINDOW
            - lax.broadcasted_iota(jnp.int32, shape, 0))
    val = _bias_from_distance(dist, tab_ref, head)
    val = jnp.where(dist >= 0, jnp.where(dist < WINDOW, val, MASK_VALUE), MASK_VALUE)
    o_ref[0] = val


def _cmp_bias_body(tab_ref, o_ref, *, tq, nc):
    head = pl.program_id(0) * GQA_GROUP + pl.program_id(2)
    t0 = pl.program_id(1) * tq
    shape = (nc, tq)
    t = t0 + lax.broadcasted_iota(jnp.int32, shape, 1)
    cmp_end = lax.broadcasted_iota(jnp.int32, shape, 0) * CMP_STRIDE + (CMP_LEN - 1)
    dist = t - cmp_end
    val = _bias_from_distance(dist, tab_ref, head)
    o_ref[0, 0] = jnp.where(dist >= 0, val, MASK_VALUE)


def _bias_tiles(rel_bias, t_len, tq):
    nc = t_len // CMP_STRIDE
    nq = t_len // tq
    rows = GQA_GROUP * tq
    smem = pl.BlockSpec(memory_space=pltpu.SMEM)
    band = pl.pallas_call(
        functools.partial(_band_bias_body, tq=tq),
        grid=(N_KV_HEADS, GQA_GROUP),
        in_specs=[smem],
        out_specs=pl.BlockSpec((1, WINDOW + tq, tq), lambda h, g: (h, 0, g)),
        out_shape=jax.ShapeDtypeStruct((N_KV_HEADS, WINDOW + tq, rows), F32),
        compiler_params=_params("arbitrary", "arbitrary"),
        name="band_bias",
    )(rel_bias)
    cmpb = pl.pallas_call(
        functools.partial(_cmp_bias_body, tq=tq, nc=nc),
        grid=(N_KV_HEADS, nq, GQA_GROUP),
        in_specs=[smem],
        out_specs=pl.BlockSpec((1, 1, nc, tq), lambda h, i, g: (h, i, 0, g)),
        out_shape=jax.ShapeDtypeStruct((N_KV_HEADS, nq, nc, rows), F32),
        compiler_params=_params("arbitrary", "arbitrary", "arbitrary"),
        name="cmp_bias",
    )(rel_bias)
    return band, cmpb


def _inproj_body(x_ref, g_ref, w_ref, wg_ref, o_ref, gate_ref, h_ref):
    @pl.when(pl.program_id(1) == 0)
    def _():
        h = _rms_norm_rows(x_ref[...], g_ref[...]).astype(BF16)
        h_ref[...] = h
        gate_ref[...] = _sigmoid(jnp.dot(h, wg_ref[...], preferred_element_type=F32))

    o_ref[...] = jnp.dot(h_ref[...], w_ref[...], preferred_element_type=F32).astype(o_ref.dtype)


def _in_projection(x2, g, w, wg, *, tm, tn):
    n = x2.shape[0]
    gcols = wg.shape[1]
    return pl.pallas_call(
        _inproj_body,
        grid=(n // tm, PROJ_COLS // tn),
        in_specs=[
            pl.BlockSpec((tm, D_MODEL), lambda i, j: (i, 0)),
            pl.BlockSpec((1, D_MODEL), lambda i, j: (0, 0)),
            pl.BlockSpec((D_MODEL, tn), lambda i, j: (0, j)),
            pl.BlockSpec((D_MODEL, gcols), lambda i, j: (0, 0)),
        ],
        out_specs=[
            pl.BlockSpec((tm, tn), lambda i, j: (i, j)),
            pl.BlockSpec((tm, gcols), lambda i, j: (i, 0)),
        ],
        out_shape=[
            jax.ShapeDtypeStruct((n, PROJ_COLS), BF16),
            jax.ShapeDtypeStruct((n, gcols), F32),
        ],
        scratch_shapes=[pltpu.VMEM((tm, D_MODEL), BF16)],
        compiler_params=_params("arbitrary", "arbitrary"),
        name="in_projection",
    )(x2, g, w, wg)


def _compress_body(tok_ref, pos_ref, w1_ref, w2_ref, o_ref, ot_ref, *, nc):
    outs = []
    for kv in range(2):
        c = tok_ref[0, 0, kv].astype(F32)
        top = jnp.dot((c + pos_ref[kv, 0:1, :]).astype(BF16), w1_ref[kv, 0], preferred_element_type=F32)
        bot = jnp.dot((c + pos_ref[kv, 1:2, :]).astype(BF16), w1_ref[kv, 1], preferred_element_type=F32)
        pre = top + pltpu.roll(bot, nc - 1, axis=0)
        act = pre * _sigmoid(pre)
        outs.append(jnp.dot(act.astype(BF16), w2_ref[kv], preferred_element_type=F32))
    kv = jnp.concatenate(outs, axis=1)
    o_ref[0, 0] = kv.astype(o_ref.dtype)
    ot_ref[0, 0] = kv.T.astype(ot_ref.dtype)


def _compress(tokc, pos, w1, w2):
    b, _, _, nc, width = tokc.shape
    return pl.pallas_call(
        functools.partial(_compress_body, nc=nc),
        grid=(b, N_KV_HEADS),
        in_specs=[
            pl.BlockSpec((1, 1, 2, nc, width), lambda i, h: (i, h, 0, 0, 0)),
            pl.BlockSpec((2, 2, width), lambda i, h: (0, 0, 0)),
            pl.BlockSpec((2, 2, width, HEAD_DIM), lambda i, h: (0, 0, 0, 0)),
            pl.BlockSpec((2, HEAD_DIM, HEAD_DIM), lambda i, h: (0, 0, 0)),
        ],
        out_specs=[
            pl.BlockSpec((1, 1, nc, KV_PAIR), lambda i, h: (i, h, 0, 0)),
            pl.BlockSpec((1, 1, KV_PAIR, nc), lambda i, h: (i, h, 0, 0)),
        ],
        out_shape=[
            jax.ShapeDtypeStruct((b, N_KV_HEADS, nc, KV_PAIR), BF16),
            jax.ShapeDtypeStruct((b, N_KV_HEADS, KV_PAIR, nc), BF16),
        ],
        compiler_params=_params("arbitrary", "arbitrary"),
        name="compress",
    )(tokc, pos, w1, w2)


_NT = (((1,), (1,)), ((), ()))


def _attn_body(q_ref, kvc_ref, kvct_ref, kvs_ref, kvw_ref, bt_ref, bc_ref, gate_ref, o_ref,
               kp_ref, vst_ref, vwt_ref, key_ref, m_ref, acc_ref, *, tq, t_len):
    qi = pl.program_id(2)
    nc = t_len // CMP_STRIDE
    ns = t_len // SLC_LEN
    rows = GQA_GROUP * tq
    n_win_tiles = WINDOW // tq
    blocks_per_tile = tq // SLC_LEN

    @pl.when(qi == 0)
    def _():
        k = kvs_ref[0][:, :HEAD_DIM]
        blk = lax.broadcasted_iota(jnp.int32, (t_len, HEAD_DIM), 0) // SLC_LEN
        col = lax.broadcasted_iota(jnp.int32, (t_len, HEAD_DIM), 1)
        onehot = jnp.where(blk == col, 1.0, 0.0).astype(BF16)
        kp_ref[...] = jnp.concatenate([k, onehot], axis=1)
        row = lax.broadcasted_iota(jnp.int32, (KV_PAIR, tq), 0)
        for c in range(t_len // tq):
            st = kvs_ref[0, c * tq:(c + 1) * tq, :].astype(F32).T
            vst_ref[c] = jnp.where(row == 0, 1.0, st).astype(BF16)
            wt = kvw_ref[0, c * tq:(c + 1) * tq, :].astype(F32).T
            vwt_ref[c] = jnp.where(row == 0, 1.0, wt).astype(BF16)

    q = q_ref[0]
    q4 = jnp.concatenate([q[:, g * HEAD_DIM:(g + 1) * HEAD_DIM] for g in range(GQA_GROUP)], axis=0)
    q4 = (q4.astype(F32) * (HEAD_DIM ** -0.5)).astype(BF16)

    kvc = kvc_ref[0, 0]
    s = lax.dot_general(kvc[:, :HEAD_DIM], q4, _NT, preferred_element_type=F32) + bc_ref[0, 0]
    m = jnp.max(s, axis=0, keepdims=True)
    p = jnp.exp(s - m)
    l = jnp.sum(p, axis=0, keepdims=True)
    pn = p * jnp.where(m > 0.5 * MASK_VALUE, 1.0 / l, 0.0)
    o_cmp = jnp.dot(kvct_ref[0, 0], pn.astype(BF16), preferred_element_type=F32)

    ps = pn[:, 0:tq] + pn[:, tq:2 * tq] + pn[:, 2 * tq:3 * tq] + pn[:, 3 * tq:4 * tq]
    sj = lax.broadcasted_iota(jnp.int32, (HEAD_DIM, nc), 0) * SLC_LEN
    ci = lax.broadcasted_iota(jnp.int32, (HEAD_DIM, nc), 1) * CMP_STRIDE
    overlap = jnp.where(ci < sj + SLC_LEN, jnp.where(ci + CMP_LEN > sj, 1.0, 0.0), 0.0).astype(BF16)
    p_hi = ps.astype(BF16)
    r_hi = ps - p_hi.astype(F32)
    p_md = r_hi.astype(BF16)
    p_lo = (r_hi - p_md.astype(F32)).astype(BF16)
    imp = (jnp.dot(overlap, p_hi, preferred_element_type=F32)
           + jnp.dot(overlap, p_md, preferred_element_type=F32)
           + jnp.dot(overlap, p_lo, preferred_element_type=F32))

    t = qi * tq + lax.broadcasted_iota(jnp.int32, (HEAD_DIM, tq), 1)
    blk = lax.broadcasted_iota(jnp.int32, (HEAD_DIM, tq), 0)
    cur = t // SLC_LEN
    key = jnp.where(blk == 0, KEY_BIG, jnp.where(blk == cur, KEY_BIG, jnp.where(blk == cur - 1, KEY_BIG, imp)))
    key = jnp.where(blk * SLC_LEN <= t, key, -KEY_BIG)
    key_ref[...] = key

    def rank_group(gi, rank):
        for jj in range(blocks_per_tile):
            j = gi * blocks_per_tile + jj
            col = key_ref[pl.ds(j, 1), :]
            gt = jnp.where(col > key, 1.0, 0.0)
            ge = jnp.where(col >= key, 1.0, 0.0)
            rank = rank + jnp.where(blk > j, ge, gt)
        return rank

    rank = lax.fori_loop(0, qi + 1, rank_group, jnp.zeros((HEAD_DIM, tq), F32))
    pen_t = jnp.where(rank < float(min(SLC_TOPK, ns)), 0.0, BLOCK_PENALTY)
    pen = jnp.concatenate([pen_t, jnp.zeros_like(pen_t)], axis=0).T[:, :HEAD_DIM].astype(BF16)
    q_sel = jnp.concatenate([q4, jnp.concatenate([pen] * GQA_GROUP, axis=0)], axis=1)
    q_win = jnp.concatenate([q4, jnp.zeros((rows, HEAD_DIM), BF16)], axis=1)

    def tile(qx, k_ref, vt_ref, kj, bias_off, first):
        start = pl.multiple_of(kj * tq, tq)
        sc = lax.dot_general(k_ref[pl.ds(start, tq), :], qx, _NT, preferred_element_type=F32)
        if bias_off is not None:
            sc = sc + bt_ref[0, bias_off:bias_off + tq, :]
        m_tile = jnp.max(sc, axis=0, keepdims=True)
        if first:
            m_new = m_tile
            pr = jnp.exp(sc - m_new).astype(BF16)
            acc_ref[...] = jnp.dot(vt_ref[kj], pr, preferred_element_type=F32)
        else:
            m_old = m_ref[...]
            m_new = jnp.maximum(m_old, m_tile)
            alpha = jnp.exp(m_old - m_new)
            pr = jnp.exp(sc - m_new).astype(BF16)
            acc_ref[...] = alpha * acc_ref[...] + jnp.dot(vt_ref[kj], pr, preferred_element_type=F32)
        m_ref[...] = m_new

    def finish():
        acc = acc_ref[...]
        return acc * (1.0 / acc[0:1, :])

    kw_ref = kvw_ref.at[0]
    tile(q_win, kw_ref, vwt_ref, qi, WINDOW, True)
    for d in range(1, n_win_tiles + 1):
        @pl.when(qi >= d)
        def _(d=d):
            tile(q_win, kw_ref, vwt_ref, qi - d, WINDOW - d * tq, False)
    o_win = finish()

    tile(q_sel, kp_ref, vst_ref, qi, WINDOW, True)

    @pl.when(qi >= 1)
    def _():
        tile(q_sel, kp_ref, vst_ref, qi - 1, WINDOW - tq, False)

    def far_tile(kj, carry):
        tile(q_sel, kp_ref, vst_ref, kj, None, False)
        return carry

    lax.fori_loop(0, jnp.maximum(qi - 1, 0), far_tile, 0)
    o_sel = finish()

    gates_t = gate_ref[0].T
    outs = []
    for g in range(GQA_GROUP):
        sl = slice(g * tq, (g + 1) * tq)
        comb = (gates_t[g:g + 1, :] * o_cmp[:, sl]
                + gates_t[GQA_GROUP + g:GQA_GROUP + g + 1, :] * o_sel[:, sl]
                + gates_t[2 * GQA_GROUP + g:2 * GQA_GROUP + g + 1, :] * o_win[:, sl])
        outs.append(comb.T[:, HEAD_DIM:])
    o_ref[0] = jnp.concatenate(outs, axis=1).astype(o_ref.dtype)


def _attention(proj3, kvc, kvct, band, cmpb, gates3, *, tq):
    b, t_len, _ = proj3.shape
    nc = t_len // CMP_STRIDE
    rows = GQA_GROUP * tq
    qw = GQA_GROUP * HEAD_DIM
    slc_blk = COL_KV // KV_PAIR + N_KV_HEADS
    win_blk = COL_KV // KV_PAIR + 2 * N_KV_HEADS
    return pl.pallas_call(
        functools.partial(_attn_body, tq=tq, t_len=t_len),
        grid=(b, N_KV_HEADS, t_len // tq),
        in_specs=[
            pl.BlockSpec((1, tq, qw), lambda i, h, q: (i, q, h)),
            pl.BlockSpec((1, 1, nc, KV_PAIR), lambda i, h, q: (i, h, 0, 0)),
            pl.BlockSpec((1, 1, KV_PAIR, nc), lambda i, h, q: (i, h, 0, 0)),
            pl.BlockSpec((1, t_len, KV_PAIR), lambda i, h, q: (i, 0, slc_blk + h)),
            pl.BlockSpec((1, t_len, KV_PAIR), lambda i, h, q: (i, 0, win_blk + h)),
            pl.BlockSpec((1, WINDOW + tq, rows), lambda i, h, q: (h, 0, 0)),
            pl.BlockSpec((1, 1, nc, rows), lambda i, h, q: (h, q, 0, 0)),
            pl.BlockSpec((1, tq, LANES), lambda i, h, q: (i, q, h)),
        ],
        out_specs=pl.BlockSpec((1, tq, qw), lambda i, h, q: (i, q, h)),
        out_shape=jax.ShapeDtypeStruct((b, t_len, ATTN_WIDTH), BF16),
        scratch_shapes=[
            pltpu.VMEM((t_len, KV_PAIR), BF16),
            pltpu.VMEM((t_len // tq, KV_PAIR, tq), BF16),
            pltpu.VMEM((t_len // tq, KV_PAIR, tq), BF16),
            pltpu.VMEM((HEAD_DIM, tq), F32),
            pltpu.VMEM((1, rows), F32),
            pltpu.VMEM((KV_PAIR, rows), F32),
        ],
        compiler_params=_params("arbitrary", "arbitrary", "arbitrary"),
        name="nsa_attention",
    )(proj3, kvc, kvct, proj3, proj3, band, cmpb, gates3)


def _conformer_body(a_ref, g_ref, ah_ref, gh_ref, w_ref, b_ref, lg_ref, lb_ref, o_ref, u_ref, *, tt):
    ti = pl.program_id(1)
    u_ref[CONV_HALO:, :] = a_ref[0].astype(F32) * _sigmoid(g_ref[0].astype(F32))
    halo = ah_ref[0].astype(F32) * _sigmoid(gh_ref[0].astype(F32))
    u_ref[0:CONV_HALO, :] = jnp.where(ti > 0, halo, 0.0)
    acc = jnp.zeros((tt, CONV_CHANNELS), F32)
    base = CONV_HALO - (CONV_TAPS - 1)
    for k in range(CONV_TAPS):
        acc = acc + w_ref[k:k + 1, :] * u_ref[pl.ds(base + k, tt), :]
    acc = acc + b_ref[...]
    mu = jnp.mean(acc, axis=-1, keepdims=True)
    xc = acc - mu
    var = jnp.mean(xc * xc, axis=-1, keepdims=True)
    y = xc * lax.rsqrt(var + NORM_EPS) * lg_ref[...] + lb_ref[...]
    o_ref[0] = (y * _sigmoid(y)).astype(o_ref.dtype)


def _conformer(proj3, w, b, lg, lb, *, tt):
    bsz, t_len, _ = proj3.shape
    a_blk = COL_CONV_A // CONV_CHANNELS
    g_blk = COL_CONV_G // CONV_CHANNELS
    hpt = tt // CONV_HALO
    halo_idx = lambda i, t: jnp.maximum(t * hpt - 1, 0)
    vec = pl.BlockSpec((1, CONV_CHANNELS), lambda i, t: (0, 0))
    return pl.pallas_call(
        functools.partial(_conformer_body, tt=tt),
        grid=(bsz, t_len // tt),
        in_specs=[
            pl.BlockSpec((1, tt, CONV_CHANNELS), lambda i, t: (i, t, a_blk)),
            pl.BlockSpec((1, tt, CONV_CHANNELS), lambda i, t: (i, t, g_blk)),
            pl.BlockSpec((1, CONV_HALO, CONV_CHANNELS), lambda i, t: (i, halo_idx(i, t), a_blk)),
            pl.BlockSpec((1, CONV_HALO, CONV_CHANNELS), lambda i, t: (i, halo_idx(i, t), g_blk)),
            pl.BlockSpec((CONV_TAPS, CONV_CHANNELS), lambda i, t: (0, 0)),
            vec, vec, vec,
        ],
        out_specs=pl.BlockSpec((1, tt, CONV_CHANNELS), lambda i, t: (i, t, 0)),
        out_shape=jax.ShapeDtypeStruct((bsz, t_len, CONV_CHANNELS), BF16),
        scratch_shapes=[pltpu.VMEM((CONV_HALO + tt, CONV_CHANNELS), F32)],
        compiler_params=_params("arbitrary", "arbitrary"),
        name="conformer_conv",
    )(proj3, proj3, proj3, proj3, w, b, lg, lb)


def _outproj_body(x_ref, a_ref, c_ref, wa_ref, wc_ref, o_ref):
    o_ref[...] = (x_ref[...]
                  + jnp.dot(a_ref[...], wa_ref[...], preferred_element_type=F32)
                  + jnp.dot(c_ref[...], wc_ref[...], preferred_element_type=F32))


def _out_projection(x2, attn2, conv2, wa, wc, *, tm, tn):
    n = x2.shape[0]
    return pl.pallas_call(
        _outproj_body,
        grid=(n // tm, D_MODEL // tn),
        in_specs=[
            pl.BlockSpec((tm, tn), lambda i, j: (i, j)),
            pl.BlockSpec((tm, ATTN_WIDTH), lambda i, j: (i, 0)),
            pl.BlockSpec((tm, CONV_CHANNELS), lambda i, j: (i, 0)),
            pl.BlockSpec((ATTN_WIDTH, tn), lambda i, j: (0, j)),
            pl.BlockSpec((CONV_CHANNELS, tn), lambda i, j: (0, j)),
        ],
        out_specs=pl.BlockSpec((tm, tn), lambda i, j: (i, j)),
        out_shape=jax.ShapeDtypeStruct((n, D_MODEL), F32),
        compiler_params=_params("arbitrary", "arbitrary"),
        name="out_projection",
    )(x2, attn2, conv2, wa, wc)


def _ffn_up_body(x_ref, xh_ref, g_ref, wa_ref, wg_ref, cwa_ref, cwg_ref, cba_ref, cbg_ref, o_ref,
                 h_ref, ua_ref, ug_ref, *, tm, tiles_per_seq):
    i = pl.program_id(0)

    @pl.when(pl.program_id(1) == 0)
    def _():
        h_ref[FFN_HALO:, :] = _rms_norm_rows(x_ref[...], g_ref[...]).astype(BF16)
        hh = _rms_norm_rows(xh_ref[...], g_ref[...])
        h_ref[0:FFN_HALO, :] = jnp.where(i % tiles_per_seq != 0, hh, 0.0).astype(BF16)

    h = h_ref[...]
    ua_ref[...] = jnp.dot(h, wa_ref[...], preferred_element_type=F32)
    ug_ref[...] = jnp.dot(h, wg_ref[...], preferred_element_type=F32)

    def conv(u_ref, cw_ref, cb_ref):
        base = FFN_HALO - (FFN_CONV_TAPS - 1)
        y = cw_ref[0:1, :] * u_ref[pl.ds(base, tm), :]
        for k in range(1, FFN_CONV_TAPS):
            y = y + cw_ref[k:k + 1, :] * u_ref[pl.ds(base + k, tm), :]
        return y + cb_ref[...]

    a = conv(ua_ref, cwa_ref, cba_ref)
    gate = conv(ug_ref, cwg_ref, cbg_ref)
    o_ref[...] = (gate * _sigmoid(gate) * a).astype(o_ref.dtype)


def _ffn_up(x2, g, w_up, cw, cb, *, tm, tn, t_len):
    n = x2.shape[0]
    nj = D_FF // tn
    hpt = tm // FFN_HALO
    return pl.pallas_call(
        functools.partial(_ffn_up_body, tm=tm, tiles_per_seq=t_len // tm),
        grid=(n // tm, nj),
        in_specs=[
            pl.BlockSpec((tm, D_MODEL), lambda i, j: (i, 0)),
            pl.BlockSpec((FFN_HALO, D_MODEL), lambda i, j: (jnp.maximum(i * hpt - 1, 0), 0)),
            pl.BlockSpec((1, D_MODEL), lambda i, j: (0, 0)),
            pl.BlockSpec((D_MODEL, tn), lambda i, j: (0, j)),
            pl.BlockSpec((D_MODEL, tn), lambda i, j: (0, j + nj)),
            pl.BlockSpec((FFN_CONV_TAPS, tn), lambda i, j: (0, j)),
            pl.BlockSpec((FFN_CONV_TAPS, tn), lambda i, j: (0, j + nj)),
            pl.BlockSpec((1, tn), lambda i, j: (0, j)),
            pl.BlockSpec((1, tn), lambda i, j: (0, j + nj)),
        ],
        out_specs=pl.BlockSpec((tm, tn), lambda i, j: (i, j)),
        out_shape=jax.ShapeDtypeStruct((n, D_FF), BF16),
        scratch_shapes=[
            pltpu.VMEM((FFN_HALO + tm, D_MODEL), BF16),
            pltpu.VMEM((FFN_HALO + tm, tn), F32),
            pltpu.VMEM((FFN_HALO + tm, tn), F32),
        ],
        compiler_params=_params("arbitrary", "arbitrary"),
        name="ffn_up",
    )(x2, x2, g, w_up, w_up, cw, cw, cb, cb)


def _ffn_down_body(act_ref, w_ref, x_ref, gn_ref, o_ref, acc_ref, *, final_norm):
    k = pl.program_id(1)

    @pl.when(k == 0)
    def _():
        acc_ref[...] = jnp.zeros_like(acc_ref)

    acc_ref[...] += jnp.dot(act_ref[...], w_ref[...], preferred_element_type=F32)

    @pl.when(k == pl.num_programs(1) - 1)
    def _():
        y = x_ref[...] + acc_ref[...]
        if final_norm:
            y = _rms_norm_rows(y, gn_ref[...])
        o_ref[...] = y


def _ffn_down(act, w_down, x2, gn, *, tm, tk, final_norm):
    n = x2.shape[0]
    return pl.pallas_call(
        functools.partial(_ffn_down_body, final_norm=final_norm),
        grid=(n // tm, D_FF // tk),
        in_specs=[
            pl.BlockSpec((tm, tk), lambda i, k: (i, k)),
            pl.BlockSpec((tk, D_MODEL), lambda i, k: (k, 0)),
            pl.BlockSpec((tm, D_MODEL), lambda i, k: (i, 0)),
            pl.BlockSpec((1, D_MODEL), lambda i, k: (0, 0)),
        ],
        out_specs=pl.BlockSpec((tm, D_MODEL), lambda i, k: (i, 0)),
        out_shape=jax.ShapeDtypeStruct((n, D_MODEL), F32),
        scratch_shapes=[pltpu.VMEM((tm, D_MODEL), F32)],
        compiler_params=_params("arbitrary", "arbitrary"),
        name="ffn_down",
    )(act, w_down, x2, gn)


def _in_weight_layout(w_in_l):
    q_cols = np.arange(ATTN_WIDTH)
    kv0 = ATTN_WIDTH
    gate0 = kv0 + KV_COLS
    conv0 = gate0 + GATE_COLS
    conv_cols = conv0 + np.arange(2 * CONV_CHANNELS)
    kv_cols = []
    for br in range(N_BRANCH):
        for h in range(N_KV_HEADS):
            for kv in range(2):
                kv_cols.append(kv0 + ((br * 2 + kv) * N_KV_HEADS + h) * HEAD_DIM + np.arange(HEAD_DIM))
    perm = np.concatenate([q_cols, conv_cols] + kv_cols)
    w_main = jnp.take(w_in_l, jnp.asarray(perm, jnp.int32), axis=1).astype(BF16)
    gsrc = np.zeros((N_KV_HEADS, N_BRANCH * GQA_GROUP), np.int32)
    for h in range(N_KV_HEADS):
        for br in range(N_BRANCH):
            for g in range(GQA_GROUP):
                gsrc[h, br * GQA_GROUP + g] = gate0 + (h * GQA_GROUP + g) * N_BRANCH + br
    wg = jnp.take(w_in_l, jnp.asarray(gsrc.reshape(-1)), axis=1).reshape(D_MODEL, N_KV_HEADS, -1)
    wg = jnp.pad(wg, ((0, 0), (0, 0), (0, LANES - N_BRANCH * GQA_GROUP)))
    return w_main, wg.reshape(D_MODEL, N_KV_HEADS * LANES).astype(BF16)


def _pick(n, prefs):
    for p in prefs:
        if n % p == 0:
            return p
    return n


def kernel(x, rel_bias, mix_norm_g, w_in, cmp_pos, cmp_w1, cmp_w2, conv_w, conv_b, conv_ln_g, conv_ln_b,
           w_out, ffn_norm_g, w_up, ffn_conv_w, ffn_conv_b, w_down, final_norm_g):
    bsz, t_len, _ = x.shape
    depth = w_in.shape[0]
    n = bsz * t_len
    assert t_len % 256 == 0 and t_len // SLC_LEN <= HEAD_DIM
    tq = 256
    tm = _pick(t_len, (512, 256))
    nc = t_len // CMP_STRIDE

    band, cmpb = _bias_tiles(rel_bias, t_len, tq)
    x2 = x.reshape(n, D_MODEL)
    half = CMP_STRIDE * HEAD_DIM
    for l in range(depth):
        w_main, wg = _in_weight_layout(w_in[l])
        proj, gates = _in_projection(x2, mix_norm_g[l][None, :], w_main, wg, tm=tm, tn=768)
        proj3 = proj.reshape(bsz, t_len, PROJ_COLS)
        gates3 = gates.reshape(bsz, t_len, N_KV_HEADS * LANES)

        tokc = proj3[:, :, COL_KV:COL_KV + N_KV_HEADS * KV_PAIR]
        tokc = tokc.reshape(bsz, nc, CMP_STRIDE, N_KV_HEADS, 2, HEAD_DIM).transpose(0, 3, 4, 1, 2, 5)
        tokc = tokc.reshape(bsz, N_KV_HEADS, 2, nc, half)
        pos = cmp_pos[l].reshape(2, 2, half)
        w1 = cmp_w1[l].reshape(2, 2, half, HEAD_DIM).astype(BF16)
        kvc, kvct = _compress(tokc, pos, w1, cmp_w2[l].astype(BF16))

        attn = _attention(proj3, kvc, kvct, band, cmpb, gates3, tq=tq)
        conv = _conformer(proj3, conv_w[l], conv_b[l][None, :], conv_ln_g[l][None, :],
                          conv_ln_b[l][None, :], tt=256)
        w_o = w_out[l].astype(BF16)
        x2 = _out_projection(x2, attn.reshape(n, ATTN_WIDTH), conv.reshape(n, CONV_CHANNELS),
                             w_o[:ATTN_WIDTH], w_o[ATTN_WIDTH:], tm=tm, tn=1024)

        act = _ffn_up(x2, ffn_norm_g[l][None, :], w_up[l].astype(BF16), ffn_conv_w[l],
                      ffn_conv_b[l][None, :], tm=tm, tn=512, t_len=t_len)
        x2 = _ffn_down(act, w_down[l].astype(BF16), x2, final_norm_g[None, :], tm=tm, tk=512,
                       final_norm=(l == depth - 1))
    return x2.reshape(bsz, t_len, D_MODEL)
```

```python
import functools
import math

import numpy as np
import jax
import jax.numpy as jnp
from jax import lax
from jax.experimental import pallas as pl
from jax.experimental.pallas import tpu as pltpu

F32 = jnp.float32
BF16 = jnp.bfloat16

D_MODEL = 2048
HEAD_DIM = 64
N_KV_HEADS = 4
GQA_GROUP = 4
N_Q_HEADS = N_KV_HEADS * GQA_GROUP
N_BRANCH = 3
ATTN_WIDTH = N_Q_HEADS * HEAD_DIM
CMP_LEN = 32
CMP_STRIDE = 16
SLC_LEN = 64
SLC_TOPK = 16
WINDOW = 512
CONV_CHANNELS = D_MODEL - ATTN_WIDTH
CONV_TAPS = 31
D_FF = 5632
FFN_CONV_TAPS = 3
N_BUCKETS = 32
MAX_DISTANCE = 128
NORM_EPS = 1e-6

KV_PAIR = 2 * HEAD_DIM
KV_COLS = N_BRANCH * N_KV_HEADS * KV_PAIR
GATE_COLS = N_BRANCH * N_Q_HEADS
COL_Q = 0
COL_CONV_A = ATTN_WIDTH
COL_CONV_G = COL_CONV_A + CONV_CHANNELS
COL_KV = COL_CONV_G + CONV_CHANNELS
PROJ_COLS = COL_KV + KV_COLS

MASK_VALUE = -1e30
BLOCK_PENALTY = -1e9
KEY_BIG = 1e30

LANES = 128
VMEM_LIMIT = 56 * 1024 * 1024
CONV_HALO = 32
FFN_HALO = 16


def _t5_bucket_last_distance():
    n = np.arange(0, 4 * MAX_DISTANCE, dtype=np.int64)
    max_exact = N_BUCKETS // 2
    nf = np.maximum(n, 1).astype(np.float64)
    large = max_exact + np.floor(np.log(nf / max_exact) / math.log(MAX_DISTANCE / max_exact)
                                 * (N_BUCKETS - max_exact)).astype(np.int64)
    large = np.minimum(large, N_BUCKETS - 1)
    bucket = np.where(n < max_exact, n, large)
    last = []
    for b in range(N_BUCKETS - 1):
        idx = np.nonzero(bucket == b)[0]
        last.append(int(idx.max()) if idx.size else None)
    return last


_BUCKET_LAST = _t5_bucket_last_distance()


def _params(*sem):
    return pltpu.CompilerParams(dimension_semantics=sem, vmem_limit_bytes=VMEM_LIMIT)


def _sigmoid(x):
    return jax.nn.sigmoid(x)


def _rms_norm_rows(x, g):
    ms = jnp.mean(x * x, axis=-1, keepdims=True)
    return x * lax.rsqrt(ms + NORM_EPS) * g


def _bias_from_distance(dist, tab_ref, head):
    c_far = tab_ref[N_BUCKETS - 1, head]
    val = jnp.zeros(dist.shape, F32)
    for b in range(N_BUCKETS - 2, -1, -1):
        if _BUCKET_LAST[b] is None:
            continue
        val = jnp.where(dist <= _BUCKET_LAST[b], tab_ref[b, head] - c_far, val)
    return val


def _band_bias_body(tab_ref, o_ref, *, tq):
    head = pl.program_id(0) * GQA_GROUP + pl.program_id(1)
    shape = (WINDOW + tq, tq)
    dist = (lax.broadcasted_iota(jnp.int32, shape, 1) + WINDOW
            - lax.broadcasted_iota(jnp.int32, shape, 0))
    val = _bias_from_distance(dist, tab_ref, head)
    val = jnp.where(dist >= 0, jnp.where(dist < WINDOW, val, MASK_VALUE), MASK_VALUE)
    o_ref[0] = val


def _cmp_bias_body(tab_ref, o_ref, *, tq, nc):
    head = pl.program_id(0) * GQA_GROUP + pl.program_id(2)
    t0 = pl.program_id(1) * tq
    shape = (nc, tq)
    t = t0 + lax.broadcasted_iota(jnp.int32, shape, 1)
    cmp_end = lax.broadcasted_iota(jnp.int32, shape, 0) * CMP_STRIDE + (CMP_LEN - 1)
    dist = t - cmp_end
    val = _bias_from_distance(dist, tab_ref, head)
    o_ref[0, 0] = jnp.where(dist >= 0, val, MASK_VALUE)


def _bias_tiles(rel_bias, t_len, tq):
    nc = t_len // CMP_STRIDE
    nq = t_len // tq
    rows = GQA_GROUP * tq
    smem = pl.BlockSpec(memory_space=pltpu.SMEM)
    band = pl.pallas_call(
        functools.partial(_band_bias_body, tq=tq),
        grid=(N_KV_HEADS, GQA_GROUP),
        in_specs=[smem],
        out_specs=pl.BlockSpec((1, WINDOW + tq, tq), lambda h, g: (h, 0, g)),
        out_shape=jax.ShapeDtypeStruct((N_KV_HEADS, WINDOW + tq, rows), F32),
        compiler_params=_params("arbitrary", "arbitrary"),
        name="band_bias",
    )(rel_bias)
    cmpb = pl.pallas_call(
        functools.partial(_cmp_bias_body, tq=tq, nc=nc),
        grid=(N_KV_HEADS, nq, GQA_GROUP),
        in_specs=[smem],
        out_specs=pl.BlockSpec((1, 1, nc, tq), lambda h, i, g: (h, i, 0, g)),
        out_shape=jax.ShapeDtypeStruct((N_KV_HEADS, nq, nc, rows), F32),
        compiler_params=_params("arbitrary", "arbitrary", "arbitrary"),
        name="cmp_bias",
    )(rel_bias)
    return band, cmpb


def _inproj_body(x_ref, g_ref, w_ref, wg_ref, o_ref, gate_ref, h_ref):
    @pl.when(pl.program_id(1) == 0)
    def _():
        h = _rms_norm_rows(x_ref[...], g_ref[...]).astype(BF16)
        h_ref[...] = h
        gate_ref[...] = _sigmoid(jnp.dot(h, wg_ref[...], preferred_element_type=F32))

    o_ref[...] = jnp.dot(h_ref[...], w_ref[...], preferred_element_type=F32).astype(o_ref.dtype)


def _in_projection(x2, g, w, wg, *, tm, tn):
    n = x2.shape[0]
    gcols = wg.shape[1]
    return pl.pallas_call(
        _inproj_body,
        grid=(n // tm, PROJ_COLS // tn),
        in_specs=[
            pl.BlockSpec((tm, D_MODEL), lambda i, j: (i, 0)),
            pl.BlockSpec((1, D_MODEL), lambda i, j: (0, 0)),
            pl.BlockSpec((D_MODEL, tn), lambda i, j: (0, j)),
            pl.BlockSpec((D_MODEL, gcols), lambda i, j: (0, 0)),
        ],
        out_specs=[
            pl.BlockSpec((tm, tn), lambda i, j: (i, j)),
            pl.BlockSpec((tm, gcols), lambda i, j: (i, 0)),
        ],
        out_shape=[
            jax.ShapeDtypeStruct((n, PROJ_COLS), BF16),
            jax.ShapeDtypeStruct((n, gcols), F32),
        ],
        scratch_shapes=[pltpu.VMEM((tm, D_MODEL), BF16)],
        compiler_params=_params("arbitrary", "arbitrary"),
        name="in_projection",
    )(x2, g, w, wg)


def _compress_body(tok_ref, pos_ref, w1_ref, w2_ref, o_ref, ot_ref, *, nc):
    outs = []
    for kv in range(2):
        c = tok_ref[0, 0, kv].astype(F32)
        top = jnp.dot((c + pos_ref[kv, 0:1, :]).astype(BF16), w1_ref[kv, 0], preferred_element_type=F32)
        bot = jnp.dot((c + pos_ref[kv, 1:2, :]).astype(BF16), w1_ref[kv, 1], preferred_element_type=F32)
        pre = top + pltpu.roll(bot, nc - 1, axis=0)
        act = pre * _sigmoid(pre)
        outs.append(jnp.dot(act.astype(BF16), w2_ref[kv], preferred_element_type=F32))
    kv = jnp.concatenate(outs, axis=1)
    o_ref[0, 0] = kv.astype(o_ref.dtype)
    ot_ref[0, 0] = kv.T.astype(ot_ref.dtype)


def _compress(tokc, pos, w1, w2):
    b, _, _, nc, width = tokc.shape
    return pl.pallas_call(
        functools.partial(_compress_body, nc=nc),
        grid=(b, N_KV_HEADS),
        in_specs=[
            pl.BlockSpec((1, 1, 2, nc, width), lambda i, h: (i, h, 0, 0, 0)),
            pl.BlockSpec((2, 2, width), lambda i, h: (0, 0, 0)),
            pl.BlockSpec((2, 2, width, HEAD_DIM), lambda i, h: (0, 0, 0, 0)),
            pl.BlockSpec((2, HEAD_DIM, HEAD_DIM), lambda i, h: (0, 0, 0)),
        ],
        out_specs=[
            pl.BlockSpec((1, 1, nc, KV_PAIR), lambda i, h: (i, h, 0, 0)),
            pl.BlockSpec((1, 1, KV_PAIR, nc), lambda i, h: (i, h, 0, 0)),
        ],
        out_shape=[
            jax.ShapeDtypeStruct((b, N_KV_HEADS, nc, KV_PAIR), BF16),
            jax.ShapeDtypeStruct((b, N_KV_HEADS, KV_PAIR, nc), BF16),
        ],
        compiler_params=_params("arbitrary", "arbitrary"),
        name="compress",
    )(tokc, pos, w1, w2)


_NT = (((1,), (1,)), ((), ()))


def _attn_body(q_ref, kvc_ref, kvct_ref, kvs_ref, kvw_ref, bt_ref, bc_ref, gate_ref, o_ref,
               kp_ref, vst_ref, vwt_ref, key_ref, m_ref, acc_ref, sa_ref, sb_ref, *, tq, t_len):
    qi = pl.program_id(2)
    nc = t_len // CMP_STRIDE
    ns = t_len // SLC_LEN
    rows = GQA_GROUP * tq
    blocks_per_tile = tq // SLC_LEN

    @pl.when(qi == 0)
    def _():
        k = kvs_ref[0][:, :HEAD_DIM]
        blk = lax.broadcasted_iota(jnp.int32, (t_len, HEAD_DIM), 0) // SLC_LEN
        col = lax.broadcasted_iota(jnp.int32, (t_len, HEAD_DIM), 1)
        onehot = jnp.where(blk == col, 1.0, 0.0).astype(BF16)
        kp_ref[...] = jnp.concatenate([k, onehot], axis=1)
        row = lax.broadcasted_iota(jnp.int32, (KV_PAIR, tq), 0)
        for c in range(t_len // tq):
            st = kvs_ref[0, c * tq:(c + 1) * tq, :].astype(F32).T
            vst_ref[c] = jnp.where(row == 0, 1.0, st).astype(BF16)
            wt = kvw_ref[0, c * tq:(c + 1) * tq, :].astype(F32).T
            vwt_ref[c] = jnp.where(row == 0, 1.0, wt).astype(BF16)

    q = q_ref[0]
    q4 = jnp.concatenate([q[:, g * HEAD_DIM:(g + 1) * HEAD_DIM] for g in range(GQA_GROUP)], axis=0)
    q4 = (q4.astype(F32) * (HEAD_DIM ** -0.5)).astype(BF16)

    kvc = kvc_ref[0, 0]
    s = lax.dot_general(kvc[:, :HEAD_DIM], q4, _NT, preferred_element_type=F32) + bc_ref[0, 0]
    m = jnp.max(s, axis=0, keepdims=True)
    p = jnp.exp(s - m)
    l = jnp.sum(p, axis=0, keepdims=True)
    pn = p * jnp.where(m > 0.5 * MASK_VALUE, 1.0 / l, 0.0)
    o_cmp = jnp.dot(kvct_ref[0, 0], pn.astype(BF16), preferred_element_type=F32)

    ps = pn[:, 0:tq] + pn[:, tq:2 * tq] + pn[:, 2 * tq:3 * tq] + pn[:, 3 * tq:4 * tq]
    sj = lax.broadcasted_iota(jnp.int32, (HEAD_DIM, nc), 0) * SLC_LEN
    ci = lax.broadcasted_iota(jnp.int32, (HEAD_DIM, nc), 1) * CMP_STRIDE
    overlap = jnp.where(ci < sj + SLC_LEN, jnp.where(ci + CMP_LEN > sj, 1.0, 0.0), 0.0).astype(BF16)
    p_hi = ps.astype(BF16)
    r_hi = ps - p_hi.astype(F32)
    p_md = r_hi.astype(BF16)
    p_lo = (r_hi - p_md.astype(F32)).astype(BF16)
    imp = (jnp.dot(overlap, p_hi, preferred_element_type=F32)
           + jnp.dot(overlap, p_md, preferred_element_type=F32)
           + jnp.dot(overlap, p_lo, preferred_element_type=F32))

    t = qi * tq + lax.broadcasted_iota(jnp.int32, (HEAD_DIM, tq), 1)
    blk = lax.broadcasted_iota(jnp.int32, (HEAD_DIM, tq), 0)
    cur = t // SLC_LEN
    key = jnp.where(blk == 0, KEY_BIG, jnp.where(blk == cur, KEY_BIG, jnp.where(blk == cur - 1, KEY_BIG, imp)))
    key = jnp.where(blk * SLC_LEN <= t, key, -KEY_BIG)
    key_ref[...] = key

    def rank_group(gi, rank):
        for jj in range(blocks_per_tile):
            j = gi * blocks_per_tile + jj
            col = key_ref[pl.ds(j, 1), :]
            gt = jnp.where(col > key, 1.0, 0.0)
            ge = jnp.where(col >= key, 1.0, 0.0)
            rank = rank + jnp.where(blk > j, ge, gt)
        return rank

    rank = lax.fori_loop(0, qi + 1, rank_group, jnp.zeros((HEAD_DIM, tq), F32))
    pen_t = jnp.where(rank < float(min(SLC_TOPK, ns)), 0.0, BLOCK_PENALTY)
    pen = jnp.concatenate([pen_t, jnp.zeros_like(pen_t)], axis=0).T[:, :HEAD_DIM].astype(BF16)
    q_sel = jnp.concatenate([q4, jnp.concatenate([pen] * GQA_GROUP, axis=0)], axis=1)
    q_win = jnp.concatenate([q4, jnp.zeros((rows, HEAD_DIM), BF16)], axis=1)

    def scores(qx, k_ref, kj, bias_off):
        start = pl.multiple_of(kj * tq, tq)
        sc = lax.dot_general(k_ref[pl.ds(start, tq), :], qx, _NT, preferred_element_type=F32)
        if bias_off is not None:
            sc = sc + bt_ref[0, bias_off:bias_off + tq, :]
        return sc

    def consume(sc, vt_ref, kj, br, st, first):
        m_tile = jnp.max(sc, axis=0, keepdims=True)
        if first:
            m_new = m_tile
            pr = jnp.exp(sc - m_new).astype(BF16)
            acc_ref[br, st] = jnp.dot(vt_ref[kj], pr, preferred_element_type=F32)
        else:
            m_old = m_ref[br, st]
            m_new = jnp.maximum(m_old, m_tile)
            alpha = jnp.exp(m_old - m_new)
            pr = jnp.exp(sc - m_new).astype(BF16)
            acc_ref[br, st] = alpha * acc_ref[br, st] + jnp.dot(vt_ref[kj], pr, preferred_element_type=F32)
        m_ref[br, st] = m_new

    def finish(br):
        m0 = m_ref[br, 0]
        m1 = m_ref[br, 1]
        m_all = jnp.maximum(m0, m1)
        acc = jnp.exp(m0 - m_all) * acc_ref[br, 0] + jnp.exp(m1 - m_all) * acc_ref[br, 1]
        return acc * (1.0 / acc[0:1, :])

    kw_ref = kvw_ref.at[0]
    win, sel = 0, 1
    def pair(a, b):
        sa = scores(a[0], a[1], a[3], a[4])
        sb = scores(b[0], b[1], b[3], b[4])
        consume(sa, a[2], a[3], a[5], a[6], a[7])
        consume(sb, b[2], b[3], b[5], b[6], b[7])

    pair((q_win, kw_ref, vwt_ref, qi, WINDOW, win, 0, True),
         (q_sel, kp_ref, vst_ref, qi, WINDOW, sel, 0, True))
    for br in (win, sel):
        m_ref[br, 1] = jnp.full((1, rows), MASK_VALUE, F32)
        acc_ref[br, 1] = jnp.zeros((KV_PAIR, rows), F32)

    @pl.when(qi >= 1)
    def _():
        pair((q_win, kw_ref, vwt_ref, qi - 1, WINDOW - tq, win, 1, False),
             (q_sel, kp_ref, vst_ref, qi - 1, WINDOW - tq, sel, 1, False))

    @pl.when(qi >= 2)
    def _():
        pair((q_win, kw_ref, vwt_ref, qi - 2, WINDOW - 2 * tq, win, 0, False),
             (q_sel, kp_ref, vst_ref, qi - 2, None, sel, 0, False))

    n_far = jnp.maximum(qi - 2, 0)

    @pl.when(n_far > 0)
    def _():
        sa_ref[...] = scores(q_sel, kp_ref, 0, None)

    def far_pair(pi, carry):
        sb_ref[...] = scores(q_sel, kp_ref, 2 * pi + 1, None)
        consume(sa_ref[...], vst_ref, 2 * pi, sel, 1, False)
        sa_ref[...] = scores(q_sel, kp_ref, jnp.minimum(2 * pi + 2, n_far - 1), None)
        consume(sb_ref[...], vst_ref, 2 * pi + 1, sel, 0, False)
        return carry

    lax.fori_loop(0, n_far // 2, far_pair, 0)

    @pl.when(n_far % 2 == 1)
    def _():
        consume(sa_ref[...], vst_ref, n_far - 1, sel, 1, False)

    o_win = finish(win)
    o_sel = finish(sel)

    gates_t = gate_ref[0].T
    outs = []
    for g in range(GQA_GROUP):
        sl = slice(g * tq, (g + 1) * tq)
        comb = (gates_t[g:g + 1, :] * o_cmp[:, sl]
                + gates_t[GQA_GROUP + g:GQA_GROUP + g + 1, :] * o_sel[:, sl]
                + gates_t[2 * GQA_GROUP + g:2 * GQA_GROUP + g + 1, :] * o_win[:, sl])
        outs.append(comb.T[:, HEAD_DIM:])
    o_ref[0] = jnp.concatenate(outs, axis=1).astype(o_ref.dtype)


def _attention(proj3, kvc, kvct, band, cmpb, gates3, *, tq):
    b, t_len, _ = proj3.shape
    assert WINDOW == 2 * tq
    nc = t_len // CMP_STRIDE
    rows = GQA_GROUP * tq
    qw = GQA_GROUP * HEAD_DIM
    slc_blk = COL_KV // KV_PAIR + N_KV_HEADS
    win_blk = COL_KV // KV_PAIR + 2 * N_KV_HEADS
    return pl.pallas_call(
        functools.partial(_attn_body, tq=tq, t_len=t_len),
        grid=(b, N_KV_HEADS, t_len // tq),
        in_specs=[
            pl.BlockSpec((1, tq, qw), lambda i, h, q: (i, q, h)),
            pl.BlockSpec((1, 1, nc, KV_PAIR), lambda i, h, q: (i, h, 0, 0)),
            pl.BlockSpec((1, 1, KV_PAIR, nc), lambda i, h, q: (i, h, 0, 0)),
            pl.BlockSpec((1, t_len, KV_PAIR), lambda i, h, q: (i, 0, slc_blk + h)),
            pl.BlockSpec((1, t_len, KV_PAIR), lambda i, h, q: (i, 0, win_blk + h)),
            pl.BlockSpec((1, WINDOW + tq, rows), lambda i, h, q: (h, 0, 0)),
            pl.BlockSpec((1, 1, nc, rows), lambda i, h, q: (h, q, 0, 0)),
            pl.BlockSpec((1, tq, LANES), lambda i, h, q: (i, q, h)),
        ],
        out_specs=pl.BlockSpec((1, tq, qw), lambda i, h, q: (i, q, h)),
        out_shape=jax.ShapeDtypeStruct((b, t_len, ATTN_WIDTH), BF16),
        scratch_shapes=[
            pltpu.VMEM((t_len, KV_PAIR), BF16),
            pltpu.VMEM((t_len // tq, KV_PAIR, tq), BF16),
            pltpu.VMEM((t_len // tq, KV_PAIR, tq), BF16),
            pltpu.VMEM((HEAD_DIM, tq), F32),
            pltpu.VMEM((2, 2, 1, rows), F32),
            pltpu.VMEM((2, 2, KV_PAIR, rows), F32),
            pltpu.VMEM((tq, rows), F32),
            pltpu.VMEM((tq, rows), F32),
        ],
        compiler_params=_params("arbitrary", "arbitrary", "arbitrary"),
        name="nsa_attention",
    )(proj3, kvc, kvct, proj3, proj3, band, cmpb, gates3)


def _conformer_body(a_ref, g_ref, ah_ref, gh_ref, w_ref, b_ref, lg_ref, lb_ref, o_ref, u_ref, *, tt):
    ti = pl.program_id(1)
    u_ref[CONV_HALO:, :] = a_ref[0].astype(F32) * _sigmoid(g_ref[0].astype(F32))
    halo = ah_ref[0].astype(F32) * _sigmoid(gh_ref[0].astype(F32))
    u_ref[0:CONV_HALO, :] = jnp.where(ti > 0, halo, 0.0)
    acc = jnp.zeros((tt, CONV_CHANNELS), F32)
    base = CONV_HALO - (CONV_TAPS - 1)
    for k in range(CONV_TAPS):
        acc = acc + w_ref[k:k + 1, :] * u_ref[pl.ds(base + k, tt), :]
    acc = acc + b_ref[...]
    mu = jnp.mean(acc, axis=-1, keepdims=True)
    xc = acc - mu
    var = jnp.mean(xc * xc, axis=-1, keepdims=True)
    y = xc * lax.rsqrt(var + NORM_EPS) * lg_ref[...] + lb_ref[...]
    o_ref[0] = (y * _sigmoid(y)).astype(o_ref.dtype)


def _conformer(proj3, w, b, lg, lb, *, tt):
    bsz, t_len, _ = proj3.shape
    a_blk = COL_CONV_A // CONV_CHANNELS
    g_blk = COL_CONV_G // CONV_CHANNELS
    hpt = tt // CONV_HALO
    halo_idx = lambda i, t: jnp.maximum(t * hpt - 1, 0)
    vec = pl.BlockSpec((1, CONV_CHANNELS), lambda i, t: (0, 0))
    return pl.pallas_call(
        functools.partial(_conformer_body, tt=tt),
        grid=(bsz, t_len // tt),
        in_specs=[
            pl.BlockSpec((1, tt, CONV_CHANNELS), lambda i, t: (i, t, a_blk)),
            pl.BlockSpec((1, tt, CONV_CHANNELS), lambda i, t: (i, t, g_blk)),
            pl.BlockSpec((1, CONV_HALO, CONV_CHANNELS), lambda i, t: (i, halo_idx(i, t), a_blk)),
            pl.BlockSpec((1, CONV_HALO, CONV_CHANNELS), lambda i, t: (i, halo_idx(i, t), g_blk)),
            pl.BlockSpec((CONV_TAPS, CONV_CHANNELS), lambda i, t: (0, 0)),
            vec, vec, vec,
        ],
        out_specs=pl.BlockSpec((1, tt, CONV_CHANNELS), lambda i, t: (i, t, 0)),
        out_shape=jax.ShapeDtypeStruct((bsz, t_len, CONV_CHANNELS), BF16),
        scratch_shapes=[pltpu.VMEM((CONV_HALO + tt, CONV_CHANNELS), F32)],
        compiler_params=_params("arbitrary", "arbitrary"),
        name="conformer_conv",
    )(proj3, proj3, proj3, proj3, w, b, lg, lb)


def _outproj_body(x_ref, a_ref, c_ref, wa_ref, wc_ref, o_ref):
    o_ref[...] = (x_ref[...]
                  + jnp.dot(a_ref[...], wa_ref[...], preferred_element_type=F32)
                  + jnp.dot(c_ref[...], wc_ref[...], preferred_element_type=F32))


def _out_projection(x2, attn2, conv2, wa, wc, *, tm, tn):
    n = x2.shape[0]
    return pl.pallas_call(
        _outproj_body,
        grid=(n // tm, D_MODEL // tn),
        in_specs=[
            pl.BlockSpec((tm, tn), lambda i, j: (i, j)),
            pl.BlockSpec((tm, ATTN_WIDTH), lambda i, j: (i, 0)),
            pl.BlockSpec((tm, CONV_CHANNELS), lambda i, j: (i, 0)),
            pl.BlockSpec((ATTN_WIDTH, tn), lambda i, j: (0, j)),
            pl.BlockSpec((CONV_CHANNELS, tn), lambda i, j: (0, j)),
        ],
        out_specs=pl.BlockSpec((tm, tn), lambda i, j: (i, j)),
        out_shape=jax.ShapeDtypeStruct((n, D_MODEL), F32),
        compiler_params=_params("arbitrary", "arbitrary"),
        name="out_projection",
    )(x2, attn2, conv2, wa, wc)


def _ffn_up_body(x_ref, xh_ref, g_ref, wa_ref, wg_ref, cwa_ref, cwg_ref, cba_ref, cbg_ref, o_ref,
                 h_ref, ua_ref, ug_ref, *, tm, tiles_per_seq):
    i = pl.program_id(0)

    @pl.when(pl.program_id(1) == 0)
    def _():
        h_ref[FFN_HALO:, :] = _rms_norm_rows(x_ref[...], g_ref[...]).astype(BF16)
        hh = _rms_norm_rows(xh_ref[...], g_ref[...])
        h_ref[0:FFN_HALO, :] = jnp.where(i % tiles_per_seq != 0, hh, 0.0).astype(BF16)

    h = h_ref[...]
    ua_ref[...] = jnp.dot(h, wa_ref[...], preferred_element_type=F32)
    ug_ref[...] = jnp.dot(h, wg_ref[...], preferred_element_type=F32)

    def conv(u_ref, cw_ref, cb_ref):
        base = FFN_HALO - (FFN_CONV_TAPS - 1)
        y = cw_ref[0:1, :] * u_ref[pl.ds(base, tm), :]
        for k in range(1, FFN_CONV_TAPS):
            y = y + cw_ref[k:k + 1, :] * u_ref[pl.ds(base + k, tm), :]
        return y + cb_ref[...]

    a = conv(ua_ref, cwa_ref, cba_ref)
    gate = conv(ug_ref, cwg_ref, cbg_ref)
    o_ref[...] = (gate * _sigmoid(gate) * a).astype(o_ref.dtype)


def _ffn_up(x2, g, w_up, cw, cb, *, tm, tn, t_len):
    n = x2.shape[0]
    nj = D_FF // tn
    hpt = tm // FFN_HALO
    return pl.pallas_call(
        functools.partial(_ffn_up_body, tm=tm, tiles_per_seq=t_len // tm),
        grid=(n // tm, nj),
        in_specs=[
            pl.BlockSpec((tm, D_MODEL), lambda i, j: (i, 0)),
            pl.BlockSpec((FFN_HALO, D_MODEL), lambda i, j: (jnp.maximum(i * hpt - 1, 0), 0)),
            pl.BlockSpec((1, D_MODEL), lambda i, j: (0, 0)),
            pl.BlockSpec((D_MODEL, tn), lambda i, j: (0, j)),
            pl.BlockSpec((D_MODEL, tn), lambda i, j: (0, j + nj)),
            pl.BlockSpec((FFN_CONV_TAPS, tn), lambda i, j: (0, j)),
            pl.BlockSpec((FFN_CONV_TAPS, tn), lambda i, j: (0, j + nj)),
            pl.BlockSpec((1, tn), lambda i, j: (0, j)),
            pl.BlockSpec((1, tn), lambda i, j: (0, j + nj)),
        ],
        out_specs=pl.BlockSpec((tm, tn), lambda i, j: (i, j)),
        out_shape=jax.ShapeDtypeStruct((n, D_FF), BF16),
        scratch_shapes=[
            pltpu.VMEM((FFN_HALO + tm, D_MODEL), BF16),
            pltpu.VMEM((FFN_HALO + tm, tn), F32),
            pltpu.VMEM((FFN_HALO + tm, tn), F32),
        ],
        compiler_params=_params("arbitrary", "arbitrary"),
        name="ffn_up",
    )(x2, x2, g, w_up, w_up, cw, cw, cb, cb)


def _ffn_down_body(act_ref, w_ref, x_ref, gn_ref, o_ref, acc_ref, *, final_norm):
    k = pl.program_id(1)

    @pl.when(k == 0)
    def _():
        acc_ref[...] = jnp.zeros_like(acc_ref)

    acc_ref[...] += jnp.dot(act_ref[...], w_ref[...], preferred_element_type=F32)

    @pl.when(k == pl.num_programs(1) - 1)
    def _():
        y = x_ref[...] + acc_ref[...]
        if final_norm:
            y = _rms_norm_rows(y, gn_ref[...])
        o_ref[...] = y


def _ffn_down(act, w_down, x2, gn, *, tm, tk, final_norm):
    n = x2.shape[0]
    return pl.pallas_call(
        functools.partial(_ffn_down_body, final_norm=final_norm),
        grid=(n // tm, D_FF // tk),
        in_specs=[
            pl.BlockSpec((tm, tk), lambda i, k: (i, k)),
            pl.BlockSpec((tk, D_MODEL), lambda i, k: (k, 0)),
            pl.BlockSpec((tm, D_MODEL), lambda i, k: (i, 0)),
            pl.BlockSpec((1, D_MODEL), lambda i, k: (0, 0)),
        ],
        out_specs=pl.BlockSpec((tm, D_MODEL), lambda i, k: (i, 0)),
        out_shape=jax.ShapeDtypeStruct((n, D_MODEL), F32),
        scratch_shapes=[pltpu.VMEM((tm, D_MODEL), F32)],
        compiler_params=_params("arbitrary", "arbitrary"),
        name="ffn_down",
    )(act, w_down, x2, gn)


def _in_weight_layout(w_in_l):
    q_cols = np.arange(ATTN_WIDTH)
    kv0 = ATTN_WIDTH
    gate0 = kv0 + KV_COLS
    conv0 = gate0 + GATE_COLS
    conv_cols = conv0 + np.arange(2 * CONV_CHANNELS)
    kv_cols = []
    for br in range(N_BRANCH):
        for h in range(N_KV_HEADS):
            for kv in range(2):
                kv_cols.append(kv0 + ((br * 2 + kv) * N_KV_HEADS + h) * HEAD_DIM + np.arange(HEAD_DIM))
    perm = np.concatenate([q_cols, conv_cols] + kv_cols)
    w_main = jnp.take(w_in_l, jnp.asarray(perm, jnp.int32), axis=1).astype(BF16)
    gsrc = np.zeros((N_KV_HEADS, N_BRANCH * GQA_GROUP), np.int32)
    for h in range(N_KV_HEADS):
        for br in range(N_BRANCH):
            for g in range(GQA_GROUP):
                gsrc[h, br * GQA_GROUP + g] = gate0 + (h * GQA_GROUP + g) * N_BRANCH + br
    wg = jnp.take(w_in_l, jnp.asarray(gsrc.reshape(-1)), axis=1).reshape(D_MODEL, N_KV_HEADS, -1)
    wg = jnp.pad(wg, ((0, 0), (0, 0), (0, LANES - N_BRANCH * GQA_GROUP)))
    return w_main, wg.reshape(D_MODEL, N_KV_HEADS * LANES).astype(BF16)


def _pick(n, prefs):
    for p in prefs:
        if n % p == 0:
            return p
    return n


def kernel(x, rel_bias, mix_norm_g, w_in, cmp_pos, cmp_w1, cmp_w2, conv_w, conv_b, conv_ln_g, conv_ln_b,
           w_out, ffn_norm_g, w_up, ffn_conv_w, ffn_conv_b, w_down, final_norm_g):
    bsz, t_len, _ = x.shape
    depth = w_in.shape[0]
    n = bsz * t_len
    assert t_len % 256 == 0 and t_len // SLC_LEN <= HEAD_DIM
    tq = 256
    tm = _pick(t_len, (1024, 512, 256))
    nc = t_len // CMP_STRIDE

    band, cmpb = _bias_tiles(rel_bias, t_len, tq)
    x2 = x.reshape(n, D_MODEL)
    half = CMP_STRIDE * HEAD_DIM
    for l in range(depth):
        w_main, wg = _in_weight_layout(w_in[l])
        proj, gates = _in_projection(x2, mix_norm_g[l][None, :], w_main, wg, tm=tm, tn=768)
        proj3 = proj.reshape(bsz, t_len, PROJ_COLS)
        gates3 = gates.reshape(bsz, t_len, N_KV_HEADS * LANES)

        tokc = proj3[:, :, COL_KV:COL_KV + N_KV_HEADS * KV_PAIR]
        tokc = tokc.reshape(bsz, nc, CMP_STRIDE, N_KV_HEADS, 2, HEAD_DIM).transpose(0, 3, 4, 1, 2, 5)
        tokc = tokc.reshape(bsz, N_KV_HEADS, 2, nc, half)
        pos = cmp_pos[l].reshape(2, 2, half)
        w1 = cmp_w1[l].reshape(2, 2, half, HEAD_DIM).astype(BF16)
        kvc, kvct = _compress(tokc, pos, w1, cmp_w2[l].astype(BF16))

        attn = _attention(proj3, kvc, kvct, band, cmpb, gates3, tq=tq)
        conv = _conformer(proj3, conv_w[l], conv_b[l][None, :], conv_ln_g[l][None, :],
                          conv_ln_b[l][None, :], tt=256)
        w_o = w_out[l].astype(BF16)
        x2 = _out_projection(x2, attn.reshape(n, ATTN_WIDTH), conv.reshape(n, CONV_CHANNELS),
                             w_o[:ATTN_WIDTH], w_o[ATTN_WIDTH:], tm=tm, tn=1024)

        act = _ffn_up(x2, ffn_norm_g[l][None, :], w_up[l].astype(BF16), ffn_conv_w[l],
                      ffn_conv_b[l][None, :], tm=tm, tn=512, t_len=t_len)
        x2 = _ffn_down(act, w_down[l].astype(BF16), x2, final_norm_g[None, :], tm=tm, tk=512,
                       final_norm=(l == depth - 1))
    return x2.reshape(bsz, t_len, D_MODEL)
```

```python
import functools
import math

import numpy as np
import jax
import jax.numpy as jnp
from jax import lax
from jax.experimental import pallas as pl
from jax.experimental.pallas import tpu as pltpu

F32 = jnp.float32
BF16 = jnp.bfloat16

D_MODEL = 2048
HEAD_DIM = 64
N_KV_HEADS = 4
GQA_GROUP = 4
N_Q_HEADS = N_KV_HEADS * GQA_GROUP
N_BRANCH = 3
ATTN_WIDTH = N_Q_HEADS * HEAD_DIM
CMP_LEN = 32
CMP_STRIDE = 16
SLC_LEN = 64
SLC_TOPK = 16
WINDOW = 512
CONV_CHANNELS = D_MODEL - ATTN_WIDTH
CONV_TAPS = 31
D_FF = 5632
FFN_CONV_TAPS = 3
N_BUCKETS = 32
MAX_DISTANCE = 128
NORM_EPS = 1e-6

KV_PAIR = 2 * HEAD_DIM
KV_COLS = N_BRANCH * N_KV_HEADS * KV_PAIR
GATE_COLS = N_BRANCH * N_Q_HEADS
COL_Q = 0
COL_CONV_A = ATTN_WIDTH
COL_CONV_G = COL_CONV_A + CONV_CHANNELS
COL_KV = COL_CONV_G + CONV_CHANNELS
PROJ_COLS = COL_KV + KV_COLS

MASK_VALUE = -1e30
BLOCK_PENALTY = -1e9
KEY_BIG = 1e30

LANES = 128
SUBLANES = 8
CONV_ROWS = 128
FFN_ROWS = 256
VMEM_LIMIT = 56 * 1024 * 1024
CONV_HALO = 32
FFN_HALO = 16


def _t5_bucket_last_distance():
    n = np.arange(0, 4 * MAX_DISTANCE, dtype=np.int64)
    max_exact = N_BUCKETS // 2
    nf = np.maximum(n, 1).astype(np.float64)
    large = max_exact + np.floor(np.log(nf / max_exact) / math.log(MAX_DISTANCE / max_exact)
                                 * (N_BUCKETS - max_exact)).astype(np.int64)
    large = np.minimum(large, N_BUCKETS - 1)
    bucket = np.where(n < max_exact, n, large)
    last = []
    for b in range(N_BUCKETS - 1):
        idx = np.nonzero(bucket == b)[0]
        last.append(int(idx.max()) if idx.size else None)
    return last


_BUCKET_LAST = _t5_bucket_last_distance()


def _params(*sem):
    return pltpu.CompilerParams(dimension_semantics=sem, vmem_limit_bytes=VMEM_LIMIT)


def _sigmoid(x):
    return jax.nn.sigmoid(x)


def _rms_norm_rows(x, g):
    ms = jnp.mean(x * x, axis=-1, keepdims=True)
    return x * lax.rsqrt(ms + NORM_EPS) * g


def _bias_from_distance(dist, tab_ref, head):
    c_far = tab_ref[N_BUCKETS - 1, head]
    val = jnp.zeros(dist.shape, F32)
    for b in range(N_BUCKETS - 2, -1, -1):
        if _BUCKET_LAST[b] is None:
            continue
        val = jnp.where(dist <= _BUCKET_LAST[b], tab_ref[b, head] - c_far, val)
    return val


def _band_bias_body(tab_ref, o_ref, *, tq):
    head = pl.program_id(0) * GQA_GROUP + pl.program_id(1)
    shape = (WINDOW + tq, tq)
    dist = (lax.broadcasted_iota(jnp.int32, shape, 1) + WINDOW
            - lax.broadcasted_iota(jnp.int32, shape, 0))
    val = _bias_from_distance(dist, tab_ref, head)
    val = jnp.where(dist >= 0, jnp.where(dist < WINDOW, val, MASK_VALUE), MASK_VALUE)
    o_ref[0] = val


def _cmp_bias_body(tab_ref, o_ref, *, tq, nc):
    head = pl.program_id(0) * GQA_GROUP + pl.program_id(1)
    shape = (2 * nc, tq)
    r = lax.broadcasted_iota(jnp.int32, shape, 0)
    i = lax.broadcasted_iota(jnp.int32, shape, 1)
    dist = i - (r - nc) * CMP_STRIDE - (CMP_LEN - 1)
    val = _bias_from_distance(dist, tab_ref, head)
    o_ref[0] = jnp.where(dist >= 0, val, MASK_VALUE)


def _bias_tiles(rel_bias, t_len, tq):
    nc = t_len // CMP_STRIDE
    rows = GQA_GROUP * tq
    smem = pl.BlockSpec(memory_space=pltpu.SMEM)
    band = pl.pallas_call(
        functools.partial(_band_bias_body, tq=tq),
        grid=(N_KV_HEADS, GQA_GROUP),
        in_specs=[smem],
        out_specs=pl.BlockSpec((1, WINDOW + tq, tq), lambda h, g: (h, 0, g)),
        out_shape=jax.ShapeDtypeStruct((N_KV_HEADS, WINDOW + tq, rows), F32),
        compiler_params=_params("arbitrary", "arbitrary"),
        name="band_bias",
    )(rel_bias)
    cmpb = pl.pallas_call(
        functools.partial(_cmp_bias_body, tq=tq, nc=nc),
        grid=(N_KV_HEADS, GQA_GROUP),
        in_specs=[smem],
        out_specs=pl.BlockSpec((1, 2 * nc, tq), lambda h, g: (h, 0, g)),
        out_shape=jax.ShapeDtypeStruct((N_KV_HEADS, 2 * nc, rows), F32),
        compiler_params=_params("arbitrary", "arbitrary"),
        name="cmp_bias",
    )(rel_bias)
    return band, cmpb


def _inproj_body(x_ref, g_ref, w_ref, wg_ref, o_ref, gate_ref, h_ref):
    @pl.when(pl.program_id(1) == 0)
    def _():
        h = _rms_norm_rows(x_ref[...], g_ref[...]).astype(BF16)
        h_ref[...] = h
        gate_ref[...] = _sigmoid(jnp.dot(h, wg_ref[...], preferred_element_type=F32))

    o_ref[...] = jnp.dot(h_ref[...], w_ref[...], preferred_element_type=F32).astype(o_ref.dtype)


def _in_projection(x2, g, w, wg, *, tm, tn):
    n = x2.shape[0]
    gcols = wg.shape[1]
    return pl.pallas_call(
        _inproj_body,
        grid=(n // tm, PROJ_COLS // tn),
        in_specs=[
            pl.BlockSpec((tm, D_MODEL), lambda i, j: (i, 0)),
            pl.BlockSpec((1, D_MODEL), lambda i, j: (0, 0)),
            pl.BlockSpec((D_MODEL, tn), lambda i, j: (0, j)),
            pl.BlockSpec((D_MODEL, gcols), lambda i, j: (0, 0)),
        ],
        out_specs=[
            pl.BlockSpec((tm, tn), lambda i, j: (i, j)),
            pl.BlockSpec((tm, gcols), lambda i, j: (i, 0)),
        ],
        out_shape=[
            jax.ShapeDtypeStruct((n, PROJ_COLS), BF16),
            jax.ShapeDtypeStruct((n, gcols), F32),
        ],
        scratch_shapes=[pltpu.VMEM((tm, D_MODEL), BF16)],
        compiler_params=_params("arbitrary", "arbitrary"),
        name="in_projection",
    )(x2, g, w, wg)


CMP_WIDTH = N_KV_HEADS * KV_PAIR


def _compress_body(x_ref, pos_ref, w1_ref, w2_ref, o_ref, ot_ref, top_ref, bot_ref, *, nc):
    l = pl.program_id(1)

    @pl.when(l == 0)
    def _():
        top_ref[...] = jnp.zeros_like(top_ref)
        bot_ref[...] = jnp.zeros_like(bot_ref)

    x = x_ref[0].astype(F32)
    top_ref[...] += jnp.dot((x + pos_ref[0, 0]).astype(BF16), w1_ref[0, 0], preferred_element_type=F32)
    bot_ref[...] += jnp.dot((x + pos_ref[0, 1]).astype(BF16), w1_ref[0, 1], preferred_element_type=F32)

    @pl.when(l == pl.num_programs(1) - 1)
    def _():
        pre = top_ref[...] + pltpu.roll(bot_ref[...], nc - 1, axis=0)
        act = pre * _sigmoid(pre)
        out = jnp.dot(act.astype(BF16), w2_ref[...], preferred_element_type=F32)
        for h in range(N_KV_HEADS):
            kv = out[:, h * KV_PAIR:(h + 1) * KV_PAIR]
            o_ref[0, h] = kv.astype(o_ref.dtype)
            ot_ref[0, h] = kv.T.astype(ot_ref.dtype)


def _compress(projc, posx, w1big, w2big):
    b, nc, _ = projc.shape
    assert PROJ_COLS % CMP_WIDTH == 0 and COL_KV % CMP_WIDTH == 0
    per_tok, first = PROJ_COLS // CMP_WIDTH, COL_KV // CMP_WIDTH
    return pl.pallas_call(
        functools.partial(_compress_body, nc=nc),
        grid=(b, CMP_STRIDE),
        in_specs=[
            pl.BlockSpec((1, nc, CMP_WIDTH), lambda i, l: (i, 0, l * per_tok + first)),
            pl.BlockSpec((1, 2, 1, CMP_WIDTH), lambda i, l: (l, 0, 0, 0)),
            pl.BlockSpec((1, 2, CMP_WIDTH, CMP_WIDTH), lambda i, l: (l, 0, 0, 0)),
            pl.BlockSpec((CMP_WIDTH, CMP_WIDTH), lambda i, l: (0, 0)),
        ],
        out_specs=[
            pl.BlockSpec((1, N_KV_HEADS, nc, KV_PAIR), lambda i, l: (i, 0, 0, 0)),
            pl.BlockSpec((1, N_KV_HEADS, KV_PAIR, nc), lambda i, l: (i, 0, 0, 0)),
        ],
        out_shape=[
            jax.ShapeDtypeStruct((b, N_KV_HEADS, nc, KV_PAIR), BF16),
            jax.ShapeDtypeStruct((b, N_KV_HEADS, KV_PAIR, nc), BF16),
        ],
        scratch_shapes=[pltpu.VMEM((nc, CMP_WIDTH), F32), pltpu.VMEM((nc, CMP_WIDTH), F32)],
        compiler_params=_params("arbitrary", "arbitrary"),
        name="compress",
    )(projc, posx, w1big, w2big)


def _compress_weight_layout(pos, w1, w2):
    eye_h = jnp.eye(N_KV_HEADS, dtype=F32)
    eye_k = jnp.eye(2, dtype=F32)
    w1r = w1.reshape(2, 2, CMP_STRIDE, HEAD_DIM, HEAD_DIM)
    w1big = jnp.einsum('ktlde,hg,kj->lthkdgje', w1r, eye_h, eye_k)
    w1big = w1big.reshape(CMP_STRIDE, 2, CMP_WIDTH, CMP_WIDTH).astype(BF16)
    w2big = jnp.einsum('kde,hg,kj->hkdgje', w2, eye_h, eye_k).reshape(CMP_WIDTH, CMP_WIDTH).astype(BF16)
    posr = pos.reshape(2, 2, CMP_STRIDE, HEAD_DIM).transpose(2, 1, 0, 3)
    posx = jnp.broadcast_to(posr[:, :, None], (CMP_STRIDE, 2, N_KV_HEADS, 2, HEAD_DIM))
    return posx.reshape(CMP_STRIDE, 2, 1, CMP_WIDTH), w1big, w2big


_NT = (((1,), (1,)), ((), ()))


def _attn_body(q_ref, kvc_ref, kvct_ref, kvs_ref, kvw_ref, bt_ref, bc_ref, gate_ref, o_ref,
               kp_ref, vst_ref, vwt_ref, key_ref, m_ref, acc_ref, sa_ref, sb_ref, *, tq, t_len):
    qi = pl.program_id(2)
    nc = t_len // CMP_STRIDE
    ns = t_len // SLC_LEN
    rows = GQA_GROUP * tq
    blocks_per_tile = tq // SLC_LEN

    @pl.when(qi == 0)
    def _():
        k = kvs_ref[0][:, :HEAD_DIM]
        blk = lax.broadcasted_iota(jnp.int32, (t_len, HEAD_DIM), 0) // SLC_LEN
        col = lax.broadcasted_iota(jnp.int32, (t_len, HEAD_DIM), 1)
        onehot = jnp.where(blk == col, 1.0, 0.0).astype(BF16)
        kp_ref[...] = jnp.concatenate([k, onehot], axis=1)
        row = lax.broadcasted_iota(jnp.int32, (KV_PAIR, tq), 0)
        for c in range(t_len // tq):
            st = kvs_ref[0, c * tq:(c + 1) * tq, :].astype(F32).T
            vst_ref[c] = jnp.where(row == 0, 1.0, st).astype(BF16)
            wt = kvw_ref[0, c * tq:(c + 1) * tq, :].astype(F32).T
            vwt_ref[c] = jnp.where(row == 0, 1.0, wt).astype(BF16)

    q = q_ref[0]
    q4 = jnp.concatenate([q[:, g * HEAD_DIM:(g + 1) * HEAD_DIM] for g in range(GQA_GROUP)], axis=0)
    q4 = (q4.astype(F32) * (HEAD_DIM ** -0.5)).astype(BF16)

    kvc = kvc_ref[0, 0]
    bias_row = pl.multiple_of(nc - qi * (tq // CMP_STRIDE), tq // CMP_STRIDE)
    s = (lax.dot_general(kvc[:, :HEAD_DIM], q4, _NT, preferred_element_type=F32)
         + bc_ref[0, pl.ds(bias_row, nc), :])
    m = jnp.max(s, axis=0, keepdims=True)
    p = jnp.exp(s - m)
    l = jnp.sum(p, axis=0, keepdims=True)
    pn = p * jnp.where(m > 0.5 * MASK_VALUE, 1.0 / l, 0.0)
    o_cmp = jnp.dot(kvct_ref[0, 0], pn.astype(BF16), preferred_element_type=F32)

    ps = pn[:, 0:tq] + pn[:, tq:2 * tq] + pn[:, 2 * tq:3 * tq] + pn[:, 3 * tq:4 * tq]
    sj = lax.broadcasted_iota(jnp.int32, (HEAD_DIM, nc), 0) * SLC_LEN
    ci = lax.broadcasted_iota(jnp.int32, (HEAD_DIM, nc), 1) * CMP_STRIDE
    overlap = jnp.where(ci < sj + SLC_LEN, jnp.where(ci + CMP_LEN > sj, 1.0, 0.0), 0.0).astype(BF16)
    p_hi = ps.astype(BF16)
    r_hi = ps - p_hi.astype(F32)
    p_md = r_hi.astype(BF16)
    p_lo = (r_hi - p_md.astype(F32)).astype(BF16)
    imp = (jnp.dot(overlap, p_hi, preferred_element_type=F32)
           + jnp.dot(overlap, p_md, preferred_element_type=F32)
           + jnp.dot(overlap, p_lo, preferred_element_type=F32))

    t = qi * tq + lax.broadcasted_iota(jnp.int32, (HEAD_DIM, tq), 1)
    blk = lax.broadcasted_iota(jnp.int32, (HEAD_DIM, tq), 0)
    cur = t // SLC_LEN
    key = jnp.where(blk == 0, KEY_BIG, jnp.where(blk == cur, KEY_BIG, jnp.where(blk == cur - 1, KEY_BIG, imp)))
    key = jnp.where(blk * SLC_LEN <= t, key, -KEY_BIG)
    key_ref[...] = key

    def rank_group(gi, rank):
        for jj in range(blocks_per_tile):
            j = gi * blocks_per_tile + jj
            col = key_ref[pl.ds(j, 1), :]
            gt = jnp.where(col > key, 1.0, 0.0)
            ge = jnp.where(col >= key, 1.0, 0.0)
            rank = rank + jnp.where(blk > j, ge, gt)
        return rank

    rank = lax.fori_loop(0, qi + 1, rank_group, jnp.zeros((HEAD_DIM, tq), F32))
    pen_t = jnp.where(rank < float(min(SLC_TOPK, ns)), 0.0, BLOCK_PENALTY)
    pen = jnp.concatenate([pen_t, jnp.zeros_like(pen_t)], axis=0).T[:, :HEAD_DIM].astype(BF16)
    q_sel = jnp.concatenate([q4, jnp.concatenate([pen] * GQA_GROUP, axis=0)], axis=1)
    q_win = jnp.concatenate([q4, jnp.zeros((rows, HEAD_DIM), BF16)], axis=1)

    def scores(qx, k_ref, kj, bias_off):
        start = pl.multiple_of(kj * tq, tq)
        sc = lax.dot_general(k_ref[pl.ds(start, tq), :], qx, _NT, preferred_element_type=F32)
        if bias_off is not None:
            sc = sc + bt_ref[0, bias_off:bias_off + tq, :]
        return sc

    def consume(sc, vt_ref, kj, br, st, first):
        m_tile = jnp.max(sc, axis=0, keepdims=True)
        if first:
            m_new = m_tile
            pr = jnp.exp(sc - m_new).astype(BF16)
            acc_ref[br, st] = jnp.dot(vt_ref[kj], pr, preferred_element_type=F32)
        else:
            m_old = m_ref[br, st]
            m_new = jnp.maximum(m_old, m_tile)
            alpha = jnp.exp(m_old - m_new)
            pr = jnp.exp(sc - m_new).astype(BF16)
            acc_ref[br, st] = alpha * acc_ref[br, st] + jnp.dot(vt_ref[kj], pr, preferred_element_type=F32)
        m_ref[br, st] = m_new

    def finish(br):
        m0 = m_ref[br, 0]
        m1 = m_ref[br, 1]
        m_all = jnp.maximum(m0, m1)
        acc = jnp.exp(m0 - m_all) * acc_ref[br, 0] + jnp.exp(m1 - m_all) * acc_ref[br, 1]
        return acc * (1.0 / acc[0:1, :])

    kw_ref = kvw_ref.at[0]
    win, sel = 0, 1
    def pair(a, b):
        sa = scores(a[0], a[1], a[3], a[4])
        sb = scores(b[0], b[1], b[3], b[4])
        consume(sa, a[2], a[3], a[5], a[6], a[7])
        consume(sb, b[2], b[3], b[5], b[6], b[7])

    pair((q_win, kw_ref, vwt_ref, qi, WINDOW, win, 0, True),
         (q_sel, kp_ref, vst_ref, qi, WINDOW, sel, 0, True))
    for br in (win, sel):
        m_ref[br, 1] = jnp.full((1, rows), MASK_VALUE, F32)
        acc_ref[br, 1] = jnp.zeros((KV_PAIR, rows), F32)

    @pl.when(qi >= 1)
    def _():
        pair((q_win, kw_ref, vwt_ref, qi - 1, WINDOW - tq, win, 1, False),
             (q_sel, kp_ref, vst_ref, qi - 1, WINDOW - tq, sel, 1, False))

    @pl.when(qi >= 2)
    def _():
        pair((q_win, kw_ref, vwt_ref, qi - 2, WINDOW - 2 * tq, win, 0, False),
             (q_sel, kp_ref, vst_ref, qi - 2, None, sel, 0, False))

    n_far = jnp.maximum(qi - 2, 0)

    @pl.when(n_far > 0)
    def _():
        sa_ref[...] = scores(q_sel, kp_ref, 0, None)

    def far_pair(pi, carry):
        sb_ref[...] = scores(q_sel, kp_ref, 2 * pi + 1, None)
        consume(sa_ref[...], vst_ref, 2 * pi, sel, 1, False)
        sa_ref[...] = scores(q_sel, kp_ref, jnp.minimum(2 * pi + 2, n_far - 1), None)
        consume(sb_ref[...], vst_ref, 2 * pi + 1, sel, 0, False)
        return carry

    lax.fori_loop(0, n_far // 2, far_pair, 0)

    @pl.when(n_far % 2 == 1)
    def _():
        consume(sa_ref[...], vst_ref, n_far - 1, sel, 1, False)

    o_win = finish(win)
    o_sel = finish(sel)

    gates_t = gate_ref[0].T
    outs = []
    for g in range(GQA_GROUP):
        sl = slice(g * tq, (g + 1) * tq)
        comb = (gates_t[g:g + 1, :] * o_cmp[:, sl]
                + gates_t[GQA_GROUP + g:GQA_GROUP + g + 1, :] * o_sel[:, sl]
                + gates_t[2 * GQA_GROUP + g:2 * GQA_GROUP + g + 1, :] * o_win[:, sl])
        outs.append(comb.T[:, HEAD_DIM:])
    o_ref[0] = jnp.concatenate(outs, axis=1).astype(o_ref.dtype)


def _attention(proj3, kvc, kvct, band, cmpb, gates3, *, tq):
    b, t_len, _ = proj3.shape
    assert WINDOW == 2 * tq
    nc = t_len // CMP_STRIDE
    rows = GQA_GROUP * tq
    qw = GQA_GROUP * HEAD_DIM
    slc_blk = COL_KV // KV_PAIR + N_KV_HEADS
    win_blk = COL_KV // KV_PAIR + 2 * N_KV_HEADS
    return pl.pallas_call(
        functools.partial(_attn_body, tq=tq, t_len=t_len),
        grid=(b, N_KV_HEADS, t_len // tq),
        in_specs=[
            pl.BlockSpec((1, tq, qw), lambda i, h, q: (i, q, h)),
            pl.BlockSpec((1, 1, nc, KV_PAIR), lambda i, h, q: (i, h, 0, 0)),
            pl.BlockSpec((1, 1, KV_PAIR, nc), lambda i, h, q: (i, h, 0, 0)),
            pl.BlockSpec((1, t_len, KV_PAIR), lambda i, h, q: (i, 0, slc_blk + h)),
            pl.BlockSpec((1, t_len, KV_PAIR), lambda i, h, q: (i, 0, win_blk + h)),
            pl.BlockSpec((1, WINDOW + tq, rows), lambda i, h, q: (h, 0, 0)),
            pl.BlockSpec((1, 2 * nc, rows), lambda i, h, q: (h, 0, 0)),
            pl.BlockSpec((1, tq, LANES), lambda i, h, q: (i, q, h)),
        ],
        out_specs=pl.BlockSpec((1, tq, qw), lambda i, h, q: (i, q, h)),
        out_shape=jax.ShapeDtypeStruct((b, t_len, ATTN_WIDTH), BF16),
        scratch_shapes=[
            pltpu.VMEM((t_len, KV_PAIR), BF16),
            pltpu.VMEM((t_len // tq, KV_PAIR, tq), BF16),
            pltpu.VMEM((t_len // tq, KV_PAIR, tq), BF16),
            pltpu.VMEM((HEAD_DIM, tq), F32),
            pltpu.VMEM((2, 2, 1, rows), F32),
            pltpu.VMEM((2, 2, KV_PAIR, rows), F32),
            pltpu.VMEM((tq, rows), F32),
            pltpu.VMEM((tq, rows), F32),
        ],
        compiler_params=_params("arbitrary", "arbitrary", "arbitrary"),
        name="nsa_attention",
    )(proj3, kvc, kvct, proj3, proj3, band, cmpb, gates3)


def _conformer_body(a_ref, g_ref, ah_ref, gh_ref, w_ref, b_ref, lg_ref, lb_ref, o_ref, u_ref, y_ref, *, tt):
    ti = pl.program_id(1)
    u_ref[CONV_HALO:, :] = a_ref[0].astype(F32) * _sigmoid(g_ref[0].astype(F32))
    halo = ah_ref[0].astype(F32) * _sigmoid(gh_ref[0].astype(F32))
    u_ref[0:CONV_HALO, :] = jnp.where(ti > 0, halo, 0.0)
    base = CONV_HALO - (CONV_TAPS - 1)
    ext = CONV_ROWS + CONV_HALO
    for cb in range(CONV_CHANNELS // LANES):
        cs = slice(cb * LANES, (cb + 1) * LANES)
        wblk = w_ref[:, cs]
        for rb in range(tt // CONV_ROWS):
            ublk = u_ref[rb * CONV_ROWS:rb * CONV_ROWS + ext, cs]
            acc = jnp.zeros((CONV_ROWS, LANES), F32)
            for r in range(SUBLANES):
                ur = ublk if r == 0 else pltpu.roll(ublk, ext - r, axis=0)
                for a in range(CONV_HALO // SUBLANES + 1):
                    k = SUBLANES * a + r - base
                    if 0 <= k < CONV_TAPS:
                        acc = acc + wblk[k:k + 1, :] * ur[SUBLANES * a:SUBLANES * a + CONV_ROWS]
            y_ref[rb * CONV_ROWS:(rb + 1) * CONV_ROWS, cs] = acc
    acc = y_ref[...] + b_ref[...]
    mu = jnp.mean(acc, axis=-1, keepdims=True)
    xc = acc - mu
    var = jnp.mean(xc * xc, axis=-1, keepdims=True)
    y = xc * lax.rsqrt(var + NORM_EPS) * lg_ref[...] + lb_ref[...]
    o_ref[0] = (y * _sigmoid(y)).astype(o_ref.dtype)


def _conformer(proj3, w, b, lg, lb, *, tt):
    bsz, t_len, _ = proj3.shape
    a_blk = COL_CONV_A // CONV_CHANNELS
    g_blk = COL_CONV_G // CONV_CHANNELS
    hpt = tt // CONV_HALO
    halo_idx = lambda i, t: jnp.maximum(t * hpt - 1, 0)
    vec = pl.BlockSpec((1, CONV_CHANNELS), lambda i, t: (0, 0))
    return pl.pallas_call(
        functools.partial(_conformer_body, tt=tt),
        grid=(bsz, t_len // tt),
        in_specs=[
            pl.BlockSpec((1, tt, CONV_CHANNELS), lambda i, t: (i, t, a_blk)),
            pl.BlockSpec((1, tt, CONV_CHANNELS), lambda i, t: (i, t, g_blk)),
            pl.BlockSpec((1, CONV_HALO, CONV_CHANNELS), lambda i, t: (i, halo_idx(i, t), a_blk)),
            pl.BlockSpec((1, CONV_HALO, CONV_CHANNELS), lambda i, t: (i, halo_idx(i, t), g_blk)),
            pl.BlockSpec((CONV_TAPS, CONV_CHANNELS), lambda i, t: (0, 0)),
            vec, vec, vec,
        ],
        out_specs=pl.BlockSpec((1, tt, CONV_CHANNELS), lambda i, t: (i, t, 0)),
        out_shape=jax.ShapeDtypeStruct((bsz, t_len, CONV_CHANNELS), BF16),
        scratch_shapes=[pltpu.VMEM((CONV_HALO + tt, CONV_CHANNELS), F32),
                        pltpu.VMEM((tt, CONV_CHANNELS), F32)],
        compiler_params=_params("arbitrary", "arbitrary"),
        name="conformer_conv",
    )(proj3, proj3, proj3, proj3, w, b, lg, lb)


def _outproj_body(x_ref, a_ref, c_ref, wa_ref, wc_ref, o_ref):
    o_ref[...] = (x_ref[...]
                  + jnp.dot(a_ref[...], wa_ref[...], preferred_element_type=F32)
                  + jnp.dot(c_ref[...], wc_ref[...], preferred_element_type=F32))


def _out_projection(x2, attn2, conv2, wa, wc, *, tm, tn):
    n = x2.shape[0]
    return pl.pallas_call(
        _outproj_body,
        grid=(n // tm, D_MODEL // tn),
        in_specs=[
            pl.BlockSpec((tm, tn), lambda i, j: (i, j)),
            pl.BlockSpec((tm, ATTN_WIDTH), lambda i, j: (i, 0)),
            pl.BlockSpec((tm, CONV_CHANNELS), lambda i, j: (i, 0)),
            pl.BlockSpec((ATTN_WIDTH, tn), lambda i, j: (0, j)),
            pl.BlockSpec((CONV_CHANNELS, tn), lambda i, j: (0, j)),
        ],
        out_specs=pl.BlockSpec((tm, tn), lambda i, j: (i, j)),
        out_shape=jax.ShapeDtypeStruct((n, D_MODEL), F32),
        compiler_params=_params("arbitrary", "arbitrary"),
        name="out_projection",
    )(x2, attn2, conv2, wa, wc)


def _ffn_up_body(x_ref, xh_ref, g_ref, wa_ref, wg_ref, cwa_ref, cwg_ref, cba_ref, cbg_ref, o_ref,
                 h_ref, ua_ref, ug_ref, *, tm, tiles_per_seq):
    i = pl.program_id(0)

    @pl.when(pl.program_id(1) == 0)
    def _():
        h_ref[FFN_HALO:, :] = _rms_norm_rows(x_ref[...], g_ref[...]).astype(BF16)
        hh = _rms_norm_rows(xh_ref[...], g_ref[...])
        h_ref[0:FFN_HALO, :] = jnp.where(i % tiles_per_seq != 0, hh, 0.0).astype(BF16)

    def project(c):
        hc = h_ref[c * FFN_ROWS:(c + 1) * FFN_ROWS + FFN_HALO, :]
        ua_ref[c] = jnp.dot(hc, wa_ref[...], preferred_element_type=F32)
        ug_ref[c] = jnp.dot(hc, wg_ref[...], preferred_element_type=F32)

    def conv(u_ref, c, cw_ref, cb_ref):
        u = u_ref[c]
        y = cw_ref[FFN_CONV_TAPS - 1:FFN_CONV_TAPS, :] * u[FFN_HALO:]
        for s in range(1, FFN_CONV_TAPS):
            k = FFN_CONV_TAPS - 1 - s
            y = y + cw_ref[k:k + 1, :] * pltpu.roll(u, s, axis=0)[FFN_HALO:]
        return y + cb_ref[...]

    def finish(c):
        a = conv(ua_ref, c, cwa_ref, cba_ref)
        gate = conv(ug_ref, c, cwg_ref, cbg_ref)
        o_ref[c * FFN_ROWS:(c + 1) * FFN_ROWS, :] = (gate * _sigmoid(gate) * a).astype(o_ref.dtype)

    n_chunks = tm // FFN_ROWS
    project(0)
    for c in range(n_chunks):
        if c + 1 < n_chunks:
            project(c + 1)
        finish(c)


def _ffn_up(x2, g, w_up, cw, cb, *, tm, tn, t_len):
    n = x2.shape[0]
    nj = D_FF // tn
    hpt = tm // FFN_HALO
    return pl.pallas_call(
        functools.partial(_ffn_up_body, tm=tm, tiles_per_seq=t_len // tm),
        grid=(n // tm, nj),
        in_specs=[
            pl.BlockSpec((tm, D_MODEL), lambda i, j: (i, 0)),
            pl.BlockSpec((FFN_HALO, D_MODEL), lambda i, j: (jnp.maximum(i * hpt - 1, 0), 0)),
            pl.BlockSpec((1, D_MODEL), lambda i, j: (0, 0)),
            pl.BlockSpec((D_MODEL, tn), lambda i, j: (0, j)),
            pl.BlockSpec((D_MODEL, tn), lambda i, j: (0, j + nj)),
            pl.BlockSpec((FFN_CONV_TAPS, tn), lambda i, j: (0, j)),
            pl.BlockSpec((FFN_CONV_TAPS, tn), lambda i, j: (0, j + nj)),
            pl.BlockSpec((1, tn), lambda i, j: (0, j)),
            pl.BlockSpec((1, tn), lambda i, j: (0, j + nj)),
        ],
        out_specs=pl.BlockSpec((tm, tn), lambda i, j: (i, j)),
        out_shape=jax.ShapeDtypeStruct((n, D_FF), BF16),
        scratch_shapes=[
            pltpu.VMEM((FFN_HALO + tm, D_MODEL), BF16),
            pltpu.VMEM((tm // FFN_ROWS, FFN_HALO + FFN_ROWS, tn), F32),
            pltpu.VMEM((tm // FFN_ROWS, FFN_HALO + FFN_ROWS, tn), F32),
        ],
        compiler_params=_params("arbitrary", "arbitrary"),
        name="ffn_up",
    )(x2, x2, g, w_up, w_up, cw, cw, cb, cb)


def _ffn_down_body(act_ref, w_ref, x_ref, o_ref):
    o_ref[...] = x_ref[...] + jnp.dot(act_ref[...], w_ref[...], preferred_element_type=F32)


def _ffn_down(act, w_down, x2, *, tm, tn):
    n = x2.shape[0]
    return pl.pallas_call(
        _ffn_down_body,
        grid=(n // tm, D_MODEL // tn),
        in_specs=[
            pl.BlockSpec((tm, D_FF), lambda i, j: (i, 0)),
            pl.BlockSpec((D_FF, tn), lambda i, j: (0, j)),
            pl.BlockSpec((tm, tn), lambda i, j: (i, j)),
        ],
        out_specs=pl.BlockSpec((tm, tn), lambda i, j: (i, j)),
        out_shape=jax.ShapeDtypeStruct((n, D_MODEL), F32),
        compiler_params=_params("arbitrary", "arbitrary"),
        name="ffn_down",
    )(act, w_down, x2)


def _final_norm_body(x_ref, g_ref, o_ref):
    o_ref[...] = _rms_norm_rows(x_ref[...], g_ref[...])


def _final_norm(x2, g, *, tm):
    n = x2.shape[0]
    return pl.pallas_call(
        _final_norm_body,
        grid=(n // tm,),
        in_specs=[pl.BlockSpec((tm, D_MODEL), lambda i: (i, 0)), pl.BlockSpec((1, D_MODEL), lambda i: (0, 0))],
        out_specs=pl.BlockSpec((tm, D_MODEL), lambda i: (i, 0)),
        out_shape=jax.ShapeDtypeStruct((n, D_MODEL), F32),
        compiler_params=_params("arbitrary"),
        name="final_norm",
    )(x2, g)


def _in_weight_layout(w_in_l):
    q_cols = np.arange(ATTN_WIDTH)
    kv0 = ATTN_WIDTH
    gate0 = kv0 + KV_COLS
    conv0 = gate0 + GATE_COLS
    conv_cols = conv0 + np.arange(2 * CONV_CHANNELS)
    kv_cols = []
    for br in range(N_BRANCH):
        for h in range(N_KV_HEADS):
            for kv in range(2):
                kv_cols.append(kv0 + ((br * 2 + kv) * N_KV_HEADS + h) * HEAD_DIM + np.arange(HEAD_DIM))
    perm = np.concatenate([q_cols, conv_cols] + kv_cols)
    w_main = jnp.take(w_in_l, jnp.asarray(perm, jnp.int32), axis=1).astype(BF16)
    gsrc = np.zeros((N_KV_HEADS, N_BRANCH * GQA_GROUP), np.int32)
    for h in range(N_KV_HEADS):
        for br in range(N_BRANCH):
            for g in range(GQA_GROUP):
                gsrc[h, br * GQA_GROUP + g] = gate0 + (h * GQA_GROUP + g) * N_BRANCH + br
    wg = jnp.take(w_in_l, jnp.asarray(gsrc.reshape(-1)), axis=1).reshape(D_MODEL, N_KV_HEADS, -1)
    wg = jnp.pad(wg, ((0, 0), (0, 0), (0, LANES - N_BRANCH * GQA_GROUP)))
    return w_main, wg.reshape(D_MODEL, N_KV_HEADS * LANES).astype(BF16)


def _pick(n, prefs):
    for p in prefs:
        if n % p == 0:
            return p
    return n


def kernel(x, rel_bias, mix_norm_g, w_in, cmp_pos, cmp_w1, cmp_w2, conv_w, conv_b, conv_ln_g, conv_ln_b,
           w_out, ffn_norm_g, w_up, ffn_conv_w, ffn_conv_b, w_down, final_norm_g):
    bsz, t_len, _ = x.shape
    depth = w_in.shape[0]
    n = bsz * t_len
    assert t_len % 256 == 0 and t_len // SLC_LEN <= HEAD_DIM
    tq = 256
    tm = _pick(t_len, (1024, 512, 256))
    nc = t_len // CMP_STRIDE

    band, cmpb = _bias_tiles(rel_bias, t_len, tq)
    x2 = x.reshape(n, D_MODEL)
    for l in range(depth):
        w_main, wg = _in_weight_layout(w_in[l])
        proj, gates = _in_projection(x2, mix_norm_g[l][None, :], w_main, wg, tm=tm, tn=768)
        proj3 = proj.reshape(bsz, t_len, PROJ_COLS)
        gates3 = gates.reshape(bsz, t_len, N_KV_HEADS * LANES)

        posx, w1big, w2big = _compress_weight_layout(cmp_pos[l], cmp_w1[l], cmp_w2[l])
        kvc, kvct = _compress(proj.reshape(bsz, nc, CMP_STRIDE * PROJ_COLS), posx, w1big, w2big)

        attn = _attention(proj3, kvc, kvct, band, cmpb, gates3, tq=tq)
        conv = _conformer(proj3, conv_w[l], conv_b[l][None, :], conv_ln_g[l][None, :],
                          conv_ln_b[l][None, :], tt=256)
        w_o = w_out[l].astype(BF16)
        x2 = _out_projection(x2, attn.reshape(n, ATTN_WIDTH), conv.reshape(n, CONV_CHANNELS),
                             w_o[:ATTN_WIDTH], w_o[ATTN_WIDTH:], tm=tm, tn=1024)

        act = _ffn_up(x2, ffn_norm_g[l][None, :], w_up[l].astype(BF16), ffn_conv_w[l],
                      ffn_conv_b[l][None, :], tm=tm, tn=512, t_len=t_len)
        x2 = _ffn_down(act, w_down[l].astype(BF16), x2, tm=tm, tn=512)
    x2 = _final_norm(x2, final_norm_g[None, :], tm=_pick(n, (512, 256)))
    return x2.reshape(bsz, t_len, D_MODEL)
```

```python
import functools
import math

import numpy as np
import jax
import jax.numpy as jnp
from jax import lax
from jax.experimental import pallas as pl
from jax.experimental.pallas import tpu as pltpu

F32 = jnp.float32
BF16 = jnp.bfloat16

D_MODEL = 2048
HEAD_DIM = 64
N_KV_HEADS = 4
GQA_GROUP = 4
N_Q_HEADS = N_KV_HEADS * GQA_GROUP
N_BRANCH = 3
ATTN_WIDTH = N_Q_HEADS * HEAD_DIM
CMP_LEN = 32
CMP_STRIDE = 16
SLC_LEN = 64
SLC_TOPK = 16
WINDOW = 512
CONV_CHANNELS = D_MODEL - ATTN_WIDTH
CONV_TAPS = 31
D_FF = 5632
FFN_CONV_TAPS = 3
N_BUCKETS = 32
MAX_DISTANCE = 128
NORM_EPS = 1e-6

KV_PAIR = 2 * HEAD_DIM
KV_COLS = N_BRANCH * N_KV_HEADS * KV_PAIR
GATE_COLS = N_BRANCH * N_Q_HEADS
COL_Q = 0
COL_CONV_A = ATTN_WIDTH
COL_CONV_G = COL_CONV_A + CONV_CHANNELS
COL_KV = COL_CONV_G + CONV_CHANNELS
PROJ_COLS = COL_KV + KV_COLS

MASK_VALUE = -1e30
BLOCK_PENALTY = -1e9
KEY_BIG = 1e30

LANES = 128
SUBLANES = 8
CONV_ROWS = 128
FFN_ROWS = 1024
VMEM_LIMIT = 56 * 1024 * 1024
CONV_HALO = 32
FFN_HALO = 16


def _t5_bucket_last_distance():
    n = np.arange(0, 4 * MAX_DISTANCE, dtype=np.int64)
    max_exact = N_BUCKETS // 2
    nf = np.maximum(n, 1).astype(np.float64)
    large = max_exact + np.floor(np.log(nf / max_exact) / math.log(MAX_DISTANCE / max_exact)
                                 * (N_BUCKETS - max_exact)).astype(np.int64)
    large = np.minimum(large, N_BUCKETS - 1)
    bucket = np.where(n < max_exact, n, large)
    last = []
    for b in range(N_BUCKETS - 1):
        idx = np.nonzero(bucket == b)[0]
        last.append(int(idx.max()) if idx.size else None)
    return last


_BUCKET_LAST = _t5_bucket_last_distance()


def _params(*sem):
    return pltpu.CompilerParams(dimension_semantics=sem, vmem_limit_bytes=VMEM_LIMIT)


def _sigmoid(x):
    return jax.nn.sigmoid(x)


def _rms_norm_rows(x, g):
    ms = jnp.mean(x * x, axis=-1, keepdims=True)
    return x * lax.rsqrt(ms + NORM_EPS) * g


def _bias_from_distance(dist, tab_ref, head):
    c_far = tab_ref[N_BUCKETS - 1, head]
    val = jnp.zeros(dist.shape, F32)
    for b in range(N_BUCKETS - 2, -1, -1):
        if _BUCKET_LAST[b] is None:
            continue
        val = jnp.where(dist <= _BUCKET_LAST[b], tab_ref[b, head] - c_far, val)
    return val


def _band_bias_body(tab_ref, o_ref, *, tq):
    head = pl.program_id(0) * GQA_GROUP + pl.program_id(1)
    shape = (WINDOW + tq, tq)
    dist = (lax.broadcasted_iota(jnp.int32, shape, 1) + WINDOW
            - lax.broadcasted_iota(jnp.int32, shape, 0))
    val = _bias_from_distance(dist, tab_ref, head)
    val = jnp.where(dist >= 0, jnp.where(dist < WINDOW, val, MASK_VALUE), MASK_VALUE)
    o_ref[0] = val


def _cmp_bias_body(tab_ref, o_ref, *, tq, nc):
    head = pl.program_id(0) * GQA_GROUP + pl.program_id(1)
    shape = (2 * nc, tq)
    r = lax.broadcasted_iota(jnp.int32, shape, 0)
    i = lax.broadcasted_iota(jnp.int32, shape, 1)
    dist = i - (r - nc) * CMP_STRIDE - (CMP_LEN - 1)
    val = _bias_from_distance(dist, tab_ref, head)
    o_ref[0] = jnp.where(dist >= 0, val, MASK_VALUE)


def _bias_tiles(rel_bias, t_len, tq):
    nc = t_len // CMP_STRIDE
    rows = GQA_GROUP * tq
    smem = pl.BlockSpec(memory_space=pltpu.SMEM)
    band = pl.pallas_call(
        functools.partial(_band_bias_body, tq=tq),
        grid=(N_KV_HEADS, GQA_GROUP),
        in_specs=[smem],
        out_specs=pl.BlockSpec((1, WINDOW + tq, tq), lambda h, g: (h, 0, g)),
        out_shape=jax.ShapeDtypeStruct((N_KV_HEADS, WINDOW + tq, rows), F32),
        compiler_params=_params("arbitrary", "arbitrary"),
        name="band_bias",
    )(rel_bias)
    cmpb = pl.pallas_call(
        functools.partial(_cmp_bias_body, tq=tq, nc=nc),
        grid=(N_KV_HEADS, GQA_GROUP),
        in_specs=[smem],
        out_specs=pl.BlockSpec((1, 2 * nc, tq), lambda h, g: (h, 0, g)),
        out_shape=jax.ShapeDtypeStruct((N_KV_HEADS, 2 * nc, rows), F32),
        compiler_params=_params("arbitrary", "arbitrary"),
        name="cmp_bias",
    )(rel_bias)
    return band, cmpb


def _inproj_body(x_ref, g_ref, w_ref, wg_ref, o_ref, gate_ref, h_ref):
    @pl.when(pl.program_id(1) == 0)
    def _():
        h = _rms_norm_rows(x_ref[...], g_ref[...]).astype(BF16)
        h_ref[...] = h
        gate_ref[...] = _sigmoid(jnp.dot(h, wg_ref[...], preferred_element_type=F32))

    o_ref[...] = jnp.dot(h_ref[...], w_ref[...], preferred_element_type=F32).astype(o_ref.dtype)


def _in_projection(x2, g, w, wg, *, tm, tn):
    n = x2.shape[0]
    gcols = wg.shape[1]
    return pl.pallas_call(
        _inproj_body,
        grid=(n // tm, PROJ_COLS // tn),
        in_specs=[
            pl.BlockSpec((tm, D_MODEL), lambda i, j: (i, 0)),
            pl.BlockSpec((1, D_MODEL), lambda i, j: (0, 0)),
            pl.BlockSpec((D_MODEL, tn), lambda i, j: (0, j)),
            pl.BlockSpec((D_MODEL, gcols), lambda i, j: (0, 0)),
        ],
        out_specs=[
            pl.BlockSpec((tm, tn), lambda i, j: (i, j)),
            pl.BlockSpec((tm, gcols), lambda i, j: (i, 0)),
        ],
        out_shape=[
            jax.ShapeDtypeStruct((n, PROJ_COLS), BF16),
            jax.ShapeDtypeStruct((n, gcols), F32),
        ],
        scratch_shapes=[pltpu.VMEM((tm, D_MODEL), BF16)],
        compiler_params=_params("arbitrary", "arbitrary"),
        name="in_projection",
    )(x2, g, w, wg)


CMP_WIDTH = N_KV_HEADS * KV_PAIR


def _compress_body(x_ref, pos_ref, w1_ref, w2_ref, o_ref, ot_ref, top_ref, bot_ref, *, nc):
    l = pl.program_id(1)

    @pl.when(l == 0)
    def _():
        top_ref[...] = jnp.zeros_like(top_ref)
        bot_ref[...] = jnp.zeros_like(bot_ref)

    x = x_ref[0, 0].astype(F32)
    top_ref[...] += jnp.dot((x + pos_ref[0, 0]).astype(BF16), w1_ref[0, 0], preferred_element_type=F32)
    bot_ref[...] += jnp.dot((x + pos_ref[0, 1]).astype(BF16), w1_ref[0, 1], preferred_element_type=F32)

    @pl.when(l == pl.num_programs(1) - 1)
    def _():
        pre = top_ref[...] + pltpu.roll(bot_ref[...], nc - 1, axis=0)
        act = pre * _sigmoid(pre)
        out = jnp.dot(act.astype(BF16), w2_ref[...], preferred_element_type=F32)
        for h in range(N_KV_HEADS):
            kv = out[:, h * KV_PAIR:(h + 1) * KV_PAIR]
            o_ref[0, h] = kv.astype(o_ref.dtype)
            ot_ref[0, h] = kv.T.astype(ot_ref.dtype)


def _compress(tokl, posx, w1big, w2big):
    b, _, nc, _ = tokl.shape
    return pl.pallas_call(
        functools.partial(_compress_body, nc=nc),
        grid=(b, CMP_STRIDE),
        in_specs=[
            pl.BlockSpec((1, 1, nc, CMP_WIDTH), lambda i, l: (i, l, 0, 0)),
            pl.BlockSpec((1, 2, 1, CMP_WIDTH), lambda i, l: (l, 0, 0, 0)),
            pl.BlockSpec((1, 2, CMP_WIDTH, CMP_WIDTH), lambda i, l: (l, 0, 0, 0)),
            pl.BlockSpec((CMP_WIDTH, CMP_WIDTH), lambda i, l: (0, 0)),
        ],
        out_specs=[
            pl.BlockSpec((1, N_KV_HEADS, nc, KV_PAIR), lambda i, l: (i, 0, 0, 0)),
            pl.BlockSpec((1, N_KV_HEADS, KV_PAIR, nc), lambda i, l: (i, 0, 0, 0)),
        ],
        out_shape=[
            jax.ShapeDtypeStruct((b, N_KV_HEADS, nc, KV_PAIR), BF16),
            jax.ShapeDtypeStruct((b, N_KV_HEADS, KV_PAIR, nc), BF16),
        ],
        scratch_shapes=[pltpu.VMEM((nc, CMP_WIDTH), F32), pltpu.VMEM((nc, CMP_WIDTH), F32)],
        compiler_params=_params("arbitrary", "arbitrary"),
        name="compress",
    )(tokl, posx, w1big, w2big)


def _compress_weight_layout(pos, w1, w2):
    def block_diag(blocks):
        rows = []
        for h in range(N_KV_HEADS):
            for kv in range(2):
                off = (h * 2 + kv) * HEAD_DIM
                pad = [(0, 0)] * (blocks.ndim - 2) + [(off, CMP_WIDTH - HEAD_DIM - off)]
                rows.append(jnp.pad(blocks[kv], pad))
        return jnp.concatenate(rows, axis=-2).astype(BF16)

    w1r = w1.reshape(2, 2, CMP_STRIDE, HEAD_DIM, HEAD_DIM)
    w1big = block_diag(w1r.transpose(0, 2, 1, 3, 4))
    w2big = block_diag(w2)
    posr = pos.reshape(2, 2, CMP_STRIDE, HEAD_DIM).transpose(2, 1, 0, 3)
    posx = jnp.broadcast_to(posr[:, :, None], (CMP_STRIDE, 2, N_KV_HEADS, 2, HEAD_DIM))
    return posx.reshape(CMP_STRIDE, 2, 1, CMP_WIDTH), w1big, w2big


_NT = (((1,), (1,)), ((), ()))


def _attn_body(q_ref, kvc_ref, kvct_ref, kvs_ref, kvw_ref, bt_ref, bc_ref, gate_ref, o_ref,
               kp_ref, vst_ref, vwt_ref, key_ref, m_ref, acc_ref, sa_ref, sb_ref, *, tq, t_len):
    qi = pl.program_id(2)
    nc = t_len // CMP_STRIDE
    ns = t_len // SLC_LEN
    rows = GQA_GROUP * tq
    blocks_per_tile = tq // SLC_LEN

    @pl.when(qi == 0)
    def _():
        k = kvs_ref[0][:, :HEAD_DIM]
        blk = lax.broadcasted_iota(jnp.int32, (t_len, HEAD_DIM), 0) // SLC_LEN
        col = lax.broadcasted_iota(jnp.int32, (t_len, HEAD_DIM), 1)
        onehot = jnp.where(blk == col, 1.0, 0.0).astype(BF16)
        kp_ref[...] = jnp.concatenate([k, onehot], axis=1)
        row = lax.broadcasted_iota(jnp.int32, (KV_PAIR, tq), 0)
        for c in range(t_len // tq):
            st = kvs_ref[0, c * tq:(c + 1) * tq, :].astype(F32).T
            vst_ref[c] = jnp.where(row == 0, 1.0, st).astype(BF16)
            wt = kvw_ref[0, c * tq:(c + 1) * tq, :].astype(F32).T
            vwt_ref[c] = jnp.where(row == 0, 1.0, wt).astype(BF16)

    q = q_ref[0]
    q4 = jnp.concatenate([q[:, g * HEAD_DIM:(g + 1) * HEAD_DIM] for g in range(GQA_GROUP)], axis=0)
    q4 = (q4.astype(F32) * (HEAD_DIM ** -0.5)).astype(BF16)

    kvc = kvc_ref[0, 0]
    bias_row = pl.multiple_of(nc - qi * (tq // CMP_STRIDE), tq // CMP_STRIDE)
    s = (lax.dot_general(kvc[:, :HEAD_DIM], q4, _NT, preferred_element_type=F32)
         + bc_ref[0, pl.ds(bias_row, nc), :])
    m = jnp.max(s, axis=0, keepdims=True)
    p = jnp.exp(s - m)
    l = jnp.sum(p, axis=0, keepdims=True)
    pn = p * jnp.where(m > 0.5 * MASK_VALUE, 1.0 / l, 0.0)
    o_cmp = jnp.dot(kvct_ref[0, 0], pn.astype(BF16), preferred_element_type=F32)

    ps = pn[:, 0:tq] + pn[:, tq:2 * tq] + pn[:, 2 * tq:3 * tq] + pn[:, 3 * tq:4 * tq]
    sj = lax.broadcasted_iota(jnp.int32, (HEAD_DIM, nc), 0) * SLC_LEN
    ci = lax.broadcasted_iota(jnp.int32, (HEAD_DIM, nc), 1) * CMP_STRIDE
    overlap = jnp.where(ci < sj + SLC_LEN, jnp.where(ci + CMP_LEN > sj, 1.0, 0.0), 0.0).astype(BF16)
    p_hi = ps.astype(BF16)
    r_hi = ps - p_hi.astype(F32)
    p_md = r_hi.astype(BF16)
    p_lo = (r_hi - p_md.astype(F32)).astype(BF16)
    imp = (jnp.dot(overlap, p_hi, preferred_element_type=F32)
           + jnp.dot(overlap, p_md, preferred_element_type=F32)
           + jnp.dot(overlap, p_lo, preferred_element_type=F32))

    t = qi * tq + lax.broadcasted_iota(jnp.int32, (HEAD_DIM, tq), 1)
    blk = lax.broadcasted_iota(jnp.int32, (HEAD_DIM, tq), 0)
    cur = t // SLC_LEN
    key = jnp.where(blk == 0, KEY_BIG, jnp.where(blk == cur, KEY_BIG, jnp.where(blk == cur - 1, KEY_BIG, imp)))
    key = jnp.where(blk * SLC_LEN <= t, key, -KEY_BIG)
    key_ref[...] = key

    def rank_group(gi, rank):
        for jj in range(blocks_per_tile):
            j = gi * blocks_per_tile + jj
            col = key_ref[pl.ds(j, 1), :]
            gt = jnp.where(col > key, 1.0, 0.0)
            ge = jnp.where(col >= key, 1.0, 0.0)
            rank = rank + jnp.where(blk > j, ge, gt)
        return rank

    rank = lax.fori_loop(0, qi + 1, rank_group, jnp.zeros((HEAD_DIM, tq), F32))
    pen_t = jnp.where(rank < float(min(SLC_TOPK, ns)), 0.0, BLOCK_PENALTY)
    pen = jnp.concatenate([pen_t, jnp.zeros_like(pen_t)], axis=0).T[:, :HEAD_DIM].astype(BF16)
    q_sel = jnp.concatenate([q4, jnp.concatenate([pen] * GQA_GROUP, axis=0)], axis=1)
    q_win = jnp.concatenate([q4, jnp.zeros((rows, HEAD_DIM), BF16)], axis=1)

    def scores(qx, k_ref, kj, bias_off):
        start = pl.multiple_of(kj * tq, tq)
        sc = lax.dot_general(k_ref[pl.ds(start, tq), :], qx, _NT, preferred_element_type=F32)
        if bias_off is not None:
            sc = sc + bt_ref[0, bias_off:bias_off + tq, :]
        return sc

    def consume(sc, vt_ref, kj, br, st, first):
        m_tile = jnp.max(sc, axis=0, keepdims=True)
        if first:
            m_new = m_tile
            pr = jnp.exp(sc - m_new).astype(BF16)
            acc_ref[br, st] = jnp.dot(vt_ref[kj], pr, preferred_element_type=F32)
        else:
            m_old = m_ref[br, st]
            m_new = jnp.maximum(m_old, m_tile)
            alpha = jnp.exp(m_old - m_new)
            pr = jnp.exp(sc - m_new).astype(BF16)
            acc_ref[br, st] = alpha * acc_ref[br, st] + jnp.dot(vt_ref[kj], pr, preferred_element_type=F32)
        m_ref[br, st] = m_new

    def finish(br):
        m0 = m_ref[br, 0]
        m1 = m_ref[br, 1]
        m_all = jnp.maximum(m0, m1)
        acc = jnp.exp(m0 - m_all) * acc_ref[br, 0] + jnp.exp(m1 - m_all) * acc_ref[br, 1]
        return acc * (1.0 / acc[0:1, :])

    kw_ref = kvw_ref.at[0]
    win, sel = 0, 1
    def pair(a, b):
        sa = scores(a[0], a[1], a[3], a[4])
        sb = scores(b[0], b[1], b[3], b[4])
        consume(sa, a[2], a[3], a[5], a[6], a[7])
        consume(sb, b[2], b[3], b[5], b[6], b[7])

    pair((q_win, kw_ref, vwt_ref, qi, WINDOW, win, 0, True),
         (q_sel, kp_ref, vst_ref, qi, WINDOW, sel, 0, True))
    for br in (win, sel):
        m_ref[br, 1] = jnp.full((1, rows), MASK_VALUE, F32)
        acc_ref[br, 1] = jnp.zeros((KV_PAIR, rows), F32)

    @pl.when(qi >= 1)
    def _():
        pair((q_win, kw_ref, vwt_ref, qi - 1, WINDOW - tq, win, 1, False),
             (q_sel, kp_ref, vst_ref, qi - 1, WINDOW - tq, sel, 1, False))

    @pl.when(qi >= 2)
    def _():
        pair((q_win, kw_ref, vwt_ref, qi - 2, WINDOW - 2 * tq, win, 0, False),
             (q_sel, kp_ref, vst_ref, qi - 2, None, sel, 0, False))

    n_far = jnp.maximum(qi - 2, 0)

    @pl.when(n_far > 0)
    def _():
        sa_ref[...] = scores(q_sel, kp_ref, 0, None)

    def far_pair(pi, carry):
        sb_ref[...] = scores(q_sel, kp_ref, 2 * pi + 1, None)
        consume(sa_ref[...], vst_ref, 2 * pi, sel, 1, False)
        sa_ref[...] = scores(q_sel, kp_ref, jnp.minimum(2 * pi + 2, n_far - 1), None)
        consume(sb_ref[...], vst_ref, 2 * pi + 1, sel, 0, False)
        return carry

    lax.fori_loop(0, n_far // 2, far_pair, 0)

    @pl.when(n_far % 2 == 1)
    def _():
        consume(sa_ref[...], vst_ref, n_far - 1, sel, 1, False)

    o_win = finish(win)
    o_sel = finish(sel)

    gates_t = gate_ref[0].T
    outs = []
    for g in range(GQA_GROUP):
        sl = slice(g * tq, (g + 1) * tq)
        comb = (gates_t[g:g + 1, :] * o_cmp[:, sl]
                + gates_t[GQA_GROUP + g:GQA_GROUP + g + 1, :] * o_sel[:, sl]
                + gates_t[2 * GQA_GROUP + g:2 * GQA_GROUP + g + 1, :] * o_win[:, sl])
        outs.append(comb.T[:, HEAD_DIM:])
    o_ref[0] = jnp.concatenate(outs, axis=1).astype(o_ref.dtype)


def _attention(proj3, kvc, kvct, band, cmpb, gates3, *, tq):
    b, t_len, _ = proj3.shape
    assert WINDOW == 2 * tq
    nc = t_len // CMP_STRIDE
    rows = GQA_GROUP * tq
    qw = GQA_GROUP * HEAD_DIM
    slc_blk = COL_KV // KV_PAIR + N_KV_HEADS
    win_blk = COL_KV // KV_PAIR + 2 * N_KV_HEADS
    return pl.pallas_call(
        functools.partial(_attn_body, tq=tq, t_len=t_len),
        grid=(b, N_KV_HEADS, t_len // tq),
        in_specs=[
            pl.BlockSpec((1, tq, qw), lambda i, h, q: (i, q, h)),
            pl.BlockSpec((1, 1, nc, KV_PAIR), lambda i, h, q: (i, h, 0, 0)),
            pl.BlockSpec((1, 1, KV_PAIR, nc), lambda i, h, q: (i, h, 0, 0)),
            pl.BlockSpec((1, t_len, KV_PAIR), lambda i, h, q: (i, 0, slc_blk + h)),
            pl.BlockSpec((1, t_len, KV_PAIR), lambda i, h, q: (i, 0, win_blk + h)),
            pl.BlockSpec((1, WINDOW + tq, rows), lambda i, h, q: (h, 0, 0)),
            pl.BlockSpec((1, 2 * nc, rows), lambda i, h, q: (h, 0, 0)),
            pl.BlockSpec((1, tq, LANES), lambda i, h, q: (i, q, h)),
        ],
        out_specs=pl.BlockSpec((1, tq, qw), lambda i, h, q: (i, q, h)),
        out_shape=jax.ShapeDtypeStruct((b, t_len, ATTN_WIDTH), BF16),
        scratch_shapes=[
            pltpu.VMEM((t_len, KV_PAIR), BF16),
            pltpu.VMEM((t_len // tq, KV_PAIR, tq), BF16),
            pltpu.VMEM((t_len // tq, KV_PAIR, tq), BF16),
            pltpu.VMEM((HEAD_DIM, tq), F32),
            pltpu.VMEM((2, 2, 1, rows), F32),
            pltpu.VMEM((2, 2, KV_PAIR, rows), F32),
            pltpu.VMEM((tq, rows), F32),
            pltpu.VMEM((tq, rows), F32),
        ],
        compiler_params=_params("arbitrary", "arbitrary", "arbitrary"),
        name="nsa_attention",
    )(proj3, kvc, kvct, proj3, proj3, band, cmpb, gates3)


def _conformer_body(a_ref, g_ref, ah_ref, gh_ref, w_ref, b_ref, lg_ref, lb_ref, o_ref, u_ref, y_ref, *, tt):
    ti = pl.program_id(1)
    u_ref[CONV_HALO:, :] = a_ref[0].astype(F32) * _sigmoid(g_ref[0].astype(F32))
    halo = ah_ref[0].astype(F32) * _sigmoid(gh_ref[0].astype(F32))
    u_ref[0:CONV_HALO, :] = jnp.where(ti > 0, halo, 0.0)
    base = CONV_HALO - (CONV_TAPS - 1)
    ext = CONV_ROWS + CONV_HALO
    for cb in range(CONV_CHANNELS // LANES):
        cs = slice(cb * LANES, (cb + 1) * LANES)
        wblk = w_ref[:, cs]
        for rb in range(tt // CONV_ROWS):
            ublk = u_ref[rb * CONV_ROWS:rb * CONV_ROWS + ext, cs]
            acc = jnp.zeros((CONV_ROWS, LANES), F32)
            for r in range(SUBLANES):
                ur = ublk if r == 0 else pltpu.roll(ublk, ext - r, axis=0)
                for a in range(CONV_HALO // SUBLANES + 1):
                    k = SUBLANES * a + r - base
                    if 0 <= k < CONV_TAPS:
                        acc = acc + wblk[k:k + 1, :] * ur[SUBLANES * a:SUBLANES * a + CONV_ROWS]
            y_ref[rb * CONV_ROWS:(rb + 1) * CONV_ROWS, cs] = acc
    acc = y_ref[...] + b_ref[...]
    mu = jnp.mean(acc, axis=-1, keepdims=True)
    xc = acc - mu
    var = jnp.mean(xc * xc, axis=-1, keepdims=True)
    y = xc * lax.rsqrt(var + NORM_EPS) * lg_ref[...] + lb_ref[...]
    o_ref[0] = (y * _sigmoid(y)).astype(o_ref.dtype)


def _conformer(proj3, w, b, lg, lb, *, tt):
    bsz, t_len, _ = proj3.shape
    a_blk = COL_CONV_A // CONV_CHANNELS
    g_blk = COL_CONV_G // CONV_CHANNELS
    hpt = tt // CONV_HALO
    halo_idx = lambda i, t: jnp.maximum(t * hpt - 1, 0)
    vec = pl.BlockSpec((1, CONV_CHANNELS), lambda i, t: (0, 0))
    return pl.pallas_call(
        functools.partial(_conformer_body, tt=tt),
        grid=(bsz, t_len // tt),
        in_specs=[
            pl.BlockSpec((1, tt, CONV_CHANNELS), lambda i, t: (i, t, a_blk)),
            pl.BlockSpec((1, tt, CONV_CHANNELS), lambda i, t: (i, t, g_blk)),
            pl.BlockSpec((1, CONV_HALO, CONV_CHANNELS), lambda i, t: (i, halo_idx(i, t), a_blk)),
            pl.BlockSpec((1, CONV_HALO, CONV_CHANNELS), lambda i, t: (i, halo_idx(i, t), g_blk)),
            pl.BlockSpec((CONV_TAPS, CONV_CHANNELS), lambda i, t: (0, 0)),
            vec, vec, vec,
        ],
        out_specs=pl.BlockSpec((1, tt, CONV_CHANNELS), lambda i, t: (i, t, 0)),
        out_shape=jax.ShapeDtypeStruct((bsz, t_len, CONV_CHANNELS), BF16),
        scratch_shapes=[pltpu.VMEM((CONV_HALO + tt, CONV_CHANNELS), F32),
                        pltpu.VMEM((tt, CONV_CHANNELS), F32)],
        compiler_params=_params("arbitrary", "arbitrary"),
        name="conformer_conv",
    )(proj3, proj3, proj3, proj3, w, b, lg, lb)


def _outproj_body(x_ref, a_ref, c_ref, wa_ref, wc_ref, o_ref):
    o_ref[...] = (x_ref[...]
                  + jnp.dot(a_ref[...], wa_ref[...], preferred_element_type=F32)
                  + jnp.dot(c_ref[...], wc_ref[...], preferred_element_type=F32))


def _out_projection(x2, attn2, conv2, wa, wc, *, tm, tn):
    n = x2.shape[0]
    return pl.pallas_call(
        _outproj_body,
        grid=(n // tm, D_MODEL // tn),
        in_specs=[
            pl.BlockSpec((tm, tn), lambda i, j: (i, j)),
            pl.BlockSpec((tm, ATTN_WIDTH), lambda i, j: (i, 0)),
            pl.BlockSpec((tm, CONV_CHANNELS), lambda i, j: (i, 0)),
            pl.BlockSpec((ATTN_WIDTH, tn), lambda i, j: (0, j)),
            pl.BlockSpec((CONV_CHANNELS, tn), lambda i, j: (0, j)),
        ],
        out_specs=pl.BlockSpec((tm, tn), lambda i, j: (i, j)),
        out_shape=jax.ShapeDtypeStruct((n, D_MODEL), F32),
        compiler_params=_params("arbitrary", "arbitrary"),
        name="out_projection",
    )(x2, attn2, conv2, wa, wc)


def _ffn_up_body(x_ref, xh_ref, g_ref, wa_ref, wg_ref, cwa_ref, cwg_ref, cba_ref, cbg_ref, o_ref,
                 h_ref, ua_ref, ug_ref, *, tm, tiles_per_seq, rows):
    i = pl.program_id(0)

    @pl.when(pl.program_id(1) == 0)
    def _():
        h_ref[FFN_HALO:, :] = _rms_norm_rows(x_ref[...], g_ref[...]).astype(BF16)
        hh = _rms_norm_rows(xh_ref[...], g_ref[...])
        h_ref[0:FFN_HALO, :] = jnp.where(i % tiles_per_seq != 0, hh, 0.0).astype(BF16)

    def project(c):
        hc = h_ref[c * rows:(c + 1) * rows + FFN_HALO, :]
        ua_ref[c] = jnp.dot(hc, wa_ref[...], preferred_element_type=F32)
        ug_ref[c] = jnp.dot(hc, wg_ref[...], preferred_element_type=F32)

    def conv(u_ref, c, cw_ref, cb_ref):
        u = u_ref[c]
        y = cw_ref[FFN_CONV_TAPS - 1:FFN_CONV_TAPS, :] * u[FFN_HALO:]
        for s in range(1, FFN_CONV_TAPS):
            k = FFN_CONV_TAPS - 1 - s
            y = y + cw_ref[k:k + 1, :] * pltpu.roll(u, s, axis=0)[FFN_HALO:]
        return y + cb_ref[...]

    def finish(c):
        a = conv(ua_ref, c, cwa_ref, cba_ref)
        gate = conv(ug_ref, c, cwg_ref, cbg_ref)
        o_ref[c * rows:(c + 1) * rows, :] = (gate * _sigmoid(gate) * a).astype(o_ref.dtype)

    n_chunks = tm // rows
    project(0)
    for c in range(n_chunks):
        if c + 1 < n_chunks:
            project(c + 1)
        finish(c)


def _ffn_up(x2, g, w_up, cw, cb, *, tm, tn, t_len):
    n = x2.shape[0]
    nj = D_FF // tn
    hpt = tm // FFN_HALO
    rows = min(FFN_ROWS, tm)
    return pl.pallas_call(
        functools.partial(_ffn_up_body, tm=tm, tiles_per_seq=t_len // tm, rows=rows),
        grid=(n // tm, nj),
        in_specs=[
            pl.BlockSpec((tm, D_MODEL), lambda i, j: (i, 0)),
            pl.BlockSpec((FFN_HALO, D_MODEL), lambda i, j: (jnp.maximum(i * hpt - 1, 0), 0)),
            pl.BlockSpec((1, D_MODEL), lambda i, j: (0, 0)),
            pl.BlockSpec((D_MODEL, tn), lambda i, j: (0, j)),
            pl.BlockSpec((D_MODEL, tn), lambda i, j: (0, j + nj)),
            pl.BlockSpec((FFN_CONV_TAPS, tn), lambda i, j: (0, j)),
            pl.BlockSpec((FFN_CONV_TAPS, tn), lambda i, j: (0, j + nj)),
            pl.BlockSpec((1, tn), lambda i, j: (0, j)),
            pl.BlockSpec((1, tn), lambda i, j: (0, j + nj)),
        ],
        out_specs=pl.BlockSpec((tm, tn), lambda i, j: (i, j)),
        out_shape=jax.ShapeDtypeStruct((n, D_FF), BF16),
        scratch_shapes=[
            pltpu.VMEM((FFN_HALO + tm, D_MODEL), BF16),
            pltpu.VMEM((tm // rows, FFN_HALO + rows, tn), F32),
            pltpu.VMEM((tm // rows, FFN_HALO + rows, tn), F32),
        ],
        compiler_params=_params("arbitrary", "arbitrary"),
        name="ffn_up",
    )(x2, x2, g, w_up, w_up, cw, cw, cb, cb)


def _ffn_down_body(act_ref, w_ref, x_ref, o_ref):
    o_ref[...] = x_ref[...] + jnp.dot(act_ref[...], w_ref[...], preferred_element_type=F32)


def _ffn_down(act, w_down, x2, *, tm, tn):
    n = x2.shape[0]
    return pl.pallas_call(
        _ffn_down_body,
        grid=(n // tm, D_MODEL // tn),
        in_specs=[
            pl.BlockSpec((tm, D_FF), lambda i, j: (i, 0)),
            pl.BlockSpec((D_FF, tn), lambda i, j: (0, j)),
            pl.BlockSpec((tm, tn), lambda i, j: (i, j)),
        ],
        out_specs=pl.BlockSpec((tm, tn), lambda i, j: (i, j)),
        out_shape=jax.ShapeDtypeStruct((n, D_MODEL), F32),
        compiler_params=_params("arbitrary", "arbitrary"),
        name="ffn_down",
    )(act, w_down, x2)


def _final_norm_body(x_ref, g_ref, o_ref):
    o_ref[...] = _rms_norm_rows(x_ref[...], g_ref[...])


def _final_norm(x2, g, *, tm):
    n = x2.shape[0]
    return pl.pallas_call(
        _final_norm_body,
        grid=(n // tm,),
        in_specs=[pl.BlockSpec((tm, D_MODEL), lambda i: (i, 0)), pl.BlockSpec((1, D_MODEL), lambda i: (0, 0))],
        out_specs=pl.BlockSpec((tm, D_MODEL), lambda i: (i, 0)),
        out_shape=jax.ShapeDtypeStruct((n, D_MODEL), F32),
        compiler_params=_params("arbitrary"),
        name="final_norm",
    )(x2, g)


def _in_weight_layout(w_in_l):
    q_cols = np.arange(ATTN_WIDTH)
    kv0 = ATTN_WIDTH
    gate0 = kv0 + KV_COLS
    conv0 = gate0 + GATE_COLS
    conv_cols = conv0 + np.arange(2 * CONV_CHANNELS)
    kv_cols = []
    for br in range(N_BRANCH):
        for h in range(N_KV_HEADS):
            for kv in range(2):
                kv_cols.append(kv0 + ((br * 2 + kv) * N_KV_HEADS + h) * HEAD_DIM + np.arange(HEAD_DIM))
    perm = np.concatenate([q_cols, conv_cols] + kv_cols)
    w_main = jnp.take(w_in_l, jnp.asarray(perm, jnp.int32), axis=1).astype(BF16)
    gsrc = np.zeros((N_KV_HEADS, N_BRANCH * GQA_GROUP), np.int32)
    for h in range(N_KV_HEADS):
        for br in range(N_BRANCH):
            for g in range(GQA_GROUP):
                gsrc[h, br * GQA_GROUP + g] = gate0 + (h * GQA_GROUP + g) * N_BRANCH + br
    wg = jnp.take(w_in_l, jnp.asarray(gsrc.reshape(-1)), axis=1).reshape(D_MODEL, N_KV_HEADS, -1)
    wg = jnp.pad(wg, ((0, 0), (0, 0), (0, LANES - N_BRANCH * GQA_GROUP)))
    return w_main, wg.reshape(D_MODEL, N_KV_HEADS * LANES).astype(BF16)


def _pick(n, prefs):
    for p in prefs:
        if n % p == 0:
            return p
    return n


def kernel(x, rel_bias, mix_norm_g, w_in, cmp_pos, cmp_w1, cmp_w2, conv_w, conv_b, conv_ln_g, conv_ln_b,
           w_out, ffn_norm_g, w_up, ffn_conv_w, ffn_conv_b, w_down, final_norm_g):
    bsz, t_len, _ = x.shape
    depth = w_in.shape[0]
    n = bsz * t_len
    assert t_len % 256 == 0 and t_len // SLC_LEN <= HEAD_DIM
    tq = 256
    tm = _pick(t_len, (1024, 512, 256))
    nc = t_len // CMP_STRIDE

    band, cmpb = _bias_tiles(rel_bias, t_len, tq)
    x2 = x.reshape(n, D_MODEL)
    for l in range(depth):
        w_main, wg = _in_weight_layout(w_in[l])
        proj, gates = _in_projection(x2, mix_norm_g[l][None, :], w_main, wg, tm=tm, tn=768)
        proj3 = proj.reshape(bsz, t_len, PROJ_COLS)
        gates3 = gates.reshape(bsz, t_len, N_KV_HEADS * LANES)

        posx, w1big, w2big = _compress_weight_layout(cmp_pos[l], cmp_w1[l], cmp_w2[l])
        tokl = proj3[:, :, COL_KV:COL_KV + CMP_WIDTH].reshape(bsz, nc, CMP_STRIDE, CMP_WIDTH).transpose(0, 2, 1, 3)
        kvc, kvct = _compress(tokl, posx, w1big, w2big)

        attn = _attention(proj3, kvc, kvct, band, cmpb, gates3, tq=tq)
        conv = _conformer(proj3, conv_w[l], conv_b[l][None, :], conv_ln_g[l][None, :],
                          conv_ln_b[l][None, :], tt=256)
        w_o = w_out[l].astype(BF16)
        x2 = _out_projection(x2, attn.reshape(n, ATTN_WIDTH), conv.reshape(n, CONV_CHANNELS),
                             w_o[:ATTN_WIDTH], w_o[ATTN_WIDTH:], tm=tm, tn=1024)

        act = _ffn_up(x2, ffn_norm_g[l][None, :], w_up[l].astype(BF16), ffn_conv_w[l],
                      ffn_conv_b[l][None, :], tm=tm, tn=512, t_len=t_len)
        x2 = _ffn_down(act, w_down[l].astype(BF16), x2, tm=tm, tn=512)
    x2 = _final_norm(x2, final_norm_g[None, :], tm=_pick(n, (512, 256)))
    return x2.reshape(bsz, t_len, D_MODEL)
```

```python
import functools
import math

import numpy as np
import jax
import jax.numpy as jnp
from jax import lax
from jax.experimental import pallas as pl
from jax.experimental.pallas import tpu as pltpu

F32 = jnp.float32
BF16 = jnp.bfloat16

D_MODEL = 2048
HEAD_DIM = 64
N_KV_HEADS = 4
GQA_GROUP = 4
N_Q_HEADS = N_KV_HEADS * GQA_GROUP
N_BRANCH = 3
ATTN_WIDTH = N_Q_HEADS * HEAD_DIM
CMP_LEN = 32
CMP_STRIDE = 16
SLC_LEN = 64
SLC_TOPK = 16
WINDOW = 512
CONV_CHANNELS = D_MODEL - ATTN_WIDTH
CONV_TAPS = 31
D_FF = 5632
FFN_CONV_TAPS = 3
N_BUCKETS = 32
MAX_DISTANCE = 128
NORM_EPS = 1e-6

KV_PAIR = 2 * HEAD_DIM
KV_COLS = N_BRANCH * N_KV_HEADS * KV_PAIR
GATE_COLS = N_BRANCH * N_Q_HEADS
COL_Q = 0
COL_CONV_A = ATTN_WIDTH
COL_CONV_G = COL_CONV_A + CONV_CHANNELS
COL_KV = COL_CONV_G + CONV_CHANNELS
PROJ_COLS = COL_KV + KV_COLS

MASK_VALUE = -1e30
BLOCK_PENALTY = -1e9
KEY_BIG = 1e30
LOG2E = math.log2(math.e)
Q_SCALE = HEAD_DIM ** -0.5 * LOG2E

LANES = 128
SUBLANES = 8
CONV_ROWS = 128
FFN_ROWS = 1024
VMEM_LIMIT = 56 * 1024 * 1024
CONV_HALO = 32
FFN_HALO = 16


def _t5_bucket_last_distance():
    n = np.arange(0, 4 * MAX_DISTANCE, dtype=np.int64)
    max_exact = N_BUCKETS // 2
    nf = np.maximum(n, 1).astype(np.float64)
    large = max_exact + np.floor(np.log(nf / max_exact) / math.log(MAX_DISTANCE / max_exact)
                                 * (N_BUCKETS - max_exact)).astype(np.int64)
    large = np.minimum(large, N_BUCKETS - 1)
    bucket = np.where(n < max_exact, n, large)
    last = []
    for b in range(N_BUCKETS - 1):
        idx = np.nonzero(bucket == b)[0]
        last.append(int(idx.max()) if idx.size else None)
    return last


_BUCKET_LAST = _t5_bucket_last_distance()


def _params(*sem):
    return pltpu.CompilerParams(dimension_semantics=sem, vmem_limit_bytes=VMEM_LIMIT)


def _sigmoid(x):
    return jax.nn.sigmoid(x)


def _rms_norm_rows(x, g):
    ms = jnp.mean(x * x, axis=-1, keepdims=True)
    return x * lax.rsqrt(ms + NORM_EPS) * g


def _bias_from_distance(dist, tab_ref, head):
    c_far = tab_ref[N_BUCKETS - 1, head]
    val = jnp.zeros(dist.shape, F32)
    for b in range(N_BUCKETS - 2, -1, -1):
        if _BUCKET_LAST[b] is None:
            continue
        val = jnp.where(dist <= _BUCKET_LAST[b], tab_ref[b, head] - c_far, val)
    return val


def _band_bias_body(tab_ref, o_ref, *, tq):
    head = pl.program_id(0) * GQA_GROUP + pl.program_id(1)
    shape = (WINDOW + tq, tq)
    dist = (lax.broadcasted_iota(jnp.int32, shape, 1) + WINDOW
            - lax.broadcasted_iota(jnp.int32, shape, 0))
    val = _bias_from_distance(dist, tab_ref, head) * LOG2E
    val = jnp.where(dist >= 0, jnp.where(dist < WINDOW, val, MASK_VALUE), MASK_VALUE)
    o_ref[0] = val


def _cmp_bias_body(tab_ref, o_ref, *, tq, nc):
    head = pl.program_id(0) * GQA_GROUP + pl.program_id(1)
    shape = (2 * nc, tq)
    r = lax.broadcasted_iota(jnp.int32, shape, 0)
    i = lax.broadcasted_iota(jnp.int32, shape, 1)
    dist = i - (r - nc) * CMP_STRIDE - (CMP_LEN - 1)
    val = _bias_from_distance(dist, tab_ref, head) * LOG2E
    o_ref[0] = jnp.where(dist >= 0, val, MASK_VALUE)


def _bias_tiles(rel_bias, t_len, tq):
    nc = t_len // CMP_STRIDE
    rows = GQA_GROUP * tq
    smem = pl.BlockSpec(memory_space=pltpu.SMEM)
    band = pl.pallas_call(
        functools.partial(_band_bias_body, tq=tq),
        grid=(N_KV_HEADS, GQA_GROUP),
        in_specs=[smem],
        out_specs=pl.BlockSpec((1, WINDOW + tq, tq), lambda h, g: (h, 0, g)),
        out_shape=jax.ShapeDtypeStruct((N_KV_HEADS, WINDOW + tq, rows), F32),
        compiler_params=_params("arbitrary", "arbitrary"),
        name="band_bias",
    )(rel_bias)
    cmpb = pl.pallas_call(
        functools.partial(_cmp_bias_body, tq=tq, nc=nc),
        grid=(N_KV_HEADS, GQA_GROUP),
        in_specs=[smem],
        out_specs=pl.BlockSpec((1, 2 * nc, tq), lambda h, g: (h, 0, g)),
        out_shape=jax.ShapeDtypeStruct((N_KV_HEADS, 2 * nc, rows), F32),
        compiler_params=_params("arbitrary", "arbitrary"),
        name="cmp_bias",
    )(rel_bias)
    return band, cmpb


def _inproj_body(x_ref, g_ref, w_ref, wg_ref, cs_ref, o_ref, gate_ref, h_ref):
    @pl.when(pl.program_id(1) == 0)
    def _():
        h = _rms_norm_rows(x_ref[...], g_ref[...]).astype(BF16)
        h_ref[...] = h
        gate_ref[...] = _sigmoid(jnp.dot(h, wg_ref[...], preferred_element_type=F32))

    acc = jnp.dot(h_ref[...], w_ref[...], preferred_element_type=F32)
    o_ref[...] = (acc * cs_ref[...]).astype(o_ref.dtype)


def _in_projection(x2, g, w, wg, col_scale, *, tm, tn):
    n = x2.shape[0]
    gcols = wg.shape[1]
    return pl.pallas_call(
        _inproj_body,
        grid=(n // tm, PROJ_COLS // tn),
        in_specs=[
            pl.BlockSpec((tm, D_MODEL), lambda i, j: (i, 0)),
            pl.BlockSpec((1, D_MODEL), lambda i, j: (0, 0)),
            pl.BlockSpec((D_MODEL, tn), lambda i, j: (0, j)),
            pl.BlockSpec((D_MODEL, gcols), lambda i, j: (0, 0)),
            pl.BlockSpec((1, tn), lambda i, j: (0, j)),
        ],
        out_specs=[
            pl.BlockSpec((tm, tn), lambda i, j: (i, j)),
            pl.BlockSpec((tm, gcols), lambda i, j: (i, 0)),
        ],
        out_shape=[
            jax.ShapeDtypeStruct((n, PROJ_COLS), BF16),
            jax.ShapeDtypeStruct((n, gcols), F32),
        ],
        scratch_shapes=[pltpu.VMEM((tm, D_MODEL), BF16)],
        compiler_params=_params("arbitrary", "arbitrary"),
        name="in_projection",
    )(x2, g, w, wg, col_scale)


CMP_WIDTH = N_KV_HEADS * KV_PAIR


def _compress_body(x_ref, pos_ref, w1_ref, w2_ref, o_ref, ot_ref, top_ref, bot_ref, *, nc):
    l = pl.program_id(1)

    @pl.when(l == 0)
    def _():
        top_ref[...] = jnp.zeros_like(top_ref)
        bot_ref[...] = jnp.zeros_like(bot_ref)

    x = x_ref[0, 0].astype(F32)
    top_ref[...] += jnp.dot((x + pos_ref[0, 0]).astype(BF16), w1_ref[0, 0], preferred_element_type=F32)
    bot_ref[...] += jnp.dot((x + pos_ref[0, 1]).astype(BF16), w1_ref[0, 1], preferred_element_type=F32)

    @pl.when(l == pl.num_programs(1) - 1)
    def _():
        pre = top_ref[...] + pltpu.roll(bot_ref[...], nc - 1, axis=0)
        act = pre * _sigmoid(pre)
        out = jnp.dot(act.astype(BF16), w2_ref[...], preferred_element_type=F32)
        for h in range(N_KV_HEADS):
            kv = out[:, h * KV_PAIR:(h + 1) * KV_PAIR]
            o_ref[0, h] = kv.astype(o_ref.dtype)
            ot_ref[0, h] = kv.T.astype(ot_ref.dtype)


def _compress(tokl, posx, w1big, w2big):
    b, _, nc, _ = tokl.shape
    return pl.pallas_call(
        functools.partial(_compress_body, nc=nc),
        grid=(b, CMP_STRIDE),
        in_specs=[
            pl.BlockSpec((1, 1, nc, CMP_WIDTH), lambda i, l: (i, l, 0, 0)),
            pl.BlockSpec((1, 2, 1, CMP_WIDTH), lambda i, l: (l, 0, 0, 0)),
            pl.BlockSpec((1, 2, CMP_WIDTH, CMP_WIDTH), lambda i, l: (l, 0, 0, 0)),
            pl.BlockSpec((CMP_WIDTH, CMP_WIDTH), lambda i, l: (0, 0)),
        ],
        out_specs=[
            pl.BlockSpec((1, N_KV_HEADS, nc, KV_PAIR), lambda i, l: (i, 0, 0, 0)),
            pl.BlockSpec((1, N_KV_HEADS, KV_PAIR, nc), lambda i, l: (i, 0, 0, 0)),
        ],
        out_shape=[
            jax.ShapeDtypeStruct((b, N_KV_HEADS, nc, KV_PAIR), BF16),
            jax.ShapeDtypeStruct((b, N_KV_HEADS, KV_PAIR, nc), BF16),
        ],
        scratch_shapes=[pltpu.VMEM((nc, CMP_WIDTH), F32), pltpu.VMEM((nc, CMP_WIDTH), F32)],
        compiler_params=_params("arbitrary", "arbitrary"),
        name="compress",
    )(tokl, posx, w1big, w2big)


def _compress_weight_layout(pos, w1, w2):
    def block_diag(blocks):
        rows = []
        for h in range(N_KV_HEADS):
            for kv in range(2):
                off = (h * 2 + kv) * HEAD_DIM
                pad = [(0, 0)] * (blocks.ndim - 2) + [(off, CMP_WIDTH - HEAD_DIM - off)]
                rows.append(jnp.pad(blocks[kv], pad))
        return jnp.concatenate(rows, axis=-2).astype(BF16)

    w1r = w1.reshape(2, 2, CMP_STRIDE, HEAD_DIM, HEAD_DIM)
    w1big = block_diag(w1r.transpose(0, 2, 1, 3, 4))
    w2big = block_diag(w2)
    posr = pos.reshape(2, 2, CMP_STRIDE, HEAD_DIM).transpose(2, 1, 0, 3)
    posx = jnp.broadcast_to(posr[:, :, None], (CMP_STRIDE, 2, N_KV_HEADS, 2, HEAD_DIM))
    return posx.reshape(CMP_STRIDE, 2, 1, CMP_WIDTH), w1big, w2big


_NT = (((1,), (1,)), ((), ()))


def _attn_body(q_ref, kvc_ref, kvct_ref, kvs_ref, kvw_ref, bt_ref, bc_ref, gate_ref, o_ref,
               kp_ref, vst_ref, vwt_ref, key_ref, m_ref, acc_ref, sa_ref, sb_ref, ocmp_ref, qsel_ref, qwin_ref,
               *, tq, t_len):
    qi = pl.program_id(2)
    nc = t_len // CMP_STRIDE
    ns = t_len // SLC_LEN
    rows = GQA_GROUP * tq
    blocks_per_tile = tq // SLC_LEN

    @pl.when(qi == 0)
    def _():
        k = kvs_ref[0][:, :HEAD_DIM]
        blk = lax.broadcasted_iota(jnp.int32, (t_len, HEAD_DIM), 0) // SLC_LEN
        col = lax.broadcasted_iota(jnp.int32, (t_len, HEAD_DIM), 1)
        onehot = jnp.where(blk == col, 1.0, 0.0).astype(BF16)
        kp_ref[...] = jnp.concatenate([k, onehot], axis=1)
        row = lax.broadcasted_iota(jnp.int32, (KV_PAIR, tq), 0)
        for c in range(t_len // tq):
            st = kvs_ref[0, c * tq:(c + 1) * tq, :].astype(F32).T
            vst_ref[c] = jnp.where(row == 0, 1.0, st).astype(BF16)
            wt = kvw_ref[0, c * tq:(c + 1) * tq, :].astype(F32).T
            vwt_ref[c] = jnp.where(row == 0, 1.0, wt).astype(BF16)

    q = q_ref[0]
    q4 = jnp.concatenate([q[:, g * HEAD_DIM:(g + 1) * HEAD_DIM] for g in range(GQA_GROUP)], axis=0)
    qwin_ref[...] = jnp.concatenate([q4, jnp.zeros((rows, HEAD_DIM), BF16)], axis=1)
    kw_ref = kvw_ref.at[0]
    win, sel = 0, 1

    def cmp_scores():
        bias_row = pl.multiple_of(nc - qi * (tq // CMP_STRIDE), tq // CMP_STRIDE)
        return (lax.dot_general(kvc_ref[0, 0][:, :HEAD_DIM], q4, _NT, preferred_element_type=F32)
                + bc_ref[0, pl.ds(bias_row, nc), :])

    def cmp_finish(s):
        m = jnp.max(s, axis=0, keepdims=True)
        p = jnp.exp2(s - m)
        l = jnp.sum(p, axis=0, keepdims=True)
        pn = p * jnp.where(m > 0.5 * MASK_VALUE, 1.0 / l, 0.0)
        ocmp_ref[...] = jnp.dot(kvct_ref[0, 0], pn.astype(BF16), preferred_element_type=F32)

        ps = pn[:, 0:tq] + pn[:, tq:2 * tq] + pn[:, 2 * tq:3 * tq] + pn[:, 3 * tq:4 * tq]
        sj = lax.broadcasted_iota(jnp.int32, (HEAD_DIM, nc), 0) * SLC_LEN
        ci = lax.broadcasted_iota(jnp.int32, (HEAD_DIM, nc), 1) * CMP_STRIDE
        overlap = jnp.where(ci < sj + SLC_LEN, jnp.where(ci + CMP_LEN > sj, 1.0, 0.0), 0.0).astype(BF16)
        p_hi = ps.astype(BF16)
        r_hi = ps - p_hi.astype(F32)
        p_md = r_hi.astype(BF16)
        p_lo = (r_hi - p_md.astype(F32)).astype(BF16)
        imp = (jnp.dot(overlap, p_hi, preferred_element_type=F32)
               + jnp.dot(overlap, p_md, preferred_element_type=F32)
               + jnp.dot(overlap, p_lo, preferred_element_type=F32))
        t = qi * tq + lax.broadcasted_iota(jnp.int32, (HEAD_DIM, tq), 1)
        blk = lax.broadcasted_iota(jnp.int32, (HEAD_DIM, tq), 0)
        cur = t // SLC_LEN
        key = jnp.where(blk == 0, KEY_BIG, jnp.where(blk == cur, KEY_BIG, jnp.where(blk == cur - 1, KEY_BIG, imp)))
        key_ref[...] = jnp.where(blk * SLC_LEN <= t, key, -KEY_BIG)

    def select_blocks():
        key = key_ref[...]
        blk = lax.broadcasted_iota(jnp.int32, (HEAD_DIM, tq), 0)

        def rank_group(gi, rank):
            for jj in range(blocks_per_tile):
                j = gi * blocks_per_tile + jj
                col = key_ref[pl.ds(j, 1), :]
                gt = jnp.where(col > key, 1.0, 0.0)
                ge = jnp.where(col >= key, 1.0, 0.0)
                rank = rank + jnp.where(blk > j, ge, gt)
            return rank

        rank = lax.fori_loop(0, qi + 1, rank_group, jnp.zeros((HEAD_DIM, tq), F32))
        pen_t = jnp.where(rank < float(min(SLC_TOPK, ns)), 0.0, BLOCK_PENALTY)
        pen = jnp.concatenate([pen_t, jnp.zeros_like(pen_t)], axis=0).T[:, :HEAD_DIM].astype(BF16)
        qsel_ref[...] = jnp.concatenate([q4, jnp.concatenate([pen] * GQA_GROUP, axis=0)], axis=1)

    def scores(qx_ref, k_ref, kj, bias_off):
        start = pl.multiple_of(kj * tq, tq)
        sc = lax.dot_general(k_ref[pl.ds(start, tq), :], qx_ref[...], _NT,
                             preferred_element_type=F32)
        if bias_off is not None:
            sc = sc + bt_ref[0, bias_off:bias_off + tq, :]
        return sc

    def consume(sc, vt_ref, kj, br, st, first):
        m_tile = jnp.max(sc, axis=0, keepdims=True)
        if first:
            m_new = m_tile
            pr = jnp.exp2(sc - m_new).astype(BF16)
            acc_ref[br, st] = jnp.dot(vt_ref[kj], pr, preferred_element_type=F32)
        else:
            m_old = m_ref[br, st]
            m_new = jnp.maximum(m_old, m_tile)
            alpha = jnp.exp2(m_old - m_new)
            pr = jnp.exp2(sc - m_new).astype(BF16)
            acc_ref[br, st] = alpha * acc_ref[br, st] + jnp.dot(vt_ref[kj], pr, preferred_element_type=F32)
        m_ref[br, st] = m_new

    def finish(br):
        m0 = m_ref[br, 0]
        m1 = m_ref[br, 1]
        m_all = jnp.maximum(m0, m1)
        acc = jnp.exp2(m0 - m_all) * acc_ref[br, 0] + jnp.exp2(m1 - m_all) * acc_ref[br, 1]
        return acc * (1.0 / acc[0:1, :])

    def combine():
        o_win = finish(win)
        o_sel = finish(sel)
        o_cmp = ocmp_ref[...]
        gates_t = gate_ref[0].T
        outs = []
        for g in range(GQA_GROUP):
            sl = slice(g * tq, (g + 1) * tq)
            comb = (gates_t[g:g + 1, :] * o_cmp[:, sl]
                    + gates_t[GQA_GROUP + g:GQA_GROUP + g + 1, :] * o_sel[:, sl]
                    + gates_t[2 * GQA_GROUP + g:2 * GQA_GROUP + g + 1, :] * o_win[:, sl])
            outs.append(comb.T[:, HEAD_DIM:])
        o_ref[0] = jnp.concatenate(outs, axis=1).astype(o_ref.dtype)

    @pl.when(qi >= 2)
    def _():
        sc_cmp = cmp_scores()
        sw0 = scores(qwin_ref, kw_ref, qi, WINDOW)
        cmp_finish(sc_cmp)
        sw1 = scores(qwin_ref, kw_ref, qi - 1, WINDOW - tq)
        consume(sw0, vwt_ref, qi, win, 0, True)
        sw2 = scores(qwin_ref, kw_ref, qi - 2, WINDOW - 2 * tq)
        consume(sw1, vwt_ref, qi - 1, win, 1, True)
        consume(sw2, vwt_ref, qi - 2, win, 0, False)
        select_blocks()

        ss0 = scores(qsel_ref, kp_ref, qi, WINDOW)
        ss1 = scores(qsel_ref, kp_ref, qi - 1, WINDOW - tq)
        consume(ss0, vst_ref, qi, sel, 0, True)
        sa_ref[...] = scores(qsel_ref, kp_ref, 0, None)
        consume(ss1, vst_ref, qi - 1, sel, 1, True)
        n_far = qi - 1

        def far_pair(pi, carry):
            sb_ref[...] = scores(qsel_ref, kp_ref, 2 * pi + 1, None)
            consume(sa_ref[...], vst_ref, 2 * pi, sel, 1, False)
            sa_ref[...] = scores(qsel_ref, kp_ref, jnp.minimum(2 * pi + 2, n_far - 1), None)
            consume(sb_ref[...], vst_ref, 2 * pi + 1, sel, 0, False)
            return carry

        lax.fori_loop(0, n_far // 2, far_pair, 0)

        @pl.when(n_far % 2 == 1)
        def _():
            consume(sa_ref[...], vst_ref, n_far - 1, sel, 1, False)

        combine()

    @pl.when(qi < 2)
    def _():
        cmp_finish(cmp_scores())
        select_blocks()
        sw0 = scores(qwin_ref, kw_ref, qi, WINDOW)
        ss0 = scores(qsel_ref, kp_ref, qi, WINDOW)
        consume(sw0, vwt_ref, qi, win, 0, True)
        consume(ss0, vst_ref, qi, sel, 0, True)
        for br in (win, sel):
            m_ref[br, 1] = jnp.full((1, rows), MASK_VALUE, F32)
            acc_ref[br, 1] = jnp.zeros((KV_PAIR, rows), F32)

        @pl.when(qi == 1)
        def _():
            sw1 = scores(qwin_ref, kw_ref, 0, WINDOW - tq)
            ss1 = scores(qsel_ref, kp_ref, 0, WINDOW - tq)
            consume(sw1, vwt_ref, 0, win, 1, False)
            consume(ss1, vst_ref, 0, sel, 1, False)

        combine()


def _attention(proj3, kvc, kvct, band, cmpb, gates3, *, tq):
    b, t_len, _ = proj3.shape
    assert WINDOW == 2 * tq
    nc = t_len // CMP_STRIDE
    rows = GQA_GROUP * tq
    qw = GQA_GROUP * HEAD_DIM
    slc_blk = COL_KV // KV_PAIR + N_KV_HEADS
    win_blk = COL_KV // KV_PAIR + 2 * N_KV_HEADS
    return pl.pallas_call(
        functools.partial(_attn_body, tq=tq, t_len=t_len),
        grid=(b, N_KV_HEADS, t_len // tq),
        in_specs=[
            pl.BlockSpec((1, tq, qw), lambda i, h, q: (i, q, h)),
            pl.BlockSpec((1, 1, nc, KV_PAIR), lambda i, h, q: (i, h, 0, 0)),
            pl.BlockSpec((1, 1, KV_PAIR, nc), lambda i, h, q: (i, h, 0, 0)),
            pl.BlockSpec((1, t_len, KV_PAIR), lambda i, h, q: (i, 0, slc_blk + h)),
            pl.BlockSpec((1, t_len, KV_PAIR), lambda i, h, q: (i, 0, win_blk + h)),
            pl.BlockSpec((1, WINDOW + tq, rows), lambda i, h, q: (h, 0, 0)),
            pl.BlockSpec((1, 2 * nc, rows), lambda i, h, q: (h, 0, 0)),
            pl.BlockSpec((1, tq, LANES), lambda i, h, q: (i, q, h)),
        ],
        out_specs=pl.BlockSpec((1, tq, qw), lambda i, h, q: (i, q, h)),
        out_shape=jax.ShapeDtypeStruct((b, t_len, ATTN_WIDTH), BF16),
        scratch_shapes=[
            pltpu.VMEM((t_len, KV_PAIR), BF16),
            pltpu.VMEM((t_len // tq, KV_PAIR, tq), BF16),
            pltpu.VMEM((t_len // tq, KV_PAIR, tq), BF16),
            pltpu.VMEM((HEAD_DIM, tq), F32),
            pltpu.VMEM((2, 2, 1, rows), F32),
            pltpu.VMEM((2, 2, KV_PAIR, rows), F32),
            pltpu.VMEM((tq, rows), F32),
            pltpu.VMEM((tq, rows), F32),
            pltpu.VMEM((KV_PAIR, rows), F32),
            pltpu.VMEM((rows, KV_PAIR), BF16),
            pltpu.VMEM((rows, KV_PAIR), BF16),
        ],
        compiler_params=_params("arbitrary", "arbitrary", "arbitrary"),
        name="nsa_attention",
    )(proj3, kvc, kvct, proj3, proj3, band, cmpb, gates3)


def _conformer_body(a_ref, g_ref, ah_ref, gh_ref, w_ref, b_ref, lg_ref, lb_ref, o_ref, u_ref, y_ref, *, tt):
    ti = pl.program_id(1)
    u_ref[CONV_HALO:, :] = a_ref[0].astype(F32) * _sigmoid(g_ref[0].astype(F32))
    halo = ah_ref[0].astype(F32) * _sigmoid(gh_ref[0].astype(F32))
    u_ref[0:CONV_HALO, :] = jnp.where(ti > 0, halo, 0.0)
    base = CONV_HALO - (CONV_TAPS - 1)
    ext = CONV_ROWS + CONV_HALO
    for cb in range(CONV_CHANNELS // LANES):
        cs = slice(cb * LANES, (cb + 1) * LANES)
        wblk = w_ref[:, cs]
        for rb in range(tt // CONV_ROWS):
            ublk = u_ref[rb * CONV_ROWS:rb * CONV_ROWS + ext, cs]
            acc = jnp.zeros((CONV_ROWS, LANES), F32)
            for r in range(SUBLANES):
                ur = ublk if r == 0 else pltpu.roll(ublk, ext - r, axis=0)
                for a in range(CONV_HALO // SUBLANES + 1):
                    k = SUBLANES * a + r - base
                    if 0 <= k < CONV_TAPS:
                        acc = acc + wblk[k:k + 1, :] * ur[SUBLANES * a:SUBLANES * a + CONV_ROWS]
            y_ref[rb * CONV_ROWS:(rb + 1) * CONV_ROWS, cs] = acc
    acc = y_ref[...] + b_ref[...]
    mu = jnp.mean(acc, axis=-1, keepdims=True)
    xc = acc - mu
    var = jnp.mean(xc * xc, axis=-1, keepdims=True)
    y = xc * lax.rsqrt(var + NORM_EPS) * lg_ref[...] + lb_ref[...]
    o_ref[0] = (y * _sigmoid(y)).astype(o_ref.dtype)


def _conformer(proj3, w, b, lg, lb, *, tt):
    bsz, t_len, _ = proj3.shape
    a_blk = COL_CONV_A // CONV_CHANNELS
    g_blk = COL_CONV_G // CONV_CHANNELS
    hpt = tt // CONV_HALO
    halo_idx = lambda i, t: jnp.maximum(t * hpt - 1, 0)
    vec = pl.BlockSpec((1, CONV_CHANNELS), lambda i, t: (0, 0))
    return pl.pallas_call(
        functools.partial(_conformer_body, tt=tt),
        grid=(bsz, t_len // tt),
        in_specs=[
            pl.BlockSpec((1, tt, CONV_CHANNELS), lambda i, t: (i, t, a_blk)),
            pl.BlockSpec((1, tt, CONV_CHANNELS), lambda i, t: (i, t, g_blk)),
            pl.BlockSpec((1, CONV_HALO, CONV_CHANNELS), lambda i, t: (i, halo_idx(i, t), a_blk)),
            pl.BlockSpec((1, CONV_HALO, CONV_CHANNELS), lambda i, t: (i, halo_idx(i, t), g_blk)),
            pl.BlockSpec((CONV_TAPS, CONV_CHANNELS), lambda i, t: (0, 0)),
            vec, vec, vec,
        ],
        out_specs=pl.BlockSpec((1, tt, CONV_CHANNELS), lambda i, t: (i, t, 0)),
        out_shape=jax.ShapeDtypeStruct((bsz, t_len, CONV_CHANNELS), BF16),
        scratch_shapes=[pltpu.VMEM((CONV_HALO + tt, CONV_CHANNELS), F32),
                        pltpu.VMEM((tt, CONV_CHANNELS), F32)],
        compiler_params=_params("arbitrary", "arbitrary"),
        name="conformer_conv",
    )(proj3, proj3, proj3, proj3, w, b, lg, lb)


def _outproj_body(x_ref, a_ref, c_ref, wa_ref, wc_ref, o_ref):
    o_ref[...] = (x_ref[...]
                  + jnp.dot(a_ref[...], wa_ref[...], preferred_element_type=F32)
                  + jnp.dot(c_ref[...], wc_ref[...], preferred_element_type=F32))


def _out_projection(x2, attn2, conv2, wa, wc, *, tm, tn):
    n = x2.shape[0]
    return pl.pallas_call(
        _outproj_body,
        grid=(n // tm, D_MODEL // tn),
        in_specs=[
            pl.BlockSpec((tm, tn), lambda i, j: (i, j)),
            pl.BlockSpec((tm, ATTN_WIDTH), lambda i, j: (i, 0)),
            pl.BlockSpec((tm, CONV_CHANNELS), lambda i, j: (i, 0)),
            pl.BlockSpec((ATTN_WIDTH, tn), lambda i, j: (0, j)),
            pl.BlockSpec((CONV_CHANNELS, tn), lambda i, j: (0, j)),
        ],
        out_specs=pl.BlockSpec((tm, tn), lambda i, j: (i, j)),
        out_shape=jax.ShapeDtypeStruct((n, D_MODEL), F32),
        compiler_params=_params("arbitrary", "arbitrary"),
        name="out_projection",
    )(x2, attn2, conv2, wa, wc)


def _ffn_up_body(x_ref, xh_ref, g_ref, wa_ref, wg_ref, cwa_ref, cwg_ref, cba_ref, cbg_ref, o_ref,
                 h_ref, ua_ref, ug_ref, *, tm, tiles_per_seq, rows):
    i = pl.program_id(0)

    @pl.when(pl.program_id(1) == 0)
    def _():
        h_ref[FFN_HALO:, :] = _rms_norm_rows(x_ref[...], g_ref[...]).astype(BF16)
        hh = _rms_norm_rows(xh_ref[...], g_ref[...])
        h_ref[0:FFN_HALO, :] = jnp.where(i % tiles_per_seq != 0, hh, 0.0).astype(BF16)

    def project(c):
        hc = h_ref[c * rows:(c + 1) * rows + FFN_HALO, :]
        ua_ref[c] = jnp.dot(hc, wa_ref[...], preferred_element_type=F32)
        ug_ref[c] = jnp.dot(hc, wg_ref[...], preferred_element_type=F32)

    def conv(u_ref, c, cw_ref, cb_ref):
        u = u_ref[c]
        y = cw_ref[FFN_CONV_TAPS - 1:FFN_CONV_TAPS, :] * u[FFN_HALO:]
        for s in range(1, FFN_CONV_TAPS):
            k = FFN_CONV_TAPS - 1 - s
            y = y + cw_ref[k:k + 1, :] * pltpu.roll(u, s, axis=0)[FFN_HALO:]
        return y + cb_ref[...]

    def finish(c):
        a = conv(ua_ref, c, cwa_ref, cba_ref)
        gate = conv(ug_ref, c, cwg_ref, cbg_ref)
        o_ref[c * rows:(c + 1) * rows, :] = (gate * _sigmoid(gate) * a).astype(o_ref.dtype)

    n_chunks = tm // rows
    project(0)
    for c in range(n_chunks):
        if c + 1 < n_chunks:
            project(c + 1)
        finish(c)


def _ffn_up(x2, g, w_up, cw, cb, *, tm, tn, t_len):
    n = x2.shape[0]
    nj = D_FF // tn
    hpt = tm // FFN_HALO
    rows = min(FFN_ROWS, tm)
    return pl.pallas_call(
        functools.partial(_ffn_up_body, tm=tm, tiles_per_seq=t_len // tm, rows=rows),
        grid=(n // tm, nj),
        in_specs=[
            pl.BlockSpec((tm, D_MODEL), lambda i, j: (i, 0)),
            pl.BlockSpec((FFN_HALO, D_MODEL), lambda i, j: (jnp.maximum(i * hpt - 1, 0), 0)),
            pl.BlockSpec((1, D_MODEL), lambda i, j: (0, 0)),
            pl.BlockSpec((D_MODEL, tn), lambda i, j: (0, j)),
            pl.BlockSpec((D_MODEL, tn), lambda i, j: (0, j + nj)),
            pl.BlockSpec((FFN_CONV_TAPS, tn), lambda i, j: (0, j)),
            pl.BlockSpec((FFN_CONV_TAPS, tn), lambda i, j: (0, j + nj)),
            pl.BlockSpec((1, tn), lambda i, j: (0, j)),
            pl.BlockSpec((1, tn), lambda i, j: (0, j + nj)),
        ],
        out_specs=pl.BlockSpec((tm, tn), lambda i, j: (i, j)),
        out_shape=jax.ShapeDtypeStruct((n, D_FF), BF16),
        scratch_shapes=[
            pltpu.VMEM((FFN_HALO + tm, D_MODEL), BF16),
            pltpu.VMEM((tm // rows, FFN_HALO + rows, tn), F32),
            pltpu.VMEM((tm // rows, FFN_HALO + rows, tn), F32),
        ],
        compiler_params=_params("arbitrary", "arbitrary"),
        name="ffn_up",
    )(x2, x2, g, w_up, w_up, cw, cw, cb, cb)


def _ffn_down_body(act_ref, w_ref, x_ref, o_ref):
    o_ref[...] = x_ref[...] + jnp.dot(act_ref[...], w_ref[...], preferred_element_type=F32)


def _ffn_down(act, w_down, x2, *, tm, tn):
    n = x2.shape[0]
    return pl.pallas_call(
        _ffn_down_body,
        grid=(n // tm, D_MODEL // tn),
        in_specs=[
            pl.BlockSpec((tm, D_FF), lambda i, j: (i, 0)),
            pl.BlockSpec((D_FF, tn), lambda i, j: (0, j)),
            pl.BlockSpec((tm, tn), lambda i, j: (i, j)),
        ],
        out_specs=pl.BlockSpec((tm, tn), lambda i, j: (i, j)),
        out_shape=jax.ShapeDtypeStruct((n, D_MODEL), F32),
        compiler_params=_params("arbitrary", "arbitrary"),
        name="ffn_down",
    )(act, w_down, x2)


def _final_norm_body(x_ref, g_ref, o_ref):
    o_ref[...] = _rms_norm_rows(x_ref[...], g_ref[...])


def _final_norm(x2, g, *, tm):
    n = x2.shape[0]
    return pl.pallas_call(
        _final_norm_body,
        grid=(n // tm,),
        in_specs=[pl.BlockSpec((tm, D_MODEL), lambda i: (i, 0)), pl.BlockSpec((1, D_MODEL), lambda i: (0, 0))],
        out_specs=pl.BlockSpec((tm, D_MODEL), lambda i: (i, 0)),
        out_shape=jax.ShapeDtypeStruct((n, D_MODEL), F32),
        compiler_params=_params("arbitrary"),
        name="final_norm",
    )(x2, g)


def _in_weight_layout(w_in_l):
    q_cols = np.arange(ATTN_WIDTH)
    kv0 = ATTN_WIDTH
    gate0 = kv0 + KV_COLS
    conv0 = gate0 + GATE_COLS
    conv_cols = conv0 + np.arange(2 * CONV_CHANNELS)
    kv_cols = []
    for br in range(N_BRANCH):
        for h in range(N_KV_HEADS):
            for kv in range(2):
                kv_cols.append(kv0 + ((br * 2 + kv) * N_KV_HEADS + h) * HEAD_DIM + np.arange(HEAD_DIM))
    perm = np.concatenate([q_cols, conv_cols] + kv_cols)
    w_main = jnp.take(w_in_l, jnp.asarray(perm, jnp.int32), axis=1).astype(BF16)
    gsrc = np.zeros((N_KV_HEADS, N_BRANCH * GQA_GROUP), np.int32)
    for h in range(N_KV_HEADS):
        for br in range(N_BRANCH):
            for g in range(GQA_GROUP):
                gsrc[h, br * GQA_GROUP + g] = gate0 + (h * GQA_GROUP + g) * N_BRANCH + br
    wg = jnp.take(w_in_l, jnp.asarray(gsrc.reshape(-1)), axis=1).reshape(D_MODEL, N_KV_HEADS, -1)
    wg = jnp.pad(wg, ((0, 0), (0, 0), (0, LANES - N_BRANCH * GQA_GROUP)))
    return w_main, wg.reshape(D_MODEL, N_KV_HEADS * LANES).astype(BF16)


def _pick(n, prefs):
    for p in prefs:
        if n % p == 0:
            return p
    return n


def kernel(x, rel_bias, mix_norm_g, w_in, cmp_pos, cmp_w1, cmp_w2, conv_w, conv_b, conv_ln_g, conv_ln_b,
           w_out, ffn_norm_g, w_up, ffn_conv_w, ffn_conv_b, w_down, final_norm_g):
    bsz, t_len, _ = x.shape
    depth = w_in.shape[0]
    n = bsz * t_len
    assert t_len % 256 == 0 and t_len // SLC_LEN <= HEAD_DIM
    tq = 256
    tm = _pick(t_len, (1024, 512, 256))
    nc = t_len // CMP_STRIDE

    band, cmpb = _bias_tiles(rel_bias, t_len, tq)
    col_scale = jnp.concatenate([jnp.full((1, ATTN_WIDTH), Q_SCALE, F32),
                                 jnp.ones((1, PROJ_COLS - ATTN_WIDTH), F32)], axis=1)
    x2 = x.reshape(n, D_MODEL)
    for l in range(depth):
        w_main, wg = _in_weight_layout(w_in[l])
        proj, gates = _in_projection(x2, mix_norm_g[l][None, :], w_main, wg, col_scale, tm=tm, tn=768)
        proj3 = proj.reshape(bsz, t_len, PROJ_COLS)
        gates3 = gates.reshape(bsz, t_len, N_KV_HEADS * LANES)

        posx, w1big, w2big = _compress_weight_layout(cmp_pos[l], cmp_w1[l], cmp_w2[l])
        tokl = proj3[:, :, COL_KV:COL_KV + CMP_WIDTH].reshape(bsz, nc, CMP_STRIDE, CMP_WIDTH).transpose(0, 2, 1, 3)
        kvc, kvct = _compress(tokl, posx, w1big, w2big)

        attn = _attention(proj3, kvc, kvct, band, cmpb, gates3, tq=tq)
        conv = _conformer(proj3, conv_w[l], conv_b[l][None, :], conv_ln_g[l][None, :],
                          conv_ln_b[l][None, :], tt=256)
        w_o = w_out[l].astype(BF16)
        x2 = _out_projection(x2, attn.reshape(n, ATTN_WIDTH), conv.reshape(n, CONV_CHANNELS),
                             w_o[:ATTN_WIDTH], w_o[ATTN_WIDTH:], tm=tm, tn=1024)

        act = _ffn_up(x2, ffn_norm_g[l][None, :], w_up[l].astype(BF16), ffn_conv_w[l],
                      ffn_conv_b[l][None, :], tm=tm, tn=512, t_len=t_len)
        x2 = _ffn_down(act, w_down[l].astype(BF16), x2, tm=tm, tn=512)
    x2 = _final_norm(x2, final_norm_g[None, :], tm=_pick(n, (512, 256)))
    return x2.reshape(bsz, t_len, D_MODEL)
```

```python
import functools
import math

import numpy as np
import jax
import jax.numpy as jnp
from jax import lax
from jax.experimental import pallas as pl
from jax.experimental.pallas import tpu as pltpu

F32 = jnp.float32
BF16 = jnp.bfloat16

D_MODEL = 2048
HEAD_DIM = 64
N_KV_HEADS = 4
GQA_GROUP = 4
N_Q_HEADS = N_KV_HEADS * GQA_GROUP
N_BRANCH = 3
ATTN_WIDTH = N_Q_HEADS * HEAD_DIM
CMP_LEN = 32
CMP_STRIDE = 16
SLC_LEN = 64
SLC_TOPK = 16
WINDOW = 512
CONV_CHANNELS = D_MODEL - ATTN_WIDTH
CONV_TAPS = 31
D_FF = 5632
FFN_CONV_TAPS = 3
N_BUCKETS = 32
MAX_DISTANCE = 128
NORM_EPS = 1e-6

KV_PAIR = 2 * HEAD_DIM
KV_COLS = N_BRANCH * N_KV_HEADS * KV_PAIR
GATE_COLS = N_BRANCH * N_Q_HEADS
COL_Q = 0
COL_CONV_A = ATTN_WIDTH
COL_CONV_G = COL_CONV_A + CONV_CHANNELS
COL_KV = COL_CONV_G + CONV_CHANNELS
PROJ_COLS = COL_KV + KV_COLS

MASK_VALUE = -1e30
BLOCK_PENALTY = -1e9
KEY_BIG = 1e30
LOG2E = math.log2(math.e)
Q_SCALE = HEAD_DIM ** -0.5 * LOG2E

LANES = 128
SUBLANES = 8
CONV_ROWS = 128
FFN_ROWS = 1024
VMEM_LIMIT = 56 * 1024 * 1024
CONV_HALO = 32
FFN_HALO = 16


def _t5_bucket_last_distance():
    n = np.arange(0, 4 * MAX_DISTANCE, dtype=np.int64)
    max_exact = N_BUCKETS // 2
    nf = np.maximum(n, 1).astype(np.float64)
    large = max_exact + np.floor(np.log(nf / max_exact) / math.log(MAX_DISTANCE / max_exact)
                                 * (N_BUCKETS - max_exact)).astype(np.int64)
    large = np.minimum(large, N_BUCKETS - 1)
    bucket = np.where(n < max_exact, n, large)
    last = []
    for b in range(N_BUCKETS - 1):
        idx = np.nonzero(bucket == b)[0]
        last.append(int(idx.max()) if idx.size else None)
    return last


_BUCKET_LAST = _t5_bucket_last_distance()


def _params(*sem):
    return pltpu.CompilerParams(dimension_semantics=sem, vmem_limit_bytes=VMEM_LIMIT)


def _sigmoid(x):
    return jax.nn.sigmoid(x)


def _rms_norm_rows(x, g):
    ms = jnp.mean(x * x, axis=-1, keepdims=True)
    return x * lax.rsqrt(ms + NORM_EPS) * g


def _bias_from_distance(dist, tab_ref, head):
    c_far = tab_ref[N_BUCKETS - 1, head]
    val = jnp.zeros(dist.shape, F32)
    for b in range(N_BUCKETS - 2, -1, -1):
        if _BUCKET_LAST[b] is None:
            continue
        val = jnp.where(dist <= _BUCKET_LAST[b], tab_ref[b, head] - c_far, val)
    return val


def _band_bias_body(tab_ref, o_ref, *, tq):
    head = pl.program_id(0) * GQA_GROUP + pl.program_id(1)
    shape = (WINDOW + tq, tq)
    dist = (lax.broadcasted_iota(jnp.int32, shape, 1) + WINDOW
            - lax.broadcasted_iota(jnp.int32, shape, 0))
    val = _bias_from_distance(dist, tab_ref, head) * LOG2E
    val = jnp.where(dist >= 0, jnp.where(dist < WINDOW, val, MASK_VALUE), MASK_VALUE)
    o_ref[0] = val


def _cmp_bias_body(tab_ref, o_ref, *, tq, nc):
    head = pl.program_id(0) * GQA_GROUP + pl.program_id(1)
    shape = (2 * nc, tq)
    r = lax.broadcasted_iota(jnp.int32, shape, 0)
    i = lax.broadcasted_iota(jnp.int32, shape, 1)
    dist = i - (r - nc) * CMP_STRIDE - (CMP_LEN - 1)
    val = _bias_from_distance(dist, tab_ref, head) * LOG2E
    o_ref[0] = jnp.where(dist >= 0, val, MASK_VALUE)


def _bias_tiles(rel_bias, t_len, tq):
    nc = t_len // CMP_STRIDE
    rows = GQA_GROUP * tq
    smem = pl.BlockSpec(memory_space=pltpu.SMEM)
    band = pl.pallas_call(
        functools.partial(_band_bias_body, tq=tq),
        grid=(N_KV_HEADS, GQA_GROUP),
        in_specs=[smem],
        out_specs=pl.BlockSpec((1, WINDOW + tq, tq), lambda h, g: (h, 0, g)),
        out_shape=jax.ShapeDtypeStruct((N_KV_HEADS, WINDOW + tq, rows), F32),
        compiler_params=_params("arbitrary", "arbitrary"),
        name="band_bias",
    )(rel_bias)
    cmpb = pl.pallas_call(
        functools.partial(_cmp_bias_body, tq=tq, nc=nc),
        grid=(N_KV_HEADS, GQA_GROUP),
        in_specs=[smem],
        out_specs=pl.BlockSpec((1, 2 * nc, tq), lambda h, g: (h, 0, g)),
        out_shape=jax.ShapeDtypeStruct((N_KV_HEADS, 2 * nc, rows), F32),
        compiler_params=_params("arbitrary", "arbitrary"),
        name="cmp_bias",
    )(rel_bias)
    return band, cmpb


def _inproj_body(x_ref, g_ref, w_ref, wg_ref, cs_ref, o_ref, gate_ref, h_ref):
    @pl.when(pl.program_id(1) == 0)
    def _():
        h = _rms_norm_rows(x_ref[...], g_ref[...]).astype(BF16)
        h_ref[...] = h
        gate_ref[...] = _sigmoid(jnp.dot(h, wg_ref[...], preferred_element_type=F32))

    acc = jnp.dot(h_ref[...], w_ref[...], preferred_element_type=F32)
    o_ref[...] = (acc * cs_ref[...]).astype(o_ref.dtype)


def _in_projection(x2, g, w, wg, col_scale, *, tm, tn):
    n = x2.shape[0]
    gcols = wg.shape[1]
    return pl.pallas_call(
        _inproj_body,
        grid=(n // tm, PROJ_COLS // tn),
        in_specs=[
            pl.BlockSpec((tm, D_MODEL), lambda i, j: (i, 0)),
            pl.BlockSpec((1, D_MODEL), lambda i, j: (0, 0)),
            pl.BlockSpec((D_MODEL, tn), lambda i, j: (0, j)),
            pl.BlockSpec((D_MODEL, gcols), lambda i, j: (0, 0)),
            pl.BlockSpec((1, tn), lambda i, j: (0, j)),
        ],
        out_specs=[
            pl.BlockSpec((tm, tn), lambda i, j: (i, j)),
            pl.BlockSpec((tm, gcols), lambda i, j: (i, 0)),
        ],
        out_shape=[
            jax.ShapeDtypeStruct((n, PROJ_COLS), BF16),
            jax.ShapeDtypeStruct((n, gcols), F32),
        ],
        scratch_shapes=[pltpu.VMEM((tm, D_MODEL), BF16)],
        compiler_params=_params("arbitrary", "arbitrary"),
        name="in_projection",
    )(x2, g, w, wg, col_scale)


CMP_WIDTH = N_KV_HEADS * KV_PAIR


def _compress_body(x_ref, pos_ref, w1_ref, w2_ref, o_ref, ot_ref, top_ref, bot_ref, *, nc):
    l = pl.program_id(1)

    @pl.when(l == 0)
    def _():
        top_ref[...] = jnp.zeros_like(top_ref)
        bot_ref[...] = jnp.zeros_like(bot_ref)

    x = x_ref[0, 0].astype(F32)
    top_ref[...] += jnp.dot((x + pos_ref[0, 0]).astype(BF16), w1_ref[0, 0], preferred_element_type=F32)
    bot_ref[...] += jnp.dot((x + pos_ref[0, 1]).astype(BF16), w1_ref[0, 1], preferred_element_type=F32)

    @pl.when(l == pl.num_programs(1) - 1)
    def _():
        pre = top_ref[...] + pltpu.roll(bot_ref[...], nc - 1, axis=0)
        act = pre * _sigmoid(pre)
        out = jnp.dot(act.astype(BF16), w2_ref[...], preferred_element_type=F32)
        for h in range(N_KV_HEADS):
            kv = out[:, h * KV_PAIR:(h + 1) * KV_PAIR]
            o_ref[0, h] = kv.astype(o_ref.dtype)
            ot_ref[0, h] = kv.T.astype(ot_ref.dtype)


def _compress(tokl, posx, w1big, w2big):
    b, _, nc, _ = tokl.shape
    return pl.pallas_call(
        functools.partial(_compress_body, nc=nc),
        grid=(b, CMP_STRIDE),
        in_specs=[
            pl.BlockSpec((1, 1, nc, CMP_WIDTH), lambda i, l: (i, l, 0, 0)),
            pl.BlockSpec((1, 2, 1, CMP_WIDTH), lambda i, l: (l, 0, 0, 0)),
            pl.BlockSpec((1, 2, CMP_WIDTH, CMP_WIDTH), lambda i, l: (l, 0, 0, 0)),
            pl.BlockSpec((CMP_WIDTH, CMP_WIDTH), lambda i, l: (0, 0)),
        ],
        out_specs=[
            pl.BlockSpec((1, N_KV_HEADS, nc, KV_PAIR), lambda i, l: (i, 0, 0, 0)),
            pl.BlockSpec((1, N_KV_HEADS, KV_PAIR, nc), lambda i, l: (i, 0, 0, 0)),
        ],
        out_shape=[
            jax.ShapeDtypeStruct((b, N_KV_HEADS, nc, KV_PAIR), BF16),
            jax.ShapeDtypeStruct((b, N_KV_HEADS, KV_PAIR, nc), BF16),
        ],
        scratch_shapes=[pltpu.VMEM((nc, CMP_WIDTH), F32), pltpu.VMEM((nc, CMP_WIDTH), F32)],
        compiler_params=_params("arbitrary", "arbitrary"),
        name="compress",
    )(tokl, posx, w1big, w2big)


def _compress_weight_layout(pos, w1, w2):
    def block_diag(blocks):
        rows = []
        for h in range(N_KV_HEADS):
            for kv in range(2):
                off = (h * 2 + kv) * HEAD_DIM
                pad = [(0, 0)] * (blocks.ndim - 2) + [(off, CMP_WIDTH - HEAD_DIM - off)]
                rows.append(jnp.pad(blocks[kv], pad))
        return jnp.concatenate(rows, axis=-2).astype(BF16)

    w1r = w1.reshape(2, 2, CMP_STRIDE, HEAD_DIM, HEAD_DIM)
    w1big = block_diag(w1r.transpose(0, 2, 1, 3, 4))
    w2big = block_diag(w2)
    posr = pos.reshape(2, 2, CMP_STRIDE, HEAD_DIM).transpose(2, 1, 0, 3)
    posx = jnp.broadcast_to(posr[:, :, None], (CMP_STRIDE, 2, N_KV_HEADS, 2, HEAD_DIM))
    return posx.reshape(CMP_STRIDE, 2, 1, CMP_WIDTH), w1big, w2big


VT_ROWS = HEAD_DIM + 16


def _attn_body(q_ref, kvc_ref, kvct_ref, kvs_ref, kvw_ref, bt_ref, bc_ref, gate_ref, o_ref,
               kp_ref, vst_ref, vwt_ref, key_ref, m_ref, acc_ref, sa_ref, sb_ref, ocmp_ref, qsel_ref, qwin_ref,
               *, tq, t_len):
    qi = pl.program_id(2)
    nc = t_len // CMP_STRIDE
    ns = t_len // SLC_LEN
    rows = GQA_GROUP * tq
    blocks_per_tile = tq // SLC_LEN

    @pl.when(qi == 0)
    def _():
        k = kvs_ref[0][:, :HEAD_DIM]
        blk = lax.broadcasted_iota(jnp.int32, (t_len, HEAD_DIM), 0) // SLC_LEN
        col = lax.broadcasted_iota(jnp.int32, (t_len, HEAD_DIM), 1)
        onehot = jnp.where(blk == col, 1.0, 0.0).astype(BF16)
        kp_ref[...] = jnp.concatenate([k, onehot], axis=1)
        ones_rows = jnp.where(lax.broadcasted_iota(jnp.int32, (VT_ROWS - HEAD_DIM, tq), 0) == 0, 1.0, 0.0)
        for c in range(t_len // tq):
            st = kvs_ref[0, c * tq:(c + 1) * tq, :].astype(F32).T[HEAD_DIM:]
            vst_ref[c] = jnp.concatenate([st, ones_rows], axis=0).astype(BF16)
            wt = kvw_ref[0, c * tq:(c + 1) * tq, :].astype(F32).T[HEAD_DIM:]
            vwt_ref[c] = jnp.concatenate([wt, ones_rows], axis=0).astype(BF16)

    q_t = q_ref[0].astype(F32).T
    q4_t = jnp.concatenate([q_t[g * HEAD_DIM:(g + 1) * HEAD_DIM] for g in range(GQA_GROUP)],
                           axis=1).astype(BF16)
    qwin_ref[...] = jnp.concatenate([q4_t, jnp.zeros((HEAD_DIM, rows), BF16)], axis=0)
    qsel_ref[0:HEAD_DIM, :] = q4_t
    kw_ref = kvw_ref.at[0]
    win, sel = 0, 1

    def cmp_scores():
        bias_row = pl.multiple_of(nc - qi * (tq // CMP_STRIDE), tq // CMP_STRIDE)
        return (jnp.dot(kvc_ref[0, 0], qwin_ref[...], preferred_element_type=F32)
                + bc_ref[0, pl.ds(bias_row, nc), :])

    def cmp_finish(s):
        m = jnp.max(s, axis=0, keepdims=True)
        p = jnp.exp2(s - m)
        l = jnp.sum(p, axis=0, keepdims=True)
        pn = p * jnp.where(m > 0.5 * MASK_VALUE, 1.0 / l, 0.0)
        ocmp_ref[...] = jnp.dot(kvct_ref[0, 0, HEAD_DIM:, :], pn.astype(BF16),
                                preferred_element_type=F32)

        ps = pn[:, 0:tq] + pn[:, tq:2 * tq] + pn[:, 2 * tq:3 * tq] + pn[:, 3 * tq:4 * tq]
        sj = lax.broadcasted_iota(jnp.int32, (HEAD_DIM, nc), 0) * SLC_LEN
        ci = lax.broadcasted_iota(jnp.int32, (HEAD_DIM, nc), 1) * CMP_STRIDE
        overlap = jnp.where(ci < sj + SLC_LEN, jnp.where(ci + CMP_LEN > sj, 1.0, 0.0), 0.0).astype(BF16)
        p_hi = ps.astype(BF16)
        r_hi = ps - p_hi.astype(F32)
        p_md = r_hi.astype(BF16)
        p_lo = (r_hi - p_md.astype(F32)).astype(BF16)
        imp = (jnp.dot(overlap, p_hi, preferred_element_type=F32)
               + jnp.dot(overlap, p_md, preferred_element_type=F32)
               + jnp.dot(overlap, p_lo, preferred_element_type=F32))
        t = qi * tq + lax.broadcasted_iota(jnp.int32, (HEAD_DIM, tq), 1)
        blk = lax.broadcasted_iota(jnp.int32, (HEAD_DIM, tq), 0)
        cur = t // SLC_LEN
        key = jnp.where(blk == 0, KEY_BIG, jnp.where(blk == cur, KEY_BIG, jnp.where(blk == cur - 1, KEY_BIG, imp)))
        key_ref[...] = jnp.where(blk * SLC_LEN <= t, key, -KEY_BIG)

    def select_blocks():
        key = key_ref[...]
        blk = lax.broadcasted_iota(jnp.int32, (HEAD_DIM, tq), 0)

        def rank_group(gi, rank):
            for jj in range(blocks_per_tile):
                j = gi * blocks_per_tile + jj
                col = key_ref[pl.ds(j, 1), :]
                gt = jnp.where(col > key, 1.0, 0.0)
                ge = jnp.where(col >= key, 1.0, 0.0)
                rank = rank + jnp.where(blk > j, ge, gt)
            return rank

        rank = lax.fori_loop(0, qi + 1, rank_group, jnp.zeros((HEAD_DIM, tq), F32))
        pen = jnp.where(rank < float(min(SLC_TOPK, ns)), 0.0, BLOCK_PENALTY).astype(BF16)
        qsel_ref[HEAD_DIM:, :] = jnp.concatenate([pen] * GQA_GROUP, axis=1)

    def scores(qx_ref, k_ref, kj, bias_off):
        start = pl.multiple_of(kj * tq, tq)
        sc = jnp.dot(k_ref[pl.ds(start, tq), :], qx_ref[...], preferred_element_type=F32)
        if bias_off is not None:
            sc = sc + bt_ref[0, bias_off:bias_off + tq, :]
        return sc

    def consume(sc, vt_ref, kj, br, st, first):
        m_tile = jnp.max(sc, axis=0, keepdims=True)
        if first:
            m_new = m_tile
            pr = jnp.exp2(sc - m_new).astype(BF16)
            acc_ref[br, st] = jnp.dot(vt_ref[kj], pr, preferred_element_type=F32)
        else:
            m_old = m_ref[br, st]
            m_new = jnp.maximum(m_old, m_tile)
            alpha = jnp.exp2(m_old - m_new)
            pr = jnp.exp2(sc - m_new).astype(BF16)
            acc_ref[br, st] = alpha * acc_ref[br, st] + jnp.dot(vt_ref[kj], pr, preferred_element_type=F32)
        m_ref[br, st] = m_new

    def finish(br):
        m0 = m_ref[br, 0]
        m1 = m_ref[br, 1]
        m_all = jnp.maximum(m0, m1)
        acc = jnp.exp2(m0 - m_all) * acc_ref[br, 0] + jnp.exp2(m1 - m_all) * acc_ref[br, 1]
        return acc[0:HEAD_DIM] * (1.0 / acc[HEAD_DIM:HEAD_DIM + 1, :])

    def combine():
        o_win = finish(win)
        o_sel = finish(sel)
        o_cmp = ocmp_ref[...]
        gates_t = gate_ref[0].T
        combs = []
        for g in range(GQA_GROUP):
            sl = slice(g * tq, (g + 1) * tq)
            combs.append(gates_t[g:g + 1, :] * o_cmp[:, sl]
                         + gates_t[GQA_GROUP + g:GQA_GROUP + g + 1, :] * o_sel[:, sl]
                         + gates_t[2 * GQA_GROUP + g:2 * GQA_GROUP + g + 1, :] * o_win[:, sl])
        outs = [jnp.concatenate(combs[p:p + 2], axis=0).T for p in range(0, GQA_GROUP, 2)]
        o_ref[0] = jnp.concatenate(outs, axis=1).astype(o_ref.dtype)

    @pl.when(qi >= 2)
    def _():
        sc_cmp = cmp_scores()
        sw0 = scores(qwin_ref, kw_ref, qi, WINDOW)
        cmp_finish(sc_cmp)
        sw1 = scores(qwin_ref, kw_ref, qi - 1, WINDOW - tq)
        consume(sw0, vwt_ref, qi, win, 0, True)
        sw2 = scores(qwin_ref, kw_ref, qi - 2, WINDOW - 2 * tq)
        consume(sw1, vwt_ref, qi - 1, win, 1, True)
        select_blocks()
        consume(sw2, vwt_ref, qi - 2, win, 0, False)

        ss0 = scores(qsel_ref, kp_ref, qi, WINDOW)
        ss1 = scores(qsel_ref, kp_ref, qi - 1, WINDOW - tq)
        consume(ss0, vst_ref, qi, sel, 0, True)
        sa_ref[...] = scores(qsel_ref, kp_ref, 0, None)
        consume(ss1, vst_ref, qi - 1, sel, 1, True)
        n_far = qi - 1

        def far_pair(pi, carry):
            sb_ref[...] = scores(qsel_ref, kp_ref, 2 * pi + 1, None)
            consume(sa_ref[...], vst_ref, 2 * pi, sel, 1, False)
            sa_ref[...] = scores(qsel_ref, kp_ref, jnp.minimum(2 * pi + 2, n_far - 1), None)
            consume(sb_ref[...], vst_ref, 2 * pi + 1, sel, 0, False)
            return carry

        lax.fori_loop(0, n_far // 2, far_pair, 0)

        @pl.when(n_far % 2 == 1)
        def _():
            consume(sa_ref[...], vst_ref, n_far - 1, sel, 1, False)

        combine()

    @pl.when(qi < 2)
    def _():
        cmp_finish(cmp_scores())
        select_blocks()
        sw0 = scores(qwin_ref, kw_ref, qi, WINDOW)
        ss0 = scores(qsel_ref, kp_ref, qi, WINDOW)
        consume(sw0, vwt_ref, qi, win, 0, True)
        consume(ss0, vst_ref, qi, sel, 0, True)
        for br in (win, sel):
            m_ref[br, 1] = jnp.full((1, rows), MASK_VALUE, F32)
            acc_ref[br, 1] = jnp.zeros((VT_ROWS, rows), F32)

        @pl.when(qi == 1)
        def _():
            sw1 = scores(qwin_ref, kw_ref, 0, WINDOW - tq)
            ss1 = scores(qsel_ref, kp_ref, 0, WINDOW - tq)
            consume(sw1, vwt_ref, 0, win, 1, False)
            consume(ss1, vst_ref, 0, sel, 1, False)

        combine()


def _attention(proj3, kvc, kvct, band, cmpb, gates3, *, tq):
    b, t_len, _ = proj3.shape
    assert WINDOW == 2 * tq
    nc = t_len // CMP_STRIDE
    rows = GQA_GROUP * tq
    qw = GQA_GROUP * HEAD_DIM
    slc_blk = COL_KV // KV_PAIR + N_KV_HEADS
    win_blk = COL_KV // KV_PAIR + 2 * N_KV_HEADS
    return pl.pallas_call(
        functools.partial(_attn_body, tq=tq, t_len=t_len),
        grid=(b, N_KV_HEADS, t_len // tq),
        in_specs=[
            pl.BlockSpec((1, tq, qw), lambda i, h, q: (i, q, h)),
            pl.BlockSpec((1, 1, nc, KV_PAIR), lambda i, h, q: (i, h, 0, 0)),
            pl.BlockSpec((1, 1, KV_PAIR, nc), lambda i, h, q: (i, h, 0, 0)),
            pl.BlockSpec((1, t_len, KV_PAIR), lambda i, h, q: (i, 0, slc_blk + h)),
            pl.BlockSpec((1, t_len, KV_PAIR), lambda i, h, q: (i, 0, win_blk + h)),
            pl.BlockSpec((1, WINDOW + tq, rows), lambda i, h, q: (h, 0, 0)),
            pl.BlockSpec((1, 2 * nc, rows), lambda i, h, q: (h, 0, 0)),
            pl.BlockSpec((1, tq, LANES), lambda i, h, q: (i, q, h)),
        ],
        out_specs=pl.BlockSpec((1, tq, qw), lambda i, h, q: (i, q, h)),
        out_shape=jax.ShapeDtypeStruct((b, t_len, ATTN_WIDTH), BF16),
        scratch_shapes=[
            pltpu.VMEM((t_len, KV_PAIR), BF16),
            pltpu.VMEM((t_len // tq, VT_ROWS, tq), BF16),
            pltpu.VMEM((t_len // tq, VT_ROWS, tq), BF16),
            pltpu.VMEM((HEAD_DIM, tq), F32),
            pltpu.VMEM((2, 2, 1, rows), F32),
            pltpu.VMEM((2, 2, VT_ROWS, rows), F32),
            pltpu.VMEM((tq, rows), F32),
            pltpu.VMEM((tq, rows), F32),
            pltpu.VMEM((HEAD_DIM, rows), F32),
            pltpu.VMEM((KV_PAIR, rows), BF16),
            pltpu.VMEM((KV_PAIR, rows), BF16),
        ],
        compiler_params=_params("arbitrary", "arbitrary", "arbitrary"),
        name="nsa_attention",
    )(proj3, kvc, kvct, proj3, proj3, band, cmpb, gates3)


def _conformer_body(a_ref, g_ref, ah_ref, gh_ref, w_ref, b_ref, lg_ref, lb_ref, o_ref, u_ref, y_ref, *, tt):
    ti = pl.program_id(1)
    u_ref[CONV_HALO:, :] = a_ref[0].astype(F32) * _sigmoid(g_ref[0].astype(F32))
    halo = ah_ref[0].astype(F32) * _sigmoid(gh_ref[0].astype(F32))
    u_ref[0:CONV_HALO, :] = jnp.where(ti > 0, halo, 0.0)
    base = CONV_HALO - (CONV_TAPS - 1)
    ext = CONV_ROWS + CONV_HALO
    for cb in range(CONV_CHANNELS // LANES):
        cs = slice(cb * LANES, (cb + 1) * LANES)
        wblk = w_ref[:, cs]
        for rb in range(tt // CONV_ROWS):
            ublk = u_ref[rb * CONV_ROWS:rb * CONV_ROWS + ext, cs]
            acc = jnp.zeros((CONV_ROWS, LANES), F32)
            for r in range(SUBLANES):
                ur = ublk if r == 0 else pltpu.roll(ublk, ext - r, axis=0)
                for a in range(CONV_HALO // SUBLANES + 1):
                    k = SUBLANES * a + r - base
                    if 0 <= k < CONV_TAPS:
                        acc = acc + wblk[k:k + 1, :] * ur[SUBLANES * a:SUBLANES * a + CONV_ROWS]
            y_ref[rb * CONV_ROWS:(rb + 1) * CONV_ROWS, cs] = acc
    acc = y_ref[...] + b_ref[...]
    mu = jnp.mean(acc, axis=-1, keepdims=True)
    xc = acc - mu
    var = jnp.mean(xc * xc, axis=-1, keepdims=True)
    y = xc * lax.rsqrt(var + NORM_EPS) * lg_ref[...] + lb_ref[...]
    o_ref[0] = (y * _sigmoid(y)).astype(o_ref.dtype)


def _conformer(proj3, w, b, lg, lb, *, tt):
    bsz, t_len, _ = proj3.shape
    a_blk = COL_CONV_A // CONV_CHANNELS
    g_blk = COL_CONV_G // CONV_CHANNELS
    hpt = tt // CONV_HALO
    halo_idx = lambda i, t: jnp.maximum(t * hpt - 1, 0)
    vec = pl.BlockSpec((1, CONV_CHANNELS), lambda i, t: (0, 0))
    return pl.pallas_call(
        functools.partial(_conformer_body, tt=tt),
        grid=(bsz, t_len // tt),
        in_specs=[
            pl.BlockSpec((1, tt, CONV_CHANNELS), lambda i, t: (i, t, a_blk)),
            pl.BlockSpec((1, tt, CONV_CHANNELS), lambda i, t: (i, t, g_blk)),
            pl.BlockSpec((1, CONV_HALO, CONV_CHANNELS), lambda i, t: (i, halo_idx(i, t), a_blk)),
            pl.BlockSpec((1, CONV_HALO, CONV_CHANNELS), lambda i, t: (i, halo_idx(i, t), g_blk)),
            pl.BlockSpec((CONV_TAPS, CONV_CHANNELS), lambda i, t: (0, 0)),
            vec, vec, vec,
        ],
        out_specs=pl.BlockSpec((1, tt, CONV_CHANNELS), lambda i, t: (i, t, 0)),
        out_shape=jax.ShapeDtypeStruct((bsz, t_len, CONV_CHANNELS), BF16),
        scratch_shapes=[pltpu.VMEM((CONV_HALO + tt, CONV_CHANNELS), F32),
                        pltpu.VMEM((tt, CONV_CHANNELS), F32)],
        compiler_params=_params("arbitrary", "arbitrary"),
        name="conformer_conv",
    )(proj3, proj3, proj3, proj3, w, b, lg, lb)


def _outproj_body(x_ref, a_ref, c_ref, wa_ref, wc_ref, o_ref):
    o_ref[...] = (x_ref[...]
                  + jnp.dot(a_ref[...], wa_ref[...], preferred_element_type=F32)
                  + jnp.dot(c_ref[...], wc_ref[...], preferred_element_type=F32))


def _out_projection(x2, attn2, conv2, wa, wc, *, tm, tn):
    n = x2.shape[0]
    return pl.pallas_call(
        _outproj_body,
        grid=(n // tm, D_MODEL // tn),
        in_specs=[
            pl.BlockSpec((tm, tn), lambda i, j: (i, j)),
            pl.BlockSpec((tm, ATTN_WIDTH), lambda i, j: (i, 0)),
            pl.BlockSpec((tm, CONV_CHANNELS), lambda i, j: (i, 0)),
            pl.BlockSpec((ATTN_WIDTH, tn), lambda i, j: (0, j)),
            pl.BlockSpec((CONV_CHANNELS, tn), lambda i, j: (ATTN_WIDTH // CONV_CHANNELS, j)),
        ],
        out_specs=pl.BlockSpec((tm, tn), lambda i, j: (i, j)),
        out_shape=jax.ShapeDtypeStruct((n, D_MODEL), F32),
        compiler_params=_params("arbitrary", "arbitrary"),
        name="out_projection",
    )(x2, attn2, conv2, wa, wc)


def _ffn_up_body(x_ref, xh_ref, g_ref, wa_ref, wg_ref, cwa_ref, cwg_ref, cba_ref, cbg_ref, o_ref,
                 h_ref, ua_ref, ug_ref, *, tm, tiles_per_seq, rows):
    i = pl.program_id(0)

    @pl.when(pl.program_id(1) == 0)
    def _():
        h_ref[FFN_HALO:, :] = _rms_norm_rows(x_ref[...], g_ref[...]).astype(BF16)
        hh = _rms_norm_rows(xh_ref[...], g_ref[...])
        h_ref[0:FFN_HALO, :] = jnp.where(i % tiles_per_seq != 0, hh, 0.0).astype(BF16)

    def project(c):
        hc = h_ref[c * rows:(c + 1) * rows + FFN_HALO, :]
        ua_ref[c] = jnp.dot(hc, wa_ref[...], preferred_element_type=F32)
        ug_ref[c] = jnp.dot(hc, wg_ref[...], preferred_element_type=F32)

    def conv(u_ref, c, cw_ref, cb_ref):
        u = u_ref[c]
        y = cw_ref[FFN_CONV_TAPS - 1:FFN_CONV_TAPS, :] * u[FFN_HALO:]
        for s in range(1, FFN_CONV_TAPS):
            k = FFN_CONV_TAPS - 1 - s
            y = y + cw_ref[k:k + 1, :] * pltpu.roll(u, s, axis=0)[FFN_HALO:]
        return y + cb_ref[...]

    def finish(c):
        a = conv(ua_ref, c, cwa_ref, cba_ref)
        gate = conv(ug_ref, c, cwg_ref, cbg_ref)
        o_ref[c * rows:(c + 1) * rows, :] = (gate * _sigmoid(gate) * a).astype(o_ref.dtype)

    n_chunks = tm // rows
    project(0)
    for c in range(n_chunks):
        if c + 1 < n_chunks:
            project(c + 1)
        finish(c)


def _ffn_up(x2, g, w_up, cw, cb, *, tm, tn, t_len):
    n = x2.shape[0]
    nj = D_FF // tn
    hpt = tm // FFN_HALO
    rows = min(FFN_ROWS, tm)
    return pl.pallas_call(
        functools.partial(_ffn_up_body, tm=tm, tiles_per_seq=t_len // tm, rows=rows),
        grid=(n // tm, nj),
        in_specs=[
            pl.BlockSpec((tm, D_MODEL), lambda i, j: (i, 0)),
            pl.BlockSpec((FFN_HALO, D_MODEL), lambda i, j: (jnp.maximum(i * hpt - 1, 0), 0)),
            pl.BlockSpec((1, D_MODEL), lambda i, j: (0, 0)),
            pl.BlockSpec((D_MODEL, tn), lambda i, j: (0, j)),
            pl.BlockSpec((D_MODEL, tn), lambda i, j: (0, j + nj)),
            pl.BlockSpec((FFN_CONV_TAPS, tn), lambda i, j: (0, j)),
            pl.BlockSpec((FFN_CONV_TAPS, tn), lambda i, j: (0, j + nj)),
            pl.BlockSpec((1, tn), lambda i, j: (0, j)),
            pl.BlockSpec((1, tn), lambda i, j: (0, j + nj)),
        ],
        out_specs=pl.BlockSpec((tm, tn), lambda i, j: (i, j)),
        out_shape=jax.ShapeDtypeStruct((n, D_FF), BF16),
        scratch_shapes=[
            pltpu.VMEM((FFN_HALO + tm, D_MODEL), BF16),
            pltpu.VMEM((tm // rows, FFN_HALO + rows, tn), F32),
            pltpu.VMEM((tm // rows, FFN_HALO + rows, tn), F32),
        ],
        compiler_params=_params("arbitrary", "arbitrary"),
        name="ffn_up",
    )(x2, x2, g, w_up, w_up, cw, cw, cb, cb)


def _ffn_down_body(act_ref, w_ref, x_ref, o_ref):
    o_ref[...] = x_ref[...] + jnp.dot(act_ref[...], w_ref[...], preferred_element_type=F32)


def _ffn_down(act, w_down, x2, *, tm, tn):
    n = x2.shape[0]
    return pl.pallas_call(
        _ffn_down_body,
        grid=(n // tm, D_MODEL // tn),
        in_specs=[
            pl.BlockSpec((tm, D_FF), lambda i, j: (i, 0)),
            pl.BlockSpec((D_FF, tn), lambda i, j: (0, j)),
            pl.BlockSpec((tm, tn), lambda i, j: (i, j)),
        ],
        out_specs=pl.BlockSpec((tm, tn), lambda i, j: (i, j)),
        out_shape=jax.ShapeDtypeStruct((n, D_MODEL), F32),
        compiler_params=_params("arbitrary", "arbitrary"),
        name="ffn_down",
    )(act, w_down, x2)


def _final_norm_body(x_ref, g_ref, o_ref):
    o_ref[...] = _rms_norm_rows(x_ref[...], g_ref[...])


def _final_norm(x2, g, *, tm):
    n = x2.shape[0]
    return pl.pallas_call(
        _final_norm_body,
        grid=(n // tm,),
        in_specs=[pl.BlockSpec((tm, D_MODEL), lambda i: (i, 0)), pl.BlockSpec((1, D_MODEL), lambda i: (0, 0))],
        out_specs=pl.BlockSpec((tm, D_MODEL), lambda i: (i, 0)),
        out_shape=jax.ShapeDtypeStruct((n, D_MODEL), F32),
        compiler_params=_params("arbitrary"),
        name="final_norm",
    )(x2, g)


def _in_weight_layout(w_in_l):
    q_cols = np.arange(ATTN_WIDTH)
    kv0 = ATTN_WIDTH
    gate0 = kv0 + KV_COLS
    conv0 = gate0 + GATE_COLS
    conv_cols = conv0 + np.arange(2 * CONV_CHANNELS)
    kv_cols = []
    for br in range(N_BRANCH):
        for h in range(N_KV_HEADS):
            for kv in range(2):
                kv_cols.append(kv0 + ((br * 2 + kv) * N_KV_HEADS + h) * HEAD_DIM + np.arange(HEAD_DIM))
    perm = np.concatenate([q_cols, conv_cols] + kv_cols)
    w_main = jnp.take(w_in_l, jnp.asarray(perm, jnp.int32), axis=1).astype(BF16)
    gsrc = np.zeros((N_KV_HEADS, N_BRANCH * GQA_GROUP), np.int32)
    for h in range(N_KV_HEADS):
        for br in range(N_BRANCH):
            for g in range(GQA_GROUP):
                gsrc[h, br * GQA_GROUP + g] = gate0 + (h * GQA_GROUP + g) * N_BRANCH + br
    wg = jnp.take(w_in_l, jnp.asarray(gsrc.reshape(-1)), axis=1).reshape(D_MODEL, N_KV_HEADS, -1)
    wg = jnp.pad(wg, ((0, 0), (0, 0), (0, LANES - N_BRANCH * GQA_GROUP)))
    return w_main, wg.reshape(D_MODEL, N_KV_HEADS * LANES).astype(BF16)


def _pick(n, prefs):
    for p in prefs:
        if n % p == 0:
            return p
    return n


def kernel(x, rel_bias, mix_norm_g, w_in, cmp_pos, cmp_w1, cmp_w2, conv_w, conv_b, conv_ln_g, conv_ln_b,
           w_out, ffn_norm_g, w_up, ffn_conv_w, ffn_conv_b, w_down, final_norm_g):
    bsz, t_len, _ = x.shape
    depth = w_in.shape[0]
    n = bsz * t_len
    assert t_len % 256 == 0 and t_len // SLC_LEN <= HEAD_DIM
    tq = 256
    tm = _pick(t_len, (1024, 512, 256))
    nc = t_len // CMP_STRIDE

    band, cmpb = _bias_tiles(rel_bias, t_len, tq)
    col_scale = jnp.concatenate([jnp.full((1, ATTN_WIDTH), Q_SCALE, F32),
                                 jnp.ones((1, PROJ_COLS - ATTN_WIDTH), F32)], axis=1)
    x2 = x.reshape(n, D_MODEL)
    for l in range(depth):
        w_main, wg = _in_weight_layout(w_in[l])
        proj, gates = _in_projection(x2, mix_norm_g[l][None, :], w_main, wg, col_scale, tm=tm, tn=768)
        proj3 = proj.reshape(bsz, t_len, PROJ_COLS)
        gates3 = gates.reshape(bsz, t_len, N_KV_HEADS * LANES)

        posx, w1big, w2big = _compress_weight_layout(cmp_pos[l], cmp_w1[l], cmp_w2[l])
        tokl = proj3[:, :, COL_KV:COL_KV + CMP_WIDTH].reshape(bsz, nc, CMP_STRIDE, CMP_WIDTH).transpose(0, 2, 1, 3)
        kvc, kvct = _compress(tokl, posx, w1big, w2big)

        attn = _attention(proj3, kvc, kvct, band, cmpb, gates3, tq=tq)
        conv = _conformer(proj3, conv_w[l], conv_b[l][None, :], conv_ln_g[l][None, :],
                          conv_ln_b[l][None, :], tt=256)
        w_o = w_out[l].astype(BF16)
        x2 = _out_projection(x2, attn.reshape(n, ATTN_WIDTH), conv.reshape(n, CONV_CHANNELS),
                             w_o, w_o, tm=tm, tn=1024)

        act = _ffn_up(x2, ffn_norm_g[l][None, :], w_up[l].astype(BF16), ffn_conv_w[l],
                      ffn_conv_b[l][None, :], tm=tm, tn=512, t_len=t_len)
        x2 = _ffn_down(act, w_down[l].astype(BF16), x2, tm=tm, tn=512)
    x2 = _final_norm(x2, final_norm_g[None, :], tm=_pick(n, (512, 256)))
    return x2.reshape(bsz, t_len, D_MODEL)
```

```python
import functools
import math

import numpy as np
import jax
import jax.numpy as jnp
from jax import lax
from jax.experimental import pallas as pl
from jax.experimental.pallas import tpu as pltpu

F32 = jnp.float32
BF16 = jnp.bfloat16

D_MODEL = 2048
HEAD_DIM = 64
N_KV_HEADS = 4
GQA_GROUP = 4
N_Q_HEADS = N_KV_HEADS * GQA_GROUP
N_BRANCH = 3
ATTN_WIDTH = N_Q_HEADS * HEAD_DIM
CMP_LEN = 32
CMP_STRIDE = 16
SLC_LEN = 64
SLC_TOPK = 16
WINDOW = 512
CONV_CHANNELS = D_MODEL - ATTN_WIDTH
CONV_TAPS = 31
D_FF = 5632
FFN_CONV_TAPS = 3
N_BUCKETS = 32
MAX_DISTANCE = 128
NORM_EPS = 1e-6

KV_PAIR = 2 * HEAD_DIM
KV_COLS = N_BRANCH * N_KV_HEADS * KV_PAIR
GATE_COLS = N_BRANCH * N_Q_HEADS
COL_Q = 0
COL_CONV_A = ATTN_WIDTH
COL_CONV_G = COL_CONV_A + CONV_CHANNELS
COL_KV = COL_CONV_G + CONV_CHANNELS
PROJ_COLS = COL_KV + KV_COLS

MASK_VALUE = -1e30
BLOCK_PENALTY = -1e9
KEY_BIG = 1e30
KEY_TAKEN = -3e38
LOG2E = math.log2(math.e)
Q_SCALE = HEAD_DIM ** -0.5 * LOG2E

LANES = 128
SUBLANES = 8
CONV_ROWS = 128
FFN_ROWS = 1024
VMEM_LIMIT = 56 * 1024 * 1024
CONV_HALO = 32
FFN_HALO = 16


def _t5_bucket_last_distance():
    n = np.arange(0, 4 * MAX_DISTANCE, dtype=np.int64)
    max_exact = N_BUCKETS // 2
    nf = np.maximum(n, 1).astype(np.float64)
    large = max_exact + np.floor(np.log(nf / max_exact) / math.log(MAX_DISTANCE / max_exact)
                                 * (N_BUCKETS - max_exact)).astype(np.int64)
    large = np.minimum(large, N_BUCKETS - 1)
    bucket = np.where(n < max_exact, n, large)
    last = []
    for b in range(N_BUCKETS - 1):
        idx = np.nonzero(bucket == b)[0]
        last.append(int(idx.max()) if idx.size else None)
    return last


_BUCKET_LAST = _t5_bucket_last_distance()


def _params(*sem):
    return pltpu.CompilerParams(dimension_semantics=sem, vmem_limit_bytes=VMEM_LIMIT)


def _sigmoid(x):
    return jax.nn.sigmoid(x)


def _rms_norm_rows(x, g):
    ms = jnp.mean(x * x, axis=-1, keepdims=True)
    return x * lax.rsqrt(ms + NORM_EPS) * g


def _bias_from_distance(dist, tab_ref, head):
    c_far = tab_ref[N_BUCKETS - 1, head]
    val = jnp.zeros(dist.shape, F32)
    for b in range(N_BUCKETS - 2, -1, -1):
        if _BUCKET_LAST[b] is None:
            continue
        val = jnp.where(dist <= _BUCKET_LAST[b], tab_ref[b, head] - c_far, val)
    return val


def _band_bias_body(tab_ref, o_ref, *, tq):
    head = pl.program_id(0) * GQA_GROUP + pl.program_id(1)
    shape = (WINDOW + tq, tq)
    dist = (lax.broadcasted_iota(jnp.int32, shape, 1) + WINDOW
            - lax.broadcasted_iota(jnp.int32, shape, 0))
    val = _bias_from_distance(dist, tab_ref, head) * LOG2E
    val = jnp.where(dist >= 0, jnp.where(dist < WINDOW, val, MASK_VALUE), MASK_VALUE)
    o_ref[0] = val


def _cmp_bias_body(tab_ref, o_ref, *, tq, nc):
    head = pl.program_id(0) * GQA_GROUP + pl.program_id(1)
    shape = (2 * nc, tq)
    r = lax.broadcasted_iota(jnp.int32, shape, 0)
    i = lax.broadcasted_iota(jnp.int32, shape, 1)
    dist = i - (r - nc) * CMP_STRIDE - (CMP_LEN - 1)
    val = _bias_from_distance(dist, tab_ref, head) * LOG2E
    o_ref[0] = jnp.where(dist >= 0, val, MASK_VALUE)


def _bias_tiles(rel_bias, t_len, tq):
    nc = t_len // CMP_STRIDE
    rows = GQA_GROUP * tq
    smem = pl.BlockSpec(memory_space=pltpu.SMEM)
    band = pl.pallas_call(
        functools.partial(_band_bias_body, tq=tq),
        grid=(N_KV_HEADS, GQA_GROUP),
        in_specs=[smem],
        out_specs=pl.BlockSpec((1, WINDOW + tq, tq), lambda h, g: (h, 0, g)),
        out_shape=jax.ShapeDtypeStruct((N_KV_HEADS, WINDOW + tq, rows), F32),
        compiler_params=_params("arbitrary", "arbitrary"),
        name="band_bias",
    )(rel_bias)
    cmpb = pl.pallas_call(
        functools.partial(_cmp_bias_body, tq=tq, nc=nc),
        grid=(N_KV_HEADS, GQA_GROUP),
        in_specs=[smem],
        out_specs=pl.BlockSpec((1, 2 * nc, tq), lambda h, g: (h, 0, g)),
        out_shape=jax.ShapeDtypeStruct((N_KV_HEADS, 2 * nc, rows), F32),
        compiler_params=_params("arbitrary", "arbitrary"),
        name="cmp_bias",
    )(rel_bias)
    return band, cmpb


def _inproj_body(x_ref, g_ref, w_ref, wg_ref, cs_ref, o_ref, gate_ref, h_ref):
    @pl.when(pl.program_id(1) == 0)
    def _():
        h = _rms_norm_rows(x_ref[...], g_ref[...]).astype(BF16)
        h_ref[...] = h
        gate_ref[...] = _sigmoid(jnp.dot(h, wg_ref[...], preferred_element_type=F32))

    acc = jnp.dot(h_ref[...], w_ref[...], preferred_element_type=F32)
    o_ref[...] = (acc * cs_ref[...]).astype(o_ref.dtype)


def _in_projection(x2, g, w, wg, col_scale, *, layer, tm, tn):
    n = x2.shape[0]
    gcols = wg.shape[-1]
    return pl.pallas_call(
        _inproj_body,
        grid=(n // tm, PROJ_COLS // tn),
        in_specs=[
            pl.BlockSpec((tm, D_MODEL), lambda i, j: (i, 0)),
            pl.BlockSpec((1, D_MODEL), lambda i, j: (0, 0)),
            pl.BlockSpec((None, D_MODEL, tn), lambda i, j: (layer, 0, j)),
            pl.BlockSpec((None, D_MODEL, gcols), lambda i, j: (layer, 0, 0)),
            pl.BlockSpec((1, tn), lambda i, j: (0, j)),
        ],
        out_specs=[
            pl.BlockSpec((tm, tn), lambda i, j: (i, j)),
            pl.BlockSpec((tm, gcols), lambda i, j: (i, 0)),
        ],
        out_shape=[
            jax.ShapeDtypeStruct((n, PROJ_COLS), BF16),
            jax.ShapeDtypeStruct((n, gcols), F32),
        ],
        scratch_shapes=[pltpu.VMEM((tm, D_MODEL), BF16)],
        compiler_params=_params("arbitrary", "arbitrary"),
        name="in_projection",
    )(x2, g, w, wg, col_scale)


CMP_WIDTH = N_KV_HEADS * KV_PAIR


def _compress_body(x_ref, pos_ref, w1_ref, w2_ref, o_ref, ot_ref, top_ref, bot_ref, *, nc):
    l = pl.program_id(1)

    @pl.when(l == 0)
    def _():
        top_ref[...] = jnp.zeros_like(top_ref)
        bot_ref[...] = jnp.zeros_like(bot_ref)

    x = x_ref[0, 0].astype(F32)
    top_ref[...] += jnp.dot((x + pos_ref[0, 0]).astype(BF16), w1_ref[0, 0], preferred_element_type=F32)
    bot_ref[...] += jnp.dot((x + pos_ref[0, 1]).astype(BF16), w1_ref[0, 1], preferred_element_type=F32)

    @pl.when(l == pl.num_programs(1) - 1)
    def _():
        pre = top_ref[...] + pltpu.roll(bot_ref[...], nc - 1, axis=0)
        act = pre * _sigmoid(pre)
        out = jnp.dot(act.astype(BF16), w2_ref[...], preferred_element_type=F32)
        for h in range(N_KV_HEADS):
            kv = out[:, h * KV_PAIR:(h + 1) * KV_PAIR]
            o_ref[0, h] = kv.astype(o_ref.dtype)
            ot_ref[0, h] = kv.T.astype(ot_ref.dtype)


def _compress(tokl, posx, w1big, w2big):
    b, _, nc, _ = tokl.shape
    return pl.pallas_call(
        functools.partial(_compress_body, nc=nc),
        grid=(b, CMP_STRIDE),
        in_specs=[
            pl.BlockSpec((1, 1, nc, CMP_WIDTH), lambda i, l: (i, l, 0, 0)),
            pl.BlockSpec((1, 2, 1, CMP_WIDTH), lambda i, l: (l, 0, 0, 0)),
            pl.BlockSpec((1, 2, CMP_WIDTH, CMP_WIDTH), lambda i, l: (l, 0, 0, 0)),
            pl.BlockSpec((CMP_WIDTH, CMP_WIDTH), lambda i, l: (0, 0)),
        ],
        out_specs=[
            pl.BlockSpec((1, N_KV_HEADS, nc, KV_PAIR), lambda i, l: (i, 0, 0, 0)),
            pl.BlockSpec((1, N_KV_HEADS, KV_PAIR, nc), lambda i, l: (i, 0, 0, 0)),
        ],
        out_shape=[
            jax.ShapeDtypeStruct((b, N_KV_HEADS, nc, KV_PAIR), BF16),
            jax.ShapeDtypeStruct((b, N_KV_HEADS, KV_PAIR, nc), BF16),
        ],
        scratch_shapes=[pltpu.VMEM((nc, CMP_WIDTH), F32), pltpu.VMEM((nc, CMP_WIDTH), F32)],
        compiler_params=_params("arbitrary", "arbitrary"),
        name="compress",
    )(tokl, posx, w1big, w2big)


def _compress_weight_layout(pos, w1, w2):
    def block_diag(blocks):
        rows = []
        for h in range(N_KV_HEADS):
            for kv in range(2):
                off = (h * 2 + kv) * HEAD_DIM
                pad = [(0, 0)] * (blocks.ndim - 2) + [(off, CMP_WIDTH - HEAD_DIM - off)]
                rows.append(jnp.pad(blocks[kv], pad))
        return jnp.concatenate(rows, axis=-2).astype(BF16)

    w1r = w1.reshape(2, 2, CMP_STRIDE, HEAD_DIM, HEAD_DIM)
    w1big = block_diag(w1r.transpose(0, 2, 1, 3, 4))
    w2big = block_diag(w2)
    posr = pos.reshape(2, 2, CMP_STRIDE, HEAD_DIM).transpose(2, 1, 0, 3)
    posx = jnp.broadcast_to(posr[:, :, None], (CMP_STRIDE, 2, N_KV_HEADS, 2, HEAD_DIM))
    return posx.reshape(CMP_STRIDE, 2, 1, CMP_WIDTH), w1big, w2big


VT_ROWS = HEAD_DIM + 16


def _attn_body(q_ref, kvc_ref, kvct_ref, kvs_ref, kvw_ref, bt_ref, bc_ref, gate_ref, o_ref,
               kp_ref, vst_ref, vwt_ref, key_ref, m_ref, acc_ref, sa_ref, sb_ref, ocmp_ref, qsel_ref, qwin_ref,
               *, tq, t_len):
    qi = pl.program_id(2)
    nc = t_len // CMP_STRIDE
    ns = t_len // SLC_LEN
    rows = GQA_GROUP * tq
    blocks_per_tile = tq // SLC_LEN

    @pl.when(qi == 0)
    def _():
        k = kvs_ref[0][:, :HEAD_DIM]
        blk = lax.broadcasted_iota(jnp.int32, (t_len, HEAD_DIM), 0) // SLC_LEN
        col = lax.broadcasted_iota(jnp.int32, (t_len, HEAD_DIM), 1)
        onehot = jnp.where(blk == col, 1.0, 0.0).astype(BF16)
        kp_ref[...] = jnp.concatenate([k, onehot], axis=1)
        ones_rows = jnp.where(lax.broadcasted_iota(jnp.int32, (VT_ROWS - HEAD_DIM, tq), 0) == 0, 1.0, 0.0)
        for c in range(t_len // tq):
            st = kvs_ref[0, c * tq:(c + 1) * tq, :].astype(F32).T[HEAD_DIM:]
            vst_ref[c] = jnp.concatenate([st, ones_rows], axis=0).astype(BF16)
            wt = kvw_ref[0, c * tq:(c + 1) * tq, :].astype(F32).T[HEAD_DIM:]
            vwt_ref[c] = jnp.concatenate([wt, ones_rows], axis=0).astype(BF16)

    q_t = q_ref[0].astype(F32).T
    q4_t = jnp.concatenate([q_t[g * HEAD_DIM:(g + 1) * HEAD_DIM] for g in range(GQA_GROUP)],
                           axis=1).astype(BF16)
    qwin_ref[...] = jnp.concatenate([q4_t, jnp.zeros((HEAD_DIM, rows), BF16)], axis=0)
    qsel_ref[0:HEAD_DIM, :] = q4_t
    kw_ref = kvw_ref.at[0]
    win, sel = 0, 1

    def cmp_scores():
        bias_row = pl.multiple_of(nc - qi * (tq // CMP_STRIDE), tq // CMP_STRIDE)
        return (jnp.dot(kvc_ref[0, 0], qwin_ref[...], preferred_element_type=F32)
                + bc_ref[0, pl.ds(bias_row, nc), :])

    def cmp_finish(s):
        m = jnp.max(s, axis=0, keepdims=True)
        p = jnp.exp2(s - m)
        l = jnp.sum(p, axis=0, keepdims=True)
        pn = p * jnp.where(m > 0.5 * MASK_VALUE, 1.0 / l, 0.0)
        ocmp_ref[...] = jnp.dot(kvct_ref[0, 0, HEAD_DIM:, :], pn.astype(BF16),
                                preferred_element_type=F32)

        ps = pn[:, 0:tq] + pn[:, tq:2 * tq] + pn[:, 2 * tq:3 * tq] + pn[:, 3 * tq:4 * tq]
        sj = lax.broadcasted_iota(jnp.int32, (HEAD_DIM, nc), 0) * SLC_LEN
        ci = lax.broadcasted_iota(jnp.int32, (HEAD_DIM, nc), 1) * CMP_STRIDE
        overlap = jnp.where(ci < sj + SLC_LEN, jnp.where(ci + CMP_LEN > sj, 1.0, 0.0), 0.0).astype(BF16)
        p_hi = ps.astype(BF16)
        r_hi = ps - p_hi.astype(F32)
        p_md = r_hi.astype(BF16)
        p_lo = (r_hi - p_md.astype(F32)).astype(BF16)
        imp = (jnp.dot(overlap, p_hi, preferred_element_type=F32)
               + jnp.dot(overlap, p_md, preferred_element_type=F32)
               + jnp.dot(overlap, p_lo, preferred_element_type=F32))
        t = qi * tq + lax.broadcasted_iota(jnp.int32, (HEAD_DIM, tq), 1)
        blk = lax.broadcasted_iota(jnp.int32, (HEAD_DIM, tq), 0)
        cur = t // SLC_LEN
        key = jnp.where(blk == 0, KEY_BIG, jnp.where(blk == cur, KEY_BIG, jnp.where(blk == cur - 1, KEY_BIG, imp)))
        key_ref[...] = jnp.where(blk * SLC_LEN <= t, key, -KEY_BIG)

    def select_blocks():
        key = key_ref[...]
        blk = lax.broadcasted_iota(jnp.int32, (HEAD_DIM, tq), 0)
        pen = jnp.full((HEAD_DIM, tq), BLOCK_PENALTY, F32)
        for _ in range(min(SLC_TOPK, ns)):
            top = jnp.max(key, axis=0, keepdims=True)
            first = jnp.min(jnp.where(key == top, blk, HEAD_DIM), axis=0, keepdims=True)
            hit = blk == first
            pen = jnp.where(hit, 0.0, pen)
            key = jnp.where(hit, KEY_TAKEN, key)
        qsel_ref[HEAD_DIM:, :] = jnp.concatenate([pen.astype(BF16)] * GQA_GROUP, axis=1)

    def scores(qx_ref, k_ref, kj, bias_off):
        start = pl.multiple_of(kj * tq, tq)
        sc = jnp.dot(k_ref[pl.ds(start, tq), :], qx_ref[...], preferred_element_type=F32)
        if bias_off is not None:
            sc = sc + bt_ref[0, bias_off:bias_off + tq, :]
        return sc

    def consume(sc, vt_ref, kj, br, st, first):
        m_tile = jnp.max(sc, axis=0, keepdims=True)
        if first:
            m_new = m_tile
            pr = jnp.exp2(sc - m_new).astype(BF16)
            acc_ref[br, st] = jnp.dot(vt_ref[kj], pr, preferred_element_type=F32)
        else:
            m_old = m_ref[br, st]
            m_new = jnp.maximum(m_old, m_tile)
            alpha = jnp.exp2(m_old - m_new)
            pr = jnp.exp2(sc - m_new).astype(BF16)
            acc_ref[br, st] = alpha * acc_ref[br, st] + jnp.dot(vt_ref[kj], pr, preferred_element_type=F32)
        m_ref[br, st] = m_new

    def finish(br):
        m0 = m_ref[br, 0]
        m1 = m_ref[br, 1]
        m_all = jnp.maximum(m0, m1)
        acc = jnp.exp2(m0 - m_all) * acc_ref[br, 0] + jnp.exp2(m1 - m_all) * acc_ref[br, 1]
        return acc[0:HEAD_DIM] * (1.0 / acc[HEAD_DIM:HEAD_DIM + 1, :])

    def combine():
        o_win = finish(win)
        o_sel = finish(sel)
        o_cmp = ocmp_ref[...]
        gates_t = gate_ref[0].T
        combs = []
        for g in range(GQA_GROUP):
            sl = slice(g * tq, (g + 1) * tq)
            combs.append(gates_t[g:g + 1, :] * o_cmp[:, sl]
                         + gates_t[GQA_GROUP + g:GQA_GROUP + g + 1, :] * o_sel[:, sl]
                         + gates_t[2 * GQA_GROUP + g:2 * GQA_GROUP + g + 1, :] * o_win[:, sl])
        outs = [jnp.concatenate(combs[p:p + 2], axis=0).T for p in range(0, GQA_GROUP, 2)]
        o_ref[0] = jnp.concatenate(outs, axis=1).astype(o_ref.dtype)

    @pl.when(qi >= 2)
    def _():
        sc_cmp = cmp_scores()
        sw0 = scores(qwin_ref, kw_ref, qi, WINDOW)
        cmp_finish(sc_cmp)
        sw1 = scores(qwin_ref, kw_ref, qi - 1, WINDOW - tq)
        consume(sw0, vwt_ref, qi, win, 0, True)
        sw2 = scores(qwin_ref, kw_ref, qi - 2, WINDOW - 2 * tq)
        consume(sw1, vwt_ref, qi - 1, win, 1, True)
        select_blocks()
        consume(sw2, vwt_ref, qi - 2, win, 0, False)

        ss0 = scores(qsel_ref, kp_ref, qi, WINDOW)
        ss1 = scores(qsel_ref, kp_ref, qi - 1, WINDOW - tq)
        consume(ss0, vst_ref, qi, sel, 0, True)
        sa_ref[...] = scores(qsel_ref, kp_ref, 0, None)
        consume(ss1, vst_ref, qi - 1, sel, 1, True)
        n_far = qi - 1

        def far_pair(pi, carry):
            sb_ref[...] = scores(qsel_ref, kp_ref, 2 * pi + 1, None)
            consume(sa_ref[...], vst_ref, 2 * pi, sel, 1, False)
            sa_ref[...] = scores(qsel_ref, kp_ref, jnp.minimum(2 * pi + 2, n_far - 1), None)
            consume(sb_ref[...], vst_ref, 2 * pi + 1, sel, 0, False)
            return carry

        lax.fori_loop(0, n_far // 2, far_pair, 0)

        @pl.when(n_far % 2 == 1)
        def _():
            consume(sa_ref[...], vst_ref, n_far - 1, sel, 1, False)

        combine()

    @pl.when(qi < 2)
    def _():
        cmp_finish(cmp_scores())
        select_blocks()
        sw0 = scores(qwin_ref, kw_ref, qi, WINDOW)
        ss0 = scores(qsel_ref, kp_ref, qi, WINDOW)
        consume(sw0, vwt_ref, qi, win, 0, True)
        consume(ss0, vst_ref, qi, sel, 0, True)
        for br in (win, sel):
            m_ref[br, 1] = jnp.full((1, rows), MASK_VALUE, F32)
            acc_ref[br, 1] = jnp.zeros((VT_ROWS, rows), F32)

        @pl.when(qi == 1)
        def _():
            sw1 = scores(qwin_ref, kw_ref, 0, WINDOW - tq)
            ss1 = scores(qsel_ref, kp_ref, 0, WINDOW - tq)
            consume(sw1, vwt_ref, 0, win, 1, False)
            consume(ss1, vst_ref, 0, sel, 1, False)

        combine()


def _attention(proj3, kvc, kvct, band, cmpb, gates3, *, tq):
    b, t_len, _ = proj3.shape
    assert WINDOW == 2 * tq
    nc = t_len // CMP_STRIDE
    rows = GQA_GROUP * tq
    qw = GQA_GROUP * HEAD_DIM
    slc_blk = COL_KV // KV_PAIR + N_KV_HEADS
    win_blk = COL_KV // KV_PAIR + 2 * N_KV_HEADS
    return pl.pallas_call(
        functools.partial(_attn_body, tq=tq, t_len=t_len),
        grid=(b, N_KV_HEADS, t_len // tq),
        in_specs=[
            pl.BlockSpec((1, tq, qw), lambda i, h, q: (i, q, h)),
            pl.BlockSpec((1, 1, nc, KV_PAIR), lambda i, h, q: (i, h, 0, 0)),
            pl.BlockSpec((1, 1, KV_PAIR, nc), lambda i, h, q: (i, h, 0, 0)),
            pl.BlockSpec((1, t_len, KV_PAIR), lambda i, h, q: (i, 0, slc_blk + h)),
            pl.BlockSpec((1, t_len, KV_PAIR), lambda i, h, q: (i, 0, win_blk + h)),
            pl.BlockSpec((1, WINDOW + tq, rows), lambda i, h, q: (h, 0, 0)),
            pl.BlockSpec((1, 2 * nc, rows), lambda i, h, q: (h, 0, 0)),
            pl.BlockSpec((1, tq, LANES), lambda i, h, q: (i, q, h)),
        ],
        out_specs=pl.BlockSpec((1, tq, qw), lambda i, h, q: (i, q, h)),
        out_shape=jax.ShapeDtypeStruct((b, t_len, ATTN_WIDTH), BF16),
        scratch_shapes=[
            pltpu.VMEM((t_len, KV_PAIR), BF16),
            pltpu.VMEM((t_len // tq, VT_ROWS, tq), BF16),
            pltpu.VMEM((t_len // tq, VT_ROWS, tq), BF16),
            pltpu.VMEM((HEAD_DIM, tq), F32),
            pltpu.VMEM((2, 2, 1, rows), F32),
            pltpu.VMEM((2, 2, VT_ROWS, rows), F32),
            pltpu.VMEM((tq, rows), F32),
            pltpu.VMEM((tq, rows), F32),
            pltpu.VMEM((HEAD_DIM, rows), F32),
            pltpu.VMEM((KV_PAIR, rows), BF16),
            pltpu.VMEM((KV_PAIR, rows), BF16),
        ],
        compiler_params=_params("arbitrary", "arbitrary", "arbitrary"),
        name="nsa_attention",
    )(proj3, kvc, kvct, proj3, proj3, band, cmpb, gates3)


def _conformer_body(a_ref, g_ref, ah_ref, gh_ref, w_ref, b_ref, lg_ref, lb_ref, o_ref, u_ref, y_ref, *, tt):
    ti = pl.program_id(1)
    u_ref[CONV_HALO:, :] = a_ref[0].astype(F32) * _sigmoid(g_ref[0].astype(F32))
    halo = ah_ref[0].astype(F32) * _sigmoid(gh_ref[0].astype(F32))
    u_ref[0:CONV_HALO, :] = jnp.where(ti > 0, halo, 0.0)
    base = CONV_HALO - (CONV_TAPS - 1)
    ext = CONV_ROWS + CONV_HALO
    for cb in range(CONV_CHANNELS // LANES):
        cs = slice(cb * LANES, (cb + 1) * LANES)
        wblk = w_ref[:, cs]
        for rb in range(tt // CONV_ROWS):
            ublk = u_ref[rb * CONV_ROWS:rb * CONV_ROWS + ext, cs]
            acc = jnp.zeros((CONV_ROWS, LANES), F32)
            for r in range(SUBLANES):
                ur = ublk if r == 0 else pltpu.roll(ublk, ext - r, axis=0)
                for a in range(CONV_HALO // SUBLANES + 1):
                    k = SUBLANES * a + r - base
                    if 0 <= k < CONV_TAPS:
                        acc = acc + wblk[k:k + 1, :] * ur[SUBLANES * a:SUBLANES * a + CONV_ROWS]
            y_ref[rb * CONV_ROWS:(rb + 1) * CONV_ROWS, cs] = acc
    acc = y_ref[...] + b_ref[...]
    mu = jnp.mean(acc, axis=-1, keepdims=True)
    xc = acc - mu
    var = jnp.mean(xc * xc, axis=-1, keepdims=True)
    y = xc * lax.rsqrt(var + NORM_EPS) * lg_ref[...] + lb_ref[...]
    o_ref[0] = (y * _sigmoid(y)).astype(o_ref.dtype)


def _conformer(proj3, w, b, lg, lb, *, tt):
    bsz, t_len, _ = proj3.shape
    a_blk = COL_CONV_A // CONV_CHANNELS
    g_blk = COL_CONV_G // CONV_CHANNELS
    hpt = tt // CONV_HALO
    halo_idx = lambda i, t: jnp.maximum(t * hpt - 1, 0)
    vec = pl.BlockSpec((1, CONV_CHANNELS), lambda i, t: (0, 0))
    return pl.pallas_call(
        functools.partial(_conformer_body, tt=tt),
        grid=(bsz, t_len // tt),
        in_specs=[
            pl.BlockSpec((1, tt, CONV_CHANNELS), lambda i, t: (i, t, a_blk)),
            pl.BlockSpec((1, tt, CONV_CHANNELS), lambda i, t: (i, t, g_blk)),
            pl.BlockSpec((1, CONV_HALO, CONV_CHANNELS), lambda i, t: (i, halo_idx(i, t), a_blk)),
            pl.BlockSpec((1, CONV_HALO, CONV_CHANNELS), lambda i, t: (i, halo_idx(i, t), g_blk)),
            pl.BlockSpec((CONV_TAPS, CONV_CHANNELS), lambda i, t: (0, 0)),
            vec, vec, vec,
        ],
        out_specs=pl.BlockSpec((1, tt, CONV_CHANNELS), lambda i, t: (i, t, 0)),
        out_shape=jax.ShapeDtypeStruct((bsz, t_len, CONV_CHANNELS), BF16),
        scratch_shapes=[pltpu.VMEM((CONV_HALO + tt, CONV_CHANNELS), F32),
                        pltpu.VMEM((tt, CONV_CHANNELS), F32)],
        compiler_params=_params("arbitrary", "arbitrary"),
        name="conformer_conv",
    )(proj3, proj3, proj3, proj3, w, b, lg, lb)


def _outproj_body(x_ref, a_ref, c_ref, wa_ref, wc_ref, o_ref):
    o_ref[...] = (x_ref[...]
                  + jnp.dot(a_ref[...], wa_ref[...], preferred_element_type=F32)
                  + jnp.dot(c_ref[...], wc_ref[...], preferred_element_type=F32))


def _out_projection(x2, attn2, conv2, w_out, *, layer, tm, tn):
    n = x2.shape[0]
    return pl.pallas_call(
        _outproj_body,
        grid=(n // tm, D_MODEL // tn),
        in_specs=[
            pl.BlockSpec((tm, tn), lambda i, j: (i, j)),
            pl.BlockSpec((tm, ATTN_WIDTH), lambda i, j: (i, 0)),
            pl.BlockSpec((tm, CONV_CHANNELS), lambda i, j: (i, 0)),
            pl.BlockSpec((None, ATTN_WIDTH, tn), lambda i, j: (layer, 0, j)),
            pl.BlockSpec((None, CONV_CHANNELS, tn), lambda i, j: (layer, ATTN_WIDTH // CONV_CHANNELS, j)),
        ],
        out_specs=pl.BlockSpec((tm, tn), lambda i, j: (i, j)),
        out_shape=jax.ShapeDtypeStruct((n, D_MODEL), F32),
        compiler_params=_params("arbitrary", "arbitrary"),
        name="out_projection",
    )(x2, attn2, conv2, w_out, w_out)


def _ffn_up_body(x_ref, xh_ref, g_ref, wa_ref, wg_ref, cwa_ref, cwg_ref, cba_ref, cbg_ref, o_ref,
                 h_ref, ua_ref, ug_ref, *, tm, tiles_per_seq, rows):
    i = pl.program_id(0)

    @pl.when(pl.program_id(1) == 0)
    def _():
        h_ref[FFN_HALO:, :] = _rms_norm_rows(x_ref[...], g_ref[...]).astype(BF16)
        hh = _rms_norm_rows(xh_ref[...], g_ref[...])
        h_ref[0:FFN_HALO, :] = jnp.where(i % tiles_per_seq != 0, hh, 0.0).astype(BF16)

    def project(c):
        hc = h_ref[c * rows:(c + 1) * rows + FFN_HALO, :]
        ua_ref[c] = jnp.dot(hc, wa_ref[...], preferred_element_type=F32)
        ug_ref[c] = jnp.dot(hc, wg_ref[...], preferred_element_type=F32)

    def conv(u_ref, c, cw_ref, cb_ref):
        u = u_ref[c]
        y = cw_ref[FFN_CONV_TAPS - 1:FFN_CONV_TAPS, :] * u[FFN_HALO:]
        for s in range(1, FFN_CONV_TAPS):
            k = FFN_CONV_TAPS - 1 - s
            y = y + cw_ref[k:k + 1, :] * pltpu.roll(u, s, axis=0)[FFN_HALO:]
        return y + cb_ref[...]

    def finish(c):
        a = conv(ua_ref, c, cwa_ref, cba_ref)
        gate = conv(ug_ref, c, cwg_ref, cbg_ref)
        o_ref[c * rows:(c + 1) * rows, :] = (gate * _sigmoid(gate) * a).astype(o_ref.dtype)

    n_chunks = tm // rows
    project(0)
    for c in range(n_chunks):
        if c + 1 < n_chunks:
            project(c + 1)
        finish(c)


def _ffn_up(x2, g, w_up, cw, cb, *, layer, tm, tn, t_len):
    n = x2.shape[0]
    nj = D_FF // tn
    hpt = tm // FFN_HALO
    rows = min(FFN_ROWS, tm)
    return pl.pallas_call(
        functools.partial(_ffn_up_body, tm=tm, tiles_per_seq=t_len // tm, rows=rows),
        grid=(n // tm, nj),
        in_specs=[
            pl.BlockSpec((tm, D_MODEL), lambda i, j: (i, 0)),
            pl.BlockSpec((FFN_HALO, D_MODEL), lambda i, j: (jnp.maximum(i * hpt - 1, 0), 0)),
            pl.BlockSpec((1, D_MODEL), lambda i, j: (0, 0)),
            pl.BlockSpec((None, D_MODEL, tn), lambda i, j: (layer, 0, j)),
            pl.BlockSpec((None, D_MODEL, tn), lambda i, j: (layer, 0, j + nj)),
            pl.BlockSpec((FFN_CONV_TAPS, tn), lambda i, j: (0, j)),
            pl.BlockSpec((FFN_CONV_TAPS, tn), lambda i, j: (0, j + nj)),
            pl.BlockSpec((1, tn), lambda i, j: (0, j)),
            pl.BlockSpec((1, tn), lambda i, j: (0, j + nj)),
        ],
        out_specs=pl.BlockSpec((tm, tn), lambda i, j: (i, j)),
        out_shape=jax.ShapeDtypeStruct((n, D_FF), BF16),
        scratch_shapes=[
            pltpu.VMEM((FFN_HALO + tm, D_MODEL), BF16),
            pltpu.VMEM((tm // rows, FFN_HALO + rows, tn), F32),
            pltpu.VMEM((tm // rows, FFN_HALO + rows, tn), F32),
        ],
        compiler_params=_params("arbitrary", "arbitrary"),
        name="ffn_up",
    )(x2, x2, g, w_up, w_up, cw, cw, cb, cb)


def _ffn_down_body(act_ref, w_ref, x_ref, o_ref):
    o_ref[...] = x_ref[...] + jnp.dot(act_ref[...], w_ref[...], preferred_element_type=F32)


def _ffn_down(act, w_down, x2, *, layer, tm, tn):
    n = x2.shape[0]
    return pl.pallas_call(
        _ffn_down_body,
        grid=(n // tm, D_MODEL // tn),
        in_specs=[
            pl.BlockSpec((tm, D_FF), lambda i, j: (i, 0)),
            pl.BlockSpec((None, D_FF, tn), lambda i, j: (layer, 0, j)),
            pl.BlockSpec((tm, tn), lambda i, j: (i, j)),
        ],
        out_specs=pl.BlockSpec((tm, tn), lambda i, j: (i, j)),
        out_shape=jax.ShapeDtypeStruct((n, D_MODEL), F32),
        compiler_params=_params("arbitrary", "arbitrary"),
        name="ffn_down",
    )(act, w_down, x2)


def _final_norm_body(x_ref, g_ref, o_ref):
    o_ref[...] = _rms_norm_rows(x_ref[...], g_ref[...])


def _final_norm(x2, g, *, tm):
    n = x2.shape[0]
    return pl.pallas_call(
        _final_norm_body,
        grid=(n // tm,),
        in_specs=[pl.BlockSpec((tm, D_MODEL), lambda i: (i, 0)), pl.BlockSpec((1, D_MODEL), lambda i: (0, 0))],
        out_specs=pl.BlockSpec((tm, D_MODEL), lambda i: (i, 0)),
        out_shape=jax.ShapeDtypeStruct((n, D_MODEL), F32),
        compiler_params=_params("arbitrary"),
        name="final_norm",
    )(x2, g)


def _in_weight_layout(w_in):
    depth = w_in.shape[0]
    kv0 = ATTN_WIDTH
    gate0 = kv0 + KV_COLS
    conv0 = gate0 + GATE_COLS
    parts = [w_in[..., :ATTN_WIDTH], w_in[..., conv0:conv0 + 2 * CONV_CHANNELS]]
    for br in range(N_BRANCH):
        for h in range(N_KV_HEADS):
            for kv in range(2):
                c0 = kv0 + ((br * 2 + kv) * N_KV_HEADS + h) * HEAD_DIM
                parts.append(w_in[..., c0:c0 + HEAD_DIM])
    w_main = jnp.concatenate(parts, axis=-1).astype(BF16)
    gsrc = np.zeros((N_KV_HEADS, N_BRANCH * GQA_GROUP), np.int32)
    for h in range(N_KV_HEADS):
        for br in range(N_BRANCH):
            for g in range(GQA_GROUP):
                gsrc[h, br * GQA_GROUP + g] = gate0 + (h * GQA_GROUP + g) * N_BRANCH + br
    wg = jnp.take(w_in, jnp.asarray(gsrc.reshape(-1)), axis=-1).reshape(depth, D_MODEL, N_KV_HEADS, -1)
    wg = jnp.pad(wg, ((0, 0), (0, 0), (0, 0), (0, LANES - N_BRANCH * GQA_GROUP)))
    return w_main, wg.reshape(depth, D_MODEL, N_KV_HEADS * LANES).astype(BF16)


def _pick(n, prefs):
    for p in prefs:
        if n % p == 0:
            return p
    return n


def kernel(x, rel_bias, mix_norm_g, w_in, cmp_pos, cmp_w1, cmp_w2, conv_w, conv_b, conv_ln_g, conv_ln_b,
           w_out, ffn_norm_g, w_up, ffn_conv_w, ffn_conv_b, w_down, final_norm_g):
    bsz, t_len, _ = x.shape
    depth = w_in.shape[0]
    n = bsz * t_len
    assert t_len % 256 == 0 and t_len // SLC_LEN <= HEAD_DIM
    tq = 256
    tm = _pick(t_len, (1024, 512, 256))
    nc = t_len // CMP_STRIDE

    band, cmpb = _bias_tiles(rel_bias, t_len, tq)
    col_scale = jnp.concatenate([jnp.full((1, ATTN_WIDTH), Q_SCALE, F32),
                                 jnp.ones((1, PROJ_COLS - ATTN_WIDTH), F32)], axis=1)
    x2 = x.reshape(n, D_MODEL)
    w_main, wg = _in_weight_layout(w_in)
    w_out_b, w_up_b, w_down_b = w_out.astype(BF16), w_up.astype(BF16), w_down.astype(BF16)
    for l in range(depth):
        proj, gates = _in_projection(x2, mix_norm_g[l][None, :], w_main, wg, col_scale, layer=l, tm=tm, tn=768)
        proj3 = proj.reshape(bsz, t_len, PROJ_COLS)
        gates3 = gates.reshape(bsz, t_len, N_KV_HEADS * LANES)

        posx, w1big, w2big = _compress_weight_layout(cmp_pos[l], cmp_w1[l], cmp_w2[l])
        tokl = proj3[:, :, COL_KV:COL_KV + CMP_WIDTH].reshape(bsz, nc, CMP_STRIDE, CMP_WIDTH).transpose(0, 2, 1, 3)
        kvc, kvct = _compress(tokl, posx, w1big, w2big)

        attn = _attention(proj3, kvc, kvct, band, cmpb, gates3, tq=tq)
        conv = _conformer(proj3, conv_w[l], conv_b[l][None, :], conv_ln_g[l][None, :],
                          conv_ln_b[l][None, :], tt=256)
        x2 = _out_projection(x2, attn.reshape(n, ATTN_WIDTH), conv.reshape(n, CONV_CHANNELS),
                             w_out_b, layer=l, tm=tm, tn=1024)

        act = _ffn_up(x2, ffn_norm_g[l][None, :], w_up_b, ffn_conv_w[l],
                      ffn_conv_b[l][None, :], layer=l, tm=tm, tn=512, t_len=t_len)
        x2 = _ffn_down(act, w_down_b, x2, layer=l, tm=tm, tn=512)
    x2 = _final_norm(x2, final_norm_g[None, :], tm=_pick(n, (512, 256)))
    return x2.reshape(bsz, t_len, D_MODEL)
```

```python
import functools
import math

import numpy as np
import jax
import jax.numpy as jnp
from jax import lax
from jax.experimental import pallas as pl
from jax.experimental.pallas import tpu as pltpu

F32 = jnp.float32
BF16 = jnp.bfloat16

D_MODEL = 2048
HEAD_DIM = 64
N_KV_HEADS = 4
GQA_GROUP = 4
N_Q_HEADS = N_KV_HEADS * GQA_GROUP
N_BRANCH = 3
ATTN_WIDTH = N_Q_HEADS * HEAD_DIM
CMP_LEN = 32
CMP_STRIDE = 16
SLC_LEN = 64
SLC_TOPK = 16
WINDOW = 512
CONV_CHANNELS = D_MODEL - ATTN_WIDTH
CONV_TAPS = 31
D_FF = 5632
FFN_CONV_TAPS = 3
N_BUCKETS = 32
MAX_DISTANCE = 128
NORM_EPS = 1e-6

KV_PAIR = 2 * HEAD_DIM
KV_COLS = N_BRANCH * N_KV_HEADS * KV_PAIR
GATE_COLS = N_BRANCH * N_Q_HEADS
COL_Q = 0
COL_CONV_A = ATTN_WIDTH
COL_CONV_G = COL_CONV_A + CONV_CHANNELS
COL_KV = COL_CONV_G + CONV_CHANNELS
PROJ_COLS = COL_KV + KV_COLS

MASK_VALUE = -1e30
BLOCK_PENALTY = -1e9
KEY_BIG = 1e30
KEY_TAKEN = -3e38
LOG2E = math.log2(math.e)
Q_SCALE = HEAD_DIM ** -0.5 * LOG2E

LANES = 128
SUBLANES = 8
CONV_ROWS = 128
FFN_ROWS = 1024
VMEM_LIMIT = 56 * 1024 * 1024
CONV_HALO = 32
FFN_HALO = 16


def _t5_bucket_last_distance():
    n = np.arange(0, 4 * MAX_DISTANCE, dtype=np.int64)
    max_exact = N_BUCKETS // 2
    nf = np.maximum(n, 1).astype(np.float64)
    large = max_exact + np.floor(np.log(nf / max_exact) / math.log(MAX_DISTANCE / max_exact)
                                 * (N_BUCKETS - max_exact)).astype(np.int64)
    large = np.minimum(large, N_BUCKETS - 1)
    bucket = np.where(n < max_exact, n, large)
    last = []
    for b in range(N_BUCKETS - 1):
        idx = np.nonzero(bucket == b)[0]
        last.append(int(idx.max()) if idx.size else None)
    return last


_BUCKET_LAST = _t5_bucket_last_distance()


def _params(*sem):
    return pltpu.CompilerParams(dimension_semantics=sem, vmem_limit_bytes=VMEM_LIMIT)


def _sigmoid(x):
    return jax.nn.sigmoid(x)


def _rms_norm_rows(x, g):
    ms = jnp.mean(x * x, axis=-1, keepdims=True)
    return x * lax.rsqrt(ms + NORM_EPS) * g


def _bias_from_distance(dist, tab_ref, head):
    c_far = tab_ref[N_BUCKETS - 1, head]
    val = jnp.zeros(dist.shape, F32)
    for b in range(N_BUCKETS - 2, -1, -1):
        if _BUCKET_LAST[b] is None:
            continue
        val = jnp.where(dist <= _BUCKET_LAST[b], tab_ref[b, head] - c_far, val)
    return val


def _band_bias_body(tab_ref, o_ref, *, tq):
    head = pl.program_id(0) * GQA_GROUP + pl.program_id(1)
    shape = (WINDOW + tq, tq)
    dist = (lax.broadcasted_iota(jnp.int32, shape, 1) + WINDOW
            - lax.broadcasted_iota(jnp.int32, shape, 0))
    val = _bias_from_distance(dist, tab_ref, head) * LOG2E
    val = jnp.where(dist >= 0, jnp.where(dist < WINDOW, val, MASK_VALUE), MASK_VALUE)
    o_ref[0] = val


def _cmp_bias_body(tab_ref, o_ref, *, tq, nc):
    head = pl.program_id(0) * GQA_GROUP + pl.program_id(1)
    shape = (2 * nc, tq)
    r = lax.broadcasted_iota(jnp.int32, shape, 0)
    i = lax.broadcasted_iota(jnp.int32, shape, 1)
    dist = i - (r - nc) * CMP_STRIDE - (CMP_LEN - 1)
    val = _bias_from_distance(dist, tab_ref, head) * LOG2E
    o_ref[0] = jnp.where(dist >= 0, val, MASK_VALUE)


def _bias_tiles(rel_bias, t_len, tq):
    nc = t_len // CMP_STRIDE
    rows = GQA_GROUP * tq
    smem = pl.BlockSpec(memory_space=pltpu.SMEM)
    band = pl.pallas_call(
        functools.partial(_band_bias_body, tq=tq),
        grid=(N_KV_HEADS, GQA_GROUP),
        in_specs=[smem],
        out_specs=pl.BlockSpec((1, WINDOW + tq, tq), lambda h, g: (h, 0, g)),
        out_shape=jax.ShapeDtypeStruct((N_KV_HEADS, WINDOW + tq, rows), F32),
        compiler_params=_params("arbitrary", "arbitrary"),
        name="band_bias",
    )(rel_bias)
    cmpb = pl.pallas_call(
        functools.partial(_cmp_bias_body, tq=tq, nc=nc),
        grid=(N_KV_HEADS, GQA_GROUP),
        in_specs=[smem],
        out_specs=pl.BlockSpec((1, 2 * nc, tq), lambda h, g: (h, 0, g)),
        out_shape=jax.ShapeDtypeStruct((N_KV_HEADS, 2 * nc, rows), F32),
        compiler_params=_params("arbitrary", "arbitrary"),
        name="cmp_bias",
    )(rel_bias)
    return band, cmpb


def _inproj_body(x_ref, g_ref, w_ref, wg_ref, cs_ref, o_ref, gate_ref, h_ref):
    @pl.when(pl.program_id(1) == 0)
    def _():
        h = _rms_norm_rows(x_ref[...], g_ref[...]).astype(BF16)
        h_ref[...] = h
        gate_ref[...] = _sigmoid(jnp.dot(h, wg_ref[...], preferred_element_type=F32))

    acc = jnp.dot(h_ref[...], w_ref[...], preferred_element_type=F32)
    o_ref[...] = (acc * cs_ref[...]).astype(o_ref.dtype)


def _in_projection(x2, g, w, wg, col_scale, *, layer, tm, tn):
    n = x2.shape[0]
    gcols = wg.shape[-1]
    return pl.pallas_call(
        _inproj_body,
        grid=(n // tm, PROJ_COLS // tn),
        in_specs=[
            pl.BlockSpec((tm, D_MODEL), lambda i, j: (i, 0)),
            pl.BlockSpec((1, D_MODEL), lambda i, j: (0, 0)),
            pl.BlockSpec((None, D_MODEL, tn), lambda i, j: (layer, 0, j)),
            pl.BlockSpec((None, D_MODEL, gcols), lambda i, j: (layer, 0, 0)),
            pl.BlockSpec((1, tn), lambda i, j: (0, j)),
        ],
        out_specs=[
            pl.BlockSpec((tm, tn), lambda i, j: (i, j)),
            pl.BlockSpec((tm, gcols), lambda i, j: (i, 0)),
        ],
        out_shape=[
            jax.ShapeDtypeStruct((n, PROJ_COLS), BF16),
            jax.ShapeDtypeStruct((n, gcols), F32),
        ],
        scratch_shapes=[pltpu.VMEM((tm, D_MODEL), BF16)],
        compiler_params=_params("arbitrary", "arbitrary"),
        name="in_projection",
    )(x2, g, w, wg, col_scale)


CMP_WIDTH = N_KV_HEADS * KV_PAIR


def _compress_body(x_ref, pos_ref, w1_ref, w2_ref, o_ref, ot_ref, top_ref, bot_ref, *, nc):
    l = pl.program_id(1)

    @pl.when(l == 0)
    def _():
        top_ref[...] = jnp.zeros_like(top_ref)
        bot_ref[...] = jnp.zeros_like(bot_ref)

    x = x_ref[0, 0].astype(F32)
    top_ref[...] += jnp.dot((x + pos_ref[0, 0]).astype(BF16), w1_ref[0, 0], preferred_element_type=F32)
    bot_ref[...] += jnp.dot((x + pos_ref[0, 1]).astype(BF16), w1_ref[0, 1], preferred_element_type=F32)

    @pl.when(l == pl.num_programs(1) - 1)
    def _():
        pre = top_ref[...] + pltpu.roll(bot_ref[...], nc - 1, axis=0)
        act = pre * _sigmoid(pre)
        out = jnp.dot(act.astype(BF16), w2_ref[...], preferred_element_type=F32)
        for h in range(N_KV_HEADS):
            kv = out[:, h * KV_PAIR:(h + 1) * KV_PAIR]
            o_ref[0, h] = kv.astype(o_ref.dtype)
            ot_ref[0, h] = kv.T.astype(ot_ref.dtype)


def _compress(tokl, posx, w1big, w2big):
    b, _, nc, _ = tokl.shape
    return pl.pallas_call(
        functools.partial(_compress_body, nc=nc),
        grid=(b, CMP_STRIDE),
        in_specs=[
            pl.BlockSpec((1, 1, nc, CMP_WIDTH), lambda i, l: (i, l, 0, 0)),
            pl.BlockSpec((1, 2, 1, CMP_WIDTH), lambda i, l: (l, 0, 0, 0)),
            pl.BlockSpec((1, 2, CMP_WIDTH, CMP_WIDTH), lambda i, l: (l, 0, 0, 0)),
            pl.BlockSpec((CMP_WIDTH, CMP_WIDTH), lambda i, l: (0, 0)),
        ],
        out_specs=[
            pl.BlockSpec((1, N_KV_HEADS, nc, KV_PAIR), lambda i, l: (i, 0, 0, 0)),
            pl.BlockSpec((1, N_KV_HEADS, KV_PAIR, nc), lambda i, l: (i, 0, 0, 0)),
        ],
        out_shape=[
            jax.ShapeDtypeStruct((b, N_KV_HEADS, nc, KV_PAIR), BF16),
            jax.ShapeDtypeStruct((b, N_KV_HEADS, KV_PAIR, nc), BF16),
        ],
        scratch_shapes=[pltpu.VMEM((nc, CMP_WIDTH), F32), pltpu.VMEM((nc, CMP_WIDTH), F32)],
        compiler_params=_params("arbitrary", "arbitrary"),
        name="compress",
    )(tokl, posx, w1big, w2big)


def _compress_weight_layout(pos, w1, w2):
    def block_diag(blocks):
        rows = []
        for h in range(N_KV_HEADS):
            for kv in range(2):
                off = (h * 2 + kv) * HEAD_DIM
                pad = [(0, 0)] * (blocks.ndim - 2) + [(off, CMP_WIDTH - HEAD_DIM - off)]
                rows.append(jnp.pad(blocks[kv], pad))
        return jnp.concatenate(rows, axis=-2).astype(BF16)

    w1r = w1.reshape(2, 2, CMP_STRIDE, HEAD_DIM, HEAD_DIM)
    w1big = block_diag(w1r.transpose(0, 2, 1, 3, 4))
    w2big = block_diag(w2)
    posr = pos.reshape(2, 2, CMP_STRIDE, HEAD_DIM).transpose(2, 1, 0, 3)
    posx = jnp.broadcast_to(posr[:, :, None], (CMP_STRIDE, 2, N_KV_HEADS, 2, HEAD_DIM))
    return posx.reshape(CMP_STRIDE, 2, 1, CMP_WIDTH), w1big, w2big


VT_ROWS = HEAD_DIM + 16


def _attn_body(q_ref, kvc_ref, kvct_ref, kvs_ref, kvw_ref, bt_ref, bc_ref, gate_ref, o_ref,
               kp_ref, vst_ref, vwt_ref, key_ref, m_ref, acc_ref, sa_ref, sb_ref, ocmp_ref, qsel_ref, qwin_ref,
               *, tq, t_len):
    qi = pl.program_id(2)
    nc = t_len // CMP_STRIDE
    ns = t_len // SLC_LEN
    rows = GQA_GROUP * tq
    blocks_per_tile = tq // SLC_LEN

    @pl.when(qi == 0)
    def _():
        k = kvs_ref[0][:, :HEAD_DIM]
        blk = lax.broadcasted_iota(jnp.int32, (t_len, HEAD_DIM), 0) // SLC_LEN
        col = lax.broadcasted_iota(jnp.int32, (t_len, HEAD_DIM), 1)
        onehot = jnp.where(blk == col, 1.0, 0.0).astype(BF16)
        kp_ref[...] = jnp.concatenate([k, onehot], axis=1)
        ones_rows = jnp.where(lax.broadcasted_iota(jnp.int32, (VT_ROWS - HEAD_DIM, tq), 0) == 0, 1.0, 0.0)
        for c in range(t_len // tq):
            st = kvs_ref[0, c * tq:(c + 1) * tq, :].astype(F32).T[HEAD_DIM:]
            vst_ref[c] = jnp.concatenate([st, ones_rows], axis=0).astype(BF16)
            wt = kvw_ref[0, c * tq:(c + 1) * tq, :].astype(F32).T[HEAD_DIM:]
            vwt_ref[c] = jnp.concatenate([wt, ones_rows], axis=0).astype(BF16)

    q_t = q_ref[0].astype(F32).T
    q4_t = jnp.concatenate([q_t[g * HEAD_DIM:(g + 1) * HEAD_DIM] for g in range(GQA_GROUP)],
                           axis=1).astype(BF16)
    qwin_ref[...] = jnp.concatenate([q4_t, jnp.zeros((HEAD_DIM, rows), BF16)], axis=0)
    qsel_ref[0:HEAD_DIM, :] = q4_t
    kw_ref = kvw_ref.at[0]
    win, sel = 0, 1

    def cmp_scores():
        bias_row = pl.multiple_of(nc - qi * (tq // CMP_STRIDE), tq // CMP_STRIDE)
        return (jnp.dot(kvc_ref[0, 0], qwin_ref[...], preferred_element_type=F32)
                + bc_ref[0, pl.ds(bias_row, nc), :])

    def cmp_finish(s):
        m = jnp.max(s, axis=0, keepdims=True)
        p = jnp.exp2(s - m)
        l = jnp.sum(p, axis=0, keepdims=True)
        pn = p * jnp.where(m > 0.5 * MASK_VALUE, 1.0 / l, 0.0)
        ocmp_ref[...] = jnp.dot(kvct_ref[0, 0, HEAD_DIM:, :], pn.astype(BF16),
                                preferred_element_type=F32)

        ps = pn[:, 0:tq] + pn[:, tq:2 * tq] + pn[:, 2 * tq:3 * tq] + pn[:, 3 * tq:4 * tq]
        sj = lax.broadcasted_iota(jnp.int32, (HEAD_DIM, nc), 0) * SLC_LEN
        ci = lax.broadcasted_iota(jnp.int32, (HEAD_DIM, nc), 1) * CMP_STRIDE
        overlap = jnp.where(ci < sj + SLC_LEN, jnp.where(ci + CMP_LEN > sj, 1.0, 0.0), 0.0).astype(BF16)
        p_hi = ps.astype(BF16)
        r_hi = ps - p_hi.astype(F32)
        p_md = r_hi.astype(BF16)
        p_lo = (r_hi - p_md.astype(F32)).astype(BF16)
        imp = (jnp.dot(overlap, p_hi, preferred_element_type=F32)
               + jnp.dot(overlap, p_md, preferred_element_type=F32)
               + jnp.dot(overlap, p_lo, preferred_element_type=F32))
        t = qi * tq + lax.broadcasted_iota(jnp.int32, (HEAD_DIM, tq), 1)
        blk = lax.broadcasted_iota(jnp.int32, (HEAD_DIM, tq), 0)
        cur = t // SLC_LEN
        key = jnp.where(blk == 0, KEY_BIG, jnp.where(blk == cur, KEY_BIG, jnp.where(blk == cur - 1, KEY_BIG, imp)))
        key_ref[...] = jnp.where(blk * SLC_LEN <= t, key, -KEY_BIG)

    def select_blocks():
        key = key_ref[...]
        blk = lax.broadcasted_iota(jnp.int32, (HEAD_DIM, tq), 0)
        pen = jnp.full((HEAD_DIM, tq), BLOCK_PENALTY, F32)
        for _ in range(min(SLC_TOPK, ns)):
            top = jnp.max(key, axis=0, keepdims=True)
            first = jnp.min(jnp.where(key == top, blk, HEAD_DIM), axis=0, keepdims=True)
            hit = blk == first
            pen = jnp.where(hit, 0.0, pen)
            key = jnp.where(hit, KEY_TAKEN, key)
        qsel_ref[HEAD_DIM:, :] = jnp.concatenate([pen.astype(BF16)] * GQA_GROUP, axis=1)

    def scores(qx_ref, k_ref, kj, bias_off):
        start = pl.multiple_of(kj * tq, tq)
        sc = jnp.dot(k_ref[pl.ds(start, tq), :], qx_ref[...], preferred_element_type=F32)
        if bias_off is not None:
            sc = sc + bt_ref[0, bias_off:bias_off + tq, :]
        return sc

    def consume(sc, vt_ref, kj, br, st, first):
        m_tile = jnp.max(sc, axis=0, keepdims=True)
        if first:
            m_new = m_tile
            pr = jnp.exp2(sc - m_new).astype(BF16)
            acc_ref[br, st] = jnp.dot(vt_ref[kj], pr, preferred_element_type=F32)
        else:
            m_old = m_ref[br, st]
            m_new = jnp.maximum(m_old, m_tile)
            alpha = jnp.exp2(m_old - m_new)
            pr = jnp.exp2(sc - m_new).astype(BF16)
            acc_ref[br, st] = alpha * acc_ref[br, st] + jnp.dot(vt_ref[kj], pr, preferred_element_type=F32)
        m_ref[br, st] = m_new

    def finish(br):
        m0 = m_ref[br, 0]
        m1 = m_ref[br, 1]
        m_all = jnp.maximum(m0, m1)
        acc = jnp.exp2(m0 - m_all) * acc_ref[br, 0] + jnp.exp2(m1 - m_all) * acc_ref[br, 1]
        return acc[0:HEAD_DIM] * (1.0 / acc[HEAD_DIM:HEAD_DIM + 1, :])

    def combine():
        o_win = finish(win)
        o_sel = finish(sel)
        o_cmp = ocmp_ref[...]
        gates_t = gate_ref[0].T
        combs = []
        for g in range(GQA_GROUP):
            sl = slice(g * tq, (g + 1) * tq)
            combs.append(gates_t[g:g + 1, :] * o_cmp[:, sl]
                         + gates_t[GQA_GROUP + g:GQA_GROUP + g + 1, :] * o_sel[:, sl]
                         + gates_t[2 * GQA_GROUP + g:2 * GQA_GROUP + g + 1, :] * o_win[:, sl])
        outs = [jnp.concatenate(combs[p:p + 2], axis=0).T for p in range(0, GQA_GROUP, 2)]
        o_ref[0] = jnp.concatenate(outs, axis=1).astype(o_ref.dtype)

    @pl.when(qi >= 2)
    def _():
        sc_cmp = cmp_scores()
        sw0 = scores(qwin_ref, kw_ref, qi, WINDOW)
        cmp_finish(sc_cmp)
        sw1 = scores(qwin_ref, kw_ref, qi - 1, WINDOW - tq)
        consume(sw0, vwt_ref, qi, win, 0, True)
        sw2 = scores(qwin_ref, kw_ref, qi - 2, WINDOW - 2 * tq)
        consume(sw1, vwt_ref, qi - 1, win, 1, True)
        select_blocks()
        consume(sw2, vwt_ref, qi - 2, win, 0, False)

        ss0 = scores(qsel_ref, kp_ref, qi, WINDOW)
        ss1 = scores(qsel_ref, kp_ref, qi - 1, WINDOW - tq)
        consume(ss0, vst_ref, qi, sel, 0, True)
        sa_ref[...] = scores(qsel_ref, kp_ref, 0, None)
        consume(ss1, vst_ref, qi - 1, sel, 1, True)
        n_far = qi - 1

        def far_pair(pi, carry):
            sb_ref[...] = scores(qsel_ref, kp_ref, 2 * pi + 1, None)
            consume(sa_ref[...], vst_ref, 2 * pi, sel, 1, False)
            sa_ref[...] = scores(qsel_ref, kp_ref, jnp.minimum(2 * pi + 2, n_far - 1), None)
            consume(sb_ref[...], vst_ref, 2 * pi + 1, sel, 0, False)
            return carry

        lax.fori_loop(0, n_far // 2, far_pair, 0)

        @pl.when(n_far % 2 == 1)
        def _():
            consume(sa_ref[...], vst_ref, n_far - 1, sel, 1, False)

        combine()

    @pl.when(qi < 2)
    def _():
        cmp_finish(cmp_scores())
        select_blocks()
        sw0 = scores(qwin_ref, kw_ref, qi, WINDOW)
        ss0 = scores(qsel_ref, kp_ref, qi, WINDOW)
        consume(sw0, vwt_ref, qi, win, 0, True)
        consume(ss0, vst_ref, qi, sel, 0, True)
        for br in (win, sel):
            m_ref[br, 1] = jnp.full((1, rows), MASK_VALUE, F32)
            acc_ref[br, 1] = jnp.zeros((VT_ROWS, rows), F32)

        @pl.when(qi == 1)
        def _():
            sw1 = scores(qwin_ref, kw_ref, 0, WINDOW - tq)
            ss1 = scores(qsel_ref, kp_ref, 0, WINDOW - tq)
            consume(sw1, vwt_ref, 0, win, 1, False)
            consume(ss1, vst_ref, 0, sel, 1, False)

        combine()


def _attention(proj3, kvc, kvct, band, cmpb, gates3, *, tq):
    b, t_len, _ = proj3.shape
    assert WINDOW == 2 * tq
    nc = t_len // CMP_STRIDE
    rows = GQA_GROUP * tq
    qw = GQA_GROUP * HEAD_DIM
    slc_blk = COL_KV // KV_PAIR + N_KV_HEADS
    win_blk = COL_KV // KV_PAIR + 2 * N_KV_HEADS
    return pl.pallas_call(
        functools.partial(_attn_body, tq=tq, t_len=t_len),
        grid=(b, N_KV_HEADS, t_len // tq),
        in_specs=[
            pl.BlockSpec((1, tq, qw), lambda i, h, q: (i, q, h)),
            pl.BlockSpec((1, 1, nc, KV_PAIR), lambda i, h, q: (i, h, 0, 0)),
            pl.BlockSpec((1, 1, KV_PAIR, nc), lambda i, h, q: (i, h, 0, 0)),
            pl.BlockSpec((1, t_len, KV_PAIR), lambda i, h, q: (i, 0, slc_blk + h)),
            pl.BlockSpec((1, t_len, KV_PAIR), lambda i, h, q: (i, 0, win_blk + h)),
            pl.BlockSpec((1, WINDOW + tq, rows), lambda i, h, q: (h, 0, 0)),
            pl.BlockSpec((1, 2 * nc, rows), lambda i, h, q: (h, 0, 0)),
            pl.BlockSpec((1, tq, LANES), lambda i, h, q: (i, q, h)),
        ],
        out_specs=pl.BlockSpec((1, tq, qw), lambda i, h, q: (i, q, h)),
        out_shape=jax.ShapeDtypeStruct((b, t_len, ATTN_WIDTH), BF16),
        scratch_shapes=[
            pltpu.VMEM((t_len, KV_PAIR), BF16),
            pltpu.VMEM((t_len // tq, VT_ROWS, tq), BF16),
            pltpu.VMEM((t_len // tq, VT_ROWS, tq), BF16),
            pltpu.VMEM((HEAD_DIM, tq), F32),
            pltpu.VMEM((2, 2, 1, rows), F32),
            pltpu.VMEM((2, 2, VT_ROWS, rows), F32),
            pltpu.VMEM((tq, rows), F32),
            pltpu.VMEM((tq, rows), F32),
            pltpu.VMEM((HEAD_DIM, rows), F32),
            pltpu.VMEM((KV_PAIR, rows), BF16),
            pltpu.VMEM((KV_PAIR, rows), BF16),
        ],
        compiler_params=_params("arbitrary", "arbitrary", "arbitrary"),
        name="nsa_attention",
    )(proj3, kvc, kvct, proj3, proj3, band, cmpb, gates3)


def _conformer_body(a_ref, g_ref, ah_ref, gh_ref, w_ref, b_ref, lg_ref, lb_ref, o_ref, u_ref, y_ref, *, tt):
    ti = pl.program_id(1)
    u_ref[CONV_HALO:, :] = a_ref[0].astype(F32) * _sigmoid(g_ref[0].astype(F32))
    halo = ah_ref[0].astype(F32) * _sigmoid(gh_ref[0].astype(F32))
    u_ref[0:CONV_HALO, :] = jnp.where(ti > 0, halo, 0.0)
    base = CONV_HALO - (CONV_TAPS - 1)
    ext = CONV_ROWS + CONV_HALO
    for cb in range(CONV_CHANNELS // LANES):
        cs = slice(cb * LANES, (cb + 1) * LANES)
        wblk = w_ref[:, cs]
        for rb in range(tt // CONV_ROWS):
            ublk = u_ref[rb * CONV_ROWS:rb * CONV_ROWS + ext, cs]
            acc = jnp.zeros((CONV_ROWS, LANES), F32)
            for r in range(SUBLANES):
                ur = ublk if r == 0 else pltpu.roll(ublk, ext - r, axis=0)
                for a in range(CONV_HALO // SUBLANES + 1):
                    k = SUBLANES * a + r - base
                    if 0 <= k < CONV_TAPS:
                        acc = acc + wblk[k:k + 1, :] * ur[SUBLANES * a:SUBLANES * a + CONV_ROWS]
            y_ref[rb * CONV_ROWS:(rb + 1) * CONV_ROWS, cs] = acc
    acc = y_ref[...] + b_ref[...]
    mu = jnp.mean(acc, axis=-1, keepdims=True)
    xc = acc - mu
    var = jnp.mean(xc * xc, axis=-1, keepdims=True)
    y = xc * lax.rsqrt(var + NORM_EPS) * lg_ref[...] + lb_ref[...]
    o_ref[0] = (y * _sigmoid(y)).astype(o_ref.dtype)


def _conformer(proj3, w, b, lg, lb, *, tt):
    bsz, t_len, _ = proj3.shape
    a_blk = COL_CONV_A // CONV_CHANNELS
    g_blk = COL_CONV_G // CONV_CHANNELS
    hpt = tt // CONV_HALO
    halo_idx = lambda i, t: jnp.maximum(t * hpt - 1, 0)
    vec = pl.BlockSpec((1, CONV_CHANNELS), lambda i, t: (0, 0))
    return pl.pallas_call(
        functools.partial(_conformer_body, tt=tt),
        grid=(bsz, t_len // tt),
        in_specs=[
            pl.BlockSpec((1, tt, CONV_CHANNELS), lambda i, t: (i, t, a_blk)),
            pl.BlockSpec((1, tt, CONV_CHANNELS), lambda i, t: (i, t, g_blk)),
            pl.BlockSpec((1, CONV_HALO, CONV_CHANNELS), lambda i, t: (i, halo_idx(i, t), a_blk)),
            pl.BlockSpec((1, CONV_HALO, CONV_CHANNELS), lambda i, t: (i, halo_idx(i, t), g_blk)),
            pl.BlockSpec((CONV_TAPS, CONV_CHANNELS), lambda i, t: (0, 0)),
            vec, vec, vec,
        ],
        out_specs=pl.BlockSpec((1, tt, CONV_CHANNELS), lambda i, t: (i, t, 0)),
        out_shape=jax.ShapeDtypeStruct((bsz, t_len, CONV_CHANNELS), BF16),
        scratch_shapes=[pltpu.VMEM((CONV_HALO + tt, CONV_CHANNELS), F32),
                        pltpu.VMEM((tt, CONV_CHANNELS), F32)],
        compiler_params=_params("arbitrary", "arbitrary"),
        name="conformer_conv",
    )(proj3, proj3, proj3, proj3, w, b, lg, lb)


def _outproj_body(x_ref, a_ref, c_ref, wa_ref, wc_ref, o_ref):
    o_ref[...] = (x_ref[...]
                  + jnp.dot(a_ref[...], wa_ref[...], preferred_element_type=F32)
                  + jnp.dot(c_ref[...], wc_ref[...], preferred_element_type=F32))


def _out_projection(x2, attn2, conv2, w_out, *, layer, tm, tn):
    n = x2.shape[0]
    return pl.pallas_call(
        _outproj_body,
        grid=(n // tm, D_MODEL // tn),
        in_specs=[
            pl.BlockSpec((tm, tn), lambda i, j: (i, j)),
            pl.BlockSpec((tm, ATTN_WIDTH), lambda i, j: (i, 0)),
            pl.BlockSpec((tm, CONV_CHANNELS), lambda i, j: (i, 0)),
            pl.BlockSpec((None, ATTN_WIDTH, tn), lambda i, j: (layer, 0, j)),
            pl.BlockSpec((None, CONV_CHANNELS, tn), lambda i, j: (layer, ATTN_WIDTH // CONV_CHANNELS, j)),
        ],
        out_specs=pl.BlockSpec((tm, tn), lambda i, j: (i, j)),
        out_shape=jax.ShapeDtypeStruct((n, D_MODEL), F32),
        compiler_params=_params("arbitrary", "arbitrary"),
        name="out_projection",
    )(x2, attn2, conv2, w_out, w_out)


def _ffn_up_body(x_ref, xh_ref, g_ref, wa_ref, wg_ref, cwa_ref, cwg_ref, cba_ref, cbg_ref, o_ref,
                 h_ref, ua_ref, ug_ref, *, tm, tiles_per_seq, rows):
    i = pl.program_id(0)

    @pl.when(pl.program_id(1) == 0)
    def _():
        h_ref[FFN_HALO:, :] = _rms_norm_rows(x_ref[...], g_ref[...]).astype(BF16)
        hh = _rms_norm_rows(xh_ref[...], g_ref[...])
        h_ref[0:FFN_HALO, :] = jnp.where(i % tiles_per_seq != 0, hh, 0.0).astype(BF16)

    units = [(c, slice(0, o_ref.shape[1])) for c in range(tm // rows)]

    def project(c, cs):
        hc = h_ref[c * rows:(c + 1) * rows + FFN_HALO, :]
        ua_ref[c, :, cs] = jnp.dot(hc, wa_ref[:, cs], preferred_element_type=F32)
        ug_ref[c, :, cs] = jnp.dot(hc, wg_ref[:, cs], preferred_element_type=F32)

    def conv(u_ref, c, cs, cw_ref, cb_ref):
        u = u_ref[c, :, cs]
        y = cw_ref[FFN_CONV_TAPS - 1:FFN_CONV_TAPS, cs] * u[FFN_HALO:]
        for s in range(1, FFN_CONV_TAPS):
            k = FFN_CONV_TAPS - 1 - s
            y = y + cw_ref[k:k + 1, cs] * pltpu.roll(u, s, axis=0)[FFN_HALO:]
        return y + cb_ref[:, cs]

    def finish(c, cs):
        a = conv(ua_ref, c, cs, cwa_ref, cba_ref)
        gate = conv(ug_ref, c, cs, cwg_ref, cbg_ref)
        o_ref[c * rows:(c + 1) * rows, cs] = (gate * _sigmoid(gate) * a).astype(o_ref.dtype)

    project(*units[0])
    for u, unit in enumerate(units):
        if u + 1 < len(units):
            project(*units[u + 1])
        finish(*unit)


def _ffn_up(x2, g, w_up, cw, cb, *, layer, tm, tn, t_len):
    n = x2.shape[0]
    nj = D_FF // tn
    hpt = tm // FFN_HALO
    rows = min(FFN_ROWS, tm)
    return pl.pallas_call(
        functools.partial(_ffn_up_body, tm=tm, tiles_per_seq=t_len // tm, rows=rows),
        grid=(n // tm, nj),
        in_specs=[
            pl.BlockSpec((tm, D_MODEL), lambda i, j: (i, 0)),
            pl.BlockSpec((FFN_HALO, D_MODEL), lambda i, j: (jnp.maximum(i * hpt - 1, 0), 0)),
            pl.BlockSpec((1, D_MODEL), lambda i, j: (0, 0)),
            pl.BlockSpec((None, D_MODEL, tn), lambda i, j: (layer, 0, j)),
            pl.BlockSpec((None, D_MODEL, tn), lambda i, j: (layer, 0, j + nj)),
            pl.BlockSpec((FFN_CONV_TAPS, tn), lambda i, j: (0, j)),
            pl.BlockSpec((FFN_CONV_TAPS, tn), lambda i, j: (0, j + nj)),
            pl.BlockSpec((1, tn), lambda i, j: (0, j)),
            pl.BlockSpec((1, tn), lambda i, j: (0, j + nj)),
        ],
        out_specs=pl.BlockSpec((tm, tn), lambda i, j: (i, j)),
        out_shape=jax.ShapeDtypeStruct((n, D_FF), BF16),
        scratch_shapes=[
            pltpu.VMEM((FFN_HALO + tm, D_MODEL), BF16),
            pltpu.VMEM((tm // rows, FFN_HALO + rows, tn), F32),
            pltpu.VMEM((tm // rows, FFN_HALO + rows, tn), F32),
        ],
        compiler_params=_params("arbitrary", "arbitrary"),
        name="ffn_up",
    )(x2, x2, g, w_up, w_up, cw, cw, cb, cb)


def _ffn_down_body(act_ref, w_ref, x_ref, o_ref):
    o_ref[...] = x_ref[...] + jnp.dot(act_ref[...], w_ref[...], preferred_element_type=F32)


def _ffn_down(act, w_down, x2, *, layer, tm, tn):
    n = x2.shape[0]
    return pl.pallas_call(
        _ffn_down_body,
        grid=(n // tm, D_MODEL // tn),
        in_specs=[
            pl.BlockSpec((tm, D_FF), lambda i, j: (i, 0)),
            pl.BlockSpec((None, D_FF, tn), lambda i, j: (layer, 0, j)),
            pl.BlockSpec((tm, tn), lambda i, j: (i, j)),
        ],
        out_specs=pl.BlockSpec((tm, tn), lambda i, j: (i, j)),
        out_shape=jax.ShapeDtypeStruct((n, D_MODEL), F32),
        compiler_params=_params("arbitrary", "arbitrary"),
        name="ffn_down",
    )(act, w_down, x2)


def _final_norm_body(x_ref, g_ref, o_ref):
    o_ref[...] = _rms_norm_rows(x_ref[...], g_ref[...])


def _final_norm(x2, g, *, tm):
    n = x2.shape[0]
    return pl.pallas_call(
        _final_norm_body,
        grid=(n // tm,),
        in_specs=[pl.BlockSpec((tm, D_MODEL), lambda i: (i, 0)), pl.BlockSpec((1, D_MODEL), lambda i: (0, 0))],
        out_specs=pl.BlockSpec((tm, D_MODEL), lambda i: (i, 0)),
        out_shape=jax.ShapeDtypeStruct((n, D_MODEL), F32),
        compiler_params=_params("arbitrary"),
        name="final_norm",
    )(x2, g)


def _in_weight_layout(w_in):
    depth = w_in.shape[0]
    kv0 = ATTN_WIDTH
    gate0 = kv0 + KV_COLS
    conv0 = gate0 + GATE_COLS
    parts = [w_in[..., :ATTN_WIDTH], w_in[..., conv0:conv0 + 2 * CONV_CHANNELS]]
    for br in range(N_BRANCH):
        for h in range(N_KV_HEADS):
            for kv in range(2):
                c0 = kv0 + ((br * 2 + kv) * N_KV_HEADS + h) * HEAD_DIM
                parts.append(w_in[..., c0:c0 + HEAD_DIM])
    w_main = jnp.concatenate(parts, axis=-1).astype(BF16)
    wg = w_in[..., gate0:gate0 + GATE_COLS].reshape(depth, D_MODEL, N_KV_HEADS, GQA_GROUP, N_BRANCH)
    wg = wg.transpose(0, 1, 2, 4, 3).reshape(depth, D_MODEL, N_KV_HEADS, N_BRANCH * GQA_GROUP)
    wg = jnp.pad(wg, ((0, 0), (0, 0), (0, 0), (0, LANES - N_BRANCH * GQA_GROUP)))
    return w_main, wg.reshape(depth, D_MODEL, N_KV_HEADS * LANES).astype(BF16)


def _pick(n, prefs):
    for p in prefs:
        if n % p == 0:
            return p
    return n


def kernel(x, rel_bias, mix_norm_g, w_in, cmp_pos, cmp_w1, cmp_w2, conv_w, conv_b, conv_ln_g, conv_ln_b,
           w_out, ffn_norm_g, w_up, ffn_conv_w, ffn_conv_b, w_down, final_norm_g):
    bsz, t_len, _ = x.shape
    depth = w_in.shape[0]
    n = bsz * t_len
    assert t_len % 256 == 0 and t_len // SLC_LEN <= HEAD_DIM
    tq = 256
    tm = _pick(t_len, (1024, 512, 256))
    nc = t_len // CMP_STRIDE

    band, cmpb = _bias_tiles(rel_bias, t_len, tq)
    col_scale = jnp.concatenate([jnp.full((1, ATTN_WIDTH), Q_SCALE, F32),
                                 jnp.ones((1, PROJ_COLS - ATTN_WIDTH), F32)], axis=1)
    x2 = x.reshape(n, D_MODEL)
    w_main, wg = _in_weight_layout(w_in)
    w_out_b, w_up_b, w_down_b = w_out.astype(BF16), w_up.astype(BF16), w_down.astype(BF16)
    for l in range(depth):
        proj, gates = _in_projection(x2, mix_norm_g[l][None, :], w_main, wg, col_scale, layer=l, tm=tm, tn=768)
        proj3 = proj.reshape(bsz, t_len, PROJ_COLS)
        gates3 = gates.reshape(bsz, t_len, N_KV_HEADS * LANES)

        posx, w1big, w2big = _compress_weight_layout(cmp_pos[l], cmp_w1[l], cmp_w2[l])
        tokl = proj3[:, :, COL_KV:COL_KV + CMP_WIDTH].reshape(bsz, nc, CMP_STRIDE, CMP_WIDTH).transpose(0, 2, 1, 3)
        kvc, kvct = _compress(tokl, posx, w1big, w2big)

        attn = _attention(proj3, kvc, kvct, band, cmpb, gates3, tq=tq)
        conv = _conformer(proj3, conv_w[l], conv_b[l][None, :], conv_ln_g[l][None, :],
                          conv_ln_b[l][None, :], tt=256)
        x2 = _out_projection(x2, attn.reshape(n, ATTN_WIDTH), conv.reshape(n, CONV_CHANNELS),
                             w_out_b, layer=l, tm=tm, tn=1024)

        act = _ffn_up(x2, ffn_norm_g[l][None, :], w_up_b, ffn_conv_w[l],
                      ffn_conv_b[l][None, :], layer=l, tm=tm, tn=512, t_len=t_len)
        x2 = _ffn_down(act, w_down_b, x2, layer=l, tm=tm, tn=512)
    x2 = _final_norm(x2, final_norm_g[None, :], tm=_pick(n, (512, 256)))
    return x2.reshape(bsz, t_len, D_MODEL)
```

```python
import functools
import math

import numpy as np
import jax
import jax.numpy as jnp
from jax import lax
from jax.experimental import pallas as pl
from jax.experimental.pallas import tpu as pltpu

F32 = jnp.float32
BF16 = jnp.bfloat16

D_MODEL = 2048
HEAD_DIM = 64
N_KV_HEADS = 4
GQA_GROUP = 4
N_Q_HEADS = N_KV_HEADS * GQA_GROUP
N_BRANCH = 3
ATTN_WIDTH = N_Q_HEADS * HEAD_DIM
CMP_LEN = 32
CMP_STRIDE = 16
SLC_LEN = 64
SLC_TOPK = 16
WINDOW = 512
CONV_CHANNELS = D_MODEL - ATTN_WIDTH
CONV_TAPS = 31
D_FF = 5632
FFN_CONV_TAPS = 3
N_BUCKETS = 32
MAX_DISTANCE = 128
NORM_EPS = 1e-6

KV_PAIR = 2 * HEAD_DIM
KV_COLS = N_BRANCH * N_KV_HEADS * KV_PAIR
GATE_COLS = N_BRANCH * N_Q_HEADS
COL_Q = 0
COL_CONV_A = ATTN_WIDTH
COL_CONV_G = COL_CONV_A + CONV_CHANNELS
COL_KV = COL_CONV_G + CONV_CHANNELS
PROJ_COLS = COL_KV + KV_COLS

MASK_VALUE = -1e30
BLOCK_PENALTY = -1e9
KEY_BIG = 1e30
KEY_TAKEN = -3e38
LOG2E = math.log2(math.e)
Q_SCALE = HEAD_DIM ** -0.5 * LOG2E

LANES = 128
SUBLANES = 8
CONV_ROWS = 128
FFN_ROWS = 1024
VMEM_LIMIT = 56 * 1024 * 1024
CONV_HALO = 32
FFN_HALO = 16


def _t5_bucket_last_distance():
    n = np.arange(0, 4 * MAX_DISTANCE, dtype=np.int64)
    max_exact = N_BUCKETS // 2
    nf = np.maximum(n, 1).astype(np.float64)
    large = max_exact + np.floor(np.log(nf / max_exact) / math.log(MAX_DISTANCE / max_exact)
                                 * (N_BUCKETS - max_exact)).astype(np.int64)
    large = np.minimum(large, N_BUCKETS - 1)
    bucket = np.where(n < max_exact, n, large)
    last = []
    for b in range(N_BUCKETS - 1):
        idx = np.nonzero(bucket == b)[0]
        last.append(int(idx.max()) if idx.size else None)
    return last


_BUCKET_LAST = _t5_bucket_last_distance()


def _params(*sem):
    return pltpu.CompilerParams(dimension_semantics=sem, vmem_limit_bytes=VMEM_LIMIT)


def _sigmoid(x):
    return jax.nn.sigmoid(x)


def _rms_norm_rows(x, g):
    ms = jnp.mean(x * x, axis=-1, keepdims=True)
    return x * lax.rsqrt(ms + NORM_EPS) * g


def _bias_from_distance(dist, tab_ref, head):
    c_far = tab_ref[N_BUCKETS - 1, head]
    val = jnp.zeros(dist.shape, F32)
    for b in range(N_BUCKETS - 2, -1, -1):
        if _BUCKET_LAST[b] is None:
            continue
        val = jnp.where(dist <= _BUCKET_LAST[b], tab_ref[b, head] - c_far, val)
    return val


def _band_bias_body(tab_ref, o_ref, *, tq):
    head = pl.program_id(0) * GQA_GROUP + pl.program_id(1)
    shape = (WINDOW + tq, tq)
    dist = (lax.broadcasted_iota(jnp.int32, shape, 1) + WINDOW
            - lax.broadcasted_iota(jnp.int32, shape, 0))
    val = _bias_from_distance(dist, tab_ref, head) * LOG2E
    val = jnp.where(dist >= 0, jnp.where(dist < WINDOW, val, MASK_VALUE), MASK_VALUE)
    o_ref[0] = val


def _cmp_bias_body(tab_ref, o_ref, *, tq, nc):
    head = pl.program_id(0) * GQA_GROUP + pl.program_id(1)
    shape = (2 * nc, tq)
    r = lax.broadcasted_iota(jnp.int32, shape, 0)
    i = lax.broadcasted_iota(jnp.int32, shape, 1)
    dist = i - (r - nc) * CMP_STRIDE - (CMP_LEN - 1)
    val = _bias_from_distance(dist, tab_ref, head) * LOG2E
    o_ref[0] = jnp.where(dist >= 0, val, MASK_VALUE)


def _bias_tiles(rel_bias, t_len, tq):
    nc = t_len // CMP_STRIDE
    rows = GQA_GROUP * tq
    smem = pl.BlockSpec(memory_space=pltpu.SMEM)
    band = pl.pallas_call(
        functools.partial(_band_bias_body, tq=tq),
        grid=(N_KV_HEADS, GQA_GROUP),
        in_specs=[smem],
        out_specs=pl.BlockSpec((1, WINDOW + tq, tq), lambda h, g: (h, 0, g)),
        out_shape=jax.ShapeDtypeStruct((N_KV_HEADS, WINDOW + tq, rows), F32),
        compiler_params=_params("arbitrary", "arbitrary"),
        name="band_bias",
    )(rel_bias)
    cmpb = pl.pallas_call(
        functools.partial(_cmp_bias_body, tq=tq, nc=nc),
        grid=(N_KV_HEADS, GQA_GROUP),
        in_specs=[smem],
        out_specs=pl.BlockSpec((1, 2 * nc, tq), lambda h, g: (h, 0, g)),
        out_shape=jax.ShapeDtypeStruct((N_KV_HEADS, 2 * nc, rows), F32),
        compiler_params=_params("arbitrary", "arbitrary"),
        name="cmp_bias",
    )(rel_bias)
    return band, cmpb


def _inproj_body(x_ref, g_ref, w_ref, wg_ref, cs_ref, o_ref, gate_ref, h_ref):
    @pl.when(pl.program_id(1) == 0)
    def _():
        h = _rms_norm_rows(x_ref[...], g_ref[...]).astype(BF16)
        h_ref[...] = h
        gate_ref[...] = _sigmoid(jnp.dot(h, wg_ref[...], preferred_element_type=F32))

    acc = jnp.dot(h_ref[...], w_ref[...], preferred_element_type=F32)
    o_ref[...] = (acc * cs_ref[...]).astype(o_ref.dtype)


def _in_projection(x2, g, w, wg, col_scale, *, layer, tm, tn):
    n = x2.shape[0]
    gcols = wg.shape[-1]
    return pl.pallas_call(
        _inproj_body,
        grid=(n // tm, PROJ_COLS // tn),
        in_specs=[
            pl.BlockSpec((tm, D_MODEL), lambda i, j: (i, 0)),
            pl.BlockSpec((1, D_MODEL), lambda i, j: (0, 0)),
            pl.BlockSpec((None, D_MODEL, tn), lambda i, j: (layer, 0, j)),
            pl.BlockSpec((None, D_MODEL, gcols), lambda i, j: (layer, 0, 0)),
            pl.BlockSpec((1, tn), lambda i, j: (0, j)),
        ],
        out_specs=[
            pl.BlockSpec((tm, tn), lambda i, j: (i, j)),
            pl.BlockSpec((tm, gcols), lambda i, j: (i, 0)),
        ],
        out_shape=[
            jax.ShapeDtypeStruct((n, PROJ_COLS), BF16),
            jax.ShapeDtypeStruct((n, gcols), F32),
        ],
        scratch_shapes=[pltpu.VMEM((tm, D_MODEL), BF16)],
        compiler_params=_params("arbitrary", "arbitrary"),
        name="in_projection",
    )(x2, g, w, wg, col_scale)


CMP_WIDTH = N_KV_HEADS * KV_PAIR


def _compress_body(x_ref, pos_ref, w1_ref, w2_ref, o_ref, ot_ref, top_ref, bot_ref, *, nc):
    l = pl.program_id(1)

    @pl.when(l == 0)
    def _():
        top_ref[...] = jnp.zeros_like(top_ref)
        bot_ref[...] = jnp.zeros_like(bot_ref)

    x = x_ref[0, 0].astype(F32)
    top_ref[...] += jnp.dot((x + pos_ref[0, 0]).astype(BF16), w1_ref[0, 0], preferred_element_type=F32)
    bot_ref[...] += jnp.dot((x + pos_ref[0, 1]).astype(BF16), w1_ref[0, 1], preferred_element_type=F32)

    @pl.when(l == pl.num_programs(1) - 1)
    def _():
        pre = top_ref[...] + pltpu.roll(bot_ref[...], nc - 1, axis=0)
        act = pre * _sigmoid(pre)
        out = jnp.dot(act.astype(BF16), w2_ref[...], preferred_element_type=F32)
        for h in range(N_KV_HEADS):
            kv = out[:, h * KV_PAIR:(h + 1) * KV_PAIR]
            o_ref[0, h] = kv.astype(o_ref.dtype)
            ot_ref[0, h] = kv.T.astype(ot_ref.dtype)


def _compress(tokl, posx, w1big, w2big):
    b, _, nc, _ = tokl.shape
    return pl.pallas_call(
        functools.partial(_compress_body, nc=nc),
        grid=(b, CMP_STRIDE),
        in_specs=[
            pl.BlockSpec((1, 1, nc, CMP_WIDTH), lambda i, l: (i, l, 0, 0)),
            pl.BlockSpec((1, 2, 1, CMP_WIDTH), lambda i, l: (l, 0, 0, 0)),
            pl.BlockSpec((1, 2, CMP_WIDTH, CMP_WIDTH), lambda i, l: (l, 0, 0, 0)),
            pl.BlockSpec((CMP_WIDTH, CMP_WIDTH), lambda i, l: (0, 0)),
        ],
        out_specs=[
            pl.BlockSpec((1, N_KV_HEADS, nc, KV_PAIR), lambda i, l: (i, 0, 0, 0)),
            pl.BlockSpec((1, N_KV_HEADS, KV_PAIR, nc), lambda i, l: (i, 0, 0, 0)),
        ],
        out_shape=[
            jax.ShapeDtypeStruct((b, N_KV_HEADS, nc, KV_PAIR), BF16),
            jax.ShapeDtypeStruct((b, N_KV_HEADS, KV_PAIR, nc), BF16),
        ],
        scratch_shapes=[pltpu.VMEM((nc, CMP_WIDTH), F32), pltpu.VMEM((nc, CMP_WIDTH), F32)],
        compiler_params=_params("arbitrary", "arbitrary"),
        name="compress",
    )(tokl, posx, w1big, w2big)


def _compress_weight_layout(pos, w1, w2):
    def block_diag(blocks):
        rows = []
        for h in range(N_KV_HEADS):
            for kv in range(2):
                off = (h * 2 + kv) * HEAD_DIM
                pad = [(0, 0)] * (blocks.ndim - 2) + [(off, CMP_WIDTH - HEAD_DIM - off)]
                rows.append(jnp.pad(blocks[kv], pad))
        return jnp.concatenate(rows, axis=-2).astype(BF16)

    w1r = w1.reshape(2, 2, CMP_STRIDE, HEAD_DIM, HEAD_DIM)
    w1big = block_diag(w1r.transpose(0, 2, 1, 3, 4))
    w2big = block_diag(w2)
    posr = pos.reshape(2, 2, CMP_STRIDE, HEAD_DIM).transpose(2, 1, 0, 3)
    posx = jnp.broadcast_to(posr[:, :, None], (CMP_STRIDE, 2, N_KV_HEADS, 2, HEAD_DIM))
    return posx.reshape(CMP_STRIDE, 2, 1, CMP_WIDTH), w1big, w2big


VT_ROWS = HEAD_DIM + 16


def _attn_body(q_ref, kvc_ref, kvct_ref, kvs_ref, kvw_ref, bt_ref, bc_ref, gate_ref, o_ref,
               kp_ref, vst_ref, vwt_ref, key_ref, m_ref, acc_ref, sa_ref, sb_ref, ocmp_ref, qsel_ref, qwin_ref,
               *, tq, t_len):
    qi = pl.program_id(2)
    nc = t_len // CMP_STRIDE
    ns = t_len // SLC_LEN
    rows = GQA_GROUP * tq
    blocks_per_tile = tq // SLC_LEN

    @pl.when(qi == 0)
    def _():
        k = kvs_ref[0][:, :HEAD_DIM]
        blk = lax.broadcasted_iota(jnp.int32, (t_len, HEAD_DIM), 0) // SLC_LEN
        col = lax.broadcasted_iota(jnp.int32, (t_len, HEAD_DIM), 1)
        onehot = jnp.where(blk == col, 1.0, 0.0).astype(BF16)
        kp_ref[...] = jnp.concatenate([k, onehot], axis=1)
        ones_rows = jnp.where(lax.broadcasted_iota(jnp.int32, (VT_ROWS - HEAD_DIM, tq), 0) == 0, 1.0, 0.0)
        for c in range(t_len // tq):
            st = kvs_ref[0, c * tq:(c + 1) * tq, :].astype(F32).T[HEAD_DIM:]
            vst_ref[c] = jnp.concatenate([st, ones_rows], axis=0).astype(BF16)
            wt = kvw_ref[0, c * tq:(c + 1) * tq, :].astype(F32).T[HEAD_DIM:]
            vwt_ref[c] = jnp.concatenate([wt, ones_rows], axis=0).astype(BF16)

    q_t = q_ref[0].astype(F32).T
    q4_t = jnp.concatenate([q_t[g * HEAD_DIM:(g + 1) * HEAD_DIM] for g in range(GQA_GROUP)],
                           axis=1).astype(BF16)
    qwin_ref[...] = jnp.concatenate([q4_t, jnp.zeros((HEAD_DIM, rows), BF16)], axis=0)
    qsel_ref[0:HEAD_DIM, :] = q4_t
    kw_ref = kvw_ref.at[0]
    win, sel = 0, 1

    def cmp_scores():
        bias_row = pl.multiple_of(nc - qi * (tq // CMP_STRIDE), tq // CMP_STRIDE)
        return (jnp.dot(kvc_ref[0, 0], qwin_ref[...], preferred_element_type=F32)
                + bc_ref[0, pl.ds(bias_row, nc), :])

    def cmp_finish(s):
        m = jnp.max(s, axis=0, keepdims=True)
        p = jnp.exp2(s - m)
        l = jnp.sum(p, axis=0, keepdims=True)
        pn = p * jnp.where(m > 0.5 * MASK_VALUE, 1.0 / l, 0.0)
        ocmp_ref[...] = jnp.dot(kvct_ref[0, 0, HEAD_DIM:, :], pn.astype(BF16),
                                preferred_element_type=F32)

        ps = pn[:, 0:tq] + pn[:, tq:2 * tq] + pn[:, 2 * tq:3 * tq] + pn[:, 3 * tq:4 * tq]
        sj = lax.broadcasted_iota(jnp.int32, (HEAD_DIM, nc), 0) * SLC_LEN
        ci = lax.broadcasted_iota(jnp.int32, (HEAD_DIM, nc), 1) * CMP_STRIDE
        overlap = jnp.where(ci < sj + SLC_LEN, jnp.where(ci + CMP_LEN > sj, 1.0, 0.0), 0.0).astype(BF16)
        p_hi = ps.astype(BF16)
        r_hi = ps - p_hi.astype(F32)
        p_md = r_hi.astype(BF16)
        p_lo = (r_hi - p_md.astype(F32)).astype(BF16)
        imp = (jnp.dot(overlap, p_hi, preferred_element_type=F32)
               + jnp.dot(overlap, p_md, preferred_element_type=F32)
               + jnp.dot(overlap, p_lo, preferred_element_type=F32))
        t = qi * tq + lax.broadcasted_iota(jnp.int32, (HEAD_DIM, tq), 1)
        blk = lax.broadcasted_iota(jnp.int32, (HEAD_DIM, tq), 0)
        cur = t // SLC_LEN
        key = jnp.where(blk == 0, KEY_BIG, jnp.where(blk == cur, KEY_BIG, jnp.where(blk == cur - 1, KEY_BIG, imp)))
        key_ref[...] = jnp.where(blk * SLC_LEN <= t, key, -KEY_BIG)

    def select_blocks():
        key = key_ref[...]
        blk = lax.broadcasted_iota(jnp.int32, (HEAD_DIM, tq), 0)
        pen = jnp.full((HEAD_DIM, tq), BLOCK_PENALTY, F32)
        for _ in range(min(SLC_TOPK, ns)):
            top = jnp.max(key, axis=0, keepdims=True)
            first = jnp.min(jnp.where(key == top, blk, HEAD_DIM), axis=0, keepdims=True)
            hit = blk == first
            pen = jnp.where(hit, 0.0, pen)
            key = jnp.where(hit, KEY_TAKEN, key)
        qsel_ref[HEAD_DIM:, :] = jnp.concatenate([pen.astype(BF16)] * GQA_GROUP, axis=1)

    def scores(qx_ref, k_ref, kj, bias_off):
        start = pl.multiple_of(kj * tq, tq)
        sc = jnp.dot(k_ref[pl.ds(start, tq), :], qx_ref[...], preferred_element_type=F32)
        if bias_off is not None:
            sc = sc + bt_ref[0, bias_off:bias_off + tq, :]
        return sc

    def consume(sc, vt_ref, kj, br, st, first):
        m_tile = jnp.max(sc, axis=0, keepdims=True)
        if first:
            m_new = m_tile
            pr = jnp.exp2(sc - m_new).astype(BF16)
            acc_ref[br, st] = jnp.dot(vt_ref[kj], pr, preferred_element_type=F32)
        else:
            m_old = m_ref[br, st]
            m_new = jnp.maximum(m_old, m_tile)
            alpha = jnp.exp2(m_old - m_new)
            pr = jnp.exp2(sc - m_new).astype(BF16)
            acc_ref[br, st] = alpha * acc_ref[br, st] + jnp.dot(vt_ref[kj], pr, preferred_element_type=F32)
        m_ref[br, st] = m_new

    def finish(br):
        m0 = m_ref[br, 0]
        m1 = m_ref[br, 1]
        m_all = jnp.maximum(m0, m1)
        acc = jnp.exp2(m0 - m_all) * acc_ref[br, 0] + jnp.exp2(m1 - m_all) * acc_ref[br, 1]
        return acc[0:HEAD_DIM] * (1.0 / acc[HEAD_DIM:HEAD_DIM + 1, :])

    def combine():
        o_win = finish(win)
        o_sel = finish(sel)
        o_cmp = ocmp_ref[...]
        gates_t = gate_ref[0].T
        combs = []
        for g in range(GQA_GROUP):
            sl = slice(g * tq, (g + 1) * tq)
            combs.append(gates_t[g:g + 1, :] * o_cmp[:, sl]
                         + gates_t[GQA_GROUP + g:GQA_GROUP + g + 1, :] * o_sel[:, sl]
                         + gates_t[2 * GQA_GROUP + g:2 * GQA_GROUP + g + 1, :] * o_win[:, sl])
        outs = [jnp.concatenate(combs[p:p + 2], axis=0).T for p in range(0, GQA_GROUP, 2)]
        o_ref[0] = jnp.concatenate(outs, axis=1).astype(o_ref.dtype)

    @pl.when(qi >= 2)
    def _():
        sc_cmp = cmp_scores()
        sw0 = scores(qwin_ref, kw_ref, qi, WINDOW)
        cmp_finish(sc_cmp)
        sw1 = scores(qwin_ref, kw_ref, qi - 1, WINDOW - tq)
        consume(sw0, vwt_ref, qi, win, 0, True)
        sw2 = scores(qwin_ref, kw_ref, qi - 2, WINDOW - 2 * tq)
        consume(sw1, vwt_ref, qi - 1, win, 1, True)
        select_blocks()
        consume(sw2, vwt_ref, qi - 2, win, 0, False)

        ss0 = scores(qsel_ref, kp_ref, qi, WINDOW)
        ss1 = scores(qsel_ref, kp_ref, qi - 1, WINDOW - tq)
        consume(ss0, vst_ref, qi, sel, 0, True)
        sa_ref[...] = scores(qsel_ref, kp_ref, 0, None)
        consume(ss1, vst_ref, qi - 1, sel, 1, True)
        n_far = qi - 1

        def far_pair(pi):
            sb_ref[...] = scores(qsel_ref, kp_ref, 2 * pi + 1, None)
            consume(sa_ref[...], vst_ref, 2 * pi, sel, 1, False)
            sa_ref[...] = scores(qsel_ref, kp_ref, jnp.minimum(2 * pi + 2, n_far - 1), None)
            consume(sb_ref[...], vst_ref, 2 * pi + 1, sel, 0, False)

        n_pairs = n_far // 2

        def far_quad(qd, carry):
            far_pair(2 * qd)
            far_pair(2 * qd + 1)
            return carry

        lax.fori_loop(0, n_pairs // 2, far_quad, 0)

        @pl.when(n_pairs % 2 == 1)
        def _():
            far_pair(n_pairs - 1)

        @pl.when(n_far % 2 == 1)
        def _():
            consume(sa_ref[...], vst_ref, n_far - 1, sel, 1, False)

        combine()

    @pl.when(qi < 2)
    def _():
        cmp_finish(cmp_scores())
        select_blocks()
        sw0 = scores(qwin_ref, kw_ref, qi, WINDOW)
        ss0 = scores(qsel_ref, kp_ref, qi, WINDOW)
        consume(sw0, vwt_ref, qi, win, 0, True)
        consume(ss0, vst_ref, qi, sel, 0, True)
        for br in (win, sel):
            m_ref[br, 1] = jnp.full((1, rows), MASK_VALUE, F32)
            acc_ref[br, 1] = jnp.zeros((VT_ROWS, rows), F32)

        @pl.when(qi == 1)
        def _():
            sw1 = scores(qwin_ref, kw_ref, 0, WINDOW - tq)
            ss1 = scores(qsel_ref, kp_ref, 0, WINDOW - tq)
            consume(sw1, vwt_ref, 0, win, 1, False)
            consume(ss1, vst_ref, 0, sel, 1, False)

        combine()


def _attention(proj3, kvc, kvct, band, cmpb, gates3, *, tq):
    b, t_len, _ = proj3.shape
    assert WINDOW == 2 * tq
    nc = t_len // CMP_STRIDE
    rows = GQA_GROUP * tq
    qw = GQA_GROUP * HEAD_DIM
    slc_blk = COL_KV // KV_PAIR + N_KV_HEADS
    win_blk = COL_KV // KV_PAIR + 2 * N_KV_HEADS
    return pl.pallas_call(
        functools.partial(_attn_body, tq=tq, t_len=t_len),
        grid=(b, N_KV_HEADS, t_len // tq),
        in_specs=[
            pl.BlockSpec((1, tq, qw), lambda i, h, q: (i, q, h)),
            pl.BlockSpec((1, 1, nc, KV_PAIR), lambda i, h, q: (i, h, 0, 0)),
            pl.BlockSpec((1, 1, KV_PAIR, nc), lambda i, h, q: (i, h, 0, 0)),
            pl.BlockSpec((1, t_len, KV_PAIR), lambda i, h, q: (i, 0, slc_blk + h)),
            pl.BlockSpec((1, t_len, KV_PAIR), lambda i, h, q: (i, 0, win_blk + h)),
            pl.BlockSpec((1, WINDOW + tq, rows), lambda i, h, q: (h, 0, 0)),
            pl.BlockSpec((1, 2 * nc, rows), lambda i, h, q: (h, 0, 0)),
            pl.BlockSpec((1, tq, LANES), lambda i, h, q: (i, q, h)),
        ],
        out_specs=pl.BlockSpec((1, tq, qw), lambda i, h, q: (i, q, h)),
        out_shape=jax.ShapeDtypeStruct((b, t_len, ATTN_WIDTH), BF16),
        scratch_shapes=[
            pltpu.VMEM((t_len, KV_PAIR), BF16),
            pltpu.VMEM((t_len // tq, VT_ROWS, tq), BF16),
            pltpu.VMEM((t_len // tq, VT_ROWS, tq), BF16),
            pltpu.VMEM((HEAD_DIM, tq), F32),
            pltpu.VMEM((2, 2, 1, rows), F32),
            pltpu.VMEM((2, 2, VT_ROWS, rows), F32),
            pltpu.VMEM((tq, rows), F32),
            pltpu.VMEM((tq, rows), F32),
            pltpu.VMEM((HEAD_DIM, rows), F32),
            pltpu.VMEM((KV_PAIR, rows), BF16),
            pltpu.VMEM((KV_PAIR, rows), BF16),
        ],
        compiler_params=_params("arbitrary", "arbitrary", "arbitrary"),
        name="nsa_attention",
    )(proj3, kvc, kvct, proj3, proj3, band, cmpb, gates3)


def _conformer_body(a_ref, g_ref, ah_ref, gh_ref, w_ref, b_ref, lg_ref, lb_ref, o_ref, u_ref, y_ref, *, tt):
    ti = pl.program_id(1)
    u_ref[CONV_HALO:, :] = a_ref[0].astype(F32) * _sigmoid(g_ref[0].astype(F32))
    halo = ah_ref[0].astype(F32) * _sigmoid(gh_ref[0].astype(F32))
    u_ref[0:CONV_HALO, :] = jnp.where(ti > 0, halo, 0.0)
    base = CONV_HALO - (CONV_TAPS - 1)
    ext = CONV_ROWS + CONV_HALO
    for cb in range(CONV_CHANNELS // LANES):
        cs = slice(cb * LANES, (cb + 1) * LANES)
        wblk = w_ref[:, cs]
        for rb in range(tt // CONV_ROWS):
            ublk = u_ref[rb * CONV_ROWS:rb * CONV_ROWS + ext, cs]
            acc = jnp.zeros((CONV_ROWS, LANES), F32)
            for r in range(SUBLANES):
                ur = ublk if r == 0 else pltpu.roll(ublk, ext - r, axis=0)
                for a in range(CONV_HALO // SUBLANES + 1):
                    k = SUBLANES * a + r - base
                    if 0 <= k < CONV_TAPS:
                        acc = acc + wblk[k:k + 1, :] * ur[SUBLANES * a:SUBLANES * a + CONV_ROWS]
            y_ref[rb * CONV_ROWS:(rb + 1) * CONV_ROWS, cs] = acc
    acc = y_ref[...] + b_ref[...]
    mu = jnp.mean(acc, axis=-1, keepdims=True)
    xc = acc - mu
    var = jnp.mean(xc * xc, axis=-1, keepdims=True)
    y = xc * lax.rsqrt(var + NORM_EPS) * lg_ref[...] + lb_ref[...]
    o_ref[0] = (y * _sigmoid(y)).astype(o_ref.dtype)


def _conformer(proj3, w, b, lg, lb, *, tt):
    bsz, t_len, _ = proj3.shape
    a_blk = COL_CONV_A // CONV_CHANNELS
    g_blk = COL_CONV_G // CONV_CHANNELS
    hpt = tt // CONV_HALO
    halo_idx = lambda i, t: jnp.maximum(t * hpt - 1, 0)
    vec = pl.BlockSpec((1, CONV_CHANNELS), lambda i, t: (0, 0))
    return pl.pallas_call(
        functools.partial(_conformer_body, tt=tt),
        grid=(bsz, t_len // tt),
        in_specs=[
            pl.BlockSpec((1, tt, CONV_CHANNELS), lambda i, t: (i, t, a_blk)),
            pl.BlockSpec((1, tt, CONV_CHANNELS), lambda i, t: (i, t, g_blk)),
            pl.BlockSpec((1, CONV_HALO, CONV_CHANNELS), lambda i, t: (i, halo_idx(i, t), a_blk)),
            pl.BlockSpec((1, CONV_HALO, CONV_CHANNELS), lambda i, t: (i, halo_idx(i, t), g_blk)),
            pl.BlockSpec((CONV_TAPS, CONV_CHANNELS), lambda i, t: (0, 0)),
            vec, vec, vec,
        ],
        out_specs=pl.BlockSpec((1, tt, CONV_CHANNELS), lambda i, t: (i, t, 0)),
        out_shape=jax.ShapeDtypeStruct((bsz, t_len, CONV_CHANNELS), BF16),
        scratch_shapes=[pltpu.VMEM((CONV_HALO + tt, CONV_CHANNELS), F32),
                        pltpu.VMEM((tt, CONV_CHANNELS), F32)],
        compiler_params=_params("arbitrary", "arbitrary"),
        name="conformer_conv",
    )(proj3, proj3, proj3, proj3, w, b, lg, lb)


def _outproj_body(x_ref, a_ref, c_ref, wa_ref, wc_ref, o_ref):
    o_ref[...] = (x_ref[...]
                  + jnp.dot(a_ref[...], wa_ref[...], preferred_element_type=F32)
                  + jnp.dot(c_ref[...], wc_ref[...], preferred_element_type=F32))


def _out_projection(x2, attn2, conv2, w_out, *, layer, tm, tn):
    n = x2.shape[0]
    return pl.pallas_call(
        _outproj_body,
        grid=(n // tm, D_MODEL // tn),
        in_specs=[
            pl.BlockSpec((tm, tn), lambda i, j: (i, j)),
            pl.BlockSpec((tm, ATTN_WIDTH), lambda i, j: (i, 0)),
            pl.BlockSpec((tm, CONV_CHANNELS), lambda i, j: (i, 0)),
            pl.BlockSpec((None, ATTN_WIDTH, tn), lambda i, j: (layer, 0, j)),
            pl.BlockSpec((None, CONV_CHANNELS, tn), lambda i, j: (layer, ATTN_WIDTH // CONV_CHANNELS, j)),
        ],
        out_specs=pl.BlockSpec((tm, tn), lambda i, j: (i, j)),
        out_shape=jax.ShapeDtypeStruct((n, D_MODEL), F32),
        compiler_params=_params("arbitrary", "arbitrary"),
        name="out_projection",
    )(x2, attn2, conv2, w_out, w_out)


def _ffn_up_body(x_ref, xh_ref, g_ref, wa_ref, wg_ref, cwa_ref, cwg_ref, cba_ref, cbg_ref, o_ref,
                 h_ref, ua_ref, ug_ref, *, tm, tiles_per_seq, rows):
    i = pl.program_id(0)

    @pl.when(pl.program_id(1) == 0)
    def _():
        h_ref[FFN_HALO:, :] = _rms_norm_rows(x_ref[...], g_ref[...]).astype(BF16)
        hh = _rms_norm_rows(xh_ref[...], g_ref[...])
        h_ref[0:FFN_HALO, :] = jnp.where(i % tiles_per_seq != 0, hh, 0.0).astype(BF16)

    units = [(c, slice(0, o_ref.shape[1])) for c in range(tm // rows)]

    def project(c, cs):
        hc = h_ref[c * rows:(c + 1) * rows + FFN_HALO, :]
        ua_ref[c, :, cs] = jnp.dot(hc, wa_ref[:, cs], preferred_element_type=F32)
        ug_ref[c, :, cs] = jnp.dot(hc, wg_ref[:, cs], preferred_element_type=F32)

    def conv(u_ref, c, cs, cw_ref, cb_ref):
        u = u_ref[c, :, cs]
        y = cw_ref[FFN_CONV_TAPS - 1:FFN_CONV_TAPS, cs] * u[FFN_HALO:]
        for s in range(1, FFN_CONV_TAPS):
            k = FFN_CONV_TAPS - 1 - s
            y = y + cw_ref[k:k + 1, cs] * pltpu.roll(u, s, axis=0)[FFN_HALO:]
        return y + cb_ref[:, cs]

    def finish(c, cs):
        a = conv(ua_ref, c, cs, cwa_ref, cba_ref)
        gate = conv(ug_ref, c, cs, cwg_ref, cbg_ref)
        o_ref[c * rows:(c + 1) * rows, cs] = (gate * _sigmoid(gate) * a).astype(o_ref.dtype)

    project(*units[0])
    for u, unit in enumerate(units):
        if u + 1 < len(units):
            project(*units[u + 1])
        finish(*unit)


def _ffn_up(x2, g, w_up, cw, cb, *, layer, tm, tn, t_len):
    n = x2.shape[0]
    nj = D_FF // tn
    hpt = tm // FFN_HALO
    rows = min(FFN_ROWS, tm)
    return pl.pallas_call(
        functools.partial(_ffn_up_body, tm=tm, tiles_per_seq=t_len // tm, rows=rows),
        grid=(n // tm, nj),
        in_specs=[
            pl.BlockSpec((tm, D_MODEL), lambda i, j: (i, 0)),
            pl.BlockSpec((FFN_HALO, D_MODEL), lambda i, j: (jnp.maximum(i * hpt - 1, 0), 0)),
            pl.BlockSpec((1, D_MODEL), lambda i, j: (0, 0)),
            pl.BlockSpec((None, D_MODEL, tn), lambda i, j: (layer, 0, j)),
            pl.BlockSpec((None, D_MODEL, tn), lambda i, j: (layer, 0, j + nj)),
            pl.BlockSpec((FFN_CONV_TAPS, tn), lambda i, j: (0, j)),
            pl.BlockSpec((FFN_CONV_TAPS, tn), lambda i, j: (0, j + nj)),
            pl.BlockSpec((1, tn), lambda i, j: (0, j)),
            pl.BlockSpec((1, tn), lambda i, j: (0, j + nj)),
        ],
        out_specs=pl.BlockSpec((tm, tn), lambda i, j: (i, j)),
        out_shape=jax.ShapeDtypeStruct((n, D_FF), BF16),
        scratch_shapes=[
            pltpu.VMEM((FFN_HALO + tm, D_MODEL), BF16),
            pltpu.VMEM((tm // rows, FFN_HALO + rows, tn), F32),
            pltpu.VMEM((tm // rows, FFN_HALO + rows, tn), F32),
        ],
        compiler_params=_params("arbitrary", "arbitrary"),
        name="ffn_up",
    )(x2, x2, g, w_up, w_up, cw, cw, cb, cb)


def _ffn_down_body(act_ref, w_ref, x_ref, o_ref):
    o_ref[...] = x_ref[...] + jnp.dot(act_ref[...], w_ref[...], preferred_element_type=F32)


def _ffn_down(act, w_down, x2, *, layer, tm, tn):
    n = x2.shape[0]
    return pl.pallas_call(
        _ffn_down_body,
        grid=(n // tm, D_MODEL // tn),
        in_specs=[
            pl.BlockSpec((tm, D_FF), lambda i, j: (i, 0)),
            pl.BlockSpec((None, D_FF, tn), lambda i, j: (layer, 0, j)),
            pl.BlockSpec((tm, tn), lambda i, j: (i, j)),
        ],
        out_specs=pl.BlockSpec((tm, tn), lambda i, j: (i, j)),
        out_shape=jax.ShapeDtypeStruct((n, D_MODEL), F32),
        compiler_params=_params("arbitrary", "arbitrary"),
        name="ffn_down",
    )(act, w_down, x2)


def _final_norm_body(x_ref, g_ref, o_ref):
    o_ref[...] = _rms_norm_rows(x_ref[...], g_ref[...])


def _final_norm(x2, g, *, tm):
    n = x2.shape[0]
    return pl.pallas_call(
        _final_norm_body,
        grid=(n // tm,),
        in_specs=[pl.BlockSpec((tm, D_MODEL), lambda i: (i, 0)), pl.BlockSpec((1, D_MODEL), lambda i: (0, 0))],
        out_specs=pl.BlockSpec((tm, D_MODEL), lambda i: (i, 0)),
        out_shape=jax.ShapeDtypeStruct((n, D_MODEL), F32),
        compiler_params=_params("arbitrary"),
        name="final_norm",
    )(x2, g)


def _in_weight_layout(w_in):
    depth = w_in.shape[0]
    kv0 = ATTN_WIDTH
    gate0 = kv0 + KV_COLS
    conv0 = gate0 + GATE_COLS
    parts = [w_in[..., :ATTN_WIDTH], w_in[..., conv0:conv0 + 2 * CONV_CHANNELS]]
    for br in range(N_BRANCH):
        for h in range(N_KV_HEADS):
            for kv in range(2):
                c0 = kv0 + ((br * 2 + kv) * N_KV_HEADS + h) * HEAD_DIM
                parts.append(w_in[..., c0:c0 + HEAD_DIM])
    w_main = jnp.concatenate(parts, axis=-1).astype(BF16)
    wg = w_in[..., gate0:gate0 + GATE_COLS].reshape(depth, D_MODEL, N_KV_HEADS, GQA_GROUP, N_BRANCH)
    wg = wg.transpose(0, 1, 2, 4, 3).reshape(depth, D_MODEL, N_KV_HEADS, N_BRANCH * GQA_GROUP)
    wg = jnp.pad(wg, ((0, 0), (0, 0), (0, 0), (0, LANES - N_BRANCH * GQA_GROUP)))
    return w_main, wg.reshape(depth, D_MODEL, N_KV_HEADS * LANES).astype(BF16)


def _pick(n, prefs):
    for p in prefs:
        if n % p == 0:
            return p
    return n


def kernel(x, rel_bias, mix_norm_g, w_in, cmp_pos, cmp_w1, cmp_w2, conv_w, conv_b, conv_ln_g, conv_ln_b,
           w_out, ffn_norm_g, w_up, ffn_conv_w, ffn_conv_b, w_down, final_norm_g):
    bsz, t_len, _ = x.shape
    depth = w_in.shape[0]
    n = bsz * t_len
    assert t_len % 256 == 0 and t_len // SLC_LEN <= HEAD_DIM
    tq = 256
    tm = _pick(t_len, (1024, 512, 256))
    nc = t_len // CMP_STRIDE

    band, cmpb = _bias_tiles(rel_bias, t_len, tq)
    col_scale = jnp.concatenate([jnp.full((1, ATTN_WIDTH), Q_SCALE, F32),
                                 jnp.ones((1, PROJ_COLS - ATTN_WIDTH), F32)], axis=1)
    x2 = x.reshape(n, D_MODEL)
    w_main, wg = _in_weight_layout(w_in)
    w_out_b, w_up_b, w_down_b = w_out.astype(BF16), w_up.astype(BF16), w_down.astype(BF16)
    for l in range(depth):
        proj, gates = _in_projection(x2, mix_norm_g[l][None, :], w_main, wg, col_scale, layer=l, tm=tm, tn=768)
        proj3 = proj.reshape(bsz, t_len, PROJ_COLS)
        gates3 = gates.reshape(bsz, t_len, N_KV_HEADS * LANES)

        posx, w1big, w2big = _compress_weight_layout(cmp_pos[l], cmp_w1[l], cmp_w2[l])
        tokl = proj3[:, :, COL_KV:COL_KV + CMP_WIDTH].reshape(bsz, nc, CMP_STRIDE, CMP_WIDTH).transpose(0, 2, 1, 3)
        kvc, kvct = _compress(tokl, posx, w1big, w2big)

        attn = _attention(proj3, kvc, kvct, band, cmpb, gates3, tq=tq)
        conv = _conformer(proj3, conv_w[l], conv_b[l][None, :], conv_ln_g[l][None, :],
                          conv_ln_b[l][None, :], tt=256)
        x2 = _out_projection(x2, attn.reshape(n, ATTN_WIDTH), conv.reshape(n, CONV_CHANNELS),
                             w_out_b, layer=l, tm=tm, tn=1024)

        act = _ffn_up(x2, ffn_norm_g[l][None, :], w_up_b, ffn_conv_w[l],
                      ffn_conv_b[l][None, :], layer=l, tm=tm, tn=512, t_len=t_len)
        x2 = _ffn_down(act, w_down_b, x2, layer=l, tm=tm, tn=512)
    x2 = _final_norm(x2, final_norm_g[None, :], tm=_pick(n, (512, 256)))
    return x2.reshape(bsz, t_len, D_MODEL)
```

```python
import functools
import math

import numpy as np
import jax
import jax.numpy as jnp
from jax import lax
from jax.experimental import pallas as pl
from jax.experimental.pallas import tpu as pltpu

F32 = jnp.float32
BF16 = jnp.bfloat16

D_MODEL = 2048
HEAD_DIM = 64
N_KV_HEADS = 4
GQA_GROUP = 4
N_Q_HEADS = N_KV_HEADS * GQA_GROUP
N_BRANCH = 3
ATTN_WIDTH = N_Q_HEADS * HEAD_DIM
CMP_LEN = 32
CMP_STRIDE = 16
SLC_LEN = 64
SLC_TOPK = 16
WINDOW = 512
CONV_CHANNELS = D_MODEL - ATTN_WIDTH
CONV_TAPS = 31
D_FF = 5632
FFN_CONV_TAPS = 3
N_BUCKETS = 32
MAX_DISTANCE = 128
NORM_EPS = 1e-6

KV_PAIR = 2 * HEAD_DIM
KV_COLS = N_BRANCH * N_KV_HEADS * KV_PAIR
GATE_COLS = N_BRANCH * N_Q_HEADS
COL_Q = 0
COL_CONV_A = ATTN_WIDTH
COL_CONV_G = COL_CONV_A + CONV_CHANNELS
COL_KV = COL_CONV_G + CONV_CHANNELS
PROJ_COLS = COL_KV + KV_COLS

MASK_VALUE = -1e30
BLOCK_PENALTY = -1e9
KEY_BIG = 1e30
KEY_TAKEN = -3e38
LOG2E = math.log2(math.e)
Q_SCALE = HEAD_DIM ** -0.5 * LOG2E

LANES = 128
SUBLANES = 8
CONV_ROWS = 128
FFN_ROWS = 1024
VMEM_LIMIT = 56 * 1024 * 1024
CONV_HALO = 32
FFN_HALO = 16


def _t5_bucket_last_distance():
    n = np.arange(0, 4 * MAX_DISTANCE, dtype=np.int64)
    max_exact = N_BUCKETS // 2
    nf = np.maximum(n, 1).astype(np.float64)
    large = max_exact + np.floor(np.log(nf / max_exact) / math.log(MAX_DISTANCE / max_exact)
                                 * (N_BUCKETS - max_exact)).astype(np.int64)
    large = np.minimum(large, N_BUCKETS - 1)
    bucket = np.where(n < max_exact, n, large)
    last = []
    for b in range(N_BUCKETS - 1):
        idx = np.nonzero(bucket == b)[0]
        last.append(int(idx.max()) if idx.size else None)
    return last


_BUCKET_LAST = _t5_bucket_last_distance()


def _params(*sem):
    return pltpu.CompilerParams(dimension_semantics=sem, vmem_limit_bytes=VMEM_LIMIT)


def _sigmoid(x):
    return jax.nn.sigmoid(x)


def _rms_norm_rows(x, g):
    ms = jnp.mean(x * x, axis=-1, keepdims=True)
    return x * lax.rsqrt(ms + NORM_EPS) * g


def _bias_from_distance(dist, tab_ref, head):
    c_far = tab_ref[N_BUCKETS - 1, head]
    val = jnp.zeros(dist.shape, F32)
    for b in range(N_BUCKETS - 2, -1, -1):
        if _BUCKET_LAST[b] is None:
            continue
        val = jnp.where(dist <= _BUCKET_LAST[b], tab_ref[b, head] - c_far, val)
    return val


def _band_bias_body(tab_ref, o_ref, *, tq):
    head = pl.program_id(0) * GQA_GROUP + pl.program_id(1)
    shape = (WINDOW + tq, tq)
    dist = (lax.broadcasted_iota(jnp.int32, shape, 1) + WINDOW
            - lax.broadcasted_iota(jnp.int32, shape, 0))
    val = _bias_from_distance(dist, tab_ref, head) * LOG2E
    val = jnp.where(dist >= 0, jnp.where(dist < WINDOW, val, MASK_VALUE), MASK_VALUE)
    o_ref[0] = val


def _cmp_bias_body(tab_ref, o_ref, *, tq, nc):
    head = pl.program_id(0) * GQA_GROUP + pl.program_id(1)
    shape = (2 * nc, tq)
    r = lax.broadcasted_iota(jnp.int32, shape, 0)
    i = lax.broadcasted_iota(jnp.int32, shape, 1)
    dist = i - (r - nc) * CMP_STRIDE - (CMP_LEN - 1)
    val = _bias_from_distance(dist, tab_ref, head) * LOG2E
    o_ref[0] = jnp.where(dist >= 0, val, MASK_VALUE)


def _bias_tiles(rel_bias, t_len, tq):
    nc = t_len // CMP_STRIDE
    rows = GQA_GROUP * tq
    smem = pl.BlockSpec(memory_space=pltpu.SMEM)
    band = pl.pallas_call(
        functools.partial(_band_bias_body, tq=tq),
        grid=(N_KV_HEADS, GQA_GROUP),
        in_specs=[smem],
        out_specs=pl.BlockSpec((1, WINDOW + tq, tq), lambda h, g: (h, 0, g)),
        out_shape=jax.ShapeDtypeStruct((N_KV_HEADS, WINDOW + tq, rows), F32),
        compiler_params=_params("arbitrary", "arbitrary"),
        name="band_bias",
    )(rel_bias)
    cmpb = pl.pallas_call(
        functools.partial(_cmp_bias_body, tq=tq, nc=nc),
        grid=(N_KV_HEADS, GQA_GROUP),
        in_specs=[smem],
        out_specs=pl.BlockSpec((1, 2 * nc, tq), lambda h, g: (h, 0, g)),
        out_shape=jax.ShapeDtypeStruct((N_KV_HEADS, 2 * nc, rows), F32),
        compiler_params=_params("arbitrary", "arbitrary"),
        name="cmp_bias",
    )(rel_bias)
    return band, cmpb


def _inproj_body(x_ref, g_ref, w_ref, wg_ref, cs_ref, o_ref, gate_ref, h_ref):
    @pl.when(pl.program_id(1) == 0)
    def _():
        h = _rms_norm_rows(x_ref[...], g_ref[...]).astype(BF16)
        h_ref[...] = h
        gate_ref[...] = _sigmoid(jnp.dot(h, wg_ref[...], preferred_element_type=F32))

    acc = jnp.dot(h_ref[...], w_ref[...], preferred_element_type=F32)
    o_ref[...] = (acc * cs_ref[...]).astype(o_ref.dtype)


def _in_projection(x2, g, w, wg, col_scale, *, layer, tm, tn):
    n = x2.shape[0]
    gcols = wg.shape[-1]
    return pl.pallas_call(
        _inproj_body,
        grid=(n // tm, PROJ_COLS // tn),
        in_specs=[
            pl.BlockSpec((tm, D_MODEL), lambda i, j: (i, 0)),
            pl.BlockSpec((1, D_MODEL), lambda i, j: (0, 0)),
            pl.BlockSpec((None, D_MODEL, tn), lambda i, j: (layer, 0, j)),
            pl.BlockSpec((None, D_MODEL, gcols), lambda i, j: (layer, 0, 0)),
            pl.BlockSpec((1, tn), lambda i, j: (0, j)),
        ],
        out_specs=[
            pl.BlockSpec((tm, tn), lambda i, j: (i, j)),
            pl.BlockSpec((tm, gcols), lambda i, j: (i, 0)),
        ],
        out_shape=[
            jax.ShapeDtypeStruct((n, PROJ_COLS), BF16),
            jax.ShapeDtypeStruct((n, gcols), F32),
        ],
        scratch_shapes=[pltpu.VMEM((tm, D_MODEL), BF16)],
        compiler_params=_params("arbitrary", "arbitrary"),
        name="in_projection",
    )(x2, g, w, wg, col_scale)


CMP_WIDTH = N_KV_HEADS * KV_PAIR


def _compress_body(x_ref, pos_ref, w1_ref, w2_ref, o_ref, ot_ref, top_ref, bot_ref, *, nc):
    l = pl.program_id(1)

    @pl.when(l == 0)
    def _():
        top_ref[...] = jnp.zeros_like(top_ref)
        bot_ref[...] = jnp.zeros_like(bot_ref)

    x = x_ref[0, 0].astype(F32)
    top_ref[...] += jnp.dot((x + pos_ref[0, 0]).astype(BF16), w1_ref[0, 0], preferred_element_type=F32)
    bot_ref[...] += jnp.dot((x + pos_ref[0, 1]).astype(BF16), w1_ref[0, 1], preferred_element_type=F32)

    @pl.when(l == pl.num_programs(1) - 1)
    def _():
        pre = top_ref[...] + pltpu.roll(bot_ref[...], nc - 1, axis=0)
        act = pre * _sigmoid(pre)
        out = jnp.dot(act.astype(BF16), w2_ref[...], preferred_element_type=F32)
        for h in range(N_KV_HEADS):
            kv = out[:, h * KV_PAIR:(h + 1) * KV_PAIR]
            o_ref[0, h] = kv.astype(o_ref.dtype)
            ot_ref[0, h] = kv.T.astype(ot_ref.dtype)


def _compress(tokl, posx, w1big, w2big):
    b, _, nc, _ = tokl.shape
    return pl.pallas_call(
        functools.partial(_compress_body, nc=nc),
        grid=(b, CMP_STRIDE),
        in_specs=[
            pl.BlockSpec((1, 1, nc, CMP_WIDTH), lambda i, l: (i, l, 0, 0)),
            pl.BlockSpec((1, 2, 1, CMP_WIDTH), lambda i, l: (l, 0, 0, 0)),
            pl.BlockSpec((1, 2, CMP_WIDTH, CMP_WIDTH), lambda i, l: (l, 0, 0, 0)),
            pl.BlockSpec((CMP_WIDTH, CMP_WIDTH), lambda i, l: (0, 0)),
        ],
        out_specs=[
            pl.BlockSpec((1, N_KV_HEADS, nc, KV_PAIR), lambda i, l: (i, 0, 0, 0)),
            pl.BlockSpec((1, N_KV_HEADS, KV_PAIR, nc), lambda i, l: (i, 0, 0, 0)),
        ],
        out_shape=[
            jax.ShapeDtypeStruct((b, N_KV_HEADS, nc, KV_PAIR), BF16),
            jax.ShapeDtypeStruct((b, N_KV_HEADS, KV_PAIR, nc), BF16),
        ],
        scratch_shapes=[pltpu.VMEM((nc, CMP_WIDTH), F32), pltpu.VMEM((nc, CMP_WIDTH), F32)],
        compiler_params=_params("arbitrary", "arbitrary"),
        name="compress",
    )(tokl, posx, w1big, w2big)


def _compress_weight_layout(pos, w1, w2):
    def block_diag(blocks):
        rows = []
        for h in range(N_KV_HEADS):
            for kv in range(2):
                off = (h * 2 + kv) * HEAD_DIM
                pad = [(0, 0)] * (blocks.ndim - 2) + [(off, CMP_WIDTH - HEAD_DIM - off)]
                rows.append(jnp.pad(blocks[kv], pad))
        return jnp.concatenate(rows, axis=-2).astype(BF16)

    w1r = w1.reshape(2, 2, CMP_STRIDE, HEAD_DIM, HEAD_DIM)
    w1big = block_diag(w1r.transpose(0, 2, 1, 3, 4))
    w2big = block_diag(w2)
    posr = pos.reshape(2, 2, CMP_STRIDE, HEAD_DIM).transpose(2, 1, 0, 3)
    posx = jnp.broadcast_to(posr[:, :, None], (CMP_STRIDE, 2, N_KV_HEADS, 2, HEAD_DIM))
    return posx.reshape(CMP_STRIDE, 2, 1, CMP_WIDTH), w1big, w2big


VT_ROWS = HEAD_DIM + 16


def _attn_body(q_ref, kvc_ref, kvct_ref, kvs_ref, kvw_ref, bt_ref, bc_ref, gate_ref, o_ref,
               kp_ref, vst_ref, vwt_ref, key_ref, m_ref, acc_ref, sa_ref, sb_ref, ocmp_ref, qsel_ref, qwin_ref,
               *, tq, t_len):
    qi = pl.program_id(2)
    nc = t_len // CMP_STRIDE
    ns = t_len // SLC_LEN
    rows = GQA_GROUP * tq
    blocks_per_tile = tq // SLC_LEN

    @pl.when(qi == 0)
    def _():
        k = kvs_ref[0][:, :HEAD_DIM]
        blk = lax.broadcasted_iota(jnp.int32, (t_len, HEAD_DIM), 0) // SLC_LEN
        col = lax.broadcasted_iota(jnp.int32, (t_len, HEAD_DIM), 1)
        onehot = jnp.where(blk == col, 1.0, 0.0).astype(BF16)
        kp_ref[...] = jnp.concatenate([k, onehot], axis=1)
        ones_rows = jnp.where(lax.broadcasted_iota(jnp.int32, (VT_ROWS - HEAD_DIM, tq), 0) == 0, 1.0, 0.0)
        for c in range(t_len // tq):
            st = kvs_ref[0, c * tq:(c + 1) * tq, :].astype(F32).T[HEAD_DIM:]
            vst_ref[c] = jnp.concatenate([st, ones_rows], axis=0).astype(BF16)
            wt = kvw_ref[0, c * tq:(c + 1) * tq, :].astype(F32).T[HEAD_DIM:]
            vwt_ref[c] = jnp.concatenate([wt, ones_rows], axis=0).astype(BF16)

    q_t = q_ref[0].astype(F32).T
    q4_t = jnp.concatenate([q_t[g * HEAD_DIM:(g + 1) * HEAD_DIM] for g in range(GQA_GROUP)],
                           axis=1).astype(BF16)
    qwin_ref[...] = jnp.concatenate([q4_t, jnp.zeros((HEAD_DIM, rows), BF16)], axis=0)
    qsel_ref[0:HEAD_DIM, :] = q4_t
    kw_ref = kvw_ref.at[0]
    win, sel = 0, 1

    def cmp_scores():
        bias_row = pl.multiple_of(nc - qi * (tq // CMP_STRIDE), tq // CMP_STRIDE)
        return (jnp.dot(kvc_ref[0, 0], qwin_ref[...], preferred_element_type=F32)
                + bc_ref[0, pl.ds(bias_row, nc), :])

    def cmp_finish(s):
        m = jnp.max(s, axis=0, keepdims=True)
        p = jnp.exp2(s - m)
        l = jnp.sum(p, axis=0, keepdims=True)
        pn = p * jnp.where(m > 0.5 * MASK_VALUE, 1.0 / l, 0.0)
        ocmp_ref[...] = jnp.dot(kvct_ref[0, 0, HEAD_DIM:, :], pn.astype(BF16),
                                preferred_element_type=F32)

        ps = pn[:, 0:tq] + pn[:, tq:2 * tq] + pn[:, 2 * tq:3 * tq] + pn[:, 3 * tq:4 * tq]
        sj = lax.broadcasted_iota(jnp.int32, (HEAD_DIM, nc), 0) * SLC_LEN
        ci = lax.broadcasted_iota(jnp.int32, (HEAD_DIM, nc), 1) * CMP_STRIDE
        overlap = jnp.where(ci < sj + SLC_LEN, jnp.where(ci + CMP_LEN > sj, 1.0, 0.0), 0.0).astype(BF16)
        p_hi = ps.astype(BF16)
        r_hi = ps - p_hi.astype(F32)
        p_md = r_hi.astype(BF16)
        p_lo = (r_hi - p_md.astype(F32)).astype(BF16)
        imp = (jnp.dot(overlap, p_hi, preferred_element_type=F32)
               + jnp.dot(overlap, p_md, preferred_element_type=F32)
               + jnp.dot(overlap, p_lo, preferred_element_type=F32))
        t = qi * tq + lax.broadcasted_iota(jnp.int32, (HEAD_DIM, tq), 1)
        blk = lax.broadcasted_iota(jnp.int32, (HEAD_DIM, tq), 0)
        cur = t // SLC_LEN
        key = jnp.where(blk == 0, KEY_BIG, jnp.where(blk == cur, KEY_BIG, jnp.where(blk == cur - 1, KEY_BIG, imp)))
        key_ref[...] = jnp.where(blk * SLC_LEN <= t, key, -KEY_BIG)

    def select_blocks():
        key = key_ref[...]
        blk = lax.broadcasted_iota(jnp.int32, (HEAD_DIM, tq), 0)
        pen = jnp.full((HEAD_DIM, tq), BLOCK_PENALTY, F32)
        for _ in range(min(SLC_TOPK, ns)):
            top = jnp.max(key, axis=0, keepdims=True)
            first = jnp.min(jnp.where(key == top, blk, HEAD_DIM), axis=0, keepdims=True)
            hit = blk == first
            pen = jnp.where(hit, 0.0, pen)
            key = jnp.where(hit, KEY_TAKEN, key)
        qsel_ref[HEAD_DIM:, :] = jnp.concatenate([pen.astype(BF16)] * GQA_GROUP, axis=1)

    def scores(qx_ref, k_ref, kj, bias_off):
        start = pl.multiple_of(kj * tq, tq)
        sc = jnp.dot(k_ref[pl.ds(start, tq), :], qx_ref[...], preferred_element_type=F32)
        if bias_off is not None:
            sc = sc + bt_ref[0, bias_off:bias_off + tq, :]
        return sc

    def consume(sc, vt_ref, kj, br, st, first):
        m_tile = jnp.max(sc, axis=0, keepdims=True)
        if first:
            m_new = m_tile
            pr = jnp.exp2(sc - m_new).astype(BF16)
            acc_ref[br, st] = jnp.dot(vt_ref[kj], pr, preferred_element_type=F32)
        else:
            m_old = m_ref[br, st]
            m_new = jnp.maximum(m_old, m_tile)
            alpha = jnp.exp2(m_old - m_new)
            pr = jnp.exp2(sc - m_new).astype(BF16)
            acc_ref[br, st] = alpha * acc_ref[br, st] + jnp.dot(vt_ref[kj], pr, preferred_element_type=F32)
        m_ref[br, st] = m_new

    def finish(br):
        m0 = m_ref[br, 0]
        m1 = m_ref[br, 1]
        m_all = jnp.maximum(m0, m1)
        acc = jnp.exp2(m0 - m_all) * acc_ref[br, 0] + jnp.exp2(m1 - m_all) * acc_ref[br, 1]
        return acc[0:HEAD_DIM] * (1.0 / acc[HEAD_DIM:HEAD_DIM + 1, :])

    def combine():
        o_win = finish(win)
        o_sel = finish(sel)
        o_cmp = ocmp_ref[...]
        gates_t = gate_ref[0].T
        combs = []
        for g in range(GQA_GROUP):
            sl = slice(g * tq, (g + 1) * tq)
            combs.append(gates_t[g:g + 1, :] * o_cmp[:, sl]
                         + gates_t[GQA_GROUP + g:GQA_GROUP + g + 1, :] * o_sel[:, sl]
                         + gates_t[2 * GQA_GROUP + g:2 * GQA_GROUP + g + 1, :] * o_win[:, sl])
        outs = [jnp.concatenate(combs[p:p + 2], axis=0).T for p in range(0, GQA_GROUP, 2)]
        o_ref[0] = jnp.concatenate(outs, axis=1).astype(o_ref.dtype)

    @pl.when(qi >= 2)
    def _():
        sc_cmp = cmp_scores()
        sw0 = scores(qwin_ref, kw_ref, qi, WINDOW)
        cmp_finish(sc_cmp)
        sw1 = scores(qwin_ref, kw_ref, qi - 1, WINDOW - tq)
        consume(sw0, vwt_ref, qi, win, 0, True)
        sw2 = scores(qwin_ref, kw_ref, qi - 2, WINDOW - 2 * tq)
        consume(sw1, vwt_ref, qi - 1, win, 1, True)
        select_blocks()
        consume(sw2, vwt_ref, qi - 2, win, 0, False)

        ss0 = scores(qsel_ref, kp_ref, qi, WINDOW)
        ss1 = scores(qsel_ref, kp_ref, qi - 1, WINDOW - tq)
        consume(ss0, vst_ref, qi, sel, 0, True)
        sa_ref[...] = scores(qsel_ref, kp_ref, 0, None)
        consume(ss1, vst_ref, qi - 1, sel, 1, True)
        n_far = qi - 1

        def far_pair(pi):
            sb_ref[...] = scores(qsel_ref, kp_ref, 2 * pi + 1, None)
            consume(sa_ref[...], vst_ref, 2 * pi, sel, 1, False)
            sa_ref[...] = scores(qsel_ref, kp_ref, jnp.minimum(2 * pi + 2, n_far - 1), None)
            consume(sb_ref[...], vst_ref, 2 * pi + 1, sel, 0, False)

        n_pairs = n_far // 2

        def far_quad(qd, carry):
            far_pair(2 * qd)
            far_pair(2 * qd + 1)
            return carry

        lax.fori_loop(0, n_pairs // 2, far_quad, 0)

        @pl.when(n_pairs % 2 == 1)
        def _():
            far_pair(n_pairs - 1)

        @pl.when(n_far % 2 == 1)
        def _():
            consume(sa_ref[...], vst_ref, n_far - 1, sel, 1, False)

        combine()

    @pl.when(qi < 2)
    def _():
        cmp_finish(cmp_scores())
        select_blocks()
        sw0 = scores(qwin_ref, kw_ref, qi, WINDOW)
        ss0 = scores(qsel_ref, kp_ref, qi, WINDOW)
        consume(sw0, vwt_ref, qi, win, 0, True)
        consume(ss0, vst_ref, qi, sel, 0, True)
        for br in (win, sel):
            m_ref[br, 1] = jnp.full((1, rows), MASK_VALUE, F32)
            acc_ref[br, 1] = jnp.zeros((VT_ROWS, rows), F32)

        @pl.when(qi == 1)
        def _():
            sw1 = scores(qwin_ref, kw_ref, 0, WINDOW - tq)
            ss1 = scores(qsel_ref, kp_ref, 0, WINDOW - tq)
            consume(sw1, vwt_ref, 0, win, 1, False)
            consume(ss1, vst_ref, 0, sel, 1, False)

        combine()


def _attention(proj3, kvc, kvct, band, cmpb, gates3, *, tq):
    b, t_len, _ = proj3.shape
    assert WINDOW == 2 * tq
    nc = t_len // CMP_STRIDE
    rows = GQA_GROUP * tq
    qw = GQA_GROUP * HEAD_DIM
    slc_blk = COL_KV // KV_PAIR + N_KV_HEADS
    win_blk = COL_KV // KV_PAIR + 2 * N_KV_HEADS
    return pl.pallas_call(
        functools.partial(_attn_body, tq=tq, t_len=t_len),
        grid=(b, N_KV_HEADS, t_len // tq),
        in_specs=[
            pl.BlockSpec((1, tq, qw), lambda i, h, q: (i, q, h)),
            pl.BlockSpec((1, 1, nc, KV_PAIR), lambda i, h, q: (i, h, 0, 0)),
            pl.BlockSpec((1, 1, KV_PAIR, nc), lambda i, h, q: (i, h, 0, 0)),
            pl.BlockSpec((1, t_len, KV_PAIR), lambda i, h, q: (i, 0, slc_blk + h)),
            pl.BlockSpec((1, t_len, KV_PAIR), lambda i, h, q: (i, 0, win_blk + h)),
            pl.BlockSpec((1, WINDOW + tq, rows), lambda i, h, q: (h, 0, 0)),
            pl.BlockSpec((1, 2 * nc, rows), lambda i, h, q: (h, 0, 0)),
            pl.BlockSpec((1, tq, LANES), lambda i, h, q: (i, q, h)),
        ],
        out_specs=pl.BlockSpec((1, tq, qw), lambda i, h, q: (i, q, h)),
        out_shape=jax.ShapeDtypeStruct((b, t_len, ATTN_WIDTH), BF16),
        scratch_shapes=[
            pltpu.VMEM((t_len, KV_PAIR), BF16),
            pltpu.VMEM((t_len // tq, VT_ROWS, tq), BF16),
            pltpu.VMEM((t_len // tq, VT_ROWS, tq), BF16),
            pltpu.VMEM((HEAD_DIM, tq), F32),
            pltpu.VMEM((2, 2, 1, rows), F32),
            pltpu.VMEM((2, 2, VT_ROWS, rows), F32),
            pltpu.VMEM((tq, rows), F32),
            pltpu.VMEM((tq, rows), F32),
            pltpu.VMEM((HEAD_DIM, rows), F32),
            pltpu.VMEM((KV_PAIR, rows), BF16),
            pltpu.VMEM((KV_PAIR, rows), BF16),
        ],
        compiler_params=_params("arbitrary", "arbitrary", "arbitrary"),
        name="nsa_attention",
    )(proj3, kvc, kvct, proj3, proj3, band, cmpb, gates3)


def _conformer_body(a_ref, g_ref, ah_ref, gh_ref, w_ref, b_ref, lg_ref, lb_ref, o_ref, u_ref, y_ref, *, tt):
    ti = pl.program_id(1)
    u_ref[CONV_HALO:, :] = a_ref[0].astype(F32) * _sigmoid(g_ref[0].astype(F32))
    halo = ah_ref[0].astype(F32) * _sigmoid(gh_ref[0].astype(F32))
    u_ref[0:CONV_HALO, :] = jnp.where(ti > 0, halo, 0.0)
    base = CONV_HALO - (CONV_TAPS - 1)
    ext = CONV_ROWS + CONV_HALO
    for cb in range(CONV_CHANNELS // LANES):
        cs = slice(cb * LANES, (cb + 1) * LANES)
        wblk = w_ref[:, cs]
        for rb in range(tt // CONV_ROWS):
            ublk = u_ref[rb * CONV_ROWS:rb * CONV_ROWS + ext, cs]
            acc = jnp.zeros((CONV_ROWS, LANES), F32)
            for r in range(SUBLANES):
                ur = ublk if r == 0 else pltpu.roll(ublk, ext - r, axis=0)
                for a in range(CONV_HALO // SUBLANES + 1):
                    k = SUBLANES * a + r - base
                    if 0 <= k < CONV_TAPS:
                        acc = acc + wblk[k:k + 1, :] * ur[SUBLANES * a:SUBLANES * a + CONV_ROWS]
            y_ref[rb * CONV_ROWS:(rb + 1) * CONV_ROWS, cs] = acc
    acc = y_ref[...] + b_ref[...]
    mu = jnp.mean(acc, axis=-1, keepdims=True)
    xc = acc - mu
    var = jnp.mean(xc * xc, axis=-1, keepdims=True)
    y = xc * lax.rsqrt(var + NORM_EPS) * lg_ref[...] + lb_ref[...]
    o_ref[0] = (y * _sigmoid(y)).astype(o_ref.dtype)


def _conformer(proj3, w, b, lg, lb, *, tt):
    bsz, t_len, _ = proj3.shape
    a_blk = COL_CONV_A // CONV_CHANNELS
    g_blk = COL_CONV_G // CONV_CHANNELS
    hpt = tt // CONV_HALO
    halo_idx = lambda i, t: jnp.maximum(t * hpt - 1, 0)
    vec = pl.BlockSpec((1, CONV_CHANNELS), lambda i, t: (0, 0))
    return pl.pallas_call(
        functools.partial(_conformer_body, tt=tt),
        grid=(bsz, t_len // tt),
        in_specs=[
            pl.BlockSpec((1, tt, CONV_CHANNELS), lambda i, t: (i, t, a_blk)),
            pl.BlockSpec((1, tt, CONV_CHANNELS), lambda i, t: (i, t, g_blk)),
            pl.BlockSpec((1, CONV_HALO, CONV_CHANNELS), lambda i, t: (i, halo_idx(i, t), a_blk)),
            pl.BlockSpec((1, CONV_HALO, CONV_CHANNELS), lambda i, t: (i, halo_idx(i, t), g_blk)),
            pl.BlockSpec((CONV_TAPS, CONV_CHANNELS), lambda i, t: (0, 0)),
            vec, vec, vec,
        ],
        out_specs=pl.BlockSpec((1, tt, CONV_CHANNELS), lambda i, t: (i, t, 0)),
        out_shape=jax.ShapeDtypeStruct((bsz, t_len, CONV_CHANNELS), BF16),
        scratch_shapes=[pltpu.VMEM((CONV_HALO + tt, CONV_CHANNELS), F32),
                        pltpu.VMEM((tt, CONV_CHANNELS), F32)],
        compiler_params=_params("arbitrary", "arbitrary"),
        name="conformer_conv",
    )(proj3, proj3, proj3, proj3, w, b, lg, lb)


def _outproj_body(x_ref, a_ref, c_ref, wa_ref, wc_ref, o_ref):
    o_ref[...] = (x_ref[...]
                  + jnp.dot(a_ref[...], wa_ref[...], preferred_element_type=F32)
                  + jnp.dot(c_ref[...], wc_ref[...], preferred_element_type=F32))


def _out_projection(x2, attn2, conv2, w_out, *, layer, tm, tn):
    n = x2.shape[0]
    return pl.pallas_call(
        _outproj_body,
        grid=(n // tm, D_MODEL // tn),
        in_specs=[
            pl.BlockSpec((tm, tn), lambda i, j: (i, j)),
            pl.BlockSpec((tm, ATTN_WIDTH), lambda i, j: (i, 0)),
            pl.BlockSpec((tm, CONV_CHANNELS), lambda i, j: (i, 0)),
            pl.BlockSpec((None, ATTN_WIDTH, tn), lambda i, j: (layer, 0, j)),
            pl.BlockSpec((None, CONV_CHANNELS, tn), lambda i, j: (layer, ATTN_WIDTH // CONV_CHANNELS, j)),
        ],
        out_specs=pl.BlockSpec((tm, tn), lambda i, j: (i, j)),
        out_shape=jax.ShapeDtypeStruct((n, D_MODEL), F32),
        compiler_params=_params("arbitrary", "arbitrary"),
        name="out_projection",
    )(x2, attn2, conv2, w_out, w_out)


def _ffn_up_body(x_ref, xh_ref, g_ref, wa_ref, wg_ref, cwa_ref, cwg_ref, cba_ref, cbg_ref, o_ref,
                 h_ref, ua_ref, ug_ref, *, tm, tiles_per_seq, rows):
    i = pl.program_id(0)

    @pl.when(pl.program_id(1) == 0)
    def _():
        h_ref[FFN_HALO:, :] = _rms_norm_rows(x_ref[...], g_ref[...]).astype(BF16)
        hh = _rms_norm_rows(xh_ref[...], g_ref[...])
        h_ref[0:FFN_HALO, :] = jnp.where(i % tiles_per_seq != 0, hh, 0.0).astype(BF16)

    units = [(c, slice(0, o_ref.shape[1])) for c in range(tm // rows)]

    def project(c, cs):
        hc = h_ref[c * rows:(c + 1) * rows + FFN_HALO, :]
        ug_ref[c, :, cs] = jnp.dot(hc, wg_ref[:, cs], preferred_element_type=F32)
        ua_ref[c, :, cs] = jnp.dot(hc, wa_ref[:, cs], preferred_element_type=F32)

    def conv(u_ref, c, cs, cw_ref, cb_ref):
        u = u_ref[c, :, cs]
        y = cw_ref[FFN_CONV_TAPS - 1:FFN_CONV_TAPS, cs] * u[FFN_HALO:]
        for s in range(1, FFN_CONV_TAPS):
            k = FFN_CONV_TAPS - 1 - s
            y = y + cw_ref[k:k + 1, cs] * pltpu.roll(u, s, axis=0)[FFN_HALO:]
        return y + cb_ref[:, cs]

    def finish(c, cs):
        gate = conv(ug_ref, c, cs, cwg_ref, cbg_ref)
        act = gate * _sigmoid(gate)
        a = conv(ua_ref, c, cs, cwa_ref, cba_ref)
        o_ref[c * rows:(c + 1) * rows, cs] = (act * a).astype(o_ref.dtype)

    project(*units[0])
    for u, unit in enumerate(units):
        if u + 1 < len(units):
            project(*units[u + 1])
        finish(*unit)


def _ffn_up(x2, g, w_up, cw, cb, *, layer, tm, tn, t_len):
    n = x2.shape[0]
    nj = D_FF // tn
    hpt = tm // FFN_HALO
    rows = min(FFN_ROWS, tm)
    return pl.pallas_call(
        functools.partial(_ffn_up_body, tm=tm, tiles_per_seq=t_len // tm, rows=rows),
        grid=(n // tm, nj),
        in_specs=[
            pl.BlockSpec((tm, D_MODEL), lambda i, j: (i, 0)),
            pl.BlockSpec((FFN_HALO, D_MODEL), lambda i, j: (jnp.maximum(i * hpt - 1, 0), 0)),
            pl.BlockSpec((1, D_MODEL), lambda i, j: (0, 0)),
            pl.BlockSpec((None, D_MODEL, tn), lambda i, j: (layer, 0, j)),
            pl.BlockSpec((None, D_MODEL, tn), lambda i, j: (layer, 0, j + nj)),
            pl.BlockSpec((FFN_CONV_TAPS, tn), lambda i, j: (0, j)),
            pl.BlockSpec((FFN_CONV_TAPS, tn), lambda i, j: (0, j + nj)),
            pl.BlockSpec((1, tn), lambda i, j: (0, j)),
            pl.BlockSpec((1, tn), lambda i, j: (0, j + nj)),
        ],
        out_specs=pl.BlockSpec((tm, tn), lambda i, j: (i, j)),
        out_shape=jax.ShapeDtypeStruct((n, D_FF), BF16),
        scratch_shapes=[
            pltpu.VMEM((FFN_HALO + tm, D_MODEL), BF16),
            pltpu.VMEM((tm // rows, FFN_HALO + rows, tn), F32),
            pltpu.VMEM((tm // rows, FFN_HALO + rows, tn), F32),
        ],
        compiler_params=_params("arbitrary", "arbitrary"),
        name="ffn_up",
    )(x2, x2, g, w_up, w_up, cw, cw, cb, cb)


def _ffn_down_body(act_ref, w_ref, x_ref, o_ref):
    o_ref[...] = x_ref[...] + jnp.dot(act_ref[...], w_ref[...], preferred_element_type=F32)


def _ffn_down(act, w_down, x2, *, layer, tm, tn):
    n = x2.shape[0]
    return pl.pallas_call(
        _ffn_down_body,
        grid=(n // tm, D_MODEL // tn),
        in_specs=[
            pl.BlockSpec((tm, D_FF), lambda i, j: (i, 0)),
            pl.BlockSpec((None, D_FF, tn), lambda i, j: (layer, 0, j)),
            pl.BlockSpec((tm, tn), lambda i, j: (i, j)),
        ],
        out_specs=pl.BlockSpec((tm, tn), lambda i, j: (i, j)),
        out_shape=jax.ShapeDtypeStruct((n, D_MODEL), F32),
        compiler_params=_params("arbitrary", "arbitrary"),
        name="ffn_down",
    )(act, w_down, x2)


def _final_norm_body(x_ref, g_ref, o_ref):
    o_ref[...] = _rms_norm_rows(x_ref[...], g_ref[...])


def _final_norm(x2, g, *, tm):
    n = x2.shape[0]
    return pl.pallas_call(
        _final_norm_body,
        grid=(n // tm,),
        in_specs=[pl.BlockSpec((tm, D_MODEL), lambda i: (i, 0)), pl.BlockSpec((1, D_MODEL), lambda i: (0, 0))],
        out_specs=pl.BlockSpec((tm, D_MODEL), lambda i: (i, 0)),
        out_shape=jax.ShapeDtypeStruct((n, D_MODEL), F32),
        compiler_params=_params("arbitrary"),
        name="final_norm",
    )(x2, g)


def _in_weight_layout(w_in):
    depth = w_in.shape[0]
    kv0 = ATTN_WIDTH
    gate0 = kv0 + KV_COLS
    conv0 = gate0 + GATE_COLS
    parts = [w_in[..., :ATTN_WIDTH], w_in[..., conv0:conv0 + 2 * CONV_CHANNELS]]
    for br in range(N_BRANCH):
        for h in range(N_KV_HEADS):
            for kv in range(2):
                c0 = kv0 + ((br * 2 + kv) * N_KV_HEADS + h) * HEAD_DIM
                parts.append(w_in[..., c0:c0 + HEAD_DIM])
    w_main = jnp.concatenate(parts, axis=-1).astype(BF16)
    wg = w_in[..., gate0:gate0 + GATE_COLS].reshape(depth, D_MODEL, N_KV_HEADS, GQA_GROUP, N_BRANCH)
    wg = wg.transpose(0, 1, 2, 4, 3).reshape(depth, D_MODEL, N_KV_HEADS, N_BRANCH * GQA_GROUP)
    wg = jnp.pad(wg, ((0, 0), (0, 0), (0, 0), (0, LANES - N_BRANCH * GQA_GROUP)))
    return w_main, wg.reshape(depth, D_MODEL, N_KV_HEADS * LANES).astype(BF16)


def _pick(n, prefs):
    for p in prefs:
        if n % p == 0:
            return p
    return n


def kernel(x, rel_bias, mix_norm_g, w_in, cmp_pos, cmp_w1, cmp_w2, conv_w, conv_b, conv_ln_g, conv_ln_b,
           w_out, ffn_norm_g, w_up, ffn_conv_w, ffn_conv_b, w_down, final_norm_g):
    bsz, t_len, _ = x.shape
    depth = w_in.shape[0]
    n = bsz * t_len
    assert t_len % 256 == 0 and t_len // SLC_LEN <= HEAD_DIM
    tq = 256
    tm = _pick(t_len, (1024, 512, 256))
    nc = t_len // CMP_STRIDE

    band, cmpb = _bias_tiles(rel_bias, t_len, tq)
    col_scale = jnp.concatenate([jnp.full((1, ATTN_WIDTH), Q_SCALE, F32),
                                 jnp.ones((1, PROJ_COLS - ATTN_WIDTH), F32)], axis=1)
    x2 = x.reshape(n, D_MODEL)
    w_main, wg = _in_weight_layout(w_in)
    w_out_b, w_up_b, w_down_b = w_out.astype(BF16), w_up.astype(BF16), w_down.astype(BF16)
    for l in range(depth):
        proj, gates = _in_projection(x2, mix_norm_g[l][None, :], w_main, wg, col_scale, layer=l, tm=tm, tn=768)
        proj3 = proj.reshape(bsz, t_len, PROJ_COLS)
        gates3 = gates.reshape(bsz, t_len, N_KV_HEADS * LANES)

        posx, w1big, w2big = _compress_weight_layout(cmp_pos[l], cmp_w1[l], cmp_w2[l])
        tokl = proj3[:, :, COL_KV:COL_KV + CMP_WIDTH].reshape(bsz, nc, CMP_STRIDE, CMP_WIDTH).transpose(0, 2, 1, 3)
        kvc, kvct = _compress(tokl, posx, w1big, w2big)

        attn = _attention(proj3, kvc, kvct, band, cmpb, gates3, tq=tq)
        conv = _conformer(proj3, conv_w[l], conv_b[l][None, :], conv_ln_g[l][None, :],
                          conv_ln_b[l][None, :], tt=256)
        x2 = _out_projection(x2, attn.reshape(n, ATTN_WIDTH), conv.reshape(n, CONV_CHANNELS),
                             w_out_b, layer=l, tm=tm, tn=1024)

        act = _ffn_up(x2, ffn_norm_g[l][None, :], w_up_b, ffn_conv_w[l],
                      ffn_conv_b[l][None, :], layer=l, tm=tm, tn=512, t_len=t_len)
        x2 = _ffn_down(act, w_down_b, x2, layer=l, tm=tm, tn=512)
    x2 = _final_norm(x2, final_norm_g[None, :], tm=_pick(n, (512, 256)))
    return x2.reshape(bsz, t_len, D_MODEL)
```

```python
import functools
import math

import numpy as np
import jax
import jax.numpy as jnp
from jax import lax
from jax.experimental import pallas as pl
from jax.experimental.pallas import tpu as pltpu

F32 = jnp.float32
BF16 = jnp.bfloat16

D_MODEL = 2048
HEAD_DIM = 64
N_KV_HEADS = 4
GQA_GROUP = 4
N_Q_HEADS = N_KV_HEADS * GQA_GROUP
N_BRANCH = 3
ATTN_WIDTH = N_Q_HEADS * HEAD_DIM
CMP_LEN = 32
CMP_STRIDE = 16
SLC_LEN = 64
SLC_TOPK = 16
WINDOW = 512
CONV_CHANNELS = D_MODEL - ATTN_WIDTH
CONV_TAPS = 31
D_FF = 5632
FFN_CONV_TAPS = 3
N_BUCKETS = 32
MAX_DISTANCE = 128
NORM_EPS = 1e-6

KV_PAIR = 2 * HEAD_DIM
KV_COLS = N_BRANCH * N_KV_HEADS * KV_PAIR
GATE_COLS = N_BRANCH * N_Q_HEADS
COL_Q = 0
COL_CONV_A = ATTN_WIDTH
COL_CONV_G = COL_CONV_A + CONV_CHANNELS
COL_KV = COL_CONV_G + CONV_CHANNELS
PROJ_COLS = COL_KV + KV_COLS

MASK_VALUE = -1e30
BLOCK_PENALTY = -1e9
KEY_BIG = 1e30
KEY_TAKEN = -3e38
LOG2E = math.log2(math.e)
Q_SCALE = HEAD_DIM ** -0.5 * LOG2E

LANES = 128
SUBLANES = 8
CONV_ROWS = 128
FFN_ROWS = 1024
VMEM_LIMIT = 56 * 1024 * 1024
CONV_HALO = 32
FFN_HALO = 16


def _t5_bucket_last_distance():
    n = np.arange(0, 4 * MAX_DISTANCE, dtype=np.int64)
    max_exact = N_BUCKETS // 2
    nf = np.maximum(n, 1).astype(np.float64)
    large = max_exact + np.floor(np.log(nf / max_exact) / math.log(MAX_DISTANCE / max_exact)
                                 * (N_BUCKETS - max_exact)).astype(np.int64)
    large = np.minimum(large, N_BUCKETS - 1)
    bucket = np.where(n < max_exact, n, large)
    last = []
    for b in range(N_BUCKETS - 1):
        idx = np.nonzero(bucket == b)[0]
        last.append(int(idx.max()) if idx.size else None)
    return last


_BUCKET_LAST = _t5_bucket_last_distance()


def _params(*sem):
    return pltpu.CompilerParams(dimension_semantics=sem, vmem_limit_bytes=VMEM_LIMIT)


def _sigmoid(x):
    return jax.nn.sigmoid(x)


def _rms_norm_rows(x, g):
    ms = jnp.mean(x * x, axis=-1, keepdims=True)
    return x * lax.rsqrt(ms + NORM_EPS) * g


def _bias_from_distance(dist, tab_ref, head):
    c_far = tab_ref[N_BUCKETS - 1, head]
    val = jnp.zeros(dist.shape, F32)
    for b in range(N_BUCKETS - 2, -1, -1):
        if _BUCKET_LAST[b] is None:
            continue
        val = jnp.where(dist <= _BUCKET_LAST[b], tab_ref[b, head] - c_far, val)
    return val


def _band_bias_body(tab_ref, o_ref, *, tq):
    head = pl.program_id(0) * GQA_GROUP + pl.program_id(1)
    shape = (WINDOW + tq, tq)
    dist = (lax.broadcasted_iota(jnp.int32, shape, 1) + WINDOW
            - lax.broadcasted_iota(jnp.int32, shape, 0))
    val = _bias_from_distance(dist, tab_ref, head) * LOG2E
    val = jnp.where(dist >= 0, jnp.where(dist < WINDOW, val, MASK_VALUE), MASK_VALUE)
    o_ref[0] = val


def _cmp_bias_body(tab_ref, o_ref, *, tq, nc):
    head = pl.program_id(0) * GQA_GROUP + pl.program_id(1)
    shape = (2 * nc, tq)
    r = lax.broadcasted_iota(jnp.int32, shape, 0)
    i = lax.broadcasted_iota(jnp.int32, shape, 1)
    dist = i - (r - nc) * CMP_STRIDE - (CMP_LEN - 1)
    val = _bias_from_distance(dist, tab_ref, head) * LOG2E
    o_ref[0] = jnp.where(dist >= 0, val, MASK_VALUE)


def _bias_tiles(rel_bias, t_len, tq):
    nc = t_len // CMP_STRIDE
    rows = GQA_GROUP * tq
    smem = pl.BlockSpec(memory_space=pltpu.SMEM)
    band = pl.pallas_call(
        functools.partial(_band_bias_body, tq=tq),
        grid=(N_KV_HEADS, GQA_GROUP),
        in_specs=[smem],
        out_specs=pl.BlockSpec((1, WINDOW + tq, tq), lambda h, g: (h, 0, g)),
        out_shape=jax.ShapeDtypeStruct((N_KV_HEADS, WINDOW + tq, rows), F32),
        compiler_params=_params("arbitrary", "arbitrary"),
        name="band_bias",
    )(rel_bias)
    cmpb = pl.pallas_call(
        functools.partial(_cmp_bias_body, tq=tq, nc=nc),
        grid=(N_KV_HEADS, GQA_GROUP),
        in_specs=[smem],
        out_specs=pl.BlockSpec((1, 2 * nc, tq), lambda h, g: (h, 0, g)),
        out_shape=jax.ShapeDtypeStruct((N_KV_HEADS, 2 * nc, rows), F32),
        compiler_params=_params("arbitrary", "arbitrary"),
        name="cmp_bias",
    )(rel_bias)
    return band, cmpb


def _inproj_body(x_ref, g_ref, w_ref, wg_ref, cs_ref, o_ref, gate_ref, h_ref):
    @pl.when(pl.program_id(1) == 0)
    def _():
        h = _rms_norm_rows(x_ref[...], g_ref[...]).astype(BF16)
        h_ref[...] = h
        gate_ref[...] = _sigmoid(jnp.dot(h, wg_ref[...], preferred_element_type=F32))

    acc = jnp.dot(h_ref[...], w_ref[...], preferred_element_type=F32)
    o_ref[...] = (acc * cs_ref[...]).astype(o_ref.dtype)


def _in_projection(x2, g, w, wg, col_scale, *, layer, tm, tn):
    n = x2.shape[0]
    gcols = wg.shape[-1]
    return pl.pallas_call(
        _inproj_body,
        grid=(n // tm, PROJ_COLS // tn),
        in_specs=[
            pl.BlockSpec((tm, D_MODEL), lambda i, j: (i, 0)),
            pl.BlockSpec((1, D_MODEL), lambda i, j: (0, 0)),
            pl.BlockSpec((None, D_MODEL, tn), lambda i, j: (layer, 0, j)),
            pl.BlockSpec((None, D_MODEL, gcols), lambda i, j: (layer, 0, 0)),
            pl.BlockSpec((1, tn), lambda i, j: (0, j)),
        ],
        out_specs=[
            pl.BlockSpec((tm, tn), lambda i, j: (i, j)),
            pl.BlockSpec((tm, gcols), lambda i, j: (i, 0)),
        ],
        out_shape=[
            jax.ShapeDtypeStruct((n, PROJ_COLS), BF16),
            jax.ShapeDtypeStruct((n, gcols), F32),
        ],
        scratch_shapes=[pltpu.VMEM((tm, D_MODEL), BF16)],
        compiler_params=_params("arbitrary", "arbitrary"),
        name="in_projection",
    )(x2, g, w, wg, col_scale)


CMP_WIDTH = N_KV_HEADS * KV_PAIR


def _compress_body(x_ref, pos_ref, w1_ref, w2_ref, o_ref, ot_ref, top_ref, bot_ref, *, nc):
    l = pl.program_id(1)

    @pl.when(l == 0)
    def _():
        top_ref[...] = jnp.zeros_like(top_ref)
        bot_ref[...] = jnp.zeros_like(bot_ref)

    x = x_ref[0, 0].astype(F32)
    top_ref[...] += jnp.dot((x + pos_ref[0, 0]).astype(BF16), w1_ref[0, 0], preferred_element_type=F32)
    bot_ref[...] += jnp.dot((x + pos_ref[0, 1]).astype(BF16), w1_ref[0, 1], preferred_element_type=F32)

    @pl.when(l == pl.num_programs(1) - 1)
    def _():
        pre = top_ref[...] + pltpu.roll(bot_ref[...], nc - 1, axis=0)
        act = pre * _sigmoid(pre)
        out = jnp.dot(act.astype(BF16), w2_ref[...], preferred_element_type=F32)
        for h in range(N_KV_HEADS):
            kv = out[:, h * KV_PAIR:(h + 1) * KV_PAIR]
            o_ref[0, h] = kv.astype(o_ref.dtype)
            ot_ref[0, h] = kv.T.astype(ot_ref.dtype)


def _compress(tokl, posx, w1big, w2big):
    b, _, nc, _ = tokl.shape
    return pl.pallas_call(
        functools.partial(_compress_body, nc=nc),
        grid=(b, CMP_STRIDE),
        in_specs=[
            pl.BlockSpec((1, 1, nc, CMP_WIDTH), lambda i, l: (i, l, 0, 0)),
            pl.BlockSpec((1, 2, 1, CMP_WIDTH), lambda i, l: (l, 0, 0, 0)),
            pl.BlockSpec((1, 2, CMP_WIDTH, CMP_WIDTH), lambda i, l: (l, 0, 0, 0)),
            pl.BlockSpec((CMP_WIDTH, CMP_WIDTH), lambda i, l: (0, 0)),
        ],
        out_specs=[
            pl.BlockSpec((1, N_KV_HEADS, nc, KV_PAIR), lambda i, l: (i, 0, 0, 0)),
            pl.BlockSpec((1, N_KV_HEADS, KV_PAIR, nc), lambda i, l: (i, 0, 0, 0)),
        ],
        out_shape=[
            jax.ShapeDtypeStruct((b, N_KV_HEADS, nc, KV_PAIR), BF16),
            jax.ShapeDtypeStruct((b, N_KV_HEADS, KV_PAIR, nc), BF16),
        ],
        scratch_shapes=[pltpu.VMEM((nc, CMP_WIDTH), F32), pltpu.VMEM((nc, CMP_WIDTH), F32)],
        compiler_params=_params("arbitrary", "arbitrary"),
        name="compress",
    )(tokl, posx, w1big, w2big)


def _compress_weight_layout(pos, w1, w2):
    def block_diag(blocks):
        rows = []
        for h in range(N_KV_HEADS):
            for kv in range(2):
                off = (h * 2 + kv) * HEAD_DIM
                pad = [(0, 0)] * (blocks.ndim - 2) + [(off, CMP_WIDTH - HEAD_DIM - off)]
                rows.append(jnp.pad(blocks[kv], pad))
        return jnp.concatenate(rows, axis=-2).astype(BF16)

    w1r = w1.reshape(2, 2, CMP_STRIDE, HEAD_DIM, HEAD_DIM)
    w1big = block_diag(w1r.transpose(0, 2, 1, 3, 4))
    w2big = block_diag(w2)
    posr = pos.reshape(2, 2, CMP_STRIDE, HEAD_DIM).transpose(2, 1, 0, 3)
    posx = jnp.broadcast_to(posr[:, :, None], (CMP_STRIDE, 2, N_KV_HEADS, 2, HEAD_DIM))
    return posx.reshape(CMP_STRIDE, 2, 1, CMP_WIDTH), w1big, w2big


VT_ROWS = HEAD_DIM + 16
FAR_UNROLL = 4


def _attn_body(q_ref, kvc_ref, kvct_ref, kvs_ref, kvw_ref, bt_ref, bc_ref, gate_ref, o_ref,
               kp_ref, vst_ref, vwt_ref, key_ref, m_ref, acc_ref, sa_ref, sb_ref, ocmp_ref, qsel_ref, qwin_ref,
               *, tq, t_len):
    qi = pl.program_id(2)
    nc = t_len // CMP_STRIDE
    ns = t_len // SLC_LEN
    rows = GQA_GROUP * tq
    blocks_per_tile = tq // SLC_LEN

    @pl.when(qi == 0)
    def _():
        k = kvs_ref[0][:, :HEAD_DIM]
        blk = lax.broadcasted_iota(jnp.int32, (t_len, HEAD_DIM), 0) // SLC_LEN
        col = lax.broadcasted_iota(jnp.int32, (t_len, HEAD_DIM), 1)
        onehot = jnp.where(blk == col, 1.0, 0.0).astype(BF16)
        kp_ref[...] = jnp.concatenate([k, onehot], axis=1)
        ones_rows = jnp.where(lax.broadcasted_iota(jnp.int32, (VT_ROWS - HEAD_DIM, tq), 0) == 0, 1.0, 0.0)
        for c in range(t_len // tq):
            st = kvs_ref[0, c * tq:(c + 1) * tq, :].astype(F32).T[HEAD_DIM:]
            vst_ref[c] = jnp.concatenate([st, ones_rows], axis=0).astype(BF16)
            wt = kvw_ref[0, c * tq:(c + 1) * tq, :].astype(F32).T[HEAD_DIM:]
            vwt_ref[c] = jnp.concatenate([wt, ones_rows], axis=0).astype(BF16)

    q_t = q_ref[0].astype(F32).T
    q4_t = jnp.concatenate([q_t[g * HEAD_DIM:(g + 1) * HEAD_DIM] for g in range(GQA_GROUP)],
                           axis=1).astype(BF16)
    qwin_ref[...] = jnp.concatenate([q4_t, jnp.zeros((HEAD_DIM, rows), BF16)], axis=0)
    qsel_ref[0:HEAD_DIM, :] = q4_t
    kw_ref = kvw_ref.at[0]
    win, sel = 0, 1

    def cmp_scores():
        bias_row = pl.multiple_of(nc - qi * (tq // CMP_STRIDE), tq // CMP_STRIDE)
        return (jnp.dot(kvc_ref[0, 0], qwin_ref[...], preferred_element_type=F32)
                + bc_ref[0, pl.ds(bias_row, nc), :])

    def cmp_finish(s):
        m = jnp.max(s, axis=0, keepdims=True)
        p = jnp.exp2(s - m)
        l = jnp.sum(p, axis=0, keepdims=True)
        pn = p * jnp.where(m > 0.5 * MASK_VALUE, 1.0 / l, 0.0)
        ocmp_ref[...] = jnp.dot(kvct_ref[0, 0, HEAD_DIM:, :], pn.astype(BF16),
                                preferred_element_type=F32)

        ps = pn[:, 0:tq] + pn[:, tq:2 * tq] + pn[:, 2 * tq:3 * tq] + pn[:, 3 * tq:4 * tq]
        sj = lax.broadcasted_iota(jnp.int32, (HEAD_DIM, nc), 0) * SLC_LEN
        ci = lax.broadcasted_iota(jnp.int32, (HEAD_DIM, nc), 1) * CMP_STRIDE
        overlap = jnp.where(ci < sj + SLC_LEN, jnp.where(ci + CMP_LEN > sj, 1.0, 0.0), 0.0).astype(BF16)
        p_hi = ps.astype(BF16)
        r_hi = ps - p_hi.astype(F32)
        p_md = r_hi.astype(BF16)
        p_lo = (r_hi - p_md.astype(F32)).astype(BF16)
        imp = (jnp.dot(overlap, p_hi, preferred_element_type=F32)
               + jnp.dot(overlap, p_md, preferred_element_type=F32)
               + jnp.dot(overlap, p_lo, preferred_element_type=F32))
        t = qi * tq + lax.broadcasted_iota(jnp.int32, (HEAD_DIM, tq), 1)
        blk = lax.broadcasted_iota(jnp.int32, (HEAD_DIM, tq), 0)
        cur = t // SLC_LEN
        key = jnp.where(blk == 0, KEY_BIG, jnp.where(blk == cur, KEY_BIG, jnp.where(blk == cur - 1, KEY_BIG, imp)))
        key_ref[...] = jnp.where(blk * SLC_LEN <= t, key, -KEY_BIG)

    def select_blocks():
        key = key_ref[...]
        blk = lax.broadcasted_iota(jnp.int32, (HEAD_DIM, tq), 0)
        pen = jnp.full((HEAD_DIM, tq), BLOCK_PENALTY, F32)
        for _ in range(min(SLC_TOPK, ns)):
            top = jnp.max(key, axis=0, keepdims=True)
            first = jnp.min(jnp.where(key == top, blk, HEAD_DIM), axis=0, keepdims=True)
            hit = blk == first
            pen = jnp.where(hit, 0.0, pen)
            key = jnp.where(hit, KEY_TAKEN, key)
        qsel_ref[HEAD_DIM:, :] = jnp.concatenate([pen.astype(BF16)] * GQA_GROUP, axis=1)

    def scores(qx_ref, k_ref, kj, bias_off):
        start = pl.multiple_of(kj * tq, tq)
        sc = jnp.dot(k_ref[pl.ds(start, tq), :], qx_ref[...], preferred_element_type=F32)
        if bias_off is not None:
            sc = sc + bt_ref[0, bias_off:bias_off + tq, :]
        return sc

    def consume(sc, vt_ref, kj, br, st, first):
        m_tile = jnp.max(sc, axis=0, keepdims=True)
        if first:
            m_new = m_tile
            pr = jnp.exp2(sc - m_new).astype(BF16)
            acc_ref[br, st] = jnp.dot(vt_ref[kj], pr, preferred_element_type=F32)
        else:
            m_old = m_ref[br, st]
            m_new = jnp.maximum(m_old, m_tile)
            alpha = jnp.exp2(m_old - m_new)
            pr = jnp.exp2(sc - m_new).astype(BF16)
            acc_ref[br, st] = alpha * acc_ref[br, st] + jnp.dot(vt_ref[kj], pr, preferred_element_type=F32)
        m_ref[br, st] = m_new

    def finish(br):
        m0 = m_ref[br, 0]
        m1 = m_ref[br, 1]
        m_all = jnp.maximum(m0, m1)
        acc = jnp.exp2(m0 - m_all) * acc_ref[br, 0] + jnp.exp2(m1 - m_all) * acc_ref[br, 1]
        return acc[0:HEAD_DIM] * (1.0 / acc[HEAD_DIM:HEAD_DIM + 1, :])

    def combine():
        o_win = finish(win)
        o_sel = finish(sel)
        o_cmp = ocmp_ref[...]
        gates_t = gate_ref[0].T
        combs = []
        for g in range(GQA_GROUP):
            sl = slice(g * tq, (g + 1) * tq)
            combs.append(gates_t[g:g + 1, :] * o_cmp[:, sl]
                         + gates_t[GQA_GROUP + g:GQA_GROUP + g + 1, :] * o_sel[:, sl]
                         + gates_t[2 * GQA_GROUP + g:2 * GQA_GROUP + g + 1, :] * o_win[:, sl])
        outs = [jnp.concatenate(combs[p:p + 2], axis=0).T for p in range(0, GQA_GROUP, 2)]
        o_ref[0] = jnp.concatenate(outs, axis=1).astype(o_ref.dtype)

    @pl.when(qi >= 2)
    def _():
        sc_cmp = cmp_scores()
        sw0 = scores(qwin_ref, kw_ref, qi, WINDOW)
        cmp_finish(sc_cmp)
        sw1 = scores(qwin_ref, kw_ref, qi - 1, WINDOW - tq)
        consume(sw0, vwt_ref, qi, win, 0, True)
        sw2 = scores(qwin_ref, kw_ref, qi - 2, WINDOW - 2 * tq)
        consume(sw1, vwt_ref, qi - 1, win, 1, True)
        select_blocks()
        consume(sw2, vwt_ref, qi - 2, win, 0, False)

        ss0 = scores(qsel_ref, kp_ref, qi, WINDOW)
        ss1 = scores(qsel_ref, kp_ref, qi - 1, WINDOW - tq)
        consume(ss0, vst_ref, qi, sel, 0, True)
        sa_ref[...] = scores(qsel_ref, kp_ref, 0, None)
        consume(ss1, vst_ref, qi - 1, sel, 1, True)
        n_far = qi - 1

        def far_pair(pi):
            sb_ref[...] = scores(qsel_ref, kp_ref, 2 * pi + 1, None)
            consume(sa_ref[...], vst_ref, 2 * pi, sel, 1, False)
            sa_ref[...] = scores(qsel_ref, kp_ref, jnp.minimum(2 * pi + 2, n_far - 1), None)
            consume(sb_ref[...], vst_ref, 2 * pi + 1, sel, 0, False)

        n_pairs = n_far // 2

        def far_group(gi, carry):
            for u in range(FAR_UNROLL):
                far_pair(FAR_UNROLL * gi + u)
            return carry

        lax.fori_loop(0, n_pairs // FAR_UNROLL, far_group, 0)
        done = (n_pairs // FAR_UNROLL) * FAR_UNROLL
        left = n_pairs - done
        size = FAR_UNROLL // 2
        while size >= 1:
            @pl.when(left % (2 * size) >= size)
            def _(size=size, base=done):
                for u in range(size):
                    far_pair(base + u)
            done = done + jnp.where(left % (2 * size) >= size, size, 0)
            size //= 2

        @pl.when(n_far % 2 == 1)
        def _():
            consume(sa_ref[...], vst_ref, n_far - 1, sel, 1, False)

        combine()

    @pl.when(qi < 2)
    def _():
        cmp_finish(cmp_scores())
        select_blocks()
        sw0 = scores(qwin_ref, kw_ref, qi, WINDOW)
        ss0 = scores(qsel_ref, kp_ref, qi, WINDOW)
        consume(sw0, vwt_ref, qi, win, 0, True)
        consume(ss0, vst_ref, qi, sel, 0, True)
        for br in (win, sel):
            m_ref[br, 1] = jnp.full((1, rows), MASK_VALUE, F32)
            acc_ref[br, 1] = jnp.zeros((VT_ROWS, rows), F32)

        @pl.when(qi == 1)
        def _():
            sw1 = scores(qwin_ref, kw_ref, 0, WINDOW - tq)
            ss1 = scores(qsel_ref, kp_ref, 0, WINDOW - tq)
            consume(sw1, vwt_ref, 0, win, 1, False)
            consume(ss1, vst_ref, 0, sel, 1, False)

        combine()


def _attention(proj3, kvc, kvct, band, cmpb, gates3, *, tq):
    b, t_len, _ = proj3.shape
    assert WINDOW == 2 * tq
    nc = t_len // CMP_STRIDE
    rows = GQA_GROUP * tq
    qw = GQA_GROUP * HEAD_DIM
    slc_blk = COL_KV // KV_PAIR + N_KV_HEADS
    win_blk = COL_KV // KV_PAIR + 2 * N_KV_HEADS
    return pl.pallas_call(
        functools.partial(_attn_body, tq=tq, t_len=t_len),
        grid=(b, N_KV_HEADS, t_len // tq),
        in_specs=[
            pl.BlockSpec((1, tq, qw), lambda i, h, q: (i, q, h)),
            pl.BlockSpec((1, 1, nc, KV_PAIR), lambda i, h, q: (i, h, 0, 0)),
            pl.BlockSpec((1, 1, KV_PAIR, nc), lambda i, h, q: (i, h, 0, 0)),
            pl.BlockSpec((1, t_len, KV_PAIR), lambda i, h, q: (i, 0, slc_blk + h)),
            pl.BlockSpec((1, t_len, KV_PAIR), lambda i, h, q: (i, 0, win_blk + h)),
            pl.BlockSpec((1, WINDOW + tq, rows), lambda i, h, q: (h, 0, 0)),
            pl.BlockSpec((1, 2 * nc, rows), lambda i, h, q: (h, 0, 0)),
            pl.BlockSpec((1, tq, LANES), lambda i, h, q: (i, q, h)),
        ],
        out_specs=pl.BlockSpec((1, tq, qw), lambda i, h, q: (i, q, h)),
        out_shape=jax.ShapeDtypeStruct((b, t_len, ATTN_WIDTH), BF16),
        scratch_shapes=[
            pltpu.VMEM((t_len, KV_PAIR), BF16),
            pltpu.VMEM((t_len // tq, VT_ROWS, tq), BF16),
            pltpu.VMEM((t_len // tq, VT_ROWS, tq), BF16),
            pltpu.VMEM((HEAD_DIM, tq), F32),
            pltpu.VMEM((2, 2, 1, rows), F32),
            pltpu.VMEM((2, 2, VT_ROWS, rows), F32),
            pltpu.VMEM((tq, rows), F32),
            pltpu.VMEM((tq, rows), F32),
            pltpu.VMEM((HEAD_DIM, rows), F32),
            pltpu.VMEM((KV_PAIR, rows), BF16),
            pltpu.VMEM((KV_PAIR, rows), BF16),
        ],
        compiler_params=_params("arbitrary", "arbitrary", "arbitrary"),
        name="nsa_attention",
    )(proj3, kvc, kvct, proj3, proj3, band, cmpb, gates3)


def _conformer_body(a_ref, g_ref, ah_ref, gh_ref, w_ref, b_ref, lg_ref, lb_ref, o_ref, u_ref, y_ref, *, tt):
    ti = pl.program_id(1)
    u_ref[CONV_HALO:, :] = a_ref[0].astype(F32) * _sigmoid(g_ref[0].astype(F32))
    halo = ah_ref[0].astype(F32) * _sigmoid(gh_ref[0].astype(F32))
    u_ref[0:CONV_HALO, :] = jnp.where(ti > 0, halo, 0.0)
    base = CONV_HALO - (CONV_TAPS - 1)
    ext = CONV_ROWS + CONV_HALO
    for cb in range(CONV_CHANNELS // LANES):
        cs = slice(cb * LANES, (cb + 1) * LANES)
        wblk = w_ref[:, cs]
        for rb in range(tt // CONV_ROWS):
            ublk = u_ref[rb * CONV_ROWS:rb * CONV_ROWS + ext, cs]
            acc = jnp.zeros((CONV_ROWS, LANES), F32)
            for r in range(SUBLANES):
                ur = ublk if r == 0 else pltpu.roll(ublk, ext - r, axis=0)
                for a in range(CONV_HALO // SUBLANES + 1):
                    k = SUBLANES * a + r - base
                    if 0 <= k < CONV_TAPS:
                        acc = acc + wblk[k:k + 1, :] * ur[SUBLANES * a:SUBLANES * a + CONV_ROWS]
            y_ref[rb * CONV_ROWS:(rb + 1) * CONV_ROWS, cs] = acc
    acc = y_ref[...] + b_ref[...]
    mu = jnp.mean(acc, axis=-1, keepdims=True)
    xc = acc - mu
    var = jnp.mean(xc * xc, axis=-1, keepdims=True)
    y = xc * lax.rsqrt(var + NORM_EPS) * lg_ref[...] + lb_ref[...]
    o_ref[0] = (y * _sigmoid(y)).astype(o_ref.dtype)


def _conformer(proj3, w, b, lg, lb, *, tt):
    bsz, t_len, _ = proj3.shape
    a_blk = COL_CONV_A // CONV_CHANNELS
    g_blk = COL_CONV_G // CONV_CHANNELS
    hpt = tt // CONV_HALO
    halo_idx = lambda i, t: jnp.maximum(t * hpt - 1, 0)
    vec = pl.BlockSpec((1, CONV_CHANNELS), lambda i, t: (0, 0))
    return pl.pallas_call(
        functools.partial(_conformer_body, tt=tt),
        grid=(bsz, t_len // tt),
        in_specs=[
            pl.BlockSpec((1, tt, CONV_CHANNELS), lambda i, t: (i, t, a_blk)),
            pl.BlockSpec((1, tt, CONV_CHANNELS), lambda i, t: (i, t, g_blk)),
            pl.BlockSpec((1, CONV_HALO, CONV_CHANNELS), lambda i, t: (i, halo_idx(i, t), a_blk)),
            pl.BlockSpec((1, CONV_HALO, CONV_CHANNELS), lambda i, t: (i, halo_idx(i, t), g_blk)),
            pl.BlockSpec((CONV_TAPS, CONV_CHANNELS), lambda i, t: (0, 0)),
            vec, vec, vec,
        ],
        out_specs=pl.BlockSpec((1, tt, CONV_CHANNELS), lambda i, t: (i, t, 0)),
        out_shape=jax.ShapeDtypeStruct((bsz, t_len, CONV_CHANNELS), BF16),
        scratch_shapes=[pltpu.VMEM((CONV_HALO + tt, CONV_CHANNELS), F32),
                        pltpu.VMEM((tt, CONV_CHANNELS), F32)],
        compiler_params=_params("arbitrary", "arbitrary"),
        name="conformer_conv",
    )(proj3, proj3, proj3, proj3, w, b, lg, lb)


def _outproj_body(x_ref, a_ref, c_ref, wa_ref, wc_ref, o_ref):
    o_ref[...] = (x_ref[...]
                  + jnp.dot(a_ref[...], wa_ref[...], preferred_element_type=F32)
                  + jnp.dot(c_ref[...], wc_ref[...], preferred_element_type=F32))


def _out_projection(x2, attn2, conv2, w_out, *, layer, tm, tn):
    n = x2.shape[0]
    return pl.pallas_call(
        _outproj_body,
        grid=(n // tm, D_MODEL // tn),
        in_specs=[
            pl.BlockSpec((tm, tn), lambda i, j: (i, j)),
            pl.BlockSpec((tm, ATTN_WIDTH), lambda i, j: (i, 0)),
            pl.BlockSpec((tm, CONV_CHANNELS), lambda i, j: (i, 0)),
            pl.BlockSpec((None, ATTN_WIDTH, tn), lambda i, j: (layer, 0, j)),
            pl.BlockSpec((None, CONV_CHANNELS, tn), lambda i, j: (layer, ATTN_WIDTH // CONV_CHANNELS, j)),
        ],
        out_specs=pl.BlockSpec((tm, tn), lambda i, j: (i, j)),
        out_shape=jax.ShapeDtypeStruct((n, D_MODEL), F32),
        compiler_params=_params("arbitrary", "arbitrary"),
        name="out_projection",
    )(x2, attn2, conv2, w_out, w_out)


def _ffn_up_body(x_ref, xh_ref, g_ref, wa_ref, wg_ref, cwa_ref, cwg_ref, cba_ref, cbg_ref, o_ref,
                 h_ref, ua_ref, ug_ref, *, tm, tiles_per_seq, rows):
    i = pl.program_id(0)

    @pl.when(pl.program_id(1) == 0)
    def _():
        h_ref[FFN_HALO:, :] = _rms_norm_rows(x_ref[...], g_ref[...]).astype(BF16)
        hh = _rms_norm_rows(xh_ref[...], g_ref[...])
        h_ref[0:FFN_HALO, :] = jnp.where(i % tiles_per_seq != 0, hh, 0.0).astype(BF16)

    units = [(c, slice(0, o_ref.shape[1])) for c in range(tm // rows)]

    def project(c, cs):
        hc = h_ref[c * rows:(c + 1) * rows + FFN_HALO, :]
        ug_ref[c, :, cs] = jnp.dot(hc, wg_ref[:, cs], preferred_element_type=F32)
        ua_ref[c, :, cs] = jnp.dot(hc, wa_ref[:, cs], preferred_element_type=F32)

    def conv(u_ref, c, cs, cw_ref, cb_ref):
        u = u_ref[c, :, cs]
        y = cw_ref[FFN_CONV_TAPS - 1:FFN_CONV_TAPS, cs] * u[FFN_HALO:]
        for s in range(1, FFN_CONV_TAPS):
            k = FFN_CONV_TAPS - 1 - s
            y = y + cw_ref[k:k + 1, cs] * pltpu.roll(u, s, axis=0)[FFN_HALO:]
        return y + cb_ref[:, cs]

    def finish(c, cs):
        gate = conv(ug_ref, c, cs, cwg_ref, cbg_ref)
        act = gate * _sigmoid(gate)
        a = conv(ua_ref, c, cs, cwa_ref, cba_ref)
        o_ref[c * rows:(c + 1) * rows, cs] = (act * a).astype(o_ref.dtype)

    project(*units[0])
    for u, unit in enumerate(units):
        if u + 1 < len(units):
            project(*units[u + 1])
        finish(*unit)


def _ffn_up(x2, g, w_up, cw, cb, *, layer, tm, tn, t_len):
    n = x2.shape[0]
    nj = D_FF // tn
    hpt = tm // FFN_HALO
    rows = min(FFN_ROWS, tm)
    return pl.pallas_call(
        functools.partial(_ffn_up_body, tm=tm, tiles_per_seq=t_len // tm, rows=rows),
        grid=(n // tm, nj),
        in_specs=[
            pl.BlockSpec((tm, D_MODEL), lambda i, j: (i, 0)),
            pl.BlockSpec((FFN_HALO, D_MODEL), lambda i, j: (jnp.maximum(i * hpt - 1, 0), 0)),
            pl.BlockSpec((1, D_MODEL), lambda i, j: (0, 0)),
            pl.BlockSpec((None, D_MODEL, tn), lambda i, j: (layer, 0, j)),
            pl.BlockSpec((None, D_MODEL, tn), lambda i, j: (layer, 0, j + nj)),
            pl.BlockSpec((FFN_CONV_TAPS, tn), lambda i, j: (0, j)),
            pl.BlockSpec((FFN_CONV_TAPS, tn), lambda i, j: (0, j + nj)),
            pl.BlockSpec((1, tn), lambda i, j: (0, j)),
            pl.BlockSpec((1, tn), lambda i, j: (0, j + nj)),
        ],
        out_specs=pl.BlockSpec((tm, tn), lambda i, j: (i, j)),
        out_shape=jax.ShapeDtypeStruct((n, D_FF), BF16),
        scratch_shapes=[
            pltpu.VMEM((FFN_HALO + tm, D_MODEL), BF16),
            pltpu.VMEM((tm // rows, FFN_HALO + rows, tn), F32),
            pltpu.VMEM((tm // rows, FFN_HALO + rows, tn), F32),
        ],
        compiler_params=_params("arbitrary", "arbitrary"),
        name="ffn_up",
    )(x2, x2, g, w_up, w_up, cw, cw, cb, cb)


def _ffn_down_body(act_ref, w_ref, x_ref, o_ref):
    o_ref[...] = x_ref[...] + jnp.dot(act_ref[...], w_ref[...], preferred_element_type=F32)


def _ffn_down(act, w_down, x2, *, layer, tm, tn):
    n = x2.shape[0]
    return pl.pallas_call(
        _ffn_down_body,
        grid=(n // tm, D_MODEL // tn),
        in_specs=[
            pl.BlockSpec((tm, D_FF), lambda i, j: (i, 0)),
            pl.BlockSpec((None, D_FF, tn), lambda i, j: (layer, 0, j)),
            pl.BlockSpec((tm, tn), lambda i, j: (i, j)),
        ],
        out_specs=pl.BlockSpec((tm, tn), lambda i, j: (i, j)),
        out_shape=jax.ShapeDtypeStruct((n, D_MODEL), F32),
        compiler_params=_params("arbitrary", "arbitrary"),
        name="ffn_down",
    )(act, w_down, x2)


def _final_norm_body(x_ref, g_ref, o_ref):
    o_ref[...] = _rms_norm_rows(x_ref[...], g_ref[...])


def _final_norm(x2, g, *, tm):
    n = x2.shape[0]
    return pl.pallas_call(
        _final_norm_body,
        grid=(n // tm,),
        in_specs=[pl.BlockSpec((tm, D_MODEL), lambda i: (i, 0)), pl.BlockSpec((1, D_MODEL), lambda i: (0, 0))],
        out_specs=pl.BlockSpec((tm, D_MODEL), lambda i: (i, 0)),
        out_shape=jax.ShapeDtypeStruct((n, D_MODEL), F32),
        compiler_params=_params("arbitrary"),
        name="final_norm",
    )(x2, g)


def _in_weight_layout(w_in):
    depth = w_in.shape[0]
    kv0 = ATTN_WIDTH
    gate0 = kv0 + KV_COLS
    conv0 = gate0 + GATE_COLS
    parts = [w_in[..., :ATTN_WIDTH], w_in[..., conv0:conv0 + 2 * CONV_CHANNELS]]
    for br in range(N_BRANCH):
        for h in range(N_KV_HEADS):
            for kv in range(2):
                c0 = kv0 + ((br * 2 + kv) * N_KV_HEADS + h) * HEAD_DIM
                parts.append(w_in[..., c0:c0 + HEAD_DIM])
    w_main = jnp.concatenate(parts, axis=-1).astype(BF16)
    wg = w_in[..., gate0:gate0 + GATE_COLS].reshape(depth, D_MODEL, N_KV_HEADS, GQA_GROUP, N_BRANCH)
    wg = wg.transpose(0, 1, 2, 4, 3).reshape(depth, D_MODEL, N_KV_HEADS, N_BRANCH * GQA_GROUP)
    wg = jnp.pad(wg, ((0, 0), (0, 0), (0, 0), (0, LANES - N_BRANCH * GQA_GROUP)))
    return w_main, wg.reshape(depth, D_MODEL, N_KV_HEADS * LANES).astype(BF16)


def _pick(n, prefs):
    for p in prefs:
        if n % p == 0:
            return p
    return n


def kernel(x, rel_bias, mix_norm_g, w_in, cmp_pos, cmp_w1, cmp_w2, conv_w, conv_b, conv_ln_g, conv_ln_b,
           w_out, ffn_norm_g, w_up, ffn_conv_w, ffn_conv_b, w_down, final_norm_g):
    bsz, t_len, _ = x.shape
    depth = w_in.shape[0]
    n = bsz * t_len
    assert t_len % 256 == 0 and t_len // SLC_LEN <= HEAD_DIM
    tq = 256
    tm = _pick(t_len, (1024, 512, 256))
    nc = t_len // CMP_STRIDE

    band, cmpb = _bias_tiles(rel_bias, t_len, tq)
    col_scale = jnp.concatenate([jnp.full((1, ATTN_WIDTH), Q_SCALE, F32),
                                 jnp.ones((1, PROJ_COLS - ATTN_WIDTH), F32)], axis=1)
    x2 = x.reshape(n, D_MODEL)
    w_main, wg = _in_weight_layout(w_in)
    w_out_b, w_up_b, w_down_b = w_out.astype(BF16), w_up.astype(BF16), w_down.astype(BF16)
    for l in range(depth):
        proj, gates = _in_projection(x2, mix_norm_g[l][None, :], w_main, wg, col_scale, layer=l, tm=tm, tn=768)
        proj3 = proj.reshape(bsz, t_len, PROJ_COLS)
        gates3 = gates.reshape(bsz, t_len, N_KV_HEADS * LANES)

        posx, w1big, w2big = _compress_weight_layout(cmp_pos[l], cmp_w1[l], cmp_w2[l])
        tokl = proj3[:, :, COL_KV:COL_KV + CMP_WIDTH].reshape(bsz, nc, CMP_STRIDE, CMP_WIDTH).transpose(0, 2, 1, 3)
        kvc, kvct = _compress(tokl, posx, w1big, w2big)

        attn = _attention(proj3, kvc, kvct, band, cmpb, gates3, tq=tq)
        conv = _conformer(proj3, conv_w[l], conv_b[l][None, :], conv_ln_g[l][None, :],
                          conv_ln_b[l][None, :], tt=256)
        x2 = _out_projection(x2, attn.reshape(n, ATTN_WIDTH), conv.reshape(n, CONV_CHANNELS),
                             w_out_b, layer=l, tm=tm, tn=1024)

        act = _ffn_up(x2, ffn_norm_g[l][None, :], w_up_b, ffn_conv_w[l],
                      ffn_conv_b[l][None, :], layer=l, tm=tm, tn=512, t_len=t_len)
        x2 = _ffn_down(act, w_down_b, x2, layer=l, tm=tm, tn=512)
    x2 = _final_norm(x2, final_norm_g[None, :], tm=_pick(n, (512, 256)))
    return x2.reshape(bsz, t_len, D_MODEL)
```

```python
import functools
import math

import numpy as np
import jax
import jax.numpy as jnp
from jax import lax
from jax.experimental import pallas as pl
from jax.experimental.pallas import tpu as pltpu

F32 = jnp.float32
BF16 = jnp.bfloat16

D_MODEL = 2048
HEAD_DIM = 64
N_KV_HEADS = 4
GQA_GROUP = 4
N_Q_HEADS = N_KV_HEADS * GQA_GROUP
N_BRANCH = 3
ATTN_WIDTH = N_Q_HEADS * HEAD_DIM
CMP_LEN = 32
CMP_STRIDE = 16
SLC_LEN = 64
SLC_TOPK = 16
WINDOW = 512
CONV_CHANNELS = D_MODEL - ATTN_WIDTH
CONV_TAPS = 31
D_FF = 5632
FFN_CONV_TAPS = 3
N_BUCKETS = 32
MAX_DISTANCE = 128
NORM_EPS = 1e-6

KV_PAIR = 2 * HEAD_DIM
KV_COLS = N_BRANCH * N_KV_HEADS * KV_PAIR
GATE_COLS = N_BRANCH * N_Q_HEADS
COL_Q = 0
COL_CONV_A = ATTN_WIDTH
COL_CONV_G = COL_CONV_A + CONV_CHANNELS
COL_KV = COL_CONV_G + CONV_CHANNELS
PROJ_COLS = COL_KV + KV_COLS

MASK_VALUE = -1e30
BLOCK_PENALTY = -1e9
KEY_BIG = 1e30
KEY_TAKEN = -3e38
LOG2E = math.log2(math.e)
Q_SCALE = HEAD_DIM ** -0.5 * LOG2E

LANES = 128
SUBLANES = 8
CONV_ROWS = 128
FFN_ROWS = 1024
VMEM_LIMIT = 56 * 1024 * 1024
CONV_HALO = 32
FFN_HALO = 16


def _t5_bucket_last_distance():
    n = np.arange(0, 4 * MAX_DISTANCE, dtype=np.int64)
    max_exact = N_BUCKETS // 2
    nf = np.maximum(n, 1).astype(np.float64)
    large = max_exact + np.floor(np.log(nf / max_exact) / math.log(MAX_DISTANCE / max_exact)
                                 * (N_BUCKETS - max_exact)).astype(np.int64)
    large = np.minimum(large, N_BUCKETS - 1)
    bucket = np.where(n < max_exact, n, large)
    last = []
    for b in range(N_BUCKETS - 1):
        idx = np.nonzero(bucket == b)[0]
        last.append(int(idx.max()) if idx.size else None)
    return last


_BUCKET_LAST = _t5_bucket_last_distance()


def _params(*sem):
    return pltpu.CompilerParams(dimension_semantics=sem, vmem_limit_bytes=VMEM_LIMIT)


def _sigmoid(x):
    return jax.nn.sigmoid(x)


def _rms_norm_rows(x, g):
    ms = jnp.mean(x * x, axis=-1, keepdims=True)
    return x * lax.rsqrt(ms + NORM_EPS) * g


def _bias_from_distance(dist, tab_ref, head):
    c_far = tab_ref[N_BUCKETS - 1, head]
    val = jnp.zeros(dist.shape, F32)
    for b in range(N_BUCKETS - 2, -1, -1):
        if _BUCKET_LAST[b] is None:
            continue
        val = jnp.where(dist <= _BUCKET_LAST[b], tab_ref[b, head] - c_far, val)
    return val


def _band_bias_body(tab_ref, o_ref, *, tq):
    head = pl.program_id(0) * GQA_GROUP + pl.program_id(1)
    shape = (WINDOW + tq, tq)
    dist = (lax.broadcasted_iota(jnp.int32, shape, 1) + WINDOW
            - lax.broadcasted_iota(jnp.int32, shape, 0))
    val = _bias_from_distance(dist, tab_ref, head) * LOG2E
    val = jnp.where(dist >= 0, jnp.where(dist < WINDOW, val, MASK_VALUE), MASK_VALUE)
    o_ref[0] = val


def _cmp_bias_body(tab_ref, o_ref, *, tq, nc):
    head = pl.program_id(0) * GQA_GROUP + pl.program_id(1)
    shape = (2 * nc, tq)
    r = lax.broadcasted_iota(jnp.int32, shape, 0)
    i = lax.broadcasted_iota(jnp.int32, shape, 1)
    dist = i - (r - nc) * CMP_STRIDE - (CMP_LEN - 1)
    val = _bias_from_distance(dist, tab_ref, head) * LOG2E
    o_ref[0] = jnp.where(dist >= 0, val, MASK_VALUE)


def _bias_tiles(rel_bias, t_len, tq):
    nc = t_len // CMP_STRIDE
    rows = GQA_GROUP * tq
    smem = pl.BlockSpec(memory_space=pltpu.SMEM)
    band = pl.pallas_call(
        functools.partial(_band_bias_body, tq=tq),
        grid=(N_KV_HEADS, GQA_GROUP),
        in_specs=[smem],
        out_specs=pl.BlockSpec((1, WINDOW + tq, tq), lambda h, g: (h, 0, g)),
        out_shape=jax.ShapeDtypeStruct((N_KV_HEADS, WINDOW + tq, rows), F32),
        compiler_params=_params("arbitrary", "arbitrary"),
        name="band_bias",
    )(rel_bias)
    cmpb = pl.pallas_call(
        functools.partial(_cmp_bias_body, tq=tq, nc=nc),
        grid=(N_KV_HEADS, GQA_GROUP),
        in_specs=[smem],
        out_specs=pl.BlockSpec((1, 2 * nc, tq), lambda h, g: (h, 0, g)),
        out_shape=jax.ShapeDtypeStruct((N_KV_HEADS, 2 * nc, rows), F32),
        compiler_params=_params("arbitrary", "arbitrary"),
        name="cmp_bias",
    )(rel_bias)
    return band, cmpb


def _inproj_body(x_ref, g_ref, w_ref, wg_ref, cs_ref, o_ref, gate_ref, h_ref):
    @pl.when(pl.program_id(1) == 0)
    def _():
        h = _rms_norm_rows(x_ref[...], g_ref[...]).astype(BF16)
        h_ref[...] = h
        gate_ref[...] = _sigmoid(jnp.dot(h, wg_ref[...], preferred_element_type=F32))

    acc = jnp.dot(h_ref[...], w_ref[...], preferred_element_type=F32)
    o_ref[...] = (acc * cs_ref[...]).astype(o_ref.dtype)


def _in_projection(x2, g, w, wg, col_scale, *, layer, tm, tn):
    n = x2.shape[0]
    gcols = wg.shape[-1]
    return pl.pallas_call(
        _inproj_body,
        grid=(n // tm, PROJ_COLS // tn),
        in_specs=[
            pl.BlockSpec((tm, D_MODEL), lambda i, j: (i, 0)),
            pl.BlockSpec((1, D_MODEL), lambda i, j: (0, 0)),
            pl.BlockSpec((None, D_MODEL, tn), lambda i, j: (layer, 0, j)),
            pl.BlockSpec((None, D_MODEL, gcols), lambda i, j: (layer, 0, 0)),
            pl.BlockSpec((1, tn), lambda i, j: (0, j)),
        ],
        out_specs=[
            pl.BlockSpec((tm, tn), lambda i, j: (i, j)),
            pl.BlockSpec((tm, gcols), lambda i, j: (i, 0)),
        ],
        out_shape=[
            jax.ShapeDtypeStruct((n, PROJ_COLS), BF16),
            jax.ShapeDtypeStruct((n, gcols), F32),
        ],
        scratch_shapes=[pltpu.VMEM((tm, D_MODEL), BF16)],
        compiler_params=_params("arbitrary", "arbitrary"),
        name="in_projection",
    )(x2, g, w, wg, col_scale)


CMP_WIDTH = N_KV_HEADS * KV_PAIR


def _compress_body(x_ref, pos_ref, w1_ref, w2_ref, o_ref, ot_ref, top_ref, bot_ref, *, nc):
    l = pl.program_id(1)

    @pl.when(l == 0)
    def _():
        top_ref[...] = jnp.zeros_like(top_ref)
        bot_ref[...] = jnp.zeros_like(bot_ref)

    x = x_ref[0, 0].astype(F32)
    top_ref[...] += jnp.dot((x + pos_ref[0, 0]).astype(BF16), w1_ref[0, 0], preferred_element_type=F32)
    bot_ref[...] += jnp.dot((x + pos_ref[0, 1]).astype(BF16), w1_ref[0, 1], preferred_element_type=F32)

    @pl.when(l == pl.num_programs(1) - 1)
    def _():
        pre = top_ref[...] + pltpu.roll(bot_ref[...], nc - 1, axis=0)
        act = pre * _sigmoid(pre)
        out = jnp.dot(act.astype(BF16), w2_ref[...], preferred_element_type=F32)
        for h in range(N_KV_HEADS):
            kv = out[:, h * KV_PAIR:(h + 1) * KV_PAIR]
            o_ref[0, h] = kv.astype(o_ref.dtype)
            ot_ref[0, h] = kv.T.astype(ot_ref.dtype)


def _compress(tokl, posx, w1big, w2big):
    b, _, nc, _ = tokl.shape
    return pl.pallas_call(
        functools.partial(_compress_body, nc=nc),
        grid=(b, CMP_STRIDE),
        in_specs=[
            pl.BlockSpec((1, 1, nc, CMP_WIDTH), lambda i, l: (i, l, 0, 0)),
            pl.BlockSpec((1, 2, 1, CMP_WIDTH), lambda i, l: (l, 0, 0, 0)),
            pl.BlockSpec((1, 2, CMP_WIDTH, CMP_WIDTH), lambda i, l: (l, 0, 0, 0)),
            pl.BlockSpec((CMP_WIDTH, CMP_WIDTH), lambda i, l: (0, 0)),
        ],
        out_specs=[
            pl.BlockSpec((1, N_KV_HEADS, nc, KV_PAIR), lambda i, l: (i, 0, 0, 0)),
            pl.BlockSpec((1, N_KV_HEADS, KV_PAIR, nc), lambda i, l: (i, 0, 0, 0)),
        ],
        out_shape=[
            jax.ShapeDtypeStruct((b, N_KV_HEADS, nc, KV_PAIR), BF16),
            jax.ShapeDtypeStruct((b, N_KV_HEADS, KV_PAIR, nc), BF16),
        ],
        scratch_shapes=[pltpu.VMEM((nc, CMP_WIDTH), F32), pltpu.VMEM((nc, CMP_WIDTH), F32)],
        compiler_params=_params("arbitrary", "arbitrary"),
        name="compress",
    )(tokl, posx, w1big, w2big)


def _compress_weight_layout(pos, w1, w2):
    def block_diag(blocks):
        rows = []
        for h in range(N_KV_HEADS):
            for kv in range(2):
                off = (h * 2 + kv) * HEAD_DIM
                pad = [(0, 0)] * (blocks.ndim - 2) + [(off, CMP_WIDTH - HEAD_DIM - off)]
                rows.append(jnp.pad(blocks[kv], pad))
        return jnp.concatenate(rows, axis=-2).astype(BF16)

    w1r = w1.reshape(2, 2, CMP_STRIDE, HEAD_DIM, HEAD_DIM)
    w1big = block_diag(w1r.transpose(0, 2, 1, 3, 4))
    w2big = block_diag(w2)
    posr = pos.reshape(2, 2, CMP_STRIDE, HEAD_DIM).transpose(2, 1, 0, 3)
    posx = jnp.broadcast_to(posr[:, :, None], (CMP_STRIDE, 2, N_KV_HEADS, 2, HEAD_DIM))
    return posx.reshape(CMP_STRIDE, 2, 1, CMP_WIDTH), w1big, w2big


VT_ROWS = HEAD_DIM + 16
FAR_UNROLL = 4


def _attn_body(q_ref, kvc_ref, kvct_ref, kvs_ref, kvw_ref, bt_ref, bc_ref, gate_ref, o_ref,
               kp_ref, vst_ref, vwt_ref, key_ref, m_ref, acc_ref, sa_ref, sb_ref, ocmp_ref, qsel_ref, qwin_ref,
               *, tq, t_len):
    qi = pl.program_id(2)
    nc = t_len // CMP_STRIDE
    ns = t_len // SLC_LEN
    rows = GQA_GROUP * tq
    blocks_per_tile = tq // SLC_LEN

    @pl.when(qi == 0)
    def _():
        k = kvs_ref[0][:, :HEAD_DIM]
        blk = lax.broadcasted_iota(jnp.int32, (t_len, HEAD_DIM), 0) // SLC_LEN
        col = lax.broadcasted_iota(jnp.int32, (t_len, HEAD_DIM), 1)
        onehot = jnp.where(blk == col, 1.0, 0.0).astype(BF16)
        kp_ref[...] = jnp.concatenate([k, onehot], axis=1)
        ones_rows = jnp.where(lax.broadcasted_iota(jnp.int32, (VT_ROWS - HEAD_DIM, tq), 0) == 0, 1.0, 0.0)
        for c in range(t_len // tq):
            st = kvs_ref[0, c * tq:(c + 1) * tq, :].astype(F32).T[HEAD_DIM:]
            vst_ref[c] = jnp.concatenate([st, ones_rows], axis=0).astype(BF16)
            wt = kvw_ref[0, c * tq:(c + 1) * tq, :].astype(F32).T[HEAD_DIM:]
            vwt_ref[c] = jnp.concatenate([wt, ones_rows], axis=0).astype(BF16)

    q_t = q_ref[0].astype(F32).T
    q4_t = jnp.concatenate([q_t[g * HEAD_DIM:(g + 1) * HEAD_DIM] for g in range(GQA_GROUP)],
                           axis=1).astype(BF16)
    qwin_ref[...] = jnp.concatenate([q4_t, jnp.zeros((HEAD_DIM, rows), BF16)], axis=0)
    qsel_ref[0:HEAD_DIM, :] = q4_t
    kw_ref = kvw_ref.at[0]
    win, sel = 0, 1

    def cmp_scores():
        bias_row = pl.multiple_of(nc - qi * (tq // CMP_STRIDE), tq // CMP_STRIDE)
        return (jnp.dot(kvc_ref[0, 0], qwin_ref[...], preferred_element_type=F32)
                + bc_ref[0, pl.ds(bias_row, nc), :])

    def cmp_finish(s):
        m = jnp.max(s, axis=0, keepdims=True)
        p = jnp.exp2(s - m)
        l = jnp.sum(p, axis=0, keepdims=True)
        pn = p * jnp.where(m > 0.5 * MASK_VALUE, 1.0 / l, 0.0)
        ocmp_ref[...] = jnp.dot(kvct_ref[0, 0, HEAD_DIM:, :], pn.astype(BF16),
                                preferred_element_type=F32)

        ps = pn[:, 0:tq] + pn[:, tq:2 * tq] + pn[:, 2 * tq:3 * tq] + pn[:, 3 * tq:4 * tq]
        sj = lax.broadcasted_iota(jnp.int32, (HEAD_DIM, nc), 0) * SLC_LEN
        ci = lax.broadcasted_iota(jnp.int32, (HEAD_DIM, nc), 1) * CMP_STRIDE
        overlap = jnp.where(ci < sj + SLC_LEN, jnp.where(ci + CMP_LEN > sj, 1.0, 0.0), 0.0).astype(BF16)
        p_hi = ps.astype(BF16)
        r_hi = ps - p_hi.astype(F32)
        p_md = r_hi.astype(BF16)
        p_lo = (r_hi - p_md.astype(F32)).astype(BF16)
        imp = (jnp.dot(overlap, p_hi, preferred_element_type=F32)
               + jnp.dot(overlap, p_md, preferred_element_type=F32)
               + jnp.dot(overlap, p_lo, preferred_element_type=F32))
        t = qi * tq + lax.broadcasted_iota(jnp.int32, (HEAD_DIM, tq), 1)
        blk = lax.broadcasted_iota(jnp.int32, (HEAD_DIM, tq), 0)
        cur = t // SLC_LEN
        key = jnp.where(blk == 0, KEY_BIG, jnp.where(blk == cur, KEY_BIG, jnp.where(blk == cur - 1, KEY_BIG, imp)))
        key_ref[...] = jnp.where(blk * SLC_LEN <= t, key, -KEY_BIG)

    def select_blocks():
        key = key_ref[...]
        blk = lax.broadcasted_iota(jnp.int32, (HEAD_DIM, tq), 0)
        pen = jnp.full((HEAD_DIM, tq), BLOCK_PENALTY, F32)
        for _ in range(min(SLC_TOPK, ns)):
            top = jnp.max(key, axis=0, keepdims=True)
            first = jnp.min(jnp.where(key == top, blk, HEAD_DIM), axis=0, keepdims=True)
            hit = blk == first
            pen = jnp.where(hit, 0.0, pen)
            key = jnp.where(hit, KEY_TAKEN, key)
        qsel_ref[HEAD_DIM:, :] = jnp.concatenate([pen.astype(BF16)] * GQA_GROUP, axis=1)

    def scores(qx_ref, k_ref, kj, bias_off):
        start = pl.multiple_of(kj * tq, tq)
        sc = jnp.dot(k_ref[pl.ds(start, tq), :], qx_ref[...], preferred_element_type=F32)
        if bias_off is not None:
            sc = sc + bt_ref[0, bias_off:bias_off + tq, :]
        return sc

    def consume(sc, vt_ref, kj, br, st, first):
        m_tile = jnp.max(sc, axis=0, keepdims=True)
        if first:
            m_new = m_tile
            pr = jnp.exp2(sc - m_new).astype(BF16)
            acc_ref[br, st] = jnp.dot(vt_ref[kj], pr, preferred_element_type=F32)
        else:
            m_old = m_ref[br, st]
            m_new = jnp.maximum(m_old, m_tile)
            alpha = jnp.exp2(m_old - m_new)
            pr = jnp.exp2(sc - m_new).astype(BF16)
            acc_ref[br, st] = alpha * acc_ref[br, st] + jnp.dot(vt_ref[kj], pr, preferred_element_type=F32)
        m_ref[br, st] = m_new

    def finish(br):
        m0 = m_ref[br, 0]
        m1 = m_ref[br, 1]
        m_all = jnp.maximum(m0, m1)
        acc = jnp.exp2(m0 - m_all) * acc_ref[br, 0] + jnp.exp2(m1 - m_all) * acc_ref[br, 1]
        return acc[0:HEAD_DIM] * (1.0 / acc[HEAD_DIM:HEAD_DIM + 1, :])

    def combine():
        o_win = finish(win)
        o_sel = finish(sel)
        o_cmp = ocmp_ref[...]
        gates_t = gate_ref[0].T
        combs = []
        for g in range(GQA_GROUP):
            sl = slice(g * tq, (g + 1) * tq)
            combs.append(gates_t[g:g + 1, :] * o_cmp[:, sl]
                         + gates_t[GQA_GROUP + g:GQA_GROUP + g + 1, :] * o_sel[:, sl]
                         + gates_t[2 * GQA_GROUP + g:2 * GQA_GROUP + g + 1, :] * o_win[:, sl])
        outs = [jnp.concatenate(combs[p:p + 2], axis=0).T for p in range(0, GQA_GROUP, 2)]
        o_ref[0] = jnp.concatenate(outs, axis=1).astype(o_ref.dtype)

    @pl.when(qi >= 2)
    def _():
        sc_cmp = cmp_scores()
        sw0 = scores(qwin_ref, kw_ref, qi, WINDOW)
        cmp_finish(sc_cmp)
        sw1 = scores(qwin_ref, kw_ref, qi - 1, WINDOW - tq)
        consume(sw0, vwt_ref, qi, win, 0, True)
        sw2 = scores(qwin_ref, kw_ref, qi - 2, WINDOW - 2 * tq)
        consume(sw1, vwt_ref, qi - 1, win, 1, True)
        select_blocks()
        consume(sw2, vwt_ref, qi - 2, win, 0, False)

        ss0 = scores(qsel_ref, kp_ref, qi, WINDOW)
        ss1 = scores(qsel_ref, kp_ref, qi - 1, WINDOW - tq)
        consume(ss0, vst_ref, qi, sel, 0, True)
        sa_ref[...] = scores(qsel_ref, kp_ref, 0, None)
        consume(ss1, vst_ref, qi - 1, sel, 1, True)
        n_far = qi - 1

        def far_pair(pi):
            sb_ref[...] = scores(qsel_ref, kp_ref, 2 * pi + 1, None)
            consume(sa_ref[...], vst_ref, 2 * pi, sel, 1, False)
            sa_ref[...] = scores(qsel_ref, kp_ref, jnp.minimum(2 * pi + 2, n_far - 1), None)
            consume(sb_ref[...], vst_ref, 2 * pi + 1, sel, 0, False)

        n_pairs = n_far // 2

        def far_group(gi, carry):
            for u in range(FAR_UNROLL):
                far_pair(FAR_UNROLL * gi + u)
            return carry

        lax.fori_loop(0, n_pairs // FAR_UNROLL, far_group, 0)
        done = (n_pairs // FAR_UNROLL) * FAR_UNROLL
        left = n_pairs - done
        size = FAR_UNROLL // 2
        while size >= 1:
            @pl.when(left % (2 * size) >= size)
            def _(size=size, base=done):
                for u in range(size):
                    far_pair(base + u)
            done = done + jnp.where(left % (2 * size) >= size, size, 0)
            size //= 2

        @pl.when(n_far % 2 == 1)
        def _():
            consume(sa_ref[...], vst_ref, n_far - 1, sel, 1, False)

        combine()

    @pl.when(qi < 2)
    def _():
        cmp_finish(cmp_scores())
        select_blocks()
        sw0 = scores(qwin_ref, kw_ref, qi, WINDOW)
        ss0 = scores(qsel_ref, kp_ref, qi, WINDOW)
        consume(sw0, vwt_ref, qi, win, 0, True)
        consume(ss0, vst_ref, qi, sel, 0, True)
        for br in (win, sel):
            m_ref[br, 1] = jnp.full((1, rows), MASK_VALUE, F32)
            acc_ref[br, 1] = jnp.zeros((VT_ROWS, rows), F32)

        @pl.when(qi == 1)
        def _():
            sw1 = scores(qwin_ref, kw_ref, 0, WINDOW - tq)
            ss1 = scores(qsel_ref, kp_ref, 0, WINDOW - tq)
            consume(sw1, vwt_ref, 0, win, 1, False)
            consume(ss1, vst_ref, 0, sel, 1, False)

        combine()


def _attention(proj3, kvc, kvct, band, cmpb, gates3, *, tq):
    b, t_len, _ = proj3.shape
    assert WINDOW == 2 * tq
    nc = t_len // CMP_STRIDE
    rows = GQA_GROUP * tq
    qw = GQA_GROUP * HEAD_DIM
    slc_blk = COL_KV // KV_PAIR + N_KV_HEADS
    win_blk = COL_KV // KV_PAIR + 2 * N_KV_HEADS
    return pl.pallas_call(
        functools.partial(_attn_body, tq=tq, t_len=t_len),
        grid=(b, N_KV_HEADS, t_len // tq),
        in_specs=[
            pl.BlockSpec((1, tq, qw), lambda i, h, q: (i, q, h)),
            pl.BlockSpec((1, 1, nc, KV_PAIR), lambda i, h, q: (i, h, 0, 0)),
            pl.BlockSpec((1, 1, KV_PAIR, nc), lambda i, h, q: (i, h, 0, 0)),
            pl.BlockSpec((1, t_len, KV_PAIR), lambda i, h, q: (i, 0, slc_blk + h)),
            pl.BlockSpec((1, t_len, KV_PAIR), lambda i, h, q: (i, 0, win_blk + h)),
            pl.BlockSpec((1, WINDOW + tq, rows), lambda i, h, q: (h, 0, 0)),
            pl.BlockSpec((1, 2 * nc, rows), lambda i, h, q: (h, 0, 0)),
            pl.BlockSpec((1, tq, LANES), lambda i, h, q: (i, q, h)),
        ],
        out_specs=pl.BlockSpec((1, tq, qw), lambda i, h, q: (i, q, h)),
        out_shape=jax.ShapeDtypeStruct((b, t_len, ATTN_WIDTH), BF16),
        scratch_shapes=[
            pltpu.VMEM((t_len, KV_PAIR), BF16),
            pltpu.VMEM((t_len // tq, VT_ROWS, tq), BF16),
            pltpu.VMEM((t_len // tq, VT_ROWS, tq), BF16),
            pltpu.VMEM((HEAD_DIM, tq), F32),
            pltpu.VMEM((2, 2, 1, rows), F32),
            pltpu.VMEM((2, 2, VT_ROWS, rows), F32),
            pltpu.VMEM((tq, rows), F32),
            pltpu.VMEM((tq, rows), F32),
            pltpu.VMEM((HEAD_DIM, rows), F32),
            pltpu.VMEM((KV_PAIR, rows), BF16),
            pltpu.VMEM((KV_PAIR, rows), BF16),
        ],
        compiler_params=_params("arbitrary", "arbitrary", "arbitrary"),
        name="nsa_attention",
    )(proj3, kvc, kvct, proj3, proj3, band, cmpb, gates3)


def _mix_out_body(a_ref, g_ref, ah_ref, gh_ref, w_ref, b_ref, lg_ref, lb_ref, attn_ref, x_ref, wa_ref, wc_ref,
                  o_ref, u_ref, y_ref, *, tt):
    ti = pl.program_id(1)
    acc_attn = jnp.dot(attn_ref[0], wa_ref[...], preferred_element_type=F32)
    u_ref[CONV_HALO:, :] = a_ref[0].astype(F32) * _sigmoid(g_ref[0].astype(F32))
    halo = ah_ref[0].astype(F32) * _sigmoid(gh_ref[0].astype(F32))
    u_ref[0:CONV_HALO, :] = jnp.where(ti > 0, halo, 0.0)
    base = CONV_HALO - (CONV_TAPS - 1)
    ext = CONV_ROWS + CONV_HALO
    for rb in range(tt // CONV_ROWS):
        rs = slice(rb * CONV_ROWS, (rb + 1) * CONV_ROWS)
        for cb in range(CONV_CHANNELS // LANES):
            cs = slice(cb * LANES, (cb + 1) * LANES)
            wblk = w_ref[:, cs]
            ublk = u_ref[rb * CONV_ROWS:rb * CONV_ROWS + ext, cs]
            acc = jnp.zeros((CONV_ROWS, LANES), F32)
            for r in range(SUBLANES):
                ur = ublk if r == 0 else pltpu.roll(ublk, ext - r, axis=0)
                for a in range(CONV_HALO // SUBLANES + 1):
                    k = SUBLANES * a + r - base
                    if 0 <= k < CONV_TAPS:
                        acc = acc + wblk[k:k + 1, :] * ur[SUBLANES * a:SUBLANES * a + CONV_ROWS]
            y_ref[rs, cs] = acc
        acc = y_ref[rs, :] + b_ref[...]
        mu = jnp.mean(acc, axis=-1, keepdims=True)
        xc = acc - mu
        var = jnp.mean(xc * xc, axis=-1, keepdims=True)
        y = xc * lax.rsqrt(var + NORM_EPS) * lg_ref[...] + lb_ref[...]
        conv_act = (y * _sigmoid(y)).astype(BF16)
        o_ref[0, rs, :] = (x_ref[0, rs, :] + acc_attn[rs]
                           + jnp.dot(conv_act, wc_ref[...], preferred_element_type=F32))


def _mix_out(proj3, w, b, lg, lb, attn, x3, w_out, *, layer, tt):
    bsz, t_len, _ = proj3.shape
    a_blk = COL_CONV_A // CONV_CHANNELS
    g_blk = COL_CONV_G // CONV_CHANNELS
    hpt = tt // CONV_HALO
    halo_idx = lambda i, t: jnp.maximum(t * hpt - 1, 0)
    vec = pl.BlockSpec((1, CONV_CHANNELS), lambda i, t: (0, 0))
    conv_rows = ATTN_WIDTH // CONV_CHANNELS
    return pl.pallas_call(
        functools.partial(_mix_out_body, tt=tt),
        grid=(bsz, t_len // tt),
        in_specs=[
            pl.BlockSpec((1, tt, CONV_CHANNELS), lambda i, t: (i, t, a_blk)),
            pl.BlockSpec((1, tt, CONV_CHANNELS), lambda i, t: (i, t, g_blk)),
            pl.BlockSpec((1, CONV_HALO, CONV_CHANNELS), lambda i, t: (i, halo_idx(i, t), a_blk)),
            pl.BlockSpec((1, CONV_HALO, CONV_CHANNELS), lambda i, t: (i, halo_idx(i, t), g_blk)),
            pl.BlockSpec((CONV_TAPS, CONV_CHANNELS), lambda i, t: (0, 0)),
            vec, vec, vec,
            pl.BlockSpec((1, tt, ATTN_WIDTH), lambda i, t: (i, t, 0)),
            pl.BlockSpec((1, tt, D_MODEL), lambda i, t: (i, t, 0)),
            pl.BlockSpec((None, ATTN_WIDTH, D_MODEL), lambda i, t: (layer, 0, 0)),
            pl.BlockSpec((None, CONV_CHANNELS, D_MODEL), lambda i, t: (layer, conv_rows, 0)),
        ],
        out_specs=pl.BlockSpec((1, tt, D_MODEL), lambda i, t: (i, t, 0)),
        out_shape=jax.ShapeDtypeStruct((bsz, t_len, D_MODEL), F32),
        scratch_shapes=[pltpu.VMEM((CONV_HALO + tt, CONV_CHANNELS), F32),
                        pltpu.VMEM((tt, CONV_CHANNELS), F32)],
        compiler_params=_params("arbitrary", "arbitrary"),
        name="mix_out",
    )(proj3, proj3, proj3, proj3, w, b, lg, lb, attn, x3, w_out, w_out)


def _outproj_body(x_ref, a_ref, c_ref, wa_ref, wc_ref, o_ref):
    o_ref[...] = (x_ref[...]
                  + jnp.dot(a_ref[...], wa_ref[...], preferred_element_type=F32)
                  + jnp.dot(c_ref[...], wc_ref[...], preferred_element_type=F32))


def _out_projection(x2, attn2, conv2, w_out, *, layer, tm, tn):
    n = x2.shape[0]
    return pl.pallas_call(
        _outproj_body,
        grid=(n // tm, D_MODEL // tn),
        in_specs=[
            pl.BlockSpec((tm, tn), lambda i, j: (i, j)),
            pl.BlockSpec((tm, ATTN_WIDTH), lambda i, j: (i, 0)),
            pl.BlockSpec((tm, CONV_CHANNELS), lambda i, j: (i, 0)),
            pl.BlockSpec((None, ATTN_WIDTH, tn), lambda i, j: (layer, 0, j)),
            pl.BlockSpec((None, CONV_CHANNELS, tn), lambda i, j: (layer, ATTN_WIDTH // CONV_CHANNELS, j)),
        ],
        out_specs=pl.BlockSpec((tm, tn), lambda i, j: (i, j)),
        out_shape=jax.ShapeDtypeStruct((n, D_MODEL), F32),
        compiler_params=_params("arbitrary", "arbitrary"),
        name="out_projection",
    )(x2, attn2, conv2, w_out, w_out)


def _ffn_up_body(x_ref, xh_ref, g_ref, wa_ref, wg_ref, cwa_ref, cwg_ref, cba_ref, cbg_ref, o_ref,
                 h_ref, ua_ref, ug_ref, *, tm, tiles_per_seq, rows):
    i = pl.program_id(0)

    @pl.when(pl.program_id(1) == 0)
    def _():
        h_ref[FFN_HALO:, :] = _rms_norm_rows(x_ref[...], g_ref[...]).astype(BF16)
        hh = _rms_norm_rows(xh_ref[...], g_ref[...])
        h_ref[0:FFN_HALO, :] = jnp.where(i % tiles_per_seq != 0, hh, 0.0).astype(BF16)

    units = [(c, slice(0, o_ref.shape[1])) for c in range(tm // rows)]

    def project(c, cs):
        hc = h_ref[c * rows:(c + 1) * rows + FFN_HALO, :]
        ug_ref[c, :, cs] = jnp.dot(hc, wg_ref[:, cs], preferred_element_type=F32)
        ua_ref[c, :, cs] = jnp.dot(hc, wa_ref[:, cs], preferred_element_type=F32)

    def conv(u_ref, c, cs, cw_ref, cb_ref):
        u = u_ref[c, :, cs]
        y = cw_ref[FFN_CONV_TAPS - 1:FFN_CONV_TAPS, cs] * u[FFN_HALO:]
        for s in range(1, FFN_CONV_TAPS):
            k = FFN_CONV_TAPS - 1 - s
            y = y + cw_ref[k:k + 1, cs] * pltpu.roll(u, s, axis=0)[FFN_HALO:]
        return y + cb_ref[:, cs]

    def finish(c, cs):
        gate = conv(ug_ref, c, cs, cwg_ref, cbg_ref)
        act = gate * _sigmoid(gate)
        a = conv(ua_ref, c, cs, cwa_ref, cba_ref)
        o_ref[c * rows:(c + 1) * rows, cs] = (act * a).astype(o_ref.dtype)

    project(*units[0])
    for u, unit in enumerate(units):
        if u + 1 < len(units):
            project(*units[u + 1])
        finish(*unit)


def _ffn_up(x2, g, w_up, cw, cb, *, layer, tm, tn, t_len):
    n = x2.shape[0]
    nj = D_FF // tn
    hpt = tm // FFN_HALO
    rows = min(FFN_ROWS, tm)
    return pl.pallas_call(
        functools.partial(_ffn_up_body, tm=tm, tiles_per_seq=t_len // tm, rows=rows),
        grid=(n // tm, nj),
        in_specs=[
            pl.BlockSpec((tm, D_MODEL), lambda i, j: (i, 0)),
            pl.BlockSpec((FFN_HALO, D_MODEL), lambda i, j: (jnp.maximum(i * hpt - 1, 0), 0)),
            pl.BlockSpec((1, D_MODEL), lambda i, j: (0, 0)),
            pl.BlockSpec((None, D_MODEL, tn), lambda i, j: (layer, 0, j)),
            pl.BlockSpec((None, D_MODEL, tn), lambda i, j: (layer, 0, j + nj)),
            pl.BlockSpec((FFN_CONV_TAPS, tn), lambda i, j: (0, j)),
            pl.BlockSpec((FFN_CONV_TAPS, tn), lambda i, j: (0, j + nj)),
            pl.BlockSpec((1, tn), lambda i, j: (0, j)),
            pl.BlockSpec((1, tn), lambda i, j: (0, j + nj)),
        ],
        out_specs=pl.BlockSpec((tm, tn), lambda i, j: (i, j)),
        out_shape=jax.ShapeDtypeStruct((n, D_FF), BF16),
        scratch_shapes=[
            pltpu.VMEM((FFN_HALO + tm, D_MODEL), BF16),
            pltpu.VMEM((tm // rows, FFN_HALO + rows, tn), F32),
            pltpu.VMEM((tm // rows, FFN_HALO + rows, tn), F32),
        ],
        compiler_params=_params("arbitrary", "arbitrary"),
        name="ffn_up",
    )(x2, x2, g, w_up, w_up, cw, cw, cb, cb)


def _ffn_down_body(act_ref, w_ref, x_ref, o_ref):
    o_ref[...] = x_ref[...] + jnp.dot(act_ref[...], w_ref[...], preferred_element_type=F32)


def _ffn_down(act, w_down, x2, *, layer, tm, tn):
    n = x2.shape[0]
    return pl.pallas_call(
        _ffn_down_body,
        grid=(n // tm, D_MODEL // tn),
        in_specs=[
            pl.BlockSpec((tm, D_FF), lambda i, j: (i, 0)),
            pl.BlockSpec((None, D_FF, tn), lambda i, j: (layer, 0, j)),
            pl.BlockSpec((tm, tn), lambda i, j: (i, j)),
        ],
        out_specs=pl.BlockSpec((tm, tn), lambda i, j: (i, j)),
        out_shape=jax.ShapeDtypeStruct((n, D_MODEL), F32),
        compiler_params=_params("arbitrary", "arbitrary"),
        name="ffn_down",
    )(act, w_down, x2)


def _final_norm_body(x_ref, g_ref, o_ref):
    o_ref[...] = _rms_norm_rows(x_ref[...], g_ref[...])


def _final_norm(x2, g, *, tm):
    n = x2.shape[0]
    return pl.pallas_call(
        _final_norm_body,
        grid=(n // tm,),
        in_specs=[pl.BlockSpec((tm, D_MODEL), lambda i: (i, 0)), pl.BlockSpec((1, D_MODEL), lambda i: (0, 0))],
        out_specs=pl.BlockSpec((tm, D_MODEL), lambda i: (i, 0)),
        out_shape=jax.ShapeDtypeStruct((n, D_MODEL), F32),
        compiler_params=_params("arbitrary"),
        name="final_norm",
    )(x2, g)


def _in_weight_layout(w_in):
    depth = w_in.shape[0]
    kv0 = ATTN_WIDTH
    gate0 = kv0 + KV_COLS
    conv0 = gate0 + GATE_COLS
    parts = [w_in[..., :ATTN_WIDTH], w_in[..., conv0:conv0 + 2 * CONV_CHANNELS]]
    for br in range(N_BRANCH):
        for h in range(N_KV_HEADS):
            for kv in range(2):
                c0 = kv0 + ((br * 2 + kv) * N_KV_HEADS + h) * HEAD_DIM
                parts.append(w_in[..., c0:c0 + HEAD_DIM])
    w_main = jnp.concatenate(parts, axis=-1).astype(BF16)
    wg = w_in[..., gate0:gate0 + GATE_COLS].reshape(depth, D_MODEL, N_KV_HEADS, GQA_GROUP, N_BRANCH)
    wg = wg.transpose(0, 1, 2, 4, 3).reshape(depth, D_MODEL, N_KV_HEADS, N_BRANCH * GQA_GROUP)
    wg = jnp.pad(wg, ((0, 0), (0, 0), (0, 0), (0, LANES - N_BRANCH * GQA_GROUP)))
    return w_main, wg.reshape(depth, D_MODEL, N_KV_HEADS * LANES).astype(BF16)


def _pick(n, prefs):
    for p in prefs:
        if n % p == 0:
            return p
    return n


def kernel(x, rel_bias, mix_norm_g, w_in, cmp_pos, cmp_w1, cmp_w2, conv_w, conv_b, conv_ln_g, conv_ln_b,
           w_out, ffn_norm_g, w_up, ffn_conv_w, ffn_conv_b, w_down, final_norm_g):
    bsz, t_len, _ = x.shape
    depth = w_in.shape[0]
    n = bsz * t_len
    assert t_len % 256 == 0 and t_len // SLC_LEN <= HEAD_DIM
    tq = 256
    tm = _pick(t_len, (1024, 512, 256))
    nc = t_len // CMP_STRIDE

    band, cmpb = _bias_tiles(rel_bias, t_len, tq)
    col_scale = jnp.concatenate([jnp.full((1, ATTN_WIDTH), Q_SCALE, F32),
                                 jnp.ones((1, PROJ_COLS - ATTN_WIDTH), F32)], axis=1)
    x2 = x.reshape(n, D_MODEL)
    w_main, wg = _in_weight_layout(w_in)
    w_out_b, w_up_b, w_down_b = w_out.astype(BF16), w_up.astype(BF16), w_down.astype(BF16)
    for l in range(depth):
        proj, gates = _in_projection(x2, mix_norm_g[l][None, :], w_main, wg, col_scale, layer=l, tm=tm, tn=768)
        proj3 = proj.reshape(bsz, t_len, PROJ_COLS)
        gates3 = gates.reshape(bsz, t_len, N_KV_HEADS * LANES)

        posx, w1big, w2big = _compress_weight_layout(cmp_pos[l], cmp_w1[l], cmp_w2[l])
        tokl = proj3[:, :, COL_KV:COL_KV + CMP_WIDTH].reshape(bsz, nc, CMP_STRIDE, CMP_WIDTH).transpose(0, 2, 1, 3)
        kvc, kvct = _compress(tokl, posx, w1big, w2big)

        attn = _attention(proj3, kvc, kvct, band, cmpb, gates3, tq=tq)
        x2 = _mix_out(proj3, conv_w[l], conv_b[l][None, :], conv_ln_g[l][None, :], conv_ln_b[l][None, :],
                      attn, x2.reshape(bsz, t_len, D_MODEL), w_out_b, layer=l, tt=256).reshape(n, D_MODEL)

        act = _ffn_up(x2, ffn_norm_g[l][None, :], w_up_b, ffn_conv_w[l],
                      ffn_conv_b[l][None, :], layer=l, tm=tm, tn=512, t_len=t_len)
        x2 = _ffn_down(act, w_down_b, x2, layer=l, tm=tm, tn=512)
    x2 = _final_norm(x2, final_norm_g[None, :], tm=_pick(n, (512, 256)))
    return x2.reshape(bsz, t_len, D_MODEL)
```

```python
import functools
import math

import numpy as np
import jax
import jax.numpy as jnp
from jax import lax
from jax.experimental import pallas as pl
from jax.experimental.pallas import tpu as pltpu

F32 = jnp.float32
BF16 = jnp.bfloat16

D_MODEL = 2048
HEAD_DIM = 64
N_KV_HEADS = 4
GQA_GROUP = 4
N_Q_HEADS = N_KV_HEADS * GQA_GROUP
N_BRANCH = 3
ATTN_WIDTH = N_Q_HEADS * HEAD_DIM
CMP_LEN = 32
CMP_STRIDE = 16
SLC_LEN = 64
SLC_TOPK = 16
WINDOW = 512
CONV_CHANNELS = D_MODEL - ATTN_WIDTH
CONV_TAPS = 31
D_FF = 5632
FFN_CONV_TAPS = 3
N_BUCKETS = 32
MAX_DISTANCE = 128
NORM_EPS = 1e-6

KV_PAIR = 2 * HEAD_DIM
KV_COLS = N_BRANCH * N_KV_HEADS * KV_PAIR
GATE_COLS = N_BRANCH * N_Q_HEADS
COL_Q = 0
COL_CONV_A = ATTN_WIDTH
COL_CONV_G = COL_CONV_A + CONV_CHANNELS
COL_KV = COL_CONV_G + CONV_CHANNELS
PROJ_COLS = COL_KV + KV_COLS

MASK_VALUE = -1e30
BLOCK_PENALTY = -1e9
KEY_BIG = 1e30
KEY_TAKEN = -3e38
LOG2E = math.log2(math.e)
Q_SCALE = HEAD_DIM ** -0.5 * LOG2E

LANES = 128
SUBLANES = 8
CONV_ROWS = 128
FFN_ROWS = 1024
VMEM_LIMIT = 56 * 1024 * 1024
CONV_HALO = 32
FFN_HALO = 16


def _t5_bucket_last_distance():
    n = np.arange(0, 4 * MAX_DISTANCE, dtype=np.int64)
    max_exact = N_BUCKETS // 2
    nf = np.maximum(n, 1).astype(np.float64)
    large = max_exact + np.floor(np.log(nf / max_exact) / math.log(MAX_DISTANCE / max_exact)
                                 * (N_BUCKETS - max_exact)).astype(np.int64)
    large = np.minimum(large, N_BUCKETS - 1)
    bucket = np.where(n < max_exact, n, large)
    last = []
    for b in range(N_BUCKETS - 1):
        idx = np.nonzero(bucket == b)[0]
        last.append(int(idx.max()) if idx.size else None)
    return last


_BUCKET_LAST = _t5_bucket_last_distance()


def _params(*sem):
    return pltpu.CompilerParams(dimension_semantics=sem, vmem_limit_bytes=VMEM_LIMIT)


def _sigmoid(x):
    return jax.nn.sigmoid(x)


def _rms_norm_rows(x, g):
    ms = jnp.mean(x * x, axis=-1, keepdims=True)
    return x * lax.rsqrt(ms + NORM_EPS) * g


def _bias_from_distance(dist, tab_ref, head):
    c_far = tab_ref[N_BUCKETS - 1, head]
    val = jnp.zeros(dist.shape, F32)
    for b in range(N_BUCKETS - 2, -1, -1):
        if _BUCKET_LAST[b] is None:
            continue
        val = jnp.where(dist <= _BUCKET_LAST[b], tab_ref[b, head] - c_far, val)
    return val


def _band_bias_body(tab_ref, o_ref, *, tq):
    head = pl.program_id(0) * GQA_GROUP + pl.program_id(1)
    shape = (WINDOW + tq, tq)
    dist = (lax.broadcasted_iota(jnp.int32, shape, 1) + WINDOW
            - lax.broadcasted_iota(jnp.int32, shape, 0))
    val = _bias_from_distance(dist, tab_ref, head) * LOG2E
    val = jnp.where(dist >= 0, jnp.where(dist < WINDOW, val, MASK_VALUE), MASK_VALUE)
    o_ref[0] = val


def _cmp_bias_body(tab_ref, o_ref, *, tq, nc):
    head = pl.program_id(0) * GQA_GROUP + pl.program_id(1)
    shape = (2 * nc, tq)
    r = lax.broadcasted_iota(jnp.int32, shape, 0)
    i = lax.broadcasted_iota(jnp.int32, shape, 1)
    dist = i - (r - nc) * CMP_STRIDE - (CMP_LEN - 1)
    val = _bias_from_distance(dist, tab_ref, head) * LOG2E
    o_ref[0] = jnp.where(dist >= 0, val, MASK_VALUE)


def _bias_tiles(rel_bias, t_len, tq):
    nc = t_len // CMP_STRIDE
    rows = GQA_GROUP * tq
    smem = pl.BlockSpec(memory_space=pltpu.SMEM)
    band = pl.pallas_call(
        functools.partial(_band_bias_body, tq=tq),
        grid=(N_KV_HEADS, GQA_GROUP),
        in_specs=[smem],
        out_specs=pl.BlockSpec((1, WINDOW + tq, tq), lambda h, g: (h, 0, g)),
        out_shape=jax.ShapeDtypeStruct((N_KV_HEADS, WINDOW + tq, rows), F32),
        compiler_params=_params("arbitrary", "arbitrary"),
        name="band_bias",
    )(rel_bias)
    cmpb = pl.pallas_call(
        functools.partial(_cmp_bias_body, tq=tq, nc=nc),
        grid=(N_KV_HEADS, GQA_GROUP),
        in_specs=[smem],
        out_specs=pl.BlockSpec((1, 2 * nc, tq), lambda h, g: (h, 0, g)),
        out_shape=jax.ShapeDtypeStruct((N_KV_HEADS, 2 * nc, rows), F32),
        compiler_params=_params("arbitrary", "arbitrary"),
        name="cmp_bias",
    )(rel_bias)
    return band, cmpb


def _inproj_body(x_ref, g_ref, w_ref, wg_ref, cs_ref, o_ref, gate_ref, h_ref):
    @pl.when(pl.program_id(1) == 0)
    def _():
        h = _rms_norm_rows(x_ref[...], g_ref[...]).astype(BF16)
        h_ref[...] = h
        gate_ref[...] = _sigmoid(jnp.dot(h, wg_ref[...], preferred_element_type=F32))

    acc = lax.dot_general(h_ref[...], w_ref[...], (((1,), (1,)), ((), ())),
                          preferred_element_type=F32)
    o_ref[...] = (acc * cs_ref[...]).astype(o_ref.dtype)


def _in_projection(x2, g, w, wg, col_scale, *, layer, tm, tn):
    n = x2.shape[0]
    gcols = wg.shape[-1]
    return pl.pallas_call(
        _inproj_body,
        grid=(n // tm, PROJ_COLS // tn),
        in_specs=[
            pl.BlockSpec((tm, D_MODEL), lambda i, j: (i, 0)),
            pl.BlockSpec((1, D_MODEL), lambda i, j: (0, 0)),
            pl.BlockSpec((None, tn, D_MODEL), lambda i, j: (layer, j, 0)),
            pl.BlockSpec((None, D_MODEL, gcols), lambda i, j: (layer, 0, 0)),
            pl.BlockSpec((1, tn), lambda i, j: (0, j)),
        ],
        out_specs=[
            pl.BlockSpec((tm, tn), lambda i, j: (i, j)),
            pl.BlockSpec((tm, gcols), lambda i, j: (i, 0)),
        ],
        out_shape=[
            jax.ShapeDtypeStruct((n, PROJ_COLS), BF16),
            jax.ShapeDtypeStruct((n, gcols), F32),
        ],
        scratch_shapes=[pltpu.VMEM((tm, D_MODEL), BF16)],
        compiler_params=_params("arbitrary", "arbitrary"),
        name="in_projection",
    )(x2, g, w, wg, col_scale)


CMP_WIDTH = N_KV_HEADS * KV_PAIR


def _compress_body(x_ref, pos_ref, w1_ref, w2_ref, o_ref, ot_ref, top_ref, bot_ref, *, nc):
    l = pl.program_id(1)

    @pl.when(l == 0)
    def _():
        top_ref[...] = jnp.zeros_like(top_ref)
        bot_ref[...] = jnp.zeros_like(bot_ref)

    x = x_ref[0, 0].astype(F32)
    top_ref[...] += jnp.dot((x + pos_ref[0, 0]).astype(BF16), w1_ref[0, 0], preferred_element_type=F32)
    bot_ref[...] += jnp.dot((x + pos_ref[0, 1]).astype(BF16), w1_ref[0, 1], preferred_element_type=F32)

    @pl.when(l == pl.num_programs(1) - 1)
    def _():
        pre = top_ref[...] + pltpu.roll(bot_ref[...], nc - 1, axis=0)
        act = pre * _sigmoid(pre)
        out = jnp.dot(act.astype(BF16), w2_ref[...], preferred_element_type=F32)
        for h in range(N_KV_HEADS):
            kv = out[:, h * KV_PAIR:(h + 1) * KV_PAIR]
            o_ref[0, h] = kv.astype(o_ref.dtype)
            ot_ref[0, h] = kv.T.astype(ot_ref.dtype)


def _compress(tokl, posx, w1big, w2big):
    b, _, nc, _ = tokl.shape
    return pl.pallas_call(
        functools.partial(_compress_body, nc=nc),
        grid=(b, CMP_STRIDE),
        in_specs=[
            pl.BlockSpec((1, 1, nc, CMP_WIDTH), lambda i, l: (i, l, 0, 0)),
            pl.BlockSpec((1, 2, 1, CMP_WIDTH), lambda i, l: (l, 0, 0, 0)),
            pl.BlockSpec((1, 2, CMP_WIDTH, CMP_WIDTH), lambda i, l: (l, 0, 0, 0)),
            pl.BlockSpec((CMP_WIDTH, CMP_WIDTH), lambda i, l: (0, 0)),
        ],
        out_specs=[
            pl.BlockSpec((1, N_KV_HEADS, nc, KV_PAIR), lambda i, l: (i, 0, 0, 0)),
            pl.BlockSpec((1, N_KV_HEADS, KV_PAIR, nc), lambda i, l: (i, 0, 0, 0)),
        ],
        out_shape=[
            jax.ShapeDtypeStruct((b, N_KV_HEADS, nc, KV_PAIR), BF16),
            jax.ShapeDtypeStruct((b, N_KV_HEADS, KV_PAIR, nc), BF16),
        ],
        scratch_shapes=[pltpu.VMEM((nc, CMP_WIDTH), F32), pltpu.VMEM((nc, CMP_WIDTH), F32)],
        compiler_params=_params("arbitrary", "arbitrary"),
        name="compress",
    )(tokl, posx, w1big, w2big)


def _compress_weight_layout(pos, w1, w2):
    def block_diag(blocks):
        rows = []
        for h in range(N_KV_HEADS):
            for kv in range(2):
                off = (h * 2 + kv) * HEAD_DIM
                pad = [(0, 0)] * (blocks.ndim - 2) + [(off, CMP_WIDTH - HEAD_DIM - off)]
                rows.append(jnp.pad(blocks[kv], pad))
        return jnp.concatenate(rows, axis=-2).astype(BF16)

    w1r = w1.reshape(2, 2, CMP_STRIDE, HEAD_DIM, HEAD_DIM)
    w1big = block_diag(w1r.transpose(0, 2, 1, 3, 4))
    w2big = block_diag(w2)
    posr = pos.reshape(2, 2, CMP_STRIDE, HEAD_DIM).transpose(2, 1, 0, 3)
    posx = jnp.broadcast_to(posr[:, :, None], (CMP_STRIDE, 2, N_KV_HEADS, 2, HEAD_DIM))
    return posx.reshape(CMP_STRIDE, 2, 1, CMP_WIDTH), w1big, w2big


VT_ROWS = HEAD_DIM + 16
FAR_UNROLL = 4


def _attn_body(q_ref, kvc_ref, kvct_ref, kvs_ref, kvw_ref, bt_ref, bc_ref, gate_ref, o_ref,
               kp_ref, vst_ref, vwt_ref, key_ref, m_ref, acc_ref, sa_ref, sb_ref, ocmp_ref, qsel_ref, qwin_ref,
               *, tq, t_len):
    qi = pl.program_id(2)
    nc = t_len // CMP_STRIDE
    ns = t_len // SLC_LEN
    rows = GQA_GROUP * tq
    blocks_per_tile = tq // SLC_LEN

    @pl.when(qi == 0)
    def _():
        k = kvs_ref[0][:, :HEAD_DIM]
        blk = lax.broadcasted_iota(jnp.int32, (t_len, HEAD_DIM), 0) // SLC_LEN
        col = lax.broadcasted_iota(jnp.int32, (t_len, HEAD_DIM), 1)
        onehot = jnp.where(blk == col, 1.0, 0.0).astype(BF16)
        kp_ref[...] = jnp.concatenate([k, onehot], axis=1)
        ones_rows = jnp.where(lax.broadcasted_iota(jnp.int32, (VT_ROWS - HEAD_DIM, tq), 0) == 0, 1.0, 0.0)
        for c in range(t_len // tq):
            st = kvs_ref[0, c * tq:(c + 1) * tq, :].astype(F32).T[HEAD_DIM:]
            vst_ref[c] = jnp.concatenate([st, ones_rows], axis=0).astype(BF16)
            wt = kvw_ref[0, c * tq:(c + 1) * tq, :].astype(F32).T[HEAD_DIM:]
            vwt_ref[c] = jnp.concatenate([wt, ones_rows], axis=0).astype(BF16)

    q_t = q_ref[0].astype(F32).T
    q4_t = jnp.concatenate([q_t[g * HEAD_DIM:(g + 1) * HEAD_DIM] for g in range(GQA_GROUP)],
                           axis=1).astype(BF16)
    qwin_ref[...] = jnp.concatenate([q4_t, jnp.zeros((HEAD_DIM, rows), BF16)], axis=0)
    qsel_ref[0:HEAD_DIM, :] = q4_t
    kw_ref = kvw_ref.at[0]
    win, sel = 0, 1

    def cmp_scores():
        bias_row = pl.multiple_of(nc - qi * (tq // CMP_STRIDE), tq // CMP_STRIDE)
        return (jnp.dot(kvc_ref[0, 0], qwin_ref[...], preferred_element_type=F32)
                + bc_ref[0, pl.ds(bias_row, nc), :])

    def cmp_finish(s):
        m = jnp.max(s, axis=0, keepdims=True)
        p = jnp.exp2(s - m)
        l = jnp.sum(p, axis=0, keepdims=True)
        pn = p * jnp.where(m > 0.5 * MASK_VALUE, 1.0 / l, 0.0)
        ocmp_ref[...] = jnp.dot(kvct_ref[0, 0, HEAD_DIM:, :], pn.astype(BF16),
                                preferred_element_type=F32)

        ps = pn[:, 0:tq] + pn[:, tq:2 * tq] + pn[:, 2 * tq:3 * tq] + pn[:, 3 * tq:4 * tq]
        sj = lax.broadcasted_iota(jnp.int32, (HEAD_DIM, nc), 0) * SLC_LEN
        ci = lax.broadcasted_iota(jnp.int32, (HEAD_DIM, nc), 1) * CMP_STRIDE
        overlap = jnp.where(ci < sj + SLC_LEN, jnp.where(ci + CMP_LEN > sj, 1.0, 0.0), 0.0).astype(BF16)
        p_hi = ps.astype(BF16)
        r_hi = ps - p_hi.astype(F32)
        p_md = r_hi.astype(BF16)
        p_lo = (r_hi - p_md.astype(F32)).astype(BF16)
        imp = (jnp.dot(overlap, p_hi, preferred_element_type=F32)
               + jnp.dot(overlap, p_md, preferred_element_type=F32)
               + jnp.dot(overlap, p_lo, preferred_element_type=F32))
        t = qi * tq + lax.broadcasted_iota(jnp.int32, (HEAD_DIM, tq), 1)
        blk = lax.broadcasted_iota(jnp.int32, (HEAD_DIM, tq), 0)
        cur = t // SLC_LEN
        key = jnp.where(blk == 0, KEY_BIG, jnp.where(blk == cur, KEY_BIG, jnp.where(blk == cur - 1, KEY_BIG, imp)))
        key_ref[...] = jnp.where(blk * SLC_LEN <= t, key, -KEY_BIG)

    def select_blocks():
        key = key_ref[...]
        blk = lax.broadcasted_iota(jnp.int32, (HEAD_DIM, tq), 0)
        pen = jnp.full((HEAD_DIM, tq), BLOCK_PENALTY, F32)
        for _ in range(min(SLC_TOPK, ns)):
            top = jnp.max(key, axis=0, keepdims=True)
            first = jnp.min(jnp.where(key == top, blk, HEAD_DIM), axis=0, keepdims=True)
            hit = blk == first
            pen = jnp.where(hit, 0.0, pen)
            key = jnp.where(hit, KEY_TAKEN, key)
        qsel_ref[HEAD_DIM:, :] = jnp.concatenate([pen.astype(BF16)] * GQA_GROUP, axis=1)

    def scores(qx_ref, k_ref, kj, bias_off):
        start = pl.multiple_of(kj * tq, tq)
        sc = jnp.dot(k_ref[pl.ds(start, tq), :], qx_ref[...], preferred_element_type=F32)
        if bias_off is not None:
            sc = sc + bt_ref[0, bias_off:bias_off + tq, :]
        return sc

    def consume(sc, vt_ref, kj, br, st, first):
        m_tile = jnp.max(sc, axis=0, keepdims=True)
        if first:
            m_new = m_tile
            pr = jnp.exp2(sc - m_new).astype(BF16)
            acc_ref[br, st] = jnp.dot(vt_ref[kj], pr, preferred_element_type=F32)
        else:
            m_old = m_ref[br, st]
            m_new = jnp.maximum(m_old, m_tile)
            alpha = jnp.exp2(m_old - m_new)
            pr = jnp.exp2(sc - m_new).astype(BF16)
            acc_ref[br, st] = alpha * acc_ref[br, st] + jnp.dot(vt_ref[kj], pr, preferred_element_type=F32)
        m_ref[br, st] = m_new

    def finish(br):
        m0 = m_ref[br, 0]
        m1 = m_ref[br, 1]
        m_all = jnp.maximum(m0, m1)
        acc = jnp.exp2(m0 - m_all) * acc_ref[br, 0] + jnp.exp2(m1 - m_all) * acc_ref[br, 1]
        return acc[0:HEAD_DIM] * (1.0 / acc[HEAD_DIM:HEAD_DIM + 1, :])

    def combine():
        o_win = finish(win)
        o_sel = finish(sel)
        o_cmp = ocmp_ref[...]
        gates_t = gate_ref[0].T
        combs = []
        for g in range(GQA_GROUP):
            sl = slice(g * tq, (g + 1) * tq)
            combs.append(gates_t[g:g + 1, :] * o_cmp[:, sl]
                         + gates_t[GQA_GROUP + g:GQA_GROUP + g + 1, :] * o_sel[:, sl]
                         + gates_t[2 * GQA_GROUP + g:2 * GQA_GROUP + g + 1, :] * o_win[:, sl])
        outs = [jnp.concatenate(combs[p:p + 2], axis=0).T for p in range(0, GQA_GROUP, 2)]
        o_ref[0] = jnp.concatenate(outs, axis=1).astype(o_ref.dtype)

    @pl.when(qi >= 2)
    def _():
        sc_cmp = cmp_scores()
        sw0 = scores(qwin_ref, kw_ref, qi, WINDOW)
        cmp_finish(sc_cmp)
        sw1 = scores(qwin_ref, kw_ref, qi - 1, WINDOW - tq)
        consume(sw0, vwt_ref, qi, win, 0, True)
        sw2 = scores(qwin_ref, kw_ref, qi - 2, WINDOW - 2 * tq)
        consume(sw1, vwt_ref, qi - 1, win, 1, True)
        select_blocks()
        consume(sw2, vwt_ref, qi - 2, win, 0, False)

        ss0 = scores(qsel_ref, kp_ref, qi, WINDOW)
        ss1 = scores(qsel_ref, kp_ref, qi - 1, WINDOW - tq)
        consume(ss0, vst_ref, qi, sel, 0, True)
        sa_ref[...] = scores(qsel_ref, kp_ref, 0, None)
        consume(ss1, vst_ref, qi - 1, sel, 1, True)
        n_far = qi - 1

        def far_pair(pi):
            sb_ref[...] = scores(qsel_ref, kp_ref, 2 * pi + 1, None)
            consume(sa_ref[...], vst_ref, 2 * pi, sel, 1, False)
            sa_ref[...] = scores(qsel_ref, kp_ref, jnp.minimum(2 * pi + 2, n_far - 1), None)
            consume(sb_ref[...], vst_ref, 2 * pi + 1, sel, 0, False)

        n_pairs = n_far // 2

        def far_group(gi, carry):
            for u in range(FAR_UNROLL):
                far_pair(FAR_UNROLL * gi + u)
            return carry

        lax.fori_loop(0, n_pairs // FAR_UNROLL, far_group, 0)
        done = (n_pairs // FAR_UNROLL) * FAR_UNROLL
        left = n_pairs - done
        size = FAR_UNROLL // 2
        while size >= 1:
            @pl.when(left % (2 * size) >= size)
            def _(size=size, base=done):
                for u in range(size):
                    far_pair(base + u)
            done = done + jnp.where(left % (2 * size) >= size, size, 0)
            size //= 2

        @pl.when(n_far % 2 == 1)
        def _():
            consume(sa_ref[...], vst_ref, n_far - 1, sel, 1, False)

        combine()

    @pl.when(qi < 2)
    def _():
        cmp_finish(cmp_scores())
        select_blocks()
        sw0 = scores(qwin_ref, kw_ref, qi, WINDOW)
        ss0 = scores(qsel_ref, kp_ref, qi, WINDOW)
        consume(sw0, vwt_ref, qi, win, 0, True)
        consume(ss0, vst_ref, qi, sel, 0, True)
        for br in (win, sel):
            m_ref[br, 1] = jnp.full((1, rows), MASK_VALUE, F32)
            acc_ref[br, 1] = jnp.zeros((VT_ROWS, rows), F32)

        @pl.when(qi == 1)
        def _():
            sw1 = scores(qwin_ref, kw_ref, 0, WINDOW - tq)
            ss1 = scores(qsel_ref, kp_ref, 0, WINDOW - tq)
            consume(sw1, vwt_ref, 0, win, 1, False)
            consume(ss1, vst_ref, 0, sel, 1, False)

        combine()


def _attention(proj3, kvc, kvct, band, cmpb, gates3, *, tq):
    b, t_len, _ = proj3.shape
    assert WINDOW == 2 * tq
    nc = t_len // CMP_STRIDE
    rows = GQA_GROUP * tq
    qw = GQA_GROUP * HEAD_DIM
    slc_blk = COL_KV // KV_PAIR + N_KV_HEADS
    win_blk = COL_KV // KV_PAIR + 2 * N_KV_HEADS
    return pl.pallas_call(
        functools.partial(_attn_body, tq=tq, t_len=t_len),
        grid=(b, N_KV_HEADS, t_len // tq),
        in_specs=[
            pl.BlockSpec((1, tq, qw), lambda i, h, q: (i, q, h)),
            pl.BlockSpec((1, 1, nc, KV_PAIR), lambda i, h, q: (i, h, 0, 0)),
            pl.BlockSpec((1, 1, KV_PAIR, nc), lambda i, h, q: (i, h, 0, 0)),
            pl.BlockSpec((1, t_len, KV_PAIR), lambda i, h, q: (i, 0, slc_blk + h)),
            pl.BlockSpec((1, t_len, KV_PAIR), lambda i, h, q: (i, 0, win_blk + h)),
            pl.BlockSpec((1, WINDOW + tq, rows), lambda i, h, q: (h, 0, 0)),
            pl.BlockSpec((1, 2 * nc, rows), lambda i, h, q: (h, 0, 0)),
            pl.BlockSpec((1, tq, LANES), lambda i, h, q: (i, q, h)),
        ],
        out_specs=pl.BlockSpec((1, tq, qw), lambda i, h, q: (i, q, h)),
        out_shape=jax.ShapeDtypeStruct((b, t_len, ATTN_WIDTH), BF16),
        scratch_shapes=[
            pltpu.VMEM((t_len, KV_PAIR), BF16),
            pltpu.VMEM((t_len // tq, VT_ROWS, tq), BF16),
            pltpu.VMEM((t_len // tq, VT_ROWS, tq), BF16),
            pltpu.VMEM((HEAD_DIM, tq), F32),
            pltpu.VMEM((2, 2, 1, rows), F32),
            pltpu.VMEM((2, 2, VT_ROWS, rows), F32),
            pltpu.VMEM((tq, rows), F32),
            pltpu.VMEM((tq, rows), F32),
            pltpu.VMEM((HEAD_DIM, rows), F32),
            pltpu.VMEM((KV_PAIR, rows), BF16),
            pltpu.VMEM((KV_PAIR, rows), BF16),
        ],
        compiler_params=_params("arbitrary", "arbitrary", "arbitrary"),
        name="nsa_attention",
    )(proj3, kvc, kvct, proj3, proj3, band, cmpb, gates3)


def _mix_out_body(a_ref, g_ref, ah_ref, gh_ref, w_ref, b_ref, lg_ref, lb_ref, attn_ref, x_ref, wa_ref, wc_ref,
                  o_ref, u_ref, y_ref, *, tt):
    ti = pl.program_id(1)
    acc_attn = jnp.dot(attn_ref[0], wa_ref[...], preferred_element_type=F32)
    u_ref[CONV_HALO:, :] = a_ref[0].astype(F32) * _sigmoid(g_ref[0].astype(F32))
    halo = ah_ref[0].astype(F32) * _sigmoid(gh_ref[0].astype(F32))
    u_ref[0:CONV_HALO, :] = jnp.where(ti > 0, halo, 0.0)
    base = CONV_HALO - (CONV_TAPS - 1)
    ext = CONV_ROWS + CONV_HALO
    for rb in range(tt // CONV_ROWS):
        rs = slice(rb * CONV_ROWS, (rb + 1) * CONV_ROWS)
        for cb in range(CONV_CHANNELS // LANES):
            cs = slice(cb * LANES, (cb + 1) * LANES)
            wblk = w_ref[:, cs]
            ublk = u_ref[rb * CONV_ROWS:rb * CONV_ROWS + ext, cs]
            acc = jnp.zeros((CONV_ROWS, LANES), F32)
            for r in range(SUBLANES):
                ur = ublk if r == 0 else pltpu.roll(ublk, ext - r, axis=0)
                for a in range(CONV_HALO // SUBLANES + 1):
                    k = SUBLANES * a + r - base
                    if 0 <= k < CONV_TAPS:
                        acc = acc + wblk[k:k + 1, :] * ur[SUBLANES * a:SUBLANES * a + CONV_ROWS]
            y_ref[rs, cs] = acc
        acc = y_ref[rs, :] + b_ref[...]
        mu = jnp.mean(acc, axis=-1, keepdims=True)
        xc = acc - mu
        var = jnp.mean(xc * xc, axis=-1, keepdims=True)
        y = xc * lax.rsqrt(var + NORM_EPS) * lg_ref[...] + lb_ref[...]
        conv_act = (y * _sigmoid(y)).astype(BF16)
        o_ref[0, rs, :] = (x_ref[0, rs, :] + acc_attn[rs]
                           + jnp.dot(conv_act, wc_ref[...], preferred_element_type=F32))


def _mix_out(proj3, w, b, lg, lb, attn, x3, w_out, *, layer, tt):
    bsz, t_len, _ = proj3.shape
    a_blk = COL_CONV_A // CONV_CHANNELS
    g_blk = COL_CONV_G // CONV_CHANNELS
    hpt = tt // CONV_HALO
    halo_idx = lambda i, t: jnp.maximum(t * hpt - 1, 0)
    vec = pl.BlockSpec((1, CONV_CHANNELS), lambda i, t: (0, 0))
    conv_rows = ATTN_WIDTH // CONV_CHANNELS
    return pl.pallas_call(
        functools.partial(_mix_out_body, tt=tt),
        grid=(bsz, t_len // tt),
        in_specs=[
            pl.BlockSpec((1, tt, CONV_CHANNELS), lambda i, t: (i, t, a_blk)),
            pl.BlockSpec((1, tt, CONV_CHANNELS), lambda i, t: (i, t, g_blk)),
            pl.BlockSpec((1, CONV_HALO, CONV_CHANNELS), lambda i, t: (i, halo_idx(i, t), a_blk)),
            pl.BlockSpec((1, CONV_HALO, CONV_CHANNELS), lambda i, t: (i, halo_idx(i, t), g_blk)),
            pl.BlockSpec((CONV_TAPS, CONV_CHANNELS), lambda i, t: (0, 0)),
            vec, vec, vec,
            pl.BlockSpec((1, tt, ATTN_WIDTH), lambda i, t: (i, t, 0)),
            pl.BlockSpec((1, tt, D_MODEL), lambda i, t: (i, t, 0)),
            pl.BlockSpec((None, ATTN_WIDTH, D_MODEL), lambda i, t: (layer, 0, 0)),
            pl.BlockSpec((None, CONV_CHANNELS, D_MODEL), lambda i, t: (layer, conv_rows, 0)),
        ],
        out_specs=pl.BlockSpec((1, tt, D_MODEL), lambda i, t: (i, t, 0)),
        out_shape=jax.ShapeDtypeStruct((bsz, t_len, D_MODEL), F32),
        scratch_shapes=[pltpu.VMEM((CONV_HALO + tt, CONV_CHANNELS), F32),
                        pltpu.VMEM((tt, CONV_CHANNELS), F32)],
        compiler_params=_params("arbitrary", "arbitrary"),
        name="mix_out",
    )(proj3, proj3, proj3, proj3, w, b, lg, lb, attn, x3, w_out, w_out)


def _ffn_up_body(x_ref, xh_ref, g_ref, wa_ref, wg_ref, cwa_ref, cwg_ref, cba_ref, cbg_ref, o_ref,
                 h_ref, ua_ref, ug_ref, *, tm, tiles_per_seq, rows):
    i = pl.program_id(0)

    @pl.when(pl.program_id(1) == 0)
    def _():
        h_ref[FFN_HALO:, :] = _rms_norm_rows(x_ref[...], g_ref[...]).astype(BF16)
        hh = _rms_norm_rows(xh_ref[...], g_ref[...])
        h_ref[0:FFN_HALO, :] = jnp.where(i % tiles_per_seq != 0, hh, 0.0).astype(BF16)

    units = [(c, slice(0, o_ref.shape[1])) for c in range(tm // rows)]

    def project(c, cs):
        hc = h_ref[c * rows:(c + 1) * rows + FFN_HALO, :]
        ug_ref[c, :, cs] = jnp.dot(hc, wg_ref[:, cs], preferred_element_type=F32)
        ua_ref[c, :, cs] = jnp.dot(hc, wa_ref[:, cs], preferred_element_type=F32)

    def conv(u_ref, c, cs, cw_ref, cb_ref):
        u = u_ref[c, :, cs]
        y = cw_ref[FFN_CONV_TAPS - 1:FFN_CONV_TAPS, cs] * u[FFN_HALO:]
        for s in range(1, FFN_CONV_TAPS):
            k = FFN_CONV_TAPS - 1 - s
            y = y + cw_ref[k:k + 1, cs] * pltpu.roll(u, s, axis=0)[FFN_HALO:]
        return y + cb_ref[:, cs]

    def finish(c, cs):
        gate = conv(ug_ref, c, cs, cwg_ref, cbg_ref)
        act = gate * _sigmoid(gate)
        a = conv(ua_ref, c, cs, cwa_ref, cba_ref)
        o_ref[c * rows:(c + 1) * rows, cs] = (act * a).astype(o_ref.dtype)

    project(*units[0])
    for u, unit in enumerate(units):
        if u + 1 < len(units):
            project(*units[u + 1])
        finish(*unit)


def _ffn_up(x2, g, w_up, cw, cb, *, layer, tm, tn, t_len):
    n = x2.shape[0]
    nj = D_FF // tn
    hpt = tm // FFN_HALO
    rows = min(FFN_ROWS, tm)
    return pl.pallas_call(
        functools.partial(_ffn_up_body, tm=tm, tiles_per_seq=t_len // tm, rows=rows),
        grid=(n // tm, nj),
        in_specs=[
            pl.BlockSpec((tm, D_MODEL), lambda i, j: (i, 0)),
            pl.BlockSpec((FFN_HALO, D_MODEL), lambda i, j: (jnp.maximum(i * hpt - 1, 0), 0)),
            pl.BlockSpec((1, D_MODEL), lambda i, j: (0, 0)),
            pl.BlockSpec((None, D_MODEL, tn), lambda i, j: (layer, 0, j)),
            pl.BlockSpec((None, D_MODEL, tn), lambda i, j: (layer, 0, j + nj)),
            pl.BlockSpec((FFN_CONV_TAPS, tn), lambda i, j: (0, j)),
            pl.BlockSpec((FFN_CONV_TAPS, tn), lambda i, j: (0, j + nj)),
            pl.BlockSpec((1, tn), lambda i, j: (0, j)),
            pl.BlockSpec((1, tn), lambda i, j: (0, j + nj)),
        ],
        out_specs=pl.BlockSpec((tm, tn), lambda i, j: (i, j)),
        out_shape=jax.ShapeDtypeStruct((n, D_FF), BF16),
        scratch_shapes=[
            pltpu.VMEM((FFN_HALO + tm, D_MODEL), BF16),
            pltpu.VMEM((tm // rows, FFN_HALO + rows, tn), F32),
            pltpu.VMEM((tm // rows, FFN_HALO + rows, tn), F32),
        ],
        compiler_params=_params("arbitrary", "arbitrary"),
        name="ffn_up",
    )(x2, x2, g, w_up, w_up, cw, cw, cb, cb)


def _ffn_down_body(act_ref, w_ref, x_ref, o_ref):
    o_ref[...] = x_ref[...] + jnp.dot(act_ref[...], w_ref[...], preferred_element_type=F32)


def _ffn_down(act, w_down, x2, *, layer, tm, tn):
    n = x2.shape[0]
    return pl.pallas_call(
        _ffn_down_body,
        grid=(n // tm, D_MODEL // tn),
        in_specs=[
            pl.BlockSpec((tm, D_FF), lambda i, j: (i, 0)),
            pl.BlockSpec((None, D_FF, tn), lambda i, j: (layer, 0, j)),
            pl.BlockSpec((tm, tn), lambda i, j: (i, j)),
        ],
        out_specs=pl.BlockSpec((tm, tn), lambda i, j: (i, j)),
        out_shape=jax.ShapeDtypeStruct((n, D_MODEL), F32),
        compiler_params=_params("arbitrary", "arbitrary"),
        name="ffn_down",
    )(act, w_down, x2)


def _final_norm_body(x_ref, g_ref, o_ref):
    o_ref[...] = _rms_norm_rows(x_ref[...], g_ref[...])


def _final_norm(x2, g, *, tm):
    n = x2.shape[0]
    return pl.pallas_call(
        _final_norm_body,
        grid=(n // tm,),
        in_specs=[pl.BlockSpec((tm, D_MODEL), lambda i: (i, 0)), pl.BlockSpec((1, D_MODEL), lambda i: (0, 0))],
        out_specs=pl.BlockSpec((tm, D_MODEL), lambda i: (i, 0)),
        out_shape=jax.ShapeDtypeStruct((n, D_MODEL), F32),
        compiler_params=_params("arbitrary"),
        name="final_norm",
    )(x2, g)


def _in_weight_layout(w_in):
    depth = w_in.shape[0]
    kv0 = ATTN_WIDTH
    gate0 = kv0 + KV_COLS
    conv0 = gate0 + GATE_COLS
    w_t = jnp.swapaxes(w_in, 1, 2)
    parts = [w_t[:, :ATTN_WIDTH], w_t[:, conv0:conv0 + 2 * CONV_CHANNELS]]
    for br in range(N_BRANCH):
        for h in range(N_KV_HEADS):
            for kv in range(2):
                c0 = kv0 + ((br * 2 + kv) * N_KV_HEADS + h) * HEAD_DIM
                parts.append(w_t[:, c0:c0 + HEAD_DIM])
    w_main = jnp.concatenate(parts, axis=1).astype(BF16)
    wg = w_in[..., gate0:gate0 + GATE_COLS].reshape(depth, D_MODEL, N_KV_HEADS, GQA_GROUP, N_BRANCH)
    wg = wg.transpose(0, 1, 2, 4, 3).reshape(depth, D_MODEL, N_KV_HEADS, N_BRANCH * GQA_GROUP)
    wg = jnp.pad(wg, ((0, 0), (0, 0), (0, 0), (0, LANES - N_BRANCH * GQA_GROUP)))
    return w_main, wg.reshape(depth, D_MODEL, N_KV_HEADS * LANES).astype(BF16)


def _pick(n, prefs):
    for p in prefs:
        if n % p == 0:
            return p
    return n


def kernel(x, rel_bias, mix_norm_g, w_in, cmp_pos, cmp_w1, cmp_w2, conv_w, conv_b, conv_ln_g, conv_ln_b,
           w_out, ffn_norm_g, w_up, ffn_conv_w, ffn_conv_b, w_down, final_norm_g):
    bsz, t_len, _ = x.shape
    depth = w_in.shape[0]
    n = bsz * t_len
    assert t_len % 256 == 0 and t_len // SLC_LEN <= HEAD_DIM
    tq = 256
    tm = _pick(t_len, (1024, 512, 256))
    nc = t_len // CMP_STRIDE

    band, cmpb = _bias_tiles(rel_bias, t_len, tq)
    col_scale = jnp.concatenate([jnp.full((1, ATTN_WIDTH), Q_SCALE, F32),
                                 jnp.ones((1, PROJ_COLS - ATTN_WIDTH), F32)], axis=1)
    x2 = x.reshape(n, D_MODEL)
    w_main, wg = _in_weight_layout(w_in)
    w_out_b, w_up_b, w_down_b = w_out.astype(BF16), w_up.astype(BF16), w_down.astype(BF16)
    for l in range(depth):
        proj, gates = _in_projection(x2, mix_norm_g[l][None, :], w_main, wg, col_scale, layer=l, tm=tm, tn=768)
        proj3 = proj.reshape(bsz, t_len, PROJ_COLS)
        gates3 = gates.reshape(bsz, t_len, N_KV_HEADS * LANES)

        posx, w1big, w2big = _compress_weight_layout(cmp_pos[l], cmp_w1[l], cmp_w2[l])
        tokl = proj3[:, :, COL_KV:COL_KV + CMP_WIDTH].reshape(bsz, nc, CMP_STRIDE, CMP_WIDTH).transpose(0, 2, 1, 3)
        kvc, kvct = _compress(tokl, posx, w1big, w2big)

        attn = _attention(proj3, kvc, kvct, band, cmpb, gates3, tq=tq)
        x2 = _mix_out(proj3, conv_w[l], conv_b[l][None, :], conv_ln_g[l][None, :], conv_ln_b[l][None, :],
                      attn, x2.reshape(bsz, t_len, D_MODEL), w_out_b, layer=l, tt=256).reshape(n, D_MODEL)

        act = _ffn_up(x2, ffn_norm_g[l][None, :], w_up_b, ffn_conv_w[l],
                      ffn_conv_b[l][None, :], layer=l, tm=tm, tn=512, t_len=t_len)
        x2 = _ffn_down(act, w_down_b, x2, layer=l, tm=tm, tn=512)
    x2 = _final_norm(x2, final_norm_g[None, :], tm=_pick(n, (512, 256)))
    return x2.reshape(bsz, t_len, D_MODEL)
```

```python
import functools
import math

import numpy as np
import jax
import jax.numpy as jnp
from jax import lax
from jax.experimental import pallas as pl
from jax.experimental.pallas import tpu as pltpu

F32 = jnp.float32
BF16 = jnp.bfloat16

D_MODEL = 2048
HEAD_DIM = 64
N_KV_HEADS = 4
GQA_GROUP = 4
N_Q_HEADS = N_KV_HEADS * GQA_GROUP
N_BRANCH = 3
ATTN_WIDTH = N_Q_HEADS * HEAD_DIM
CMP_LEN = 32
CMP_STRIDE = 16
SLC_LEN = 64
SLC_TOPK = 16
WINDOW = 512
CONV_CHANNELS = D_MODEL - ATTN_WIDTH
CONV_TAPS = 31
D_FF = 5632
FFN_CONV_TAPS = 3
N_BUCKETS = 32
MAX_DISTANCE = 128
NORM_EPS = 1e-6

KV_PAIR = 2 * HEAD_DIM
KV_COLS = N_BRANCH * N_KV_HEADS * KV_PAIR
GATE_COLS = N_BRANCH * N_Q_HEADS
COL_Q = 0
COL_CONV_A = ATTN_WIDTH
COL_CONV_G = COL_CONV_A + CONV_CHANNELS
COL_KV = COL_CONV_G + CONV_CHANNELS
PROJ_COLS = COL_KV + KV_COLS

MASK_VALUE = -1e30
BLOCK_PENALTY = -1e9
KEY_BIG = 1e30
KEY_TAKEN = -3e38
LOG2E = math.log2(math.e)
Q_SCALE = HEAD_DIM ** -0.5 * LOG2E

LANES = 128
SUBLANES = 8
CONV_ROWS = 128
FFN_ROWS = 1024
VMEM_LIMIT = 56 * 1024 * 1024
CONV_HALO = 32
FFN_HALO = 16


def _t5_bucket_last_distance():
    n = np.arange(0, 4 * MAX_DISTANCE, dtype=np.int64)
    max_exact = N_BUCKETS // 2
    nf = np.maximum(n, 1).astype(np.float64)
    large = max_exact + np.floor(np.log(nf / max_exact) / math.log(MAX_DISTANCE / max_exact)
                                 * (N_BUCKETS - max_exact)).astype(np.int64)
    large = np.minimum(large, N_BUCKETS - 1)
    bucket = np.where(n < max_exact, n, large)
    last = []
    for b in range(N_BUCKETS - 1):
        idx = np.nonzero(bucket == b)[0]
        last.append(int(idx.max()) if idx.size else None)
    return last


_BUCKET_LAST = _t5_bucket_last_distance()


def _params(*sem):
    return pltpu.CompilerParams(dimension_semantics=sem, vmem_limit_bytes=VMEM_LIMIT)


def _sigmoid(x):
    return jax.nn.sigmoid(x)


def _rms_norm_rows(x, g):
    ms = jnp.mean(x * x, axis=-1, keepdims=True)
    return x * lax.rsqrt(ms + NORM_EPS) * g


def _bias_from_distance(dist, tab_ref, head):
    c_far = tab_ref[N_BUCKETS - 1, head]
    val = jnp.zeros(dist.shape, F32)
    for b in range(N_BUCKETS - 2, -1, -1):
        if _BUCKET_LAST[b] is None:
            continue
        val = jnp.where(dist <= _BUCKET_LAST[b], tab_ref[b, head] - c_far, val)
    return val


def _band_bias_body(tab_ref, o_ref, *, tq):
    head = pl.program_id(0) * GQA_GROUP + pl.program_id(1)
    shape = (WINDOW + tq, tq)
    dist = (lax.broadcasted_iota(jnp.int32, shape, 1) + WINDOW
            - lax.broadcasted_iota(jnp.int32, shape, 0))
    val = _bias_from_distance(dist, tab_ref, head) * LOG2E
    val = jnp.where(dist >= 0, jnp.where(dist < WINDOW, val, MASK_VALUE), MASK_VALUE)
    o_ref[0] = val


def _cmp_bias_body(tab_ref, o_ref, *, tq, nc):
    head = pl.program_id(0) * GQA_GROUP + pl.program_id(1)
    shape = (2 * nc, tq)
    r = lax.broadcasted_iota(jnp.int32, shape, 0)
    i = lax.broadcasted_iota(jnp.int32, shape, 1)
    dist = i - (r - nc) * CMP_STRIDE - (CMP_LEN - 1)
    val = _bias_from_distance(dist, tab_ref, head) * LOG2E
    o_ref[0] = jnp.where(dist >= 0, val, MASK_VALUE)


def _bias_tiles(rel_bias, t_len, tq):
    nc = t_len // CMP_STRIDE
    rows = GQA_GROUP * tq
    smem = pl.BlockSpec(memory_space=pltpu.SMEM)
    band = pl.pallas_call(
        functools.partial(_band_bias_body, tq=tq),
        grid=(N_KV_HEADS, GQA_GROUP),
        in_specs=[smem],
        out_specs=pl.BlockSpec((1, WINDOW + tq, tq), lambda h, g: (h, 0, g)),
        out_shape=jax.ShapeDtypeStruct((N_KV_HEADS, WINDOW + tq, rows), F32),
        compiler_params=_params("arbitrary", "arbitrary"),
        name="band_bias",
    )(rel_bias)
    cmpb = pl.pallas_call(
        functools.partial(_cmp_bias_body, tq=tq, nc=nc),
        grid=(N_KV_HEADS, GQA_GROUP),
        in_specs=[smem],
        out_specs=pl.BlockSpec((1, 2 * nc, tq), lambda h, g: (h, 0, g)),
        out_shape=jax.ShapeDtypeStruct((N_KV_HEADS, 2 * nc, rows), F32),
        compiler_params=_params("arbitrary", "arbitrary"),
        name="cmp_bias",
    )(rel_bias)
    return band, cmpb


def _inproj_body(x_ref, g_ref, w_ref, wg_ref, cs_ref, o_ref, gate_ref, h_ref):
    @pl.when(pl.program_id(1) == 0)
    def _():
        h = _rms_norm_rows(x_ref[...], g_ref[...]).astype(BF16)
        h_ref[...] = h
        gate_ref[...] = _sigmoid(jnp.dot(h, wg_ref[...], preferred_element_type=F32))

    acc = lax.dot_general(h_ref[...], w_ref[...], (((1,), (1,)), ((), ())),
                          preferred_element_type=F32)
    o_ref[...] = (acc * cs_ref[...]).astype(o_ref.dtype)


def _in_projection(x2, g, w, wg, col_scale, *, layer, tm, tn):
    n = x2.shape[0]
    gcols = wg.shape[-1]
    return pl.pallas_call(
        _inproj_body,
        grid=(n // tm, PROJ_COLS // tn),
        in_specs=[
            pl.BlockSpec((tm, D_MODEL), lambda i, j: (i, 0)),
            pl.BlockSpec((1, D_MODEL), lambda i, j: (0, 0)),
            pl.BlockSpec((None, tn, D_MODEL), lambda i, j: (layer, j, 0)),
            pl.BlockSpec((None, D_MODEL, gcols), lambda i, j: (layer, 0, 0)),
            pl.BlockSpec((1, tn), lambda i, j: (0, j)),
        ],
        out_specs=[
            pl.BlockSpec((tm, tn), lambda i, j: (i, j)),
            pl.BlockSpec((tm, gcols), lambda i, j: (i, 0)),
        ],
        out_shape=[
            jax.ShapeDtypeStruct((n, PROJ_COLS), BF16),
            jax.ShapeDtypeStruct((n, gcols), F32),
        ],
        scratch_shapes=[pltpu.VMEM((tm, D_MODEL), BF16)],
        compiler_params=_params("arbitrary", "arbitrary"),
        name="in_projection",
    )(x2, g, w, wg, col_scale)


CMP_WIDTH = N_KV_HEADS * KV_PAIR


def _compress_body(x_ref, pos_ref, w1_ref, w2_ref, o_ref, ot_ref, top_ref, bot_ref, *, nc):
    l = pl.program_id(1)

    @pl.when(l == 0)
    def _():
        top_ref[...] = jnp.zeros_like(top_ref)
        bot_ref[...] = jnp.zeros_like(bot_ref)

    x = x_ref[0, 0].astype(F32)
    top_ref[...] += jnp.dot((x + pos_ref[0, 0]).astype(BF16), w1_ref[0, 0], preferred_element_type=F32)
    bot_ref[...] += jnp.dot((x + pos_ref[0, 1]).astype(BF16), w1_ref[0, 1], preferred_element_type=F32)

    @pl.when(l == pl.num_programs(1) - 1)
    def _():
        pre = top_ref[...] + pltpu.roll(bot_ref[...], nc - 1, axis=0)
        act = pre * _sigmoid(pre)
        out = jnp.dot(act.astype(BF16), w2_ref[...], preferred_element_type=F32)
        for h in range(N_KV_HEADS):
            kv = out[:, h * KV_PAIR:(h + 1) * KV_PAIR]
            o_ref[0, h] = kv.astype(o_ref.dtype)
            ot_ref[0, h] = kv.T.astype(ot_ref.dtype)


def _compress(tokl, posx, w1big, w2big):
    b, _, nc, _ = tokl.shape
    return pl.pallas_call(
        functools.partial(_compress_body, nc=nc),
        grid=(b, CMP_STRIDE),
        in_specs=[
            pl.BlockSpec((1, 1, nc, CMP_WIDTH), lambda i, l: (i, l, 0, 0)),
            pl.BlockSpec((1, 2, 1, CMP_WIDTH), lambda i, l: (l, 0, 0, 0)),
            pl.BlockSpec((1, 2, CMP_WIDTH, CMP_WIDTH), lambda i, l: (l, 0, 0, 0)),
            pl.BlockSpec((CMP_WIDTH, CMP_WIDTH), lambda i, l: (0, 0)),
        ],
        out_specs=[
            pl.BlockSpec((1, N_KV_HEADS, nc, KV_PAIR), lambda i, l: (i, 0, 0, 0)),
            pl.BlockSpec((1, N_KV_HEADS, KV_PAIR, nc), lambda i, l: (i, 0, 0, 0)),
        ],
        out_shape=[
            jax.ShapeDtypeStruct((b, N_KV_HEADS, nc, KV_PAIR), BF16),
            jax.ShapeDtypeStruct((b, N_KV_HEADS, KV_PAIR, nc), BF16),
        ],
        scratch_shapes=[pltpu.VMEM((nc, CMP_WIDTH), F32), pltpu.VMEM((nc, CMP_WIDTH), F32)],
        compiler_params=_params("arbitrary", "arbitrary"),
        name="compress",
    )(tokl, posx, w1big, w2big)


def _compress_weight_layout(pos, w1, w2):
    def block_diag(blocks):
        rows = []
        for h in range(N_KV_HEADS):
            for kv in range(2):
                off = (h * 2 + kv) * HEAD_DIM
                pad = [(0, 0)] * (blocks.ndim - 2) + [(off, CMP_WIDTH - HEAD_DIM - off)]
                rows.append(jnp.pad(blocks[kv], pad))
        return jnp.concatenate(rows, axis=-2).astype(BF16)

    w1r = w1.reshape(2, 2, CMP_STRIDE, HEAD_DIM, HEAD_DIM)
    w1big = block_diag(w1r.transpose(0, 2, 1, 3, 4))
    w2big = block_diag(w2)
    posr = pos.reshape(2, 2, CMP_STRIDE, HEAD_DIM).transpose(2, 1, 0, 3)
    posx = jnp.broadcast_to(posr[:, :, None], (CMP_STRIDE, 2, N_KV_HEADS, 2, HEAD_DIM))
    return posx.reshape(CMP_STRIDE, 2, 1, CMP_WIDTH), w1big, w2big


VT_ROWS = HEAD_DIM + 16
FAR_UNROLL = 4


def _attn_body(q_ref, kvc_ref, kvct_ref, kvs_ref, kvw_ref, bt_ref, bc_ref, gate_ref, o_ref,
               kp_ref, vst_ref, vwt_ref, key_ref, m_ref, acc_ref, sa_ref, sb_ref, ocmp_ref, qsel_ref, qwin_ref,
               *, tq, t_len):
    qi = pl.program_id(2)
    nc = t_len // CMP_STRIDE
    ns = t_len // SLC_LEN
    rows = GQA_GROUP * tq
    blocks_per_tile = tq // SLC_LEN

    @pl.when(qi == 0)
    def _():
        k = kvs_ref[0][:, :HEAD_DIM]
        blk = lax.broadcasted_iota(jnp.int32, (t_len, HEAD_DIM), 0) // SLC_LEN
        col = lax.broadcasted_iota(jnp.int32, (t_len, HEAD_DIM), 1)
        onehot = jnp.where(blk == col, 1.0, 0.0).astype(BF16)
        kp_ref[...] = jnp.concatenate([k, onehot], axis=1)
        ones_rows = jnp.where(lax.broadcasted_iota(jnp.int32, (VT_ROWS - HEAD_DIM, tq), 0) == 0, 1.0, 0.0)
        for c in range(t_len // tq):
            st = kvs_ref[0, c * tq:(c + 1) * tq, :].astype(F32).T[HEAD_DIM:]
            vst_ref[c] = jnp.concatenate([st, ones_rows], axis=0).astype(BF16)
            wt = kvw_ref[0, c * tq:(c + 1) * tq, :].astype(F32).T[HEAD_DIM:]
            vwt_ref[c] = jnp.concatenate([wt, ones_rows], axis=0).astype(BF16)

    q_t = q_ref[0].astype(F32).T
    q4_t = jnp.concatenate([q_t[g * HEAD_DIM:(g + 1) * HEAD_DIM] for g in range(GQA_GROUP)],
                           axis=1).astype(BF16)
    qwin_ref[...] = jnp.concatenate([q4_t, jnp.zeros((HEAD_DIM, rows), BF16)], axis=0)
    qsel_ref[0:HEAD_DIM, :] = q4_t
    kw_ref = kvw_ref.at[0]
    win, sel = 0, 1

    def cmp_scores():
        bias_row = pl.multiple_of(nc - qi * (tq // CMP_STRIDE), tq // CMP_STRIDE)
        return (jnp.dot(kvc_ref[0, 0], qwin_ref[...], preferred_element_type=F32)
                + bc_ref[0, pl.ds(bias_row, nc), :])

    def cmp_finish(s):
        m = jnp.max(s, axis=0, keepdims=True)
        p = jnp.exp2(s - m)
        l = jnp.sum(p, axis=0, keepdims=True)
        pn = p * jnp.where(m > 0.5 * MASK_VALUE, 1.0 / l, 0.0)
        ocmp_ref[...] = jnp.dot(kvct_ref[0, 0, HEAD_DIM:, :], pn.astype(BF16),
                                preferred_element_type=F32)

        ps = pn[:, 0:tq] + pn[:, tq:2 * tq] + pn[:, 2 * tq:3 * tq] + pn[:, 3 * tq:4 * tq]
        sj = lax.broadcasted_iota(jnp.int32, (HEAD_DIM, nc), 0) * SLC_LEN
        ci = lax.broadcasted_iota(jnp.int32, (HEAD_DIM, nc), 1) * CMP_STRIDE
        overlap = jnp.where(ci < sj + SLC_LEN, jnp.where(ci + CMP_LEN > sj, 1.0, 0.0), 0.0).astype(BF16)
        p_hi = ps.astype(BF16)
        r_hi = ps - p_hi.astype(F32)
        p_md = r_hi.astype(BF16)
        p_lo = (r_hi - p_md.astype(F32)).astype(BF16)
        imp = (jnp.dot(overlap, p_hi, preferred_element_type=F32)
               + jnp.dot(overlap, p_md, preferred_element_type=F32)
               + jnp.dot(overlap, p_lo, preferred_element_type=F32))
        t = qi * tq + lax.broadcasted_iota(jnp.int32, (HEAD_DIM, tq), 1)
        blk = lax.broadcasted_iota(jnp.int32, (HEAD_DIM, tq), 0)
        cur = t // SLC_LEN
        key = jnp.where(blk == 0, KEY_BIG, jnp.where(blk == cur, KEY_BIG, jnp.where(blk == cur - 1, KEY_BIG, imp)))
        key_ref[...] = jnp.where(blk * SLC_LEN <= t, key, -KEY_BIG)

    def select_blocks():
        key = key_ref[...]
        blk = lax.broadcasted_iota(jnp.int32, (HEAD_DIM, tq), 0)
        pen = jnp.full((HEAD_DIM, tq), BLOCK_PENALTY, F32)
        for _ in range(min(SLC_TOPK, ns)):
            top = jnp.max(key, axis=0, keepdims=True)
            first = jnp.min(jnp.where(key == top, blk, HEAD_DIM), axis=0, keepdims=True)
            hit = blk == first
            pen = jnp.where(hit, 0.0, pen)
            key = jnp.where(hit, KEY_TAKEN, key)
        qsel_ref[HEAD_DIM:, :] = jnp.concatenate([pen.astype(BF16)] * GQA_GROUP, axis=1)

    def scores(qx_ref, k_ref, kj, bias_off):
        start = pl.multiple_of(kj * tq, tq)
        sc = jnp.dot(k_ref[pl.ds(start, tq), :], qx_ref[...], preferred_element_type=F32)
        if bias_off is not None:
            sc = sc + bt_ref[0, bias_off:bias_off + tq, :]
        return sc

    def consume(sc, vt_ref, kj, br, st, first):
        m_tile = jnp.max(sc, axis=0, keepdims=True)
        if first:
            m_new = m_tile
            pr = jnp.exp2(sc - m_new).astype(BF16)
            acc_ref[br, st] = jnp.dot(vt_ref[kj], pr, preferred_element_type=F32)
        else:
            m_old = m_ref[br, st]
            m_new = jnp.maximum(m_old, m_tile)
            alpha = jnp.exp2(m_old - m_new)
            pr = jnp.exp2(sc - m_new).astype(BF16)
            acc_ref[br, st] = alpha * acc_ref[br, st] + jnp.dot(vt_ref[kj], pr, preferred_element_type=F32)
        m_ref[br, st] = m_new

    def finish(br):
        m0 = m_ref[br, 0]
        m1 = m_ref[br, 1]
        m_all = jnp.maximum(m0, m1)
        acc = jnp.exp2(m0 - m_all) * acc_ref[br, 0] + jnp.exp2(m1 - m_all) * acc_ref[br, 1]
        return acc[0:HEAD_DIM] * (1.0 / acc[HEAD_DIM:HEAD_DIM + 1, :])

    def combine():
        o_win = finish(win)
        o_sel = finish(sel)
        o_cmp = ocmp_ref[...]
        gates_t = gate_ref[0].T
        combs = []
        for g in range(GQA_GROUP):
            sl = slice(g * tq, (g + 1) * tq)
            combs.append(gates_t[g:g + 1, :] * o_cmp[:, sl]
                         + gates_t[GQA_GROUP + g:GQA_GROUP + g + 1, :] * o_sel[:, sl]
                         + gates_t[2 * GQA_GROUP + g:2 * GQA_GROUP + g + 1, :] * o_win[:, sl])
        outs = [jnp.concatenate(combs[p:p + 2], axis=0).T for p in range(0, GQA_GROUP, 2)]
        o_ref[0] = jnp.concatenate(outs, axis=1).astype(o_ref.dtype)

    @pl.when(qi >= 2)
    def _():
        sc_cmp = cmp_scores()
        sw0 = scores(qwin_ref, kw_ref, qi, WINDOW)
        cmp_finish(sc_cmp)
        sw1 = scores(qwin_ref, kw_ref, qi - 1, WINDOW - tq)
        consume(sw0, vwt_ref, qi, win, 0, True)
        sw2 = scores(qwin_ref, kw_ref, qi - 2, WINDOW - 2 * tq)
        consume(sw1, vwt_ref, qi - 1, win, 1, True)
        select_blocks()
        consume(sw2, vwt_ref, qi - 2, win, 0, False)

        ss0 = scores(qsel_ref, kp_ref, qi, WINDOW)
        ss1 = scores(qsel_ref, kp_ref, qi - 1, WINDOW - tq)
        consume(ss0, vst_ref, qi, sel, 0, True)
        sa_ref[...] = scores(qsel_ref, kp_ref, 0, None)
        consume(ss1, vst_ref, qi - 1, sel, 1, True)
        n_far = qi - 1

        def far_pair(pi):
            sb_ref[...] = scores(qsel_ref, kp_ref, 2 * pi + 1, None)
            consume(sa_ref[...], vst_ref, 2 * pi, sel, 1, False)
            sa_ref[...] = scores(qsel_ref, kp_ref, jnp.minimum(2 * pi + 2, n_far - 1), None)
            consume(sb_ref[...], vst_ref, 2 * pi + 1, sel, 0, False)

        n_pairs = n_far // 2

        def far_group(gi, carry):
            for u in range(FAR_UNROLL):
                far_pair(FAR_UNROLL * gi + u)
            return carry

        lax.fori_loop(0, n_pairs // FAR_UNROLL, far_group, 0)
        done = (n_pairs // FAR_UNROLL) * FAR_UNROLL
        left = n_pairs - done
        size = FAR_UNROLL // 2
        while size >= 1:
            @pl.when(left % (2 * size) >= size)
            def _(size=size, base=done):
                for u in range(size):
                    far_pair(base + u)
            done = done + jnp.where(left % (2 * size) >= size, size, 0)
            size //= 2

        @pl.when(n_far % 2 == 1)
        def _():
            consume(sa_ref[...], vst_ref, n_far - 1, sel, 1, False)

        combine()

    @pl.when(qi < 2)
    def _():
        cmp_finish(cmp_scores())
        select_blocks()
        sw0 = scores(qwin_ref, kw_ref, qi, WINDOW)
        ss0 = scores(qsel_ref, kp_ref, qi, WINDOW)
        consume(sw0, vwt_ref, qi, win, 0, True)
        consume(ss0, vst_ref, qi, sel, 0, True)
        for br in (win, sel):
            m_ref[br, 1] = jnp.full((1, rows), MASK_VALUE, F32)
            acc_ref[br, 1] = jnp.zeros((VT_ROWS, rows), F32)

        @pl.when(qi == 1)
        def _():
            sw1 = scores(qwin_ref, kw_ref, 0, WINDOW - tq)
            ss1 = scores(qsel_ref, kp_ref, 0, WINDOW - tq)
            consume(sw1, vwt_ref, 0, win, 1, False)
            consume(ss1, vst_ref, 0, sel, 1, False)

        combine()


def _attention(proj3, kvc, kvct, band, cmpb, gates3, *, tq):
    b, t_len, _ = proj3.shape
    assert WINDOW == 2 * tq
    nc = t_len // CMP_STRIDE
    rows = GQA_GROUP * tq
    qw = GQA_GROUP * HEAD_DIM
    slc_blk = COL_KV // KV_PAIR + N_KV_HEADS
    win_blk = COL_KV // KV_PAIR + 2 * N_KV_HEADS
    return pl.pallas_call(
        functools.partial(_attn_body, tq=tq, t_len=t_len),
        grid=(b, N_KV_HEADS, t_len // tq),
        in_specs=[
            pl.BlockSpec((1, tq, qw), lambda i, h, q: (i, q, h)),
            pl.BlockSpec((1, 1, nc, KV_PAIR), lambda i, h, q: (i, h, 0, 0)),
            pl.BlockSpec((1, 1, KV_PAIR, nc), lambda i, h, q: (i, h, 0, 0)),
            pl.BlockSpec((1, t_len, KV_PAIR), lambda i, h, q: (i, 0, slc_blk + h)),
            pl.BlockSpec((1, t_len, KV_PAIR), lambda i, h, q: (i, 0, win_blk + h)),
            pl.BlockSpec((1, WINDOW + tq, rows), lambda i, h, q: (h, 0, 0)),
            pl.BlockSpec((1, 2 * nc, rows), lambda i, h, q: (h, 0, 0)),
            pl.BlockSpec((1, tq, LANES), lambda i, h, q: (i, q, h)),
        ],
        out_specs=pl.BlockSpec((1, tq, qw), lambda i, h, q: (i, q, h)),
        out_shape=jax.ShapeDtypeStruct((b, t_len, ATTN_WIDTH), BF16),
        scratch_shapes=[
            pltpu.VMEM((t_len, KV_PAIR), BF16),
            pltpu.VMEM((t_len // tq, VT_ROWS, tq), BF16),
            pltpu.VMEM((t_len // tq, VT_ROWS, tq), BF16),
            pltpu.VMEM((HEAD_DIM, tq), F32),
            pltpu.VMEM((2, 2, 1, rows), F32),
            pltpu.VMEM((2, 2, VT_ROWS, rows), F32),
            pltpu.VMEM((tq, rows), F32),
            pltpu.VMEM((tq, rows), F32),
            pltpu.VMEM((HEAD_DIM, rows), F32),
            pltpu.VMEM((KV_PAIR, rows), BF16),
            pltpu.VMEM((KV_PAIR, rows), BF16),
        ],
        compiler_params=_params("arbitrary", "arbitrary", "arbitrary"),
        name="nsa_attention",
    )(proj3, kvc, kvct, proj3, proj3, band, cmpb, gates3)


def _mix_out_body(a_ref, g_ref, ah_ref, gh_ref, w_ref, b_ref, lg_ref, lb_ref, attn_ref, x_ref, wa_ref, wc_ref,
                  o_ref, u_ref, y_ref, *, tt):
    ti = pl.program_id(1)
    acc_attn = jnp.dot(attn_ref[0], wa_ref[...], preferred_element_type=F32)
    u_ref[CONV_HALO:, :] = a_ref[0].astype(F32) * _sigmoid(g_ref[0].astype(F32))
    halo = ah_ref[0].astype(F32) * _sigmoid(gh_ref[0].astype(F32))
    u_ref[0:CONV_HALO, :] = jnp.where(ti > 0, halo, 0.0)
    base = CONV_HALO - (CONV_TAPS - 1)
    ext = CONV_ROWS + CONV_HALO
    for rb in range(tt // CONV_ROWS):
        rs = slice(rb * CONV_ROWS, (rb + 1) * CONV_ROWS)
        for cb in range(CONV_CHANNELS // LANES):
            cs = slice(cb * LANES, (cb + 1) * LANES)
            wblk = w_ref[:, cs]
            ublk = u_ref[rb * CONV_ROWS:rb * CONV_ROWS + ext, cs]
            acc = jnp.zeros((CONV_ROWS, LANES), F32)
            for r in range(SUBLANES):
                ur = ublk if r == 0 else pltpu.roll(ublk, ext - r, axis=0)
                for a in range(CONV_HALO // SUBLANES + 1):
                    k = SUBLANES * a + r - base
                    if 0 <= k < CONV_TAPS:
                        acc = acc + wblk[k:k + 1, :] * ur[SUBLANES * a:SUBLANES * a + CONV_ROWS]
            y_ref[rs, cs] = acc
        acc = y_ref[rs, :] + b_ref[...]
        mu = jnp.mean(acc, axis=-1, keepdims=True)
        xc = acc - mu
        var = jnp.mean(xc * xc, axis=-1, keepdims=True)
        y = xc * lax.rsqrt(var + NORM_EPS) * lg_ref[...] + lb_ref[...]
        conv_act = (y * _sigmoid(y)).astype(BF16)
        o_ref[0, rs, :] = (x_ref[0, rs, :] + acc_attn[rs]
                           + jnp.dot(conv_act, wc_ref[...], preferred_element_type=F32))


def _mix_out(proj3, w, b, lg, lb, attn, x3, w_out, *, layer, tt):
    bsz, t_len, _ = proj3.shape
    a_blk = COL_CONV_A // CONV_CHANNELS
    g_blk = COL_CONV_G // CONV_CHANNELS
    hpt = tt // CONV_HALO
    halo_idx = lambda i, t: jnp.maximum(t * hpt - 1, 0)
    vec = pl.BlockSpec((1, CONV_CHANNELS), lambda i, t: (0, 0))
    conv_rows = ATTN_WIDTH // CONV_CHANNELS
    return pl.pallas_call(
        functools.partial(_mix_out_body, tt=tt),
        grid=(bsz, t_len // tt),
        in_specs=[
            pl.BlockSpec((1, tt, CONV_CHANNELS), lambda i, t: (i, t, a_blk)),
            pl.BlockSpec((1, tt, CONV_CHANNELS), lambda i, t: (i, t, g_blk)),
            pl.BlockSpec((1, CONV_HALO, CONV_CHANNELS), lambda i, t: (i, halo_idx(i, t), a_blk)),
            pl.BlockSpec((1, CONV_HALO, CONV_CHANNELS), lambda i, t: (i, halo_idx(i, t), g_blk)),
            pl.BlockSpec((CONV_TAPS, CONV_CHANNELS), lambda i, t: (0, 0)),
            vec, vec, vec,
            pl.BlockSpec((1, tt, ATTN_WIDTH), lambda i, t: (i, t, 0)),
            pl.BlockSpec((1, tt, D_MODEL), lambda i, t: (i, t, 0)),
            pl.BlockSpec((None, ATTN_WIDTH, D_MODEL), lambda i, t: (layer, 0, 0)),
            pl.BlockSpec((None, CONV_CHANNELS, D_MODEL), lambda i, t: (layer, conv_rows, 0)),
        ],
        out_specs=pl.BlockSpec((1, tt, D_MODEL), lambda i, t: (i, t, 0)),
        out_shape=jax.ShapeDtypeStruct((bsz, t_len, D_MODEL), F32),
        scratch_shapes=[pltpu.VMEM((CONV_HALO + tt, CONV_CHANNELS), F32),
                        pltpu.VMEM((tt, CONV_CHANNELS), F32)],
        compiler_params=_params("arbitrary", "arbitrary"),
        name="mix_out",
    )(proj3, proj3, proj3, proj3, w, b, lg, lb, attn, x3, w_out, w_out)


def _ffn_up_body(x_ref, xh_ref, g_ref, wa_ref, wg_ref, cwa_ref, cwg_ref, cba_ref, cbg_ref, o_ref,
                 h_ref, ua_ref, ug_ref, *, tm, tiles_per_seq, rows):
    i = pl.program_id(0)

    @pl.when(pl.program_id(1) == 0)
    def _():
        h_ref[FFN_HALO:, :] = _rms_norm_rows(x_ref[...], g_ref[...]).astype(BF16)
        hh = _rms_norm_rows(xh_ref[...], g_ref[...])
        h_ref[0:FFN_HALO, :] = jnp.where(i % tiles_per_seq != 0, hh, 0.0).astype(BF16)

    units = [(c, slice(0, o_ref.shape[1])) for c in range(tm // rows)]

    def project(c, cs):
        hc = h_ref[c * rows:(c + 1) * rows + FFN_HALO, :]
        ug_ref[c, :, cs] = jnp.dot(hc, wg_ref[:, cs], preferred_element_type=F32)
        ua_ref[c, :, cs] = jnp.dot(hc, wa_ref[:, cs], preferred_element_type=F32)

    def conv(u_ref, c, cs, cw_ref, cb_ref):
        u = u_ref[c, :, cs]
        y = cw_ref[FFN_CONV_TAPS - 1:FFN_CONV_TAPS, cs] * u[FFN_HALO:]
        for s in range(1, FFN_CONV_TAPS):
            k = FFN_CONV_TAPS - 1 - s
            y = y + cw_ref[k:k + 1, cs] * pltpu.roll(u, s, axis=0)[FFN_HALO:]
        return y + cb_ref[:, cs]

    def finish(c, cs):
        gate = conv(ug_ref, c, cs, cwg_ref, cbg_ref)
        act = gate * _sigmoid(gate)
        a = conv(ua_ref, c, cs, cwa_ref, cba_ref)
        o_ref[c * rows:(c + 1) * rows, cs] = (act * a).astype(o_ref.dtype)

    project(*units[0])
    for u, unit in enumerate(units):
        if u + 1 < len(units):
            project(*units[u + 1])
        finish(*unit)


def _ffn_up(x2, g, w_up, cw, cb, *, layer, tm, tn, t_len):
    n = x2.shape[0]
    nj = D_FF // tn
    hpt = tm // FFN_HALO
    rows = min(FFN_ROWS, tm)
    return pl.pallas_call(
        functools.partial(_ffn_up_body, tm=tm, tiles_per_seq=t_len // tm, rows=rows),
        grid=(n // tm, nj),
        in_specs=[
            pl.BlockSpec((tm, D_MODEL), lambda i, j: (i, 0)),
            pl.BlockSpec((FFN_HALO, D_MODEL), lambda i, j: (jnp.maximum(i * hpt - 1, 0), 0)),
            pl.BlockSpec((1, D_MODEL), lambda i, j: (0, 0)),
            pl.BlockSpec((None, D_MODEL, tn), lambda i, j: (layer, 0, j)),
            pl.BlockSpec((None, D_MODEL, tn), lambda i, j: (layer, 0, j + nj)),
            pl.BlockSpec((FFN_CONV_TAPS, tn), lambda i, j: (0, j)),
            pl.BlockSpec((FFN_CONV_TAPS, tn), lambda i, j: (0, j + nj)),
            pl.BlockSpec((1, tn), lambda i, j: (0, j)),
            pl.BlockSpec((1, tn), lambda i, j: (0, j + nj)),
        ],
        out_specs=pl.BlockSpec((tm, tn), lambda i, j: (i, j)),
        out_shape=jax.ShapeDtypeStruct((n, D_FF), BF16),
        scratch_shapes=[
            pltpu.VMEM((FFN_HALO + tm, D_MODEL), BF16),
            pltpu.VMEM((tm // rows, FFN_HALO + rows, tn), F32),
            pltpu.VMEM((tm // rows, FFN_HALO + rows, tn), F32),
        ],
        compiler_params=_params("arbitrary", "arbitrary"),
        name="ffn_up",
    )(x2, x2, g, w_up, w_up, cw, cw, cb, cb)


def _ffn_down_body(act_ref, w_ref, x_ref, o_ref):
    o_ref[...] = x_ref[...] + jnp.dot(act_ref[...], w_ref[...], preferred_element_type=F32)


def _ffn_down(act, w_down, x2, *, layer, tm, tn):
    n = x2.shape[0]
    return pl.pallas_call(
        _ffn_down_body,
        grid=(n // tm, D_MODEL // tn),
        in_specs=[
            pl.BlockSpec((tm, D_FF), lambda i, j: (i, 0)),
            pl.BlockSpec((None, D_FF, tn), lambda i, j: (layer, 0, j)),
            pl.BlockSpec((tm, tn), lambda i, j: (i, j)),
        ],
        out_specs=pl.BlockSpec((tm, tn), lambda i, j: (i, j)),
        out_shape=jax.ShapeDtypeStruct((n, D_MODEL), F32),
        compiler_params=_params("arbitrary", "arbitrary"),
        name="ffn_down",
    )(act, w_down, x2)


def _final_norm_body(x_ref, g_ref, o_ref):
    o_ref[...] = _rms_norm_rows(x_ref[...], g_ref[...])


def _final_norm(x2, g, *, tm):
    n = x2.shape[0]
    return pl.pallas_call(
        _final_norm_body,
        grid=(n // tm,),
        in_specs=[pl.BlockSpec((tm, D_MODEL), lambda i: (i, 0)), pl.BlockSpec((1, D_MODEL), lambda i: (0, 0))],
        out_specs=pl.BlockSpec((tm, D_MODEL), lambda i: (i, 0)),
        out_shape=jax.ShapeDtypeStruct((n, D_MODEL), F32),
        compiler_params=_params("arbitrary"),
        name="final_norm",
    )(x2, g)


def _in_weight_layout(w_in):
    depth = w_in.shape[0]
    kv0 = ATTN_WIDTH
    gate0 = kv0 + KV_COLS
    conv0 = gate0 + GATE_COLS
    w_t = jnp.swapaxes(w_in, 1, 2)
    parts = [w_t[:, :ATTN_WIDTH], w_t[:, conv0:conv0 + 2 * CONV_CHANNELS]]
    for br in range(N_BRANCH):
        for h in range(N_KV_HEADS):
            for kv in range(2):
                c0 = kv0 + ((br * 2 + kv) * N_KV_HEADS + h) * HEAD_DIM
                parts.append(w_t[:, c0:c0 + HEAD_DIM])
    w_main = jnp.concatenate(parts, axis=1).astype(BF16)
    wg = w_in[..., gate0:gate0 + GATE_COLS].reshape(depth, D_MODEL, N_KV_HEADS, GQA_GROUP, N_BRANCH)
    wg = wg.transpose(0, 1, 2, 4, 3).reshape(depth, D_MODEL, N_KV_HEADS, N_BRANCH * GQA_GROUP)
    wg = jnp.pad(wg, ((0, 0), (0, 0), (0, 0), (0, LANES - N_BRANCH * GQA_GROUP)))
    return w_main, wg.reshape(depth, D_MODEL, N_KV_HEADS * LANES).astype(BF16)


def _pick(n, prefs):
    for p in prefs:
        if n % p == 0:
            return p
    return n


def kernel(x, rel_bias, mix_norm_g, w_in, cmp_pos, cmp_w1, cmp_w2, conv_w, conv_b, conv_ln_g, conv_ln_b,
           w_out, ffn_norm_g, w_up, ffn_conv_w, ffn_conv_b, w_down, final_norm_g):
    bsz, t_len, _ = x.shape
    depth = w_in.shape[0]
    n = bsz * t_len
    assert t_len % 256 == 0 and t_len // SLC_LEN <= HEAD_DIM
    tq = 256
    tm = _pick(t_len, (1024, 512, 256))
    nc = t_len // CMP_STRIDE

    band, cmpb = _bias_tiles(rel_bias, t_len, tq)
    col_scale = jnp.concatenate([jnp.full((1, ATTN_WIDTH), Q_SCALE, F32),
                                 jnp.ones((1, PROJ_COLS - ATTN_WIDTH), F32)], axis=1)
    x2 = x.reshape(n, D_MODEL)
    w_main, wg = _in_weight_layout(w_in)
    w_out_b, w_up_b, w_down_b = w_out.astype(BF16), w_up.astype(BF16), w_down.astype(BF16)
    for l in range(depth):
        proj, gates = _in_projection(x2, mix_norm_g[l][None, :], w_main, wg, col_scale, layer=l, tm=tm, tn=1536)
        proj3 = proj.reshape(bsz, t_len, PROJ_COLS)
        gates3 = gates.reshape(bsz, t_len, N_KV_HEADS * LANES)

        posx, w1big, w2big = _compress_weight_layout(cmp_pos[l], cmp_w1[l], cmp_w2[l])
        tokl = proj3[:, :, COL_KV:COL_KV + CMP_WIDTH].reshape(bsz, nc, CMP_STRIDE, CMP_WIDTH).transpose(0, 2, 1, 3)
        kvc, kvct = _compress(tokl, posx, w1big, w2big)

        attn = _attention(proj3, kvc, kvct, band, cmpb, gates3, tq=tq)
        x2 = _mix_out(proj3, conv_w[l], conv_b[l][None, :], conv_ln_g[l][None, :], conv_ln_b[l][None, :],
                      attn, x2.reshape(bsz, t_len, D_MODEL), w_out_b, layer=l, tt=512).reshape(n, D_MODEL)

        act = _ffn_up(x2, ffn_norm_g[l][None, :], w_up_b, ffn_conv_w[l],
                      ffn_conv_b[l][None, :], layer=l, tm=tm, tn=512, t_len=t_len)
        x2 = _ffn_down(act, w_down_b, x2, layer=l, tm=tm, tn=512)
    x2 = _final_norm(x2, final_norm_g[None, :], tm=_pick(n, (512, 256)))
    return x2.reshape(bsz, t_len, D_MODEL)
```

```python
import functools
import math

import numpy as np
import jax
import jax.numpy as jnp
from jax import lax
from jax.experimental import pallas as pl
from jax.experimental.pallas import tpu as pltpu

F32 = jnp.float32
BF16 = jnp.bfloat16

D_MODEL = 2048
HEAD_DIM = 64
N_KV_HEADS = 4
GQA_GROUP = 4
N_Q_HEADS = N_KV_HEADS * GQA_GROUP
N_BRANCH = 3
ATTN_WIDTH = N_Q_HEADS * HEAD_DIM
CMP_LEN = 32
CMP_STRIDE = 16
SLC_LEN = 64
SLC_TOPK = 16
WINDOW = 512
CONV_CHANNELS = D_MODEL - ATTN_WIDTH
CONV_TAPS = 31
D_FF = 5632
FFN_CONV_TAPS = 3
N_BUCKETS = 32
MAX_DISTANCE = 128
NORM_EPS = 1e-6

KV_PAIR = 2 * HEAD_DIM
KV_COLS = N_BRANCH * N_KV_HEADS * KV_PAIR
GATE_COLS = N_BRANCH * N_Q_HEADS
COL_Q = 0
COL_CONV_A = ATTN_WIDTH
COL_CONV_G = COL_CONV_A + CONV_CHANNELS
COL_KV = COL_CONV_G + CONV_CHANNELS
PROJ_COLS = COL_KV + KV_COLS

MASK_VALUE = -1e30
BLOCK_PENALTY = -1e9
KEY_BIG = 1e30
KEY_TAKEN = -3e38
LOG2E = math.log2(math.e)
Q_SCALE = HEAD_DIM ** -0.5 * LOG2E

LANES = 128
SUBLANES = 8
CONV_ROWS = 128
FFN_ROWS = 1024
FFN_COLS = 512
VMEM_LIMIT = 56 * 1024 * 1024
CONV_HALO = 32
FFN_HALO = 16


def _t5_bucket_last_distance():
    n = np.arange(0, 4 * MAX_DISTANCE, dtype=np.int64)
    max_exact = N_BUCKETS // 2
    nf = np.maximum(n, 1).astype(np.float64)
    large = max_exact + np.floor(np.log(nf / max_exact) / math.log(MAX_DISTANCE / max_exact)
                                 * (N_BUCKETS - max_exact)).astype(np.int64)
    large = np.minimum(large, N_BUCKETS - 1)
    bucket = np.where(n < max_exact, n, large)
    last = []
    for b in range(N_BUCKETS - 1):
        idx = np.nonzero(bucket == b)[0]
        last.append(int(idx.max()) if idx.size else None)
    return last


_BUCKET_LAST = _t5_bucket_last_distance()


def _params(*sem):
    return pltpu.CompilerParams(dimension_semantics=sem, vmem_limit_bytes=VMEM_LIMIT)


def _sigmoid(x):
    return jax.nn.sigmoid(x)


def _rms_norm_rows(x, g):
    ms = jnp.mean(x * x, axis=-1, keepdims=True)
    return x * lax.rsqrt(ms + NORM_EPS) * g


def _bias_from_distance(dist, tab_ref, head):
    c_far = tab_ref[N_BUCKETS - 1, head]
    val = jnp.zeros(dist.shape, F32)
    for b in range(N_BUCKETS - 2, -1, -1):
        if _BUCKET_LAST[b] is None:
            continue
        val = jnp.where(dist <= _BUCKET_LAST[b], tab_ref[b, head] - c_far, val)
    return val


def _band_bias_body(tab_ref, o_ref, *, tq):
    head = pl.program_id(0) * GQA_GROUP + pl.program_id(1)
    shape = (WINDOW + tq, tq)
    dist = (lax.broadcasted_iota(jnp.int32, shape, 1) + WINDOW
            - lax.broadcasted_iota(jnp.int32, shape, 0))
    val = _bias_from_distance(dist, tab_ref, head) * LOG2E
    val = jnp.where(dist >= 0, jnp.where(dist < WINDOW, val, MASK_VALUE), MASK_VALUE)
    o_ref[0] = val


def _cmp_bias_body(tab_ref, o_ref, *, tq, nc):
    head = pl.program_id(0) * GQA_GROUP + pl.program_id(1)
    shape = (2 * nc, tq)
    r = lax.broadcasted_iota(jnp.int32, shape, 0)
    i = lax.broadcasted_iota(jnp.int32, shape, 1)
    dist = i - (r - nc) * CMP_STRIDE - (CMP_LEN - 1)
    val = _bias_from_distance(dist, tab_ref, head) * LOG2E
    o_ref[0] = jnp.where(dist >= 0, val, MASK_VALUE)


def _bias_tiles(rel_bias, t_len, tq):
    nc = t_len // CMP_STRIDE
    rows = GQA_GROUP * tq
    smem = pl.BlockSpec(memory_space=pltpu.SMEM)
    band = pl.pallas_call(
        functools.partial(_band_bias_body, tq=tq),
        grid=(N_KV_HEADS, GQA_GROUP),
        in_specs=[smem],
        out_specs=pl.BlockSpec((1, WINDOW + tq, tq), lambda h, g: (h, 0, g)),
        out_shape=jax.ShapeDtypeStruct((N_KV_HEADS, WINDOW + tq, rows), F32),
        compiler_params=_params("arbitrary", "arbitrary"),
        name="band_bias",
    )(rel_bias)
    cmpb = pl.pallas_call(
        functools.partial(_cmp_bias_body, tq=tq, nc=nc),
        grid=(N_KV_HEADS, GQA_GROUP),
        in_specs=[smem],
        out_specs=pl.BlockSpec((1, 2 * nc, tq), lambda h, g: (h, 0, g)),
        out_shape=jax.ShapeDtypeStruct((N_KV_HEADS, 2 * nc, rows), F32),
        compiler_params=_params("arbitrary", "arbitrary"),
        name="cmp_bias",
    )(rel_bias)
    return band, cmpb


def _inproj_body(x_ref, g_ref, w_ref, wg_ref, cs_ref, o_ref, gate_ref, h_ref):
    @pl.when(pl.program_id(1) == 0)
    def _():
        h = _rms_norm_rows(x_ref[...], g_ref[...]).astype(BF16)
        h_ref[...] = h
        gate_ref[...] = _sigmoid(jnp.dot(h, wg_ref[...], preferred_element_type=F32))

    acc = lax.dot_general(h_ref[...], w_ref[...], (((1,), (1,)), ((), ())),
                          preferred_element_type=F32)
    o_ref[...] = (acc * cs_ref[...]).astype(o_ref.dtype)


def _in_projection(x2, g, w, wg, col_scale, *, layer, tm, tn):
    n = x2.shape[0]
    gcols = wg.shape[-1]
    return pl.pallas_call(
        _inproj_body,
        grid=(n // tm, PROJ_COLS // tn),
        in_specs=[
            pl.BlockSpec((tm, D_MODEL), lambda i, j: (i, 0)),
            pl.BlockSpec((1, D_MODEL), lambda i, j: (0, 0)),
            pl.BlockSpec((None, tn, D_MODEL), lambda i, j: (layer, j, 0)),
            pl.BlockSpec((None, D_MODEL, gcols), lambda i, j: (layer, 0, 0)),
            pl.BlockSpec((1, tn), lambda i, j: (0, j)),
        ],
        out_specs=[
            pl.BlockSpec((tm, tn), lambda i, j: (i, j)),
            pl.BlockSpec((tm, gcols), lambda i, j: (i, 0)),
        ],
        out_shape=[
            jax.ShapeDtypeStruct((n, PROJ_COLS), BF16),
            jax.ShapeDtypeStruct((n, gcols), F32),
        ],
        scratch_shapes=[pltpu.VMEM((tm, D_MODEL), BF16)],
        compiler_params=_params("arbitrary", "arbitrary"),
        name="in_projection",
    )(x2, g, w, wg, col_scale)


CMP_WIDTH = N_KV_HEADS * KV_PAIR


def _compress_body(x_ref, pos_ref, w1_ref, w2_ref, o_ref, ot_ref, top_ref, bot_ref, *, nc):
    l = pl.program_id(1)

    @pl.when(l == 0)
    def _():
        top_ref[...] = jnp.zeros_like(top_ref)
        bot_ref[...] = jnp.zeros_like(bot_ref)

    x = x_ref[0, 0].astype(F32)
    top_ref[...] += jnp.dot((x + pos_ref[0, 0]).astype(BF16), w1_ref[0, 0], preferred_element_type=F32)
    bot_ref[...] += jnp.dot((x + pos_ref[0, 1]).astype(BF16), w1_ref[0, 1], preferred_element_type=F32)

    @pl.when(l == pl.num_programs(1) - 1)
    def _():
        pre = top_ref[...] + pltpu.roll(bot_ref[...], nc - 1, axis=0)
        act = pre * _sigmoid(pre)
        out = jnp.dot(act.astype(BF16), w2_ref[...], preferred_element_type=F32)
        for h in range(N_KV_HEADS):
            kv = out[:, h * KV_PAIR:(h + 1) * KV_PAIR]
            o_ref[0, h] = kv.astype(o_ref.dtype)
            ot_ref[0, h] = kv.T.astype(ot_ref.dtype)


def _compress(tokl, posx, w1big, w2big):
    b, _, nc, _ = tokl.shape
    return pl.pallas_call(
        functools.partial(_compress_body, nc=nc),
        grid=(b, CMP_STRIDE),
        in_specs=[
            pl.BlockSpec((1, 1, nc, CMP_WIDTH), lambda i, l: (i, l, 0, 0)),
            pl.BlockSpec((1, 2, 1, CMP_WIDTH), lambda i, l: (l, 0, 0, 0)),
            pl.BlockSpec((1, 2, CMP_WIDTH, CMP_WIDTH), lambda i, l: (l, 0, 0, 0)),
            pl.BlockSpec((CMP_WIDTH, CMP_WIDTH), lambda i, l: (0, 0)),
        ],
        out_specs=[
            pl.BlockSpec((1, N_KV_HEADS, nc, KV_PAIR), lambda i, l: (i, 0, 0, 0)),
            pl.BlockSpec((1, N_KV_HEADS, KV_PAIR, nc), lambda i, l: (i, 0, 0, 0)),
        ],
        out_shape=[
            jax.ShapeDtypeStruct((b, N_KV_HEADS, nc, KV_PAIR), BF16),
            jax.ShapeDtypeStruct((b, N_KV_HEADS, KV_PAIR, nc), BF16),
        ],
        scratch_shapes=[pltpu.VMEM((nc, CMP_WIDTH), F32), pltpu.VMEM((nc, CMP_WIDTH), F32)],
        compiler_params=_params("arbitrary", "arbitrary"),
        name="compress",
    )(tokl, posx, w1big, w2big)


def _compress_weight_layout(pos, w1, w2):
    def block_diag(blocks):
        rows = []
        for h in range(N_KV_HEADS):
            for kv in range(2):
                off = (h * 2 + kv) * HEAD_DIM
                pad = [(0, 0)] * (blocks.ndim - 2) + [(off, CMP_WIDTH - HEAD_DIM - off)]
                rows.append(jnp.pad(blocks[kv], pad))
        return jnp.concatenate(rows, axis=-2).astype(BF16)

    w1r = w1.reshape(2, 2, CMP_STRIDE, HEAD_DIM, HEAD_DIM)
    w1big = block_diag(w1r.transpose(0, 2, 1, 3, 4))
    w2big = block_diag(w2)
    posr = pos.reshape(2, 2, CMP_STRIDE, HEAD_DIM).transpose(2, 1, 0, 3)
    posx = jnp.broadcast_to(posr[:, :, None], (CMP_STRIDE, 2, N_KV_HEADS, 2, HEAD_DIM))
    return posx.reshape(CMP_STRIDE, 2, 1, CMP_WIDTH), w1big, w2big


VT_ROWS = HEAD_DIM + 16
FAR_UNROLL = 4


def _attn_body(q_ref, kvc_ref, kvct_ref, kvs_ref, kvw_ref, bt_ref, bc_ref, gate_ref, o_ref,
               kp_ref, vst_ref, vwt_ref, key_ref, m_ref, acc_ref, sa_ref, sb_ref, ocmp_ref, qsel_ref, qwin_ref,
               gt_ref, *, tq, t_len):
    kv_head = pl.program_id(0)
    qi = pl.program_id(2)
    nc = t_len // CMP_STRIDE
    ns = t_len // SLC_LEN
    rows = GQA_GROUP * tq
    blocks_per_tile = tq // SLC_LEN

    @pl.when(qi == 0)
    def _():
        k = kvs_ref[0][:, :HEAD_DIM]
        blk = lax.broadcasted_iota(jnp.int32, (t_len, HEAD_DIM), 0) // SLC_LEN
        col = lax.broadcasted_iota(jnp.int32, (t_len, HEAD_DIM), 1)
        onehot = jnp.where(blk == col, 1.0, 0.0).astype(BF16)
        kp_ref[...] = jnp.concatenate([k, onehot], axis=1)
        ones_rows = jnp.where(lax.broadcasted_iota(jnp.int32, (VT_ROWS - HEAD_DIM, tq), 0) == 0, 1.0, 0.0)
        for c in range(t_len // tq):
            st = kvs_ref[0, c * tq:(c + 1) * tq, :].astype(F32).T[HEAD_DIM:]
            vst_ref[c] = jnp.concatenate([st, ones_rows], axis=0).astype(BF16)
            wt = kvw_ref[0, c * tq:(c + 1) * tq, :].astype(F32).T[HEAD_DIM:]
            vwt_ref[c] = jnp.concatenate([wt, ones_rows], axis=0).astype(BF16)

    q_t = q_ref[0].astype(F32).T
    q4_t = jnp.concatenate([q_t[g * HEAD_DIM:(g + 1) * HEAD_DIM] for g in range(GQA_GROUP)],
                           axis=1).astype(BF16)
    qwin_ref[...] = jnp.concatenate([q4_t, jnp.zeros((HEAD_DIM, rows), BF16)], axis=0)
    qsel_ref[0:HEAD_DIM, :] = q4_t
    kw_ref = kvw_ref.at[0]
    win, sel = 0, 1

    def cmp_scores():
        bias_row = pl.multiple_of(nc - qi * (tq // CMP_STRIDE), tq // CMP_STRIDE)
        return (jnp.dot(kvc_ref[0, 0], qwin_ref[...], preferred_element_type=F32)
                + bc_ref[0, pl.ds(bias_row, nc), :])

    def cmp_finish(s):
        m = jnp.max(s, axis=0, keepdims=True)
        p = jnp.exp2(s - m)
        l = jnp.sum(p, axis=0, keepdims=True)
        pn = p * jnp.where(m > 0.5 * MASK_VALUE, 1.0 / l, 0.0)
        ocmp_ref[...] = jnp.dot(kvct_ref[0, 0, HEAD_DIM:, :], pn.astype(BF16),
                                preferred_element_type=F32)

        ps = pn[:, 0:tq] + pn[:, tq:2 * tq] + pn[:, 2 * tq:3 * tq] + pn[:, 3 * tq:4 * tq]
        sj = lax.broadcasted_iota(jnp.int32, (HEAD_DIM, nc), 0) * SLC_LEN
        ci = lax.broadcasted_iota(jnp.int32, (HEAD_DIM, nc), 1) * CMP_STRIDE
        overlap = jnp.where(ci < sj + SLC_LEN, jnp.where(ci + CMP_LEN > sj, 1.0, 0.0), 0.0).astype(BF16)
        p_hi = ps.astype(BF16)
        r_hi = ps - p_hi.astype(F32)
        p_md = r_hi.astype(BF16)
        p_lo = (r_hi - p_md.astype(F32)).astype(BF16)
        imp = (jnp.dot(overlap, p_hi, preferred_element_type=F32)
               + jnp.dot(overlap, p_md, preferred_element_type=F32)
               + jnp.dot(overlap, p_lo, preferred_element_type=F32))
        t = qi * tq + lax.broadcasted_iota(jnp.int32, (HEAD_DIM, tq), 1)
        blk = lax.broadcasted_iota(jnp.int32, (HEAD_DIM, tq), 0)
        cur = t // SLC_LEN
        key = jnp.where(blk == 0, KEY_BIG, jnp.where(blk == cur, KEY_BIG, jnp.where(blk == cur - 1, KEY_BIG, imp)))
        key_ref[...] = jnp.where(blk * SLC_LEN <= t, key, -KEY_BIG)

    def select_blocks():
        key = key_ref[...]
        blk = lax.broadcasted_iota(jnp.int32, (HEAD_DIM, tq), 0)
        pen = jnp.full((HEAD_DIM, tq), BLOCK_PENALTY, F32)
        for _ in range(min(SLC_TOPK, ns)):
            top = jnp.max(key, axis=0, keepdims=True)
            first = jnp.min(jnp.where(key == top, blk, HEAD_DIM), axis=0, keepdims=True)
            hit = blk == first
            pen = jnp.where(hit, 0.0, pen)
            key = jnp.where(hit, KEY_TAKEN, key)
        qsel_ref[HEAD_DIM:, :] = jnp.concatenate([pen.astype(BF16)] * GQA_GROUP, axis=1)

    def scores(qx_ref, k_ref, kj, bias_off):
        start = pl.multiple_of(kj * tq, tq)
        sc = jnp.dot(k_ref[pl.ds(start, tq), :], qx_ref[...], preferred_element_type=F32)
        if bias_off is not None:
            sc = sc + bt_ref[0, bias_off:bias_off + tq, :]
        return sc

    def consume(sc, vt_ref, kj, br, st, first):
        m_tile = jnp.max(sc, axis=0, keepdims=True)
        if first:
            m_new = m_tile
            pr = jnp.exp2(sc - m_new).astype(BF16)
            acc_ref[br, st] = jnp.dot(vt_ref[kj], pr, preferred_element_type=F32)
        else:
            m_old = m_ref[br, st]
            m_new = jnp.maximum(m_old, m_tile)
            alpha = jnp.exp2(m_old - m_new)
            pr = jnp.exp2(sc - m_new).astype(BF16)
            acc_ref[br, st] = alpha * acc_ref[br, st] + jnp.dot(vt_ref[kj], pr, preferred_element_type=F32)
        m_ref[br, st] = m_new

    def finish(br):
        m0 = m_ref[br, 0]
        m1 = m_ref[br, 1]
        m_all = jnp.maximum(m0, m1)
        acc = jnp.exp2(m0 - m_all) * acc_ref[br, 0] + jnp.exp2(m1 - m_all) * acc_ref[br, 1]
        return acc[0:HEAD_DIM] * (1.0 / acc[HEAD_DIM:HEAD_DIM + 1, :])

    def combine():
        o_win = finish(win)
        o_sel = finish(sel)
        o_cmp = ocmp_ref[...]
        gt_ref[...] = gate_ref[0].T
        base = kv_head * (N_BRANCH * GQA_GROUP)

        def gate(br, g):
            return gt_ref[pl.ds(base + br * GQA_GROUP + g, 1), :]

        combs = []
        for g in range(GQA_GROUP):
            sl = slice(g * tq, (g + 1) * tq)
            combs.append(gate(0, g) * o_cmp[:, sl] + gate(1, g) * o_sel[:, sl] + gate(2, g) * o_win[:, sl])
        outs = [jnp.concatenate(combs[p:p + 2], axis=0).T for p in range(0, GQA_GROUP, 2)]
        o_ref[0] = jnp.concatenate(outs, axis=1).astype(o_ref.dtype)

    @pl.when(qi >= 2)
    def _():
        sc_cmp = cmp_scores()
        sw0 = scores(qwin_ref, kw_ref, qi, WINDOW)
        cmp_finish(sc_cmp)
        sw1 = scores(qwin_ref, kw_ref, qi - 1, WINDOW - tq)
        consume(sw0, vwt_ref, qi, win, 0, True)
        sw2 = scores(qwin_ref, kw_ref, qi - 2, WINDOW - 2 * tq)
        consume(sw1, vwt_ref, qi - 1, win, 1, True)
        select_blocks()
        consume(sw2, vwt_ref, qi - 2, win, 0, False)

        ss0 = scores(qsel_ref, kp_ref, qi, WINDOW)
        ss1 = scores(qsel_ref, kp_ref, qi - 1, WINDOW - tq)
        consume(ss0, vst_ref, qi, sel, 0, True)
        sa_ref[...] = scores(qsel_ref, kp_ref, 0, None)
        consume(ss1, vst_ref, qi - 1, sel, 1, True)
        n_far = qi - 1

        def far_pair(pi):
            sb_ref[...] = scores(qsel_ref, kp_ref, 2 * pi + 1, None)
            consume(sa_ref[...], vst_ref, 2 * pi, sel, 1, False)
            sa_ref[...] = scores(qsel_ref, kp_ref, jnp.minimum(2 * pi + 2, n_far - 1), None)
            consume(sb_ref[...], vst_ref, 2 * pi + 1, sel, 0, False)

        n_pairs = n_far // 2

        def far_group(gi, carry):
            for u in range(FAR_UNROLL):
                far_pair(FAR_UNROLL * gi + u)
            return carry

        lax.fori_loop(0, n_pairs // FAR_UNROLL, far_group, 0)
        done = (n_pairs // FAR_UNROLL) * FAR_UNROLL
        left = n_pairs - done
        size = FAR_UNROLL // 2
        while size >= 1:
            @pl.when(left % (2 * size) >= size)
            def _(size=size, base=done):
                for u in range(size):
                    far_pair(base + u)
            done = done + jnp.where(left % (2 * size) >= size, size, 0)
            size //= 2

        @pl.when(n_far % 2 == 1)
        def _():
            consume(sa_ref[...], vst_ref, n_far - 1, sel, 1, False)

        combine()

    @pl.when(qi < 2)
    def _():
        cmp_finish(cmp_scores())
        select_blocks()
        sw0 = scores(qwin_ref, kw_ref, qi, WINDOW)
        ss0 = scores(qsel_ref, kp_ref, qi, WINDOW)
        consume(sw0, vwt_ref, qi, win, 0, True)
        consume(ss0, vst_ref, qi, sel, 0, True)
        for br in (win, sel):
            m_ref[br, 1] = jnp.full((1, rows), MASK_VALUE, F32)
            acc_ref[br, 1] = jnp.zeros((VT_ROWS, rows), F32)

        @pl.when(qi == 1)
        def _():
            sw1 = scores(qwin_ref, kw_ref, 0, WINDOW - tq)
            ss1 = scores(qsel_ref, kp_ref, 0, WINDOW - tq)
            consume(sw1, vwt_ref, 0, win, 1, False)
            consume(ss1, vst_ref, 0, sel, 1, False)

        combine()


def _attention(proj3, kvc, kvct, band, cmpb, gates3, *, tq):
    b, t_len, _ = proj3.shape
    assert WINDOW == 2 * tq
    nc = t_len // CMP_STRIDE
    rows = GQA_GROUP * tq
    qw = GQA_GROUP * HEAD_DIM
    slc_blk = COL_KV // KV_PAIR + N_KV_HEADS
    win_blk = COL_KV // KV_PAIR + 2 * N_KV_HEADS
    return pl.pallas_call(
        functools.partial(_attn_body, tq=tq, t_len=t_len),
        grid=(N_KV_HEADS, b, t_len // tq),
        in_specs=[
            pl.BlockSpec((1, tq, qw), lambda h, i, q: (i, q, h)),
            pl.BlockSpec((1, 1, nc, KV_PAIR), lambda h, i, q: (i, h, 0, 0)),
            pl.BlockSpec((1, 1, KV_PAIR, nc), lambda h, i, q: (i, h, 0, 0)),
            pl.BlockSpec((1, t_len, KV_PAIR), lambda h, i, q: (i, 0, slc_blk + h)),
            pl.BlockSpec((1, t_len, KV_PAIR), lambda h, i, q: (i, 0, win_blk + h)),
            pl.BlockSpec((1, WINDOW + tq, rows), lambda h, i, q: (h, 0, 0)),
            pl.BlockSpec((1, 2 * nc, rows), lambda h, i, q: (h, 0, 0)),
            pl.BlockSpec((1, tq, LANES), lambda h, i, q: (i, q, 0)),
        ],
        out_specs=pl.BlockSpec((1, tq, qw), lambda h, i, q: (i, q, h)),
        out_shape=jax.ShapeDtypeStruct((b, t_len, ATTN_WIDTH), BF16),
        scratch_shapes=[
            pltpu.VMEM((t_len, KV_PAIR), BF16),
            pltpu.VMEM((t_len // tq, VT_ROWS, tq), BF16),
            pltpu.VMEM((t_len // tq, VT_ROWS, tq), BF16),
            pltpu.VMEM((HEAD_DIM, tq), F32),
            pltpu.VMEM((2, 2, 1, rows), F32),
            pltpu.VMEM((2, 2, VT_ROWS, rows), F32),
            pltpu.VMEM((tq, rows), F32),
            pltpu.VMEM((tq, rows), F32),
            pltpu.VMEM((HEAD_DIM, rows), F32),
            pltpu.VMEM((KV_PAIR, rows), BF16),
            pltpu.VMEM((KV_PAIR, rows), BF16),
            pltpu.VMEM((LANES, tq), F32),
        ],
        compiler_params=_params("arbitrary", "arbitrary", "arbitrary"),
        name="nsa_attention",
    )(proj3, kvc, kvct, proj3, proj3, band, cmpb, gates3)


def _mix_out_body(a_ref, g_ref, ah_ref, gh_ref, w_ref, b_ref, lg_ref, lb_ref, attn_ref, x_ref, wa_ref, wc_ref,
                  o_ref, u_ref, y_ref, *, tt):
    ti = pl.program_id(1)
    acc_attn = jnp.dot(attn_ref[0], wa_ref[...], preferred_element_type=F32)
    u_ref[CONV_HALO:, :] = a_ref[0].astype(F32) * _sigmoid(g_ref[0].astype(F32))
    halo = ah_ref[0].astype(F32) * _sigmoid(gh_ref[0].astype(F32))
    u_ref[0:CONV_HALO, :] = jnp.where(ti > 0, halo, 0.0)
    base = CONV_HALO - (CONV_TAPS - 1)
    ext = CONV_ROWS + CONV_HALO
    for rb in range(tt // CONV_ROWS):
        rs = slice(rb * CONV_ROWS, (rb + 1) * CONV_ROWS)
        for cb in range(CONV_CHANNELS // LANES):
            cs = slice(cb * LANES, (cb + 1) * LANES)
            wblk = w_ref[:, cs]
            ublk = u_ref[rb * CONV_ROWS:rb * CONV_ROWS + ext, cs]
            acc = jnp.zeros((CONV_ROWS, LANES), F32)
            for r in range(SUBLANES):
                ur = ublk if r == 0 else pltpu.roll(ublk, ext - r, axis=0)
                for a in range(CONV_HALO // SUBLANES + 1):
                    k = SUBLANES * a + r - base
                    if 0 <= k < CONV_TAPS:
                        acc = acc + wblk[k:k + 1, :] * ur[SUBLANES * a:SUBLANES * a + CONV_ROWS]
            y_ref[rs, cs] = acc
        acc = y_ref[rs, :] + b_ref[...]
        mu = jnp.mean(acc, axis=-1, keepdims=True)
        xc = acc - mu
        var = jnp.mean(xc * xc, axis=-1, keepdims=True)
        y = xc * lax.rsqrt(var + NORM_EPS) * lg_ref[...] + lb_ref[...]
        conv_act = (y * _sigmoid(y)).astype(BF16)
        o_ref[0, rs, :] = (x_ref[0, rs, :] + acc_attn[rs]
                           + jnp.dot(conv_act, wc_ref[...], preferred_element_type=F32))


def _mix_out(proj3, w, b, lg, lb, attn, x3, w_out, *, layer, tt):
    bsz, t_len, _ = proj3.shape
    a_blk = COL_CONV_A // CONV_CHANNELS
    g_blk = COL_CONV_G // CONV_CHANNELS
    hpt = tt // CONV_HALO
    halo_idx = lambda i, t: jnp.maximum(t * hpt - 1, 0)
    vec = pl.BlockSpec((1, CONV_CHANNELS), lambda i, t: (0, 0))
    conv_rows = ATTN_WIDTH // CONV_CHANNELS
    return pl.pallas_call(
        functools.partial(_mix_out_body, tt=tt),
        grid=(bsz, t_len // tt),
        in_specs=[
            pl.BlockSpec((1, tt, CONV_CHANNELS), lambda i, t: (i, t, a_blk)),
            pl.BlockSpec((1, tt, CONV_CHANNELS), lambda i, t: (i, t, g_blk)),
            pl.BlockSpec((1, CONV_HALO, CONV_CHANNELS), lambda i, t: (i, halo_idx(i, t), a_blk)),
            pl.BlockSpec((1, CONV_HALO, CONV_CHANNELS), lambda i, t: (i, halo_idx(i, t), g_blk)),
            pl.BlockSpec((CONV_TAPS, CONV_CHANNELS), lambda i, t: (0, 0)),
            vec, vec, vec,
            pl.BlockSpec((1, tt, ATTN_WIDTH), lambda i, t: (i, t, 0)),
            pl.BlockSpec((1, tt, D_MODEL), lambda i, t: (i, t, 0)),
            pl.BlockSpec((None, ATTN_WIDTH, D_MODEL), lambda i, t: (layer, 0, 0)),
            pl.BlockSpec((None, CONV_CHANNELS, D_MODEL), lambda i, t: (layer, conv_rows, 0)),
        ],
        out_specs=pl.BlockSpec((1, tt, D_MODEL), lambda i, t: (i, t, 0)),
        out_shape=jax.ShapeDtypeStruct((bsz, t_len, D_MODEL), F32),
        scratch_shapes=[pltpu.VMEM((CONV_HALO + tt, CONV_CHANNELS), F32),
                        pltpu.VMEM((tt, CONV_CHANNELS), F32)],
        compiler_params=_params("arbitrary", "arbitrary"),
        name="mix_out",
    )(proj3, proj3, proj3, proj3, w, b, lg, lb, attn, x3, w_out, w_out)


def _ffn_up_body(x_ref, xh_ref, g_ref, wa_ref, wg_ref, cwa_ref, cwg_ref, cba_ref, cbg_ref, o_ref,
                 h_ref, ua_ref, ug_ref, *, tm, tiles_per_seq, rows):
    i = pl.program_id(0)

    @pl.when(pl.program_id(1) == 0)
    def _():
        h_ref[FFN_HALO:, :] = _rms_norm_rows(x_ref[...], g_ref[...]).astype(BF16)
        hh = _rms_norm_rows(xh_ref[...], g_ref[...])
        h_ref[0:FFN_HALO, :] = jnp.where(i % tiles_per_seq != 0, hh, 0.0).astype(BF16)

    units = [(u, c, slice(w, w + FFN_COLS)) for u, (c, w) in enumerate(
        (c, w) for c in range(tm // rows) for w in range(0, o_ref.shape[1], FFN_COLS))]

    def project(u, c, cs):
        hc = h_ref[c * rows:(c + 1) * rows + FFN_HALO, :]
        ug_ref[u] = jnp.dot(hc, wg_ref[:, cs], preferred_element_type=F32)
        ua_ref[u] = jnp.dot(hc, wa_ref[:, cs], preferred_element_type=F32)

    def conv(u_ref, u, cs, cw_ref, cb_ref):
        x = u_ref[u]
        y = cw_ref[FFN_CONV_TAPS - 1:FFN_CONV_TAPS, cs] * x[FFN_HALO:]
        for s in range(1, FFN_CONV_TAPS):
            k = FFN_CONV_TAPS - 1 - s
            y = y + cw_ref[k:k + 1, cs] * pltpu.roll(x, s, axis=0)[FFN_HALO:]
        return y + cb_ref[:, cs]

    def finish(u, c, cs):
        gate = conv(ug_ref, u, cs, cwg_ref, cbg_ref)
        act = gate * _sigmoid(gate)
        a = conv(ua_ref, u, cs, cwa_ref, cba_ref)
        o_ref[c * rows:(c + 1) * rows, cs] = (act * a).astype(o_ref.dtype)

    project(*units[0])
    for i_unit, unit in enumerate(units):
        if i_unit + 1 < len(units):
            project(*units[i_unit + 1])
        finish(*unit)


def _ffn_up(x2, g, w_up, cw, cb, *, layer, tm, tn, t_len):
    n = x2.shape[0]
    nj = D_FF // tn
    hpt = tm // FFN_HALO
    rows = min(FFN_ROWS, tm)
    return pl.pallas_call(
        functools.partial(_ffn_up_body, tm=tm, tiles_per_seq=t_len // tm, rows=rows),
        grid=(n // tm, nj),
        in_specs=[
            pl.BlockSpec((tm, D_MODEL), lambda i, j: (i, 0)),
            pl.BlockSpec((FFN_HALO, D_MODEL), lambda i, j: (jnp.maximum(i * hpt - 1, 0), 0)),
            pl.BlockSpec((1, D_MODEL), lambda i, j: (0, 0)),
            pl.BlockSpec((None, D_MODEL, tn), lambda i, j: (layer, 0, j)),
            pl.BlockSpec((None, D_MODEL, tn), lambda i, j: (layer, 0, j + nj)),
            pl.BlockSpec((FFN_CONV_TAPS, tn), lambda i, j: (0, j)),
            pl.BlockSpec((FFN_CONV_TAPS, tn), lambda i, j: (0, j + nj)),
            pl.BlockSpec((1, tn), lambda i, j: (0, j)),
            pl.BlockSpec((1, tn), lambda i, j: (0, j + nj)),
        ],
        out_specs=pl.BlockSpec((tm, tn), lambda i, j: (i, j)),
        out_shape=jax.ShapeDtypeStruct((n, D_FF), BF16),
        scratch_shapes=[
            pltpu.VMEM((FFN_HALO + tm, D_MODEL), BF16),
            pltpu.VMEM((tm // rows * (tn // FFN_COLS), FFN_HALO + rows, FFN_COLS), F32),
            pltpu.VMEM((tm // rows * (tn // FFN_COLS), FFN_HALO + rows, FFN_COLS), F32),
        ],
        compiler_params=_params("arbitrary", "arbitrary"),
        name="ffn_up",
    )(x2, x2, g, w_up, w_up, cw, cw, cb, cb)


def _ffn_down_body(act_ref, w_ref, x_ref, o_ref):
    o_ref[...] = x_ref[...] + jnp.dot(act_ref[...], w_ref[...], preferred_element_type=F32)


def _ffn_down(act, w_down, x2, *, layer, tm, tn):
    n = x2.shape[0]
    return pl.pallas_call(
        _ffn_down_body,
        grid=(n // tm, D_MODEL // tn),
        in_specs=[
            pl.BlockSpec((tm, D_FF), lambda i, j: (i, 0)),
            pl.BlockSpec((None, D_FF, tn), lambda i, j: (layer, 0, j)),
            pl.BlockSpec((tm, tn), lambda i, j: (i, j)),
        ],
        out_specs=pl.BlockSpec((tm, tn), lambda i, j: (i, j)),
        out_shape=jax.ShapeDtypeStruct((n, D_MODEL), F32),
        compiler_params=_params("arbitrary", "arbitrary"),
        name="ffn_down",
    )(act, w_down, x2)


def _final_norm_body(x_ref, g_ref, o_ref):
    o_ref[...] = _rms_norm_rows(x_ref[...], g_ref[...])


def _final_norm(x2, g, *, tm):
    n = x2.shape[0]
    return pl.pallas_call(
        _final_norm_body,
        grid=(n // tm,),
        in_specs=[pl.BlockSpec((tm, D_MODEL), lambda i: (i, 0)), pl.BlockSpec((1, D_MODEL), lambda i: (0, 0))],
        out_specs=pl.BlockSpec((tm, D_MODEL), lambda i: (i, 0)),
        out_shape=jax.ShapeDtypeStruct((n, D_MODEL), F32),
        compiler_params=_params("arbitrary"),
        name="final_norm",
    )(x2, g)


def _in_weight_layout(w_in):
    depth = w_in.shape[0]
    kv0 = ATTN_WIDTH
    gate0 = kv0 + KV_COLS
    conv0 = gate0 + GATE_COLS
    w_t = jnp.swapaxes(w_in, 1, 2)
    parts = [w_t[:, :ATTN_WIDTH], w_t[:, conv0:conv0 + 2 * CONV_CHANNELS]]
    for br in range(N_BRANCH):
        for h in range(N_KV_HEADS):
            for kv in range(2):
                c0 = kv0 + ((br * 2 + kv) * N_KV_HEADS + h) * HEAD_DIM
                parts.append(w_t[:, c0:c0 + HEAD_DIM])
    w_main = jnp.concatenate(parts, axis=1).astype(BF16)
    wg = w_in[..., gate0:gate0 + GATE_COLS].reshape(depth, D_MODEL, N_KV_HEADS, GQA_GROUP, N_BRANCH)
    wg = wg.transpose(0, 1, 2, 4, 3).reshape(depth, D_MODEL, N_KV_HEADS, N_BRANCH * GQA_GROUP)
    wg = wg.reshape(depth, D_MODEL, GATE_COLS)
    wg = jnp.pad(wg, ((0, 0), (0, 0), (0, LANES - GATE_COLS)))
    return w_main, wg.astype(BF16)


def _pick(n, prefs):
    for p in prefs:
        if n % p == 0:
            return p
    return n


def kernel(x, rel_bias, mix_norm_g, w_in, cmp_pos, cmp_w1, cmp_w2, conv_w, conv_b, conv_ln_g, conv_ln_b,
           w_out, ffn_norm_g, w_up, ffn_conv_w, ffn_conv_b, w_down, final_norm_g):
    bsz, t_len, _ = x.shape
    depth = w_in.shape[0]
    n = bsz * t_len
    assert t_len % 256 == 0 and t_len // SLC_LEN <= HEAD_DIM
    tq = 256
    tm = _pick(t_len, (1024, 512, 256))
    nc = t_len // CMP_STRIDE

    band, cmpb = _bias_tiles(rel_bias, t_len, tq)
    col_scale = jnp.concatenate([jnp.full((1, ATTN_WIDTH), Q_SCALE, F32),
                                 jnp.ones((1, PROJ_COLS - ATTN_WIDTH), F32)], axis=1)
    x2 = x.reshape(n, D_MODEL)
    w_main, wg = _in_weight_layout(w_in)
    w_out_b, w_up_b, w_down_b = w_out.astype(BF16), w_up.astype(BF16), w_down.astype(BF16)
    for l in range(depth):
        proj, gates = _in_projection(x2, mix_norm_g[l][None, :], w_main, wg, col_scale, layer=l, tm=tm, tn=1536)
        proj3 = proj.reshape(bsz, t_len, PROJ_COLS)
        gates3 = gates.reshape(bsz, t_len, LANES)

        posx, w1big, w2big = _compress_weight_layout(cmp_pos[l], cmp_w1[l], cmp_w2[l])
        tokl = proj3[:, :, COL_KV:COL_KV + CMP_WIDTH].reshape(bsz, nc, CMP_STRIDE, CMP_WIDTH).transpose(0, 2, 1, 3)
        kvc, kvct = _compress(tokl, posx, w1big, w2big)

        attn = _attention(proj3, kvc, kvct, band, cmpb, gates3, tq=tq)
        x2 = _mix_out(proj3, conv_w[l], conv_b[l][None, :], conv_ln_g[l][None, :], conv_ln_b[l][None, :],
                      attn, x2.reshape(bsz, t_len, D_MODEL), w_out_b, layer=l, tt=512).reshape(n, D_MODEL)

        act = _ffn_up(x2, ffn_norm_g[l][None, :], w_up_b, ffn_conv_w[l],
                      ffn_conv_b[l][None, :], layer=l, tm=tm, tn=512, t_len=t_len)
        x2 = _ffn_down(act, w_down_b, x2, layer=l, tm=tm, tn=512)
    x2 = _final_norm(x2, final_norm_g[None, :], tm=_pick(n, (512, 256)))
    return x2.reshape(bsz, t_len, D_MODEL)
```

```python
import functools
import math

import numpy as np
import jax
import jax.numpy as jnp
from jax import lax
from jax.experimental import pallas as pl
from jax.experimental.pallas import tpu as pltpu

F32 = jnp.float32
BF16 = jnp.bfloat16

D_MODEL = 2048
HEAD_DIM = 64
N_KV_HEADS = 4
GQA_GROUP = 4
N_Q_HEADS = N_KV_HEADS * GQA_GROUP
N_BRANCH = 3
ATTN_WIDTH = N_Q_HEADS * HEAD_DIM
CMP_LEN = 32
CMP_STRIDE = 16
SLC_LEN = 64
SLC_TOPK = 16
WINDOW = 512
CONV_CHANNELS = D_MODEL - ATTN_WIDTH
CONV_TAPS = 31
D_FF = 5632
FFN_CONV_TAPS = 3
N_BUCKETS = 32
MAX_DISTANCE = 128
NORM_EPS = 1e-6

KV_PAIR = 2 * HEAD_DIM
KV_COLS = N_BRANCH * N_KV_HEADS * KV_PAIR
GATE_COLS = N_BRANCH * N_Q_HEADS
COL_Q = 0
COL_CONV_A = ATTN_WIDTH
COL_CONV_G = COL_CONV_A + CONV_CHANNELS
COL_KV = COL_CONV_G + CONV_CHANNELS
PROJ_COLS = COL_KV + KV_COLS

MASK_VALUE = -1e30
BLOCK_PENALTY = -1e9
KEY_BIG = 1e30
KEY_TAKEN = -3e38
LOG2E = math.log2(math.e)
Q_SCALE = HEAD_DIM ** -0.5 * LOG2E

LANES = 128
SUBLANES = 8
CONV_ROWS = 128
FFN_ROWS = 1024
FFN_COLS = 512
VMEM_LIMIT = 56 * 1024 * 1024
CONV_HALO = 32
FFN_HALO = 16


def _t5_bucket_last_distance():
    n = np.arange(0, 4 * MAX_DISTANCE, dtype=np.int64)
    max_exact = N_BUCKETS // 2
    nf = np.maximum(n, 1).astype(np.float64)
    large = max_exact + np.floor(np.log(nf / max_exact) / math.log(MAX_DISTANCE / max_exact)
                                 * (N_BUCKETS - max_exact)).astype(np.int64)
    large = np.minimum(large, N_BUCKETS - 1)
    bucket = np.where(n < max_exact, n, large)
    last = []
    for b in range(N_BUCKETS - 1):
        idx = np.nonzero(bucket == b)[0]
        last.append(int(idx.max()) if idx.size else None)
    return last


_BUCKET_LAST = _t5_bucket_last_distance()


def _params(*sem):
    return pltpu.CompilerParams(dimension_semantics=sem, vmem_limit_bytes=VMEM_LIMIT)


def _sigmoid(x):
    return jax.nn.sigmoid(x)


def _rms_norm_rows(x, g):
    ms = jnp.mean(x * x, axis=-1, keepdims=True)
    return x * lax.rsqrt(ms + NORM_EPS) * g


def _bias_from_distance(dist, tab_ref, head):
    c_far = tab_ref[N_BUCKETS - 1, head]
    val = jnp.zeros(dist.shape, F32)
    for b in range(N_BUCKETS - 2, -1, -1):
        if _BUCKET_LAST[b] is None:
            continue
        val = jnp.where(dist <= _BUCKET_LAST[b], tab_ref[b, head] - c_far, val)
    return val


def _band_bias_body(tab_ref, o_ref, *, tq):
    head = pl.program_id(0) * GQA_GROUP + pl.program_id(1)
    shape = (WINDOW + tq, tq)
    dist = (lax.broadcasted_iota(jnp.int32, shape, 1) + WINDOW
            - lax.broadcasted_iota(jnp.int32, shape, 0))
    val = _bias_from_distance(dist, tab_ref, head) * LOG2E
    val = jnp.where(dist >= 0, jnp.where(dist < WINDOW, val, MASK_VALUE), MASK_VALUE)
    o_ref[0] = val


def _cmp_bias_body(tab_ref, o_ref, *, tq, nc):
    head = pl.program_id(0) * GQA_GROUP + pl.program_id(1)
    shape = (2 * nc, tq)
    r = lax.broadcasted_iota(jnp.int32, shape, 0)
    i = lax.broadcasted_iota(jnp.int32, shape, 1)
    dist = i - (r - nc) * CMP_STRIDE - (CMP_LEN - 1)
    val = _bias_from_distance(dist, tab_ref, head) * LOG2E
    o_ref[0] = jnp.where(dist >= 0, val, MASK_VALUE)


def _bias_tiles(rel_bias, t_len, tq):
    nc = t_len // CMP_STRIDE
    rows = GQA_GROUP * tq
    smem = pl.BlockSpec(memory_space=pltpu.SMEM)
    band = pl.pallas_call(
        functools.partial(_band_bias_body, tq=tq),
        grid=(N_KV_HEADS, GQA_GROUP),
        in_specs=[smem],
        out_specs=pl.BlockSpec((1, WINDOW + tq, tq), lambda h, g: (h, 0, g)),
        out_shape=jax.ShapeDtypeStruct((N_KV_HEADS, WINDOW + tq, rows), F32),
        compiler_params=_params("arbitrary", "arbitrary"),
        name="band_bias",
    )(rel_bias)
    cmpb = pl.pallas_call(
        functools.partial(_cmp_bias_body, tq=tq, nc=nc),
        grid=(N_KV_HEADS, GQA_GROUP),
        in_specs=[smem],
        out_specs=pl.BlockSpec((1, 2 * nc, tq), lambda h, g: (h, 0, g)),
        out_shape=jax.ShapeDtypeStruct((N_KV_HEADS, 2 * nc, rows), F32),
        compiler_params=_params("arbitrary", "arbitrary"),
        name="cmp_bias",
    )(rel_bias)
    return band, cmpb


def _inproj_body(x_ref, g_ref, w_ref, wg_ref, cs_ref, o_ref, gate_ref, h_ref):
    @pl.when(pl.program_id(1) == 0)
    def _():
        h = _rms_norm_rows(x_ref[...], g_ref[...]).astype(BF16)
        h_ref[...] = h
        gate_ref[...] = _sigmoid(jnp.dot(h, wg_ref[...], preferred_element_type=F32))

    acc = lax.dot_general(h_ref[...], w_ref[...], (((1,), (1,)), ((), ())),
                          preferred_element_type=F32)
    o_ref[...] = (acc * cs_ref[...]).astype(o_ref.dtype)


def _in_projection(x2, g, w, wg, col_scale, *, layer, tm, tn):
    n = x2.shape[0]
    gcols = wg.shape[-1]
    return pl.pallas_call(
        _inproj_body,
        grid=(n // tm, PROJ_COLS // tn),
        in_specs=[
            pl.BlockSpec((tm, D_MODEL), lambda i, j: (i, 0)),
            pl.BlockSpec((1, D_MODEL), lambda i, j: (0, 0)),
            pl.BlockSpec((None, tn, D_MODEL), lambda i, j: (layer, j, 0)),
            pl.BlockSpec((None, D_MODEL, gcols), lambda i, j: (layer, 0, 0)),
            pl.BlockSpec((1, tn), lambda i, j: (0, j)),
        ],
        out_specs=[
            pl.BlockSpec((tm, tn), lambda i, j: (i, j)),
            pl.BlockSpec((tm, gcols), lambda i, j: (i, 0)),
        ],
        out_shape=[
            jax.ShapeDtypeStruct((n, PROJ_COLS), BF16),
            jax.ShapeDtypeStruct((n, gcols), F32),
        ],
        scratch_shapes=[pltpu.VMEM((tm, D_MODEL), BF16)],
        compiler_params=_params("arbitrary", "arbitrary"),
        name="in_projection",
    )(x2, g, w, wg, col_scale)


CMP_WIDTH = N_KV_HEADS * KV_PAIR
CMP_TOKENS_PER_STEP = 4


def _compress_body(x_ref, pos_ref, w1_ref, w2_ref, o_ref, ot_ref, top_ref, bot_ref, *, nc):
    l = pl.program_id(1)

    @pl.when(l == 0)
    def _():
        top_ref[...] = jnp.zeros_like(top_ref)
        bot_ref[...] = jnp.zeros_like(bot_ref)

    top = top_ref[...]
    bot = bot_ref[...]
    for s in range(CMP_TOKENS_PER_STEP):
        x = x_ref[0, s].astype(F32)
        top = top + jnp.dot((x + pos_ref[s, 0]).astype(BF16), w1_ref[s, 0], preferred_element_type=F32)
        bot = bot + jnp.dot((x + pos_ref[s, 1]).astype(BF16), w1_ref[s, 1], preferred_element_type=F32)
    top_ref[...] = top
    bot_ref[...] = bot

    @pl.when(l == pl.num_programs(1) - 1)
    def _():
        pre = top_ref[...] + pltpu.roll(bot_ref[...], nc - 1, axis=0)
        act = pre * _sigmoid(pre)
        out = jnp.dot(act.astype(BF16), w2_ref[...], preferred_element_type=F32)
        for h in range(N_KV_HEADS):
            kv = out[:, h * KV_PAIR:(h + 1) * KV_PAIR]
            o_ref[0, h] = kv.astype(o_ref.dtype)
            ot_ref[0, h] = kv.T.astype(ot_ref.dtype)


def _compress(tokl, posx, w1big, w2big):
    b, _, nc, _ = tokl.shape
    return pl.pallas_call(
        functools.partial(_compress_body, nc=nc),
        grid=(b, CMP_STRIDE // CMP_TOKENS_PER_STEP),
        in_specs=[
            pl.BlockSpec((1, CMP_TOKENS_PER_STEP, nc, CMP_WIDTH), lambda i, l: (i, l, 0, 0)),
            pl.BlockSpec((CMP_TOKENS_PER_STEP, 2, 1, CMP_WIDTH), lambda i, l: (l, 0, 0, 0)),
            pl.BlockSpec((CMP_TOKENS_PER_STEP, 2, CMP_WIDTH, CMP_WIDTH), lambda i, l: (l, 0, 0, 0)),
            pl.BlockSpec((CMP_WIDTH, CMP_WIDTH), lambda i, l: (0, 0)),
        ],
        out_specs=[
            pl.BlockSpec((1, N_KV_HEADS, nc, KV_PAIR), lambda i, l: (i, 0, 0, 0)),
            pl.BlockSpec((1, N_KV_HEADS, KV_PAIR, nc), lambda i, l: (i, 0, 0, 0)),
        ],
        out_shape=[
            jax.ShapeDtypeStruct((b, N_KV_HEADS, nc, KV_PAIR), BF16),
            jax.ShapeDtypeStruct((b, N_KV_HEADS, KV_PAIR, nc), BF16),
        ],
        scratch_shapes=[pltpu.VMEM((nc, CMP_WIDTH), F32), pltpu.VMEM((nc, CMP_WIDTH), F32)],
        compiler_params=_params("arbitrary", "arbitrary"),
        name="compress",
    )(tokl, posx, w1big, w2big)


def _compress_weight_layout(pos, w1, w2):
    def block_diag(blocks):
        rows = []
        for h in range(N_KV_HEADS):
            for kv in range(2):
                off = (h * 2 + kv) * HEAD_DIM
                pad = [(0, 0)] * (blocks.ndim - 2) + [(off, CMP_WIDTH - HEAD_DIM - off)]
                rows.append(jnp.pad(blocks[kv], pad))
        return jnp.concatenate(rows, axis=-2).astype(BF16)

    w1r = w1.reshape(2, 2, CMP_STRIDE, HEAD_DIM, HEAD_DIM)
    w1big = block_diag(w1r.transpose(0, 2, 1, 3, 4))
    w2big = block_diag(w2)
    posr = pos.reshape(2, 2, CMP_STRIDE, HEAD_DIM).transpose(2, 1, 0, 3)
    posx = jnp.broadcast_to(posr[:, :, None], (CMP_STRIDE, 2, N_KV_HEADS, 2, HEAD_DIM))
    return posx.reshape(CMP_STRIDE, 2, 1, CMP_WIDTH), w1big, w2big


VT_ROWS = HEAD_DIM + 16
FAR_UNROLL = 4


def _attn_body(q_ref, kvc_ref, kvct_ref, kvs_ref, kvw_ref, bt_ref, bc_ref, gate_ref, o_ref,
               kp_ref, vst_ref, vwt_ref, key_ref, m_ref, acc_ref, sa_ref, sb_ref, ocmp_ref, qsel_ref, qwin_ref,
               gt_ref, *, tq, t_len):
    kv_head = pl.program_id(0)
    qi = pl.program_id(2)
    nc = t_len // CMP_STRIDE
    ns = t_len // SLC_LEN
    rows = GQA_GROUP * tq

    @pl.when(qi == 0)
    def _():
        k = kvs_ref[0][:, :HEAD_DIM]
        blk = lax.broadcasted_iota(jnp.int32, (t_len, HEAD_DIM), 0) // SLC_LEN
        col = lax.broadcasted_iota(jnp.int32, (t_len, HEAD_DIM), 1)
        onehot = jnp.where(blk == col, 1.0, 0.0).astype(BF16)
        kp_ref[...] = jnp.concatenate([k, onehot], axis=1)
        ones_rows = jnp.where(lax.broadcasted_iota(jnp.int32, (VT_ROWS - HEAD_DIM, tq), 0) == 0, 1.0, 0.0)
        for c in range(t_len // tq):
            st = kvs_ref[0, c * tq:(c + 1) * tq, :].astype(F32).T[HEAD_DIM:]
            vst_ref[c] = jnp.concatenate([st, ones_rows], axis=0).astype(BF16)
            wt = kvw_ref[0, c * tq:(c + 1) * tq, :].astype(F32).T[HEAD_DIM:]
            vwt_ref[c] = jnp.concatenate([wt, ones_rows], axis=0).astype(BF16)

    q_t = q_ref[0].astype(F32).T
    q4_t = jnp.concatenate([q_t[g * HEAD_DIM:(g + 1) * HEAD_DIM] for g in range(GQA_GROUP)],
                           axis=1).astype(BF16)
    qwin_ref[...] = jnp.concatenate([q4_t, jnp.zeros((HEAD_DIM, rows), BF16)], axis=0)
    qsel_ref[0:HEAD_DIM, :] = q4_t
    kw_ref = kvw_ref.at[0]
    win, sel = 0, 1

    def cmp_scores():
        bias_row = pl.multiple_of(nc - qi * (tq // CMP_STRIDE), tq // CMP_STRIDE)
        return (jnp.dot(kvc_ref[0, 0], qwin_ref[...], preferred_element_type=F32)
                + bc_ref[0, pl.ds(bias_row, nc), :])

    def cmp_finish(s):
        m = jnp.max(s, axis=0, keepdims=True)
        p = jnp.exp2(s - m)
        l = jnp.sum(p, axis=0, keepdims=True)
        pn = p * jnp.where(m > 0.5 * MASK_VALUE, 1.0 / l, 0.0)
        ocmp_ref[...] = jnp.dot(kvct_ref[0, 0, HEAD_DIM:, :], pn.astype(BF16),
                                preferred_element_type=F32)

        ps = pn[:, 0:tq] + pn[:, tq:2 * tq] + pn[:, 2 * tq:3 * tq] + pn[:, 3 * tq:4 * tq]
        sj = lax.broadcasted_iota(jnp.int32, (HEAD_DIM, nc), 0) * SLC_LEN
        ci = lax.broadcasted_iota(jnp.int32, (HEAD_DIM, nc), 1) * CMP_STRIDE
        overlap = jnp.where(ci < sj + SLC_LEN, jnp.where(ci + CMP_LEN > sj, 1.0, 0.0), 0.0).astype(BF16)
        p_hi = ps.astype(BF16)
        r_hi = ps - p_hi.astype(F32)
        p_md = r_hi.astype(BF16)
        p_lo = (r_hi - p_md.astype(F32)).astype(BF16)
        imp = (jnp.dot(overlap, p_hi, preferred_element_type=F32)
               + jnp.dot(overlap, p_md, preferred_element_type=F32)
               + jnp.dot(overlap, p_lo, preferred_element_type=F32))
        t = qi * tq + lax.broadcasted_iota(jnp.int32, (HEAD_DIM, tq), 1)
        blk = lax.broadcasted_iota(jnp.int32, (HEAD_DIM, tq), 0)
        cur = t // SLC_LEN
        key = jnp.where(blk == 0, KEY_BIG, jnp.where(blk == cur, KEY_BIG, jnp.where(blk == cur - 1, KEY_BIG, imp)))
        key_ref[...] = jnp.where(blk * SLC_LEN <= t, key, -KEY_BIG)

    def select_blocks():
        key = key_ref[...]
        blk = lax.broadcasted_iota(jnp.int32, (HEAD_DIM, tq), 0)
        pen = jnp.full((HEAD_DIM, tq), BLOCK_PENALTY, F32)
        for _ in range(min(SLC_TOPK, ns)):
            top = jnp.max(key, axis=0, keepdims=True)
            first = jnp.min(jnp.where(key == top, blk, HEAD_DIM), axis=0, keepdims=True)
            hit = blk == first
            pen = jnp.where(hit, 0.0, pen)
            key = jnp.where(hit, KEY_TAKEN, key)
        qsel_ref[HEAD_DIM:, :] = jnp.concatenate([pen.astype(BF16)] * GQA_GROUP, axis=1)

    def scores(qx_ref, k_ref, kj, bias_off):
        start = pl.multiple_of(kj * tq, tq)
        sc = jnp.dot(k_ref[pl.ds(start, tq), :], qx_ref[...], preferred_element_type=F32)
        if bias_off is not None:
            sc = sc + bt_ref[0, bias_off:bias_off + tq, :]
        return sc

    def consume(sc, vt_ref, kj, br, st, first):
        m_tile = jnp.max(sc, axis=0, keepdims=True)
        if first:
            m_new = m_tile
            pr = jnp.exp2(sc - m_new).astype(BF16)
            acc_ref[br, st] = jnp.dot(vt_ref[kj], pr, preferred_element_type=F32)
        else:
            m_old = m_ref[br, st]
            m_new = jnp.maximum(m_old, m_tile)
            alpha = jnp.exp2(m_old - m_new)
            pr = jnp.exp2(sc - m_new).astype(BF16)
            acc_ref[br, st] = alpha * acc_ref[br, st] + jnp.dot(vt_ref[kj], pr, preferred_element_type=F32)
        m_ref[br, st] = m_new

    def finish(br):
        m0 = m_ref[br, 0]
        m1 = m_ref[br, 1]
        m_all = jnp.maximum(m0, m1)
        acc = jnp.exp2(m0 - m_all) * acc_ref[br, 0] + jnp.exp2(m1 - m_all) * acc_ref[br, 1]
        return acc[0:HEAD_DIM] * (1.0 / acc[HEAD_DIM:HEAD_DIM + 1, :])

    def combine():
        o_win = finish(win)
        o_sel = finish(sel)
        o_cmp = ocmp_ref[...]
        gt_ref[...] = gate_ref[0].T
        base = kv_head * (N_BRANCH * GQA_GROUP)

        def gate(br, g):
            return gt_ref[pl.ds(base + br * GQA_GROUP + g, 1), :]

        combs = []
        for g in range(GQA_GROUP):
            sl = slice(g * tq, (g + 1) * tq)
            combs.append(gate(0, g) * o_cmp[:, sl] + gate(1, g) * o_sel[:, sl] + gate(2, g) * o_win[:, sl])
        outs = [jnp.concatenate(combs[p:p + 2], axis=0).T for p in range(0, GQA_GROUP, 2)]
        o_ref[0] = jnp.concatenate(outs, axis=1).astype(o_ref.dtype)

    @pl.when(qi >= 2)
    def _():
        sc_cmp = cmp_scores()
        sw0 = scores(qwin_ref, kw_ref, qi, WINDOW)
        cmp_finish(sc_cmp)
        sw1 = scores(qwin_ref, kw_ref, qi - 1, WINDOW - tq)
        consume(sw0, vwt_ref, qi, win, 0, True)
        sw2 = scores(qwin_ref, kw_ref, qi - 2, WINDOW - 2 * tq)
        consume(sw1, vwt_ref, qi - 1, win, 1, True)
        select_blocks()
        consume(sw2, vwt_ref, qi - 2, win, 0, False)

        ss0 = scores(qsel_ref, kp_ref, qi, WINDOW)
        ss1 = scores(qsel_ref, kp_ref, qi - 1, WINDOW - tq)
        consume(ss0, vst_ref, qi, sel, 0, True)
        sa_ref[...] = scores(qsel_ref, kp_ref, 0, None)
        consume(ss1, vst_ref, qi - 1, sel, 1, True)
        n_far = qi - 1

        def far_pair(pi):
            sb_ref[...] = scores(qsel_ref, kp_ref, 2 * pi + 1, None)
            consume(sa_ref[...], vst_ref, 2 * pi, sel, 1, False)
            sa_ref[...] = scores(qsel_ref, kp_ref, jnp.minimum(2 * pi + 2, n_far - 1), None)
            consume(sb_ref[...], vst_ref, 2 * pi + 1, sel, 0, False)

        n_pairs = n_far // 2

        def far_group(gi, carry):
            for u in range(FAR_UNROLL):
                far_pair(FAR_UNROLL * gi + u)
            return carry

        lax.fori_loop(0, n_pairs // FAR_UNROLL, far_group, 0)
        done = (n_pairs // FAR_UNROLL) * FAR_UNROLL
        left = n_pairs - done
        size = FAR_UNROLL // 2
        while size >= 1:
            @pl.when(left % (2 * size) >= size)
            def _(size=size, base=done):
                for u in range(size):
                    far_pair(base + u)
            done = done + jnp.where(left % (2 * size) >= size, size, 0)
            size //= 2

        @pl.when(n_far % 2 == 1)
        def _():
            consume(sa_ref[...], vst_ref, n_far - 1, sel, 1, False)

        combine()

    @pl.when(qi < 2)
    def _():
        cmp_finish(cmp_scores())
        select_blocks()
        sw0 = scores(qwin_ref, kw_ref, qi, WINDOW)
        ss0 = scores(qsel_ref, kp_ref, qi, WINDOW)
        consume(sw0, vwt_ref, qi, win, 0, True)
        consume(ss0, vst_ref, qi, sel, 0, True)
        for br in (win, sel):
            m_ref[br, 1] = jnp.full((1, rows), MASK_VALUE, F32)
            acc_ref[br, 1] = jnp.zeros((VT_ROWS, rows), F32)

        @pl.when(qi == 1)
        def _():
            sw1 = scores(qwin_ref, kw_ref, 0, WINDOW - tq)
            ss1 = scores(qsel_ref, kp_ref, 0, WINDOW - tq)
            consume(sw1, vwt_ref, 0, win, 1, False)
            consume(ss1, vst_ref, 0, sel, 1, False)

        combine()


def _attention(proj3, kvc, kvct, band, cmpb, gates3, *, tq):
    b, t_len, _ = proj3.shape
    assert WINDOW == 2 * tq
    nc = t_len // CMP_STRIDE
    rows = GQA_GROUP * tq
    qw = GQA_GROUP * HEAD_DIM
    slc_blk = COL_KV // KV_PAIR + N_KV_HEADS
    win_blk = COL_KV // KV_PAIR + 2 * N_KV_HEADS
    return pl.pallas_call(
        functools.partial(_attn_body, tq=tq, t_len=t_len),
        grid=(N_KV_HEADS, b, t_len // tq),
        in_specs=[
            pl.BlockSpec((1, tq, qw), lambda h, i, q: (i, q, h)),
            pl.BlockSpec((1, 1, nc, KV_PAIR), lambda h, i, q: (i, h, 0, 0)),
            pl.BlockSpec((1, 1, KV_PAIR, nc), lambda h, i, q: (i, h, 0, 0)),
            pl.BlockSpec((1, t_len, KV_PAIR), lambda h, i, q: (i, 0, slc_blk + h)),
            pl.BlockSpec((1, t_len, KV_PAIR), lambda h, i, q: (i, 0, win_blk + h)),
            pl.BlockSpec((1, WINDOW + tq, rows), lambda h, i, q: (h, 0, 0)),
            pl.BlockSpec((1, 2 * nc, rows), lambda h, i, q: (h, 0, 0)),
            pl.BlockSpec((1, tq, LANES), lambda h, i, q: (i, q, 0)),
        ],
        out_specs=pl.BlockSpec((1, tq, qw), lambda h, i, q: (i, q, h)),
        out_shape=jax.ShapeDtypeStruct((b, t_len, ATTN_WIDTH), BF16),
        scratch_shapes=[
            pltpu.VMEM((t_len, KV_PAIR), BF16),
            pltpu.VMEM((t_len // tq, VT_ROWS, tq), BF16),
            pltpu.VMEM((t_len // tq, VT_ROWS, tq), BF16),
            pltpu.VMEM((HEAD_DIM, tq), F32),
            pltpu.VMEM((2, 2, 1, rows), F32),
            pltpu.VMEM((2, 2, VT_ROWS, rows), F32),
            pltpu.VMEM((tq, rows), F32),
            pltpu.VMEM((tq, rows), F32),
            pltpu.VMEM((HEAD_DIM, rows), F32),
            pltpu.VMEM((KV_PAIR, rows), BF16),
            pltpu.VMEM((KV_PAIR, rows), BF16),
            pltpu.VMEM((LANES, tq), F32),
        ],
        compiler_params=_params("arbitrary", "arbitrary", "arbitrary"),
        name="nsa_attention",
    )(proj3, kvc, kvct, proj3, proj3, band, cmpb, gates3)


def _mix_out_body(a_ref, g_ref, ah_ref, gh_ref, w_ref, b_ref, lg_ref, lb_ref, attn_ref, x_ref, wa_ref, wc_ref,
                  o_ref, u_ref, y_ref, *, tt):
    ti = pl.program_id(1)
    acc_attn = jnp.dot(attn_ref[0], wa_ref[...], preferred_element_type=F32)
    u_ref[CONV_HALO:, :] = a_ref[0].astype(F32) * _sigmoid(g_ref[0].astype(F32))
    halo = ah_ref[0].astype(F32) * _sigmoid(gh_ref[0].astype(F32))
    u_ref[0:CONV_HALO, :] = jnp.where(ti > 0, halo, 0.0)
    base = CONV_HALO - (CONV_TAPS - 1)
    ext = CONV_ROWS + CONV_HALO
    for rb in range(tt // CONV_ROWS):
        rs = slice(rb * CONV_ROWS, (rb + 1) * CONV_ROWS)
        for cb in range(CONV_CHANNELS // LANES):
            cs = slice(cb * LANES, (cb + 1) * LANES)
            wblk = w_ref[:, cs]
            ublk = u_ref[rb * CONV_ROWS:rb * CONV_ROWS + ext, cs]
            acc = jnp.zeros((CONV_ROWS, LANES), F32)
            for r in range(SUBLANES):
                ur = ublk if r == 0 else pltpu.roll(ublk, ext - r, axis=0)
                for a in range(CONV_HALO // SUBLANES + 1):
                    k = SUBLANES * a + r - base
                    if 0 <= k < CONV_TAPS:
                        acc = acc + wblk[k:k + 1, :] * ur[SUBLANES * a:SUBLANES * a + CONV_ROWS]
            y_ref[rs, cs] = acc
        acc = y_ref[rs, :] + b_ref[...]
        mu = jnp.mean(acc, axis=-1, keepdims=True)
        xc = acc - mu
        var = jnp.mean(xc * xc, axis=-1, keepdims=True)
        y = xc * lax.rsqrt(var + NORM_EPS) * lg_ref[...] + lb_ref[...]
        conv_act = (y * _sigmoid(y)).astype(BF16)
        o_ref[0, rs, :] = (x_ref[0, rs, :] + acc_attn[rs]
                           + jnp.dot(conv_act, wc_ref[...], preferred_element_type=F32))


def _mix_out(proj3, w, b, lg, lb, attn, x3, w_out, *, layer, tt):
    bsz, t_len, _ = proj3.shape
    a_blk = COL_CONV_A // CONV_CHANNELS
    g_blk = COL_CONV_G // CONV_CHANNELS
    hpt = tt // CONV_HALO
    halo_idx = lambda i, t: jnp.maximum(t * hpt - 1, 0)
    vec = pl.BlockSpec((1, CONV_CHANNELS), lambda i, t: (0, 0))
    conv_rows = ATTN_WIDTH // CONV_CHANNELS
    return pl.pallas_call(
        functools.partial(_mix_out_body, tt=tt),
        grid=(bsz, t_len // tt),
        in_specs=[
            pl.BlockSpec((1, tt, CONV_CHANNELS), lambda i, t: (i, t, a_blk)),
            pl.BlockSpec((1, tt, CONV_CHANNELS), lambda i, t: (i, t, g_blk)),
            pl.BlockSpec((1, CONV_HALO, CONV_CHANNELS), lambda i, t: (i, halo_idx(i, t), a_blk)),
            pl.BlockSpec((1, CONV_HALO, CONV_CHANNELS), lambda i, t: (i, halo_idx(i, t), g_blk)),
            pl.BlockSpec((CONV_TAPS, CONV_CHANNELS), lambda i, t: (0, 0)),
            vec, vec, vec,
            pl.BlockSpec((1, tt, ATTN_WIDTH), lambda i, t: (i, t, 0)),
            pl.BlockSpec((1, tt, D_MODEL), lambda i, t: (i, t, 0)),
            pl.BlockSpec((None, ATTN_WIDTH, D_MODEL), lambda i, t: (layer, 0, 0)),
            pl.BlockSpec((None, CONV_CHANNELS, D_MODEL), lambda i, t: (layer, conv_rows, 0)),
        ],
        out_specs=pl.BlockSpec((1, tt, D_MODEL), lambda i, t: (i, t, 0)),
        out_shape=jax.ShapeDtypeStruct((bsz, t_len, D_MODEL), F32),
        scratch_shapes=[pltpu.VMEM((CONV_HALO + tt, CONV_CHANNELS), F32),
                        pltpu.VMEM((tt, CONV_CHANNELS), F32)],
        compiler_params=_params("arbitrary", "arbitrary"),
        name="mix_out",
    )(proj3, proj3, proj3, proj3, w, b, lg, lb, attn, x3, w_out, w_out)


def _ffn_up_body(x_ref, xh_ref, g_ref, wa_ref, wg_ref, cwa_ref, cwg_ref, cba_ref, cbg_ref, o_ref,
                 h_ref, ua_ref, ug_ref, *, tm, tiles_per_seq, rows):
    i = pl.program_id(0)

    @pl.when(pl.program_id(1) == 0)
    def _():
        h_ref[FFN_HALO:, :] = _rms_norm_rows(x_ref[...], g_ref[...]).astype(BF16)
        hh = _rms_norm_rows(xh_ref[...], g_ref[...])
        h_ref[0:FFN_HALO, :] = jnp.where(i % tiles_per_seq != 0, hh, 0.0).astype(BF16)

    units = [(u, c, slice(w, w + FFN_COLS)) for u, (c, w) in enumerate(
        (c, w) for c in range(tm // rows) for w in range(0, o_ref.shape[1], FFN_COLS))]

    def project(u, c, cs):
        hc = h_ref[c * rows:(c + 1) * rows + FFN_HALO, :]
        ug_ref[u] = jnp.dot(hc, wg_ref[:, cs], preferred_element_type=F32)
        ua_ref[u] = jnp.dot(hc, wa_ref[:, cs], preferred_element_type=F32)

    def conv(u_ref, u, cs, cw_ref, cb_ref):
        x = u_ref[u]
        y = cw_ref[FFN_CONV_TAPS - 1:FFN_CONV_TAPS, cs] * x[FFN_HALO:]
        for s in range(1, FFN_CONV_TAPS):
            k = FFN_CONV_TAPS - 1 - s
            y = y + cw_ref[k:k + 1, cs] * pltpu.roll(x, s, axis=0)[FFN_HALO:]
        return y + cb_ref[:, cs]

    def finish(u, c, cs):
        gate = conv(ug_ref, u, cs, cwg_ref, cbg_ref)
        act = gate * _sigmoid(gate)
        a = conv(ua_ref, u, cs, cwa_ref, cba_ref)
        o_ref[c * rows:(c + 1) * rows, cs] = (act * a).astype(o_ref.dtype)

    project(*units[0])
    for i_unit, unit in enumerate(units):
        if i_unit + 1 < len(units):
            project(*units[i_unit + 1])
        finish(*unit)


def _ffn_up(x2, g, w_up, cw, cb, *, layer, tm, tn, t_len):
    n = x2.shape[0]
    nj = D_FF // tn
    hpt = tm // FFN_HALO
    rows = min(FFN_ROWS, tm)
    return pl.pallas_call(
        functools.partial(_ffn_up_body, tm=tm, tiles_per_seq=t_len // tm, rows=rows),
        grid=(n // tm, nj),
        in_specs=[
            pl.BlockSpec((tm, D_MODEL), lambda i, j: (i, 0)),
            pl.BlockSpec((FFN_HALO, D_MODEL), lambda i, j: (jnp.maximum(i * hpt - 1, 0), 0)),
            pl.BlockSpec((1, D_MODEL), lambda i, j: (0, 0)),
            pl.BlockSpec((None, D_MODEL, tn), lambda i, j: (layer, 0, j)),
            pl.BlockSpec((None, D_MODEL, tn), lambda i, j: (layer, 0, j + nj)),
            pl.BlockSpec((FFN_CONV_TAPS, tn), lambda i, j: (0, j)),
            pl.BlockSpec((FFN_CONV_TAPS, tn), lambda i, j: (0, j + nj)),
            pl.BlockSpec((1, tn), lambda i, j: (0, j)),
            pl.BlockSpec((1, tn), lambda i, j: (0, j + nj)),
        ],
        out_specs=pl.BlockSpec((tm, tn), lambda i, j: (i, j)),
        out_shape=jax.ShapeDtypeStruct((n, D_FF), BF16),
        scratch_shapes=[
            pltpu.VMEM((FFN_HALO + tm, D_MODEL), BF16),
            pltpu.VMEM((tm // rows * (tn // FFN_COLS), FFN_HALO + rows, FFN_COLS), F32),
            pltpu.VMEM((tm // rows * (tn // FFN_COLS), FFN_HALO + rows, FFN_COLS), F32),
        ],
        compiler_params=_params("arbitrary", "arbitrary"),
        name="ffn_up",
    )(x2, x2, g, w_up, w_up, cw, cw, cb, cb)


def _ffn_down_body(act_ref, w_ref, x_ref, o_ref):
    o_ref[...] = x_ref[...] + jnp.dot(act_ref[...], w_ref[...], preferred_element_type=F32)


def _ffn_down(act, w_down, x2, *, layer, tm, tn):
    n = x2.shape[0]
    return pl.pallas_call(
        _ffn_down_body,
        grid=(n // tm, D_MODEL // tn),
        in_specs=[
            pl.BlockSpec((tm, D_FF), lambda i, j: (i, 0)),
            pl.BlockSpec((None, D_FF, tn), lambda i, j: (layer, 0, j)),
            pl.BlockSpec((tm, tn), lambda i, j: (i, j)),
        ],
        out_specs=pl.BlockSpec((tm, tn), lambda i, j: (i, j)),
        out_shape=jax.ShapeDtypeStruct((n, D_MODEL), F32),
        compiler_params=_params("arbitrary", "arbitrary"),
        name="ffn_down",
    )(act, w_down, x2)


def _final_norm_body(x_ref, g_ref, o_ref):
    o_ref[...] = _rms_norm_rows(x_ref[...], g_ref[...])


def _final_norm(x2, g, *, tm):
    n = x2.shape[0]
    return pl.pallas_call(
        _final_norm_body,
        grid=(n // tm,),
        in_specs=[pl.BlockSpec((tm, D_MODEL), lambda i: (i, 0)), pl.BlockSpec((1, D_MODEL), lambda i: (0, 0))],
        out_specs=pl.BlockSpec((tm, D_MODEL), lambda i: (i, 0)),
        out_shape=jax.ShapeDtypeStruct((n, D_MODEL), F32),
        compiler_params=_params("arbitrary"),
        name="final_norm",
    )(x2, g)


def _in_weight_layout(w_in):
    depth = w_in.shape[0]
    kv0 = ATTN_WIDTH
    gate0 = kv0 + KV_COLS
    conv0 = gate0 + GATE_COLS
    w_t = jnp.swapaxes(w_in, 1, 2)
    parts = [w_t[:, :ATTN_WIDTH], w_t[:, conv0:conv0 + 2 * CONV_CHANNELS]]
    for br in range(N_BRANCH):
        for h in range(N_KV_HEADS):
            for kv in range(2):
                c0 = kv0 + ((br * 2 + kv) * N_KV_HEADS + h) * HEAD_DIM
                parts.append(w_t[:, c0:c0 + HEAD_DIM])
    w_main = jnp.concatenate(parts, axis=1).astype(BF16)
    wg = w_in[..., gate0:gate0 + GATE_COLS].reshape(depth, D_MODEL, N_KV_HEADS, GQA_GROUP, N_BRANCH)
    wg = wg.transpose(0, 1, 2, 4, 3).reshape(depth, D_MODEL, N_KV_HEADS, N_BRANCH * GQA_GROUP)
    wg = wg.reshape(depth, D_MODEL, GATE_COLS)
    wg = jnp.pad(wg, ((0, 0), (0, 0), (0, LANES - GATE_COLS)))
    return w_main, wg.astype(BF16)


def _pick(n, prefs):
    for p in prefs:
        if n % p == 0:
            return p
    return n


def kernel(x, rel_bias, mix_norm_g, w_in, cmp_pos, cmp_w1, cmp_w2, conv_w, conv_b, conv_ln_g, conv_ln_b,
           w_out, ffn_norm_g, w_up, ffn_conv_w, ffn_conv_b, w_down, final_norm_g):
    bsz, t_len, _ = x.shape
    depth = w_in.shape[0]
    n = bsz * t_len
    assert t_len % 256 == 0 and t_len // SLC_LEN <= HEAD_DIM
    tq = 256
    tm = _pick(t_len, (1024, 512, 256))
    nc = t_len // CMP_STRIDE

    band, cmpb = _bias_tiles(rel_bias, t_len, tq)
    col_scale = jnp.concatenate([jnp.full((1, ATTN_WIDTH), Q_SCALE, F32),
                                 jnp.ones((1, PROJ_COLS - ATTN_WIDTH), F32)], axis=1)
    x2 = x.reshape(n, D_MODEL)
    w_main, wg = _in_weight_layout(w_in)
    w_out_b, w_up_b, w_down_b = w_out.astype(BF16), w_up.astype(BF16), w_down.astype(BF16)
    for l in range(depth):
        proj, gates = _in_projection(x2, mix_norm_g[l][None, :], w_main, wg, col_scale, layer=l, tm=tm, tn=1536)
        proj3 = proj.reshape(bsz, t_len, PROJ_COLS)
        gates3 = gates.reshape(bsz, t_len, LANES)

        posx, w1big, w2big = _compress_weight_layout(cmp_pos[l], cmp_w1[l], cmp_w2[l])
        tokl = proj3[:, :, COL_KV:COL_KV + CMP_WIDTH].reshape(bsz, nc, CMP_STRIDE, CMP_WIDTH).transpose(0, 2, 1, 3)
        kvc, kvct = _compress(tokl, posx, w1big, w2big)

        attn = _attention(proj3, kvc, kvct, band, cmpb, gates3, tq=tq)
        x2 = _mix_out(proj3, conv_w[l], conv_b[l][None, :], conv_ln_g[l][None, :], conv_ln_b[l][None, :],
                      attn, x2.reshape(bsz, t_len, D_MODEL), w_out_b, layer=l, tt=512).reshape(n, D_MODEL)

        act = _ffn_up(x2, ffn_norm_g[l][None, :], w_up_b, ffn_conv_w[l],
                      ffn_conv_b[l][None, :], layer=l, tm=tm, tn=512, t_len=t_len)
        x2 = _ffn_down(act, w_down_b, x2, layer=l, tm=tm, tn=512)
    x2 = _final_norm(x2, final_norm_g[None, :], tm=_pick(n, (512, 256)))
    return x2.reshape(bsz, t_len, D_MODEL)
```

```python
import functools
import math

import numpy as np
import jax
import jax.numpy as jnp
from jax import lax
from jax.experimental import pallas as pl
from jax.experimental.pallas import tpu as pltpu

F32 = jnp.float32
BF16 = jnp.bfloat16

D_MODEL = 2048
HEAD_DIM = 64
N_KV_HEADS = 4
GQA_GROUP = 4
N_Q_HEADS = N_KV_HEADS * GQA_GROUP
N_BRANCH = 3
ATTN_WIDTH = N_Q_HEADS * HEAD_DIM
CMP_LEN = 32
CMP_STRIDE = 16
SLC_LEN = 64
SLC_TOPK = 16
WINDOW = 512
CONV_CHANNELS = D_MODEL - ATTN_WIDTH
CONV_TAPS = 31
D_FF = 5632
FFN_CONV_TAPS = 3
N_BUCKETS = 32
MAX_DISTANCE = 128
NORM_EPS = 1e-6

KV_PAIR = 2 * HEAD_DIM
KV_COLS = N_BRANCH * N_KV_HEADS * KV_PAIR
GATE_COLS = N_BRANCH * N_Q_HEADS
COL_Q = 0
COL_CONV_A = ATTN_WIDTH
COL_CONV_G = COL_CONV_A + CONV_CHANNELS
COL_KV = COL_CONV_G + CONV_CHANNELS
PROJ_COLS = COL_KV + KV_COLS

MASK_VALUE = -1e30
BLOCK_PENALTY = -1e9
KEY_BIG = 1e30
KEY_TAKEN = -3e38
LOG2E = math.log2(math.e)
Q_SCALE = HEAD_DIM ** -0.5 * LOG2E

LANES = 128
SUBLANES = 8
CONV_ROWS = 128
FFN_ROWS = 1024
FFN_COLS = 512
VMEM_LIMIT = 56 * 1024 * 1024
CONV_HALO = 32
FFN_HALO = 16


def _t5_bucket_last_distance():
    n = np.arange(0, 4 * MAX_DISTANCE, dtype=np.int64)
    max_exact = N_BUCKETS // 2
    nf = np.maximum(n, 1).astype(np.float64)
    large = max_exact + np.floor(np.log(nf / max_exact) / math.log(MAX_DISTANCE / max_exact)
                                 * (N_BUCKETS - max_exact)).astype(np.int64)
    large = np.minimum(large, N_BUCKETS - 1)
    bucket = np.where(n < max_exact, n, large)
    last = []
    for b in range(N_BUCKETS - 1):
        idx = np.nonzero(bucket == b)[0]
        last.append(int(idx.max()) if idx.size else None)
    return last


_BUCKET_LAST = _t5_bucket_last_distance()


def _params(*sem):
    return pltpu.CompilerParams(dimension_semantics=sem, vmem_limit_bytes=VMEM_LIMIT)


def _sigmoid(x):
    return jax.nn.sigmoid(x)


def _rms_norm_rows(x, g):
    ms = jnp.mean(x * x, axis=-1, keepdims=True)
    return x * lax.rsqrt(ms + NORM_EPS) * g


def _bias_from_distance(dist, tab_ref, head):
    c_far = tab_ref[N_BUCKETS - 1, head]
    val = jnp.zeros(dist.shape, F32)
    for b in range(N_BUCKETS - 2, -1, -1):
        if _BUCKET_LAST[b] is None:
            continue
        val = jnp.where(dist <= _BUCKET_LAST[b], tab_ref[b, head] - c_far, val)
    return val


def _band_bias_body(tab_ref, o_ref, *, tq):
    head = pl.program_id(0) * GQA_GROUP + pl.program_id(1)
    shape = (WINDOW + tq, tq)
    dist = (lax.broadcasted_iota(jnp.int32, shape, 1) + WINDOW
            - lax.broadcasted_iota(jnp.int32, shape, 0))
    val = _bias_from_distance(dist, tab_ref, head) * LOG2E
    val = jnp.where(dist >= 0, jnp.where(dist < WINDOW, val, MASK_VALUE), MASK_VALUE)
    o_ref[0] = val


def _cmp_bias_body(tab_ref, o_ref, *, tq, nc):
    head = pl.program_id(0) * GQA_GROUP + pl.program_id(1)
    shape = (2 * nc, tq)
    r = lax.broadcasted_iota(jnp.int32, shape, 0)
    i = lax.broadcasted_iota(jnp.int32, shape, 1)
    dist = i - (r - nc) * CMP_STRIDE - (CMP_LEN - 1)
    val = _bias_from_distance(dist, tab_ref, head) * LOG2E
    o_ref[0] = jnp.where(dist >= 0, val, MASK_VALUE)


def _bias_tiles(rel_bias, t_len, tq):
    nc = t_len // CMP_STRIDE
    rows = GQA_GROUP * tq
    smem = pl.BlockSpec(memory_space=pltpu.SMEM)
    band = pl.pallas_call(
        functools.partial(_band_bias_body, tq=tq),
        grid=(N_KV_HEADS, GQA_GROUP),
        in_specs=[smem],
        out_specs=pl.BlockSpec((1, WINDOW + tq, tq), lambda h, g: (h, 0, g)),
        out_shape=jax.ShapeDtypeStruct((N_KV_HEADS, WINDOW + tq, rows), F32),
        compiler_params=_params("arbitrary", "arbitrary"),
        name="band_bias",
    )(rel_bias)
    cmpb = pl.pallas_call(
        functools.partial(_cmp_bias_body, tq=tq, nc=nc),
        grid=(N_KV_HEADS, GQA_GROUP),
        in_specs=[smem],
        out_specs=pl.BlockSpec((1, 2 * nc, tq), lambda h, g: (h, 0, g)),
        out_shape=jax.ShapeDtypeStruct((N_KV_HEADS, 2 * nc, rows), F32),
        compiler_params=_params("arbitrary", "arbitrary"),
        name="cmp_bias",
    )(rel_bias)
    return band, cmpb


def _inproj_body(x_ref, g_ref, w_ref, wg_ref, cs_ref, o_ref, gate_ref, h_ref):
    @pl.when(pl.program_id(1) == 0)
    def _():
        h = _rms_norm_rows(x_ref[...], g_ref[...]).astype(BF16)
        h_ref[...] = h
        gate_ref[...] = _sigmoid(jnp.dot(h, wg_ref[...], preferred_element_type=F32))

    acc = lax.dot_general(h_ref[...], w_ref[...], (((1,), (1,)), ((), ())),
                          preferred_element_type=F32)
    o_ref[...] = (acc * cs_ref[...]).astype(o_ref.dtype)


def _in_projection(x2, g, w, wg, col_scale, *, layer, tm, tn):
    n = x2.shape[0]
    gcols = wg.shape[-1]
    return pl.pallas_call(
        _inproj_body,
        grid=(n // tm, PROJ_COLS // tn),
        in_specs=[
            pl.BlockSpec((tm, D_MODEL), lambda i, j: (i, 0)),
            pl.BlockSpec((1, D_MODEL), lambda i, j: (0, 0)),
            pl.BlockSpec((None, tn, D_MODEL), lambda i, j: (layer, j, 0)),
            pl.BlockSpec((None, D_MODEL, gcols), lambda i, j: (layer, 0, 0)),
            pl.BlockSpec((1, tn), lambda i, j: (0, j)),
        ],
        out_specs=[
            pl.BlockSpec((tm, tn), lambda i, j: (i, j)),
            pl.BlockSpec((tm, gcols), lambda i, j: (i, 0)),
        ],
        out_shape=[
            jax.ShapeDtypeStruct((n, PROJ_COLS), BF16),
            jax.ShapeDtypeStruct((n, gcols), F32),
        ],
        scratch_shapes=[pltpu.VMEM((tm, D_MODEL), BF16)],
        compiler_params=_params("arbitrary", "arbitrary"),
        name="in_projection",
    )(x2, g, w, wg, col_scale)


CMP_WIDTH = N_KV_HEADS * KV_PAIR
CMP_TOKENS_PER_STEP = 8


def _compress_body(x_ref, pos_ref, w1_ref, w2_ref, o_ref, ot_ref, top_ref, bot_ref, *, nc):
    l = pl.program_id(1)

    @pl.when(l == 0)
    def _():
        top_ref[...] = jnp.zeros_like(top_ref)
        bot_ref[...] = jnp.zeros_like(bot_ref)

    top = top_ref[...]
    bot = bot_ref[...]
    for s in range(CMP_TOKENS_PER_STEP):
        x = x_ref[0, s].astype(F32)
        top = top + jnp.dot((x + pos_ref[s, 0]).astype(BF16), w1_ref[s, 0], preferred_element_type=F32)
        bot = bot + jnp.dot((x + pos_ref[s, 1]).astype(BF16), w1_ref[s, 1], preferred_element_type=F32)
    top_ref[...] = top
    bot_ref[...] = bot

    @pl.when(l == pl.num_programs(1) - 1)
    def _():
        pre = top_ref[...] + pltpu.roll(bot_ref[...], nc - 1, axis=0)
        act = pre * _sigmoid(pre)
        out = jnp.dot(act.astype(BF16), w2_ref[...], preferred_element_type=F32)
        for h in range(N_KV_HEADS):
            kv = out[:, h * KV_PAIR:(h + 1) * KV_PAIR]
            o_ref[0, h] = kv.astype(o_ref.dtype)
            ot_ref[0, h] = kv.T.astype(ot_ref.dtype)


def _compress(tokl, posx, w1big, w2big):
    b, _, nc, _ = tokl.shape
    return pl.pallas_call(
        functools.partial(_compress_body, nc=nc),
        grid=(b, CMP_STRIDE // CMP_TOKENS_PER_STEP),
        in_specs=[
            pl.BlockSpec((1, CMP_TOKENS_PER_STEP, nc, CMP_WIDTH), lambda i, l: (i, l, 0, 0)),
            pl.BlockSpec((CMP_TOKENS_PER_STEP, 2, 1, CMP_WIDTH), lambda i, l: (l, 0, 0, 0)),
            pl.BlockSpec((CMP_TOKENS_PER_STEP, 2, CMP_WIDTH, CMP_WIDTH), lambda i, l: (l, 0, 0, 0)),
            pl.BlockSpec((CMP_WIDTH, CMP_WIDTH), lambda i, l: (0, 0)),
        ],
        out_specs=[
            pl.BlockSpec((1, N_KV_HEADS, nc, KV_PAIR), lambda i, l: (i, 0, 0, 0)),
            pl.BlockSpec((1, N_KV_HEADS, KV_PAIR, nc), lambda i, l: (i, 0, 0, 0)),
        ],
        out_shape=[
            jax.ShapeDtypeStruct((b, N_KV_HEADS, nc, KV_PAIR), BF16),
            jax.ShapeDtypeStruct((b, N_KV_HEADS, KV_PAIR, nc), BF16),
        ],
        scratch_shapes=[pltpu.VMEM((nc, CMP_WIDTH), F32), pltpu.VMEM((nc, CMP_WIDTH), F32)],
        compiler_params=_params("arbitrary", "arbitrary"),
        name="compress",
    )(tokl, posx, w1big, w2big)


def _compress_weight_layout(pos, w1, w2):
    def block_diag(blocks):
        rows = []
        for h in range(N_KV_HEADS):
            for kv in range(2):
                off = (h * 2 + kv) * HEAD_DIM
                pad = [(0, 0)] * (blocks.ndim - 2) + [(off, CMP_WIDTH - HEAD_DIM - off)]
                rows.append(jnp.pad(blocks[kv], pad))
        return jnp.concatenate(rows, axis=-2).astype(BF16)

    w1r = w1.reshape(2, 2, CMP_STRIDE, HEAD_DIM, HEAD_DIM)
    w1big = block_diag(w1r.transpose(0, 2, 1, 3, 4))
    w2big = block_diag(w2)
    posr = pos.reshape(2, 2, CMP_STRIDE, HEAD_DIM).transpose(2, 1, 0, 3)
    posx = jnp.broadcast_to(posr[:, :, None], (CMP_STRIDE, 2, N_KV_HEADS, 2, HEAD_DIM))
    return posx.reshape(CMP_STRIDE, 2, 1, CMP_WIDTH), w1big, w2big


VT_ROWS = HEAD_DIM + 16
FAR_UNROLL = 4


def _attn_body(q_ref, kvc_ref, kvct_ref, kvs_ref, kvw_ref, bt_ref, bc_ref, gate_ref, o_ref,
               kp_ref, vst_ref, vwt_ref, key_ref, m_ref, acc_ref, sa_ref, sb_ref, ocmp_ref, qsel_ref, qwin_ref,
               gt_ref, *, tq, t_len):
    kv_head = pl.program_id(0)
    qi = pl.program_id(2)
    nc = t_len // CMP_STRIDE
    ns = t_len // SLC_LEN
    rows = GQA_GROUP * tq

    @pl.when(qi == 0)
    def _():
        k = kvs_ref[0][:, :HEAD_DIM]
        blk = lax.broadcasted_iota(jnp.int32, (t_len, HEAD_DIM), 0) // SLC_LEN
        col = lax.broadcasted_iota(jnp.int32, (t_len, HEAD_DIM), 1)
        onehot = jnp.where(blk == col, 1.0, 0.0).astype(BF16)
        kp_ref[...] = jnp.concatenate([k, onehot], axis=1)
        ones_rows = jnp.where(lax.broadcasted_iota(jnp.int32, (VT_ROWS - HEAD_DIM, tq), 0) == 0, 1.0, 0.0)
        for c in range(t_len // tq):
            st = kvs_ref[0, c * tq:(c + 1) * tq, :].astype(F32).T[HEAD_DIM:]
            vst_ref[c] = jnp.concatenate([st, ones_rows], axis=0).astype(BF16)
            wt = kvw_ref[0, c * tq:(c + 1) * tq, :].astype(F32).T[HEAD_DIM:]
            vwt_ref[c] = jnp.concatenate([wt, ones_rows], axis=0).astype(BF16)

    q_t = q_ref[0].astype(F32).T
    q4_t = jnp.concatenate([q_t[g * HEAD_DIM:(g + 1) * HEAD_DIM] for g in range(GQA_GROUP)],
                           axis=1).astype(BF16)
    qwin_ref[...] = jnp.concatenate([q4_t, jnp.zeros((HEAD_DIM, rows), BF16)], axis=0)
    qsel_ref[0:HEAD_DIM, :] = q4_t
    kw_ref = kvw_ref.at[0]
    win, sel = 0, 1

    def cmp_scores():
        bias_row = pl.multiple_of(nc - qi * (tq // CMP_STRIDE), tq // CMP_STRIDE)
        return (jnp.dot(kvc_ref[0, 0], qwin_ref[...], preferred_element_type=F32)
                + bc_ref[0, pl.ds(bias_row, nc), :])

    def cmp_finish(s):
        m = jnp.max(s, axis=0, keepdims=True)
        p = jnp.exp2(s - m)
        l = jnp.sum(p, axis=0, keepdims=True)
        pn = p * jnp.where(m > 0.5 * MASK_VALUE, 1.0 / l, 0.0)
        ocmp_ref[...] = jnp.dot(kvct_ref[0, 0, HEAD_DIM:, :], pn.astype(BF16),
                                preferred_element_type=F32)

        ps = pn[:, 0:tq] + pn[:, tq:2 * tq] + pn[:, 2 * tq:3 * tq] + pn[:, 3 * tq:4 * tq]
        sj = lax.broadcasted_iota(jnp.int32, (HEAD_DIM, nc), 0) * SLC_LEN
        ci = lax.broadcasted_iota(jnp.int32, (HEAD_DIM, nc), 1) * CMP_STRIDE
        overlap = jnp.where(ci < sj + SLC_LEN, jnp.where(ci + CMP_LEN > sj, 1.0, 0.0), 0.0).astype(BF16)
        p_hi = ps.astype(BF16)
        r_hi = ps - p_hi.astype(F32)
        p_md = r_hi.astype(BF16)
        p_lo = (r_hi - p_md.astype(F32)).astype(BF16)
        imp = (jnp.dot(overlap, p_hi, preferred_element_type=F32)
               + jnp.dot(overlap, p_md, preferred_element_type=F32)
               + jnp.dot(overlap, p_lo, preferred_element_type=F32))
        t = qi * tq + lax.broadcasted_iota(jnp.int32, (HEAD_DIM, tq), 1)
        blk = lax.broadcasted_iota(jnp.int32, (HEAD_DIM, tq), 0)
        cur = t // SLC_LEN
        key = jnp.where(blk == 0, KEY_BIG, jnp.where(blk == cur, KEY_BIG, jnp.where(blk == cur - 1, KEY_BIG, imp)))
        key_ref[...] = jnp.where(blk * SLC_LEN <= t, key, -KEY_BIG)

    def select_blocks():
        key = key_ref[...]
        blk = lax.broadcasted_iota(jnp.int32, (HEAD_DIM, tq), 0)
        pen = jnp.full((HEAD_DIM, tq), BLOCK_PENALTY, F32)
        for _ in range(min(SLC_TOPK, ns)):
            top = jnp.max(key, axis=0, keepdims=True)
            first = jnp.min(jnp.where(key == top, blk, HEAD_DIM), axis=0, keepdims=True)
            hit = blk == first
            pen = jnp.where(hit, 0.0, pen)
            key = jnp.where(hit, KEY_TAKEN, key)
        qsel_ref[HEAD_DIM:, :] = jnp.concatenate([pen.astype(BF16)] * GQA_GROUP, axis=1)

    def scores(qx_ref, k_ref, kj, bias_off):
        start = pl.multiple_of(kj * tq, tq)
        sc = jnp.dot(k_ref[pl.ds(start, tq), :], qx_ref[...], preferred_element_type=F32)
        if bias_off is not None:
            sc = sc + bt_ref[0, bias_off:bias_off + tq, :]
        return sc

    def consume(sc, vt_ref, kj, br, st, first):
        m_tile = jnp.max(sc, axis=0, keepdims=True)
        if first:
            m_new = m_tile
            pr = jnp.exp2(sc - m_new).astype(BF16)
            acc_ref[br, st] = jnp.dot(vt_ref[kj], pr, preferred_element_type=F32)
        else:
            m_old = m_ref[br, st]
            m_new = jnp.maximum(m_old, m_tile)
            alpha = jnp.exp2(m_old - m_new)
            pr = jnp.exp2(sc - m_new).astype(BF16)
            acc_ref[br, st] = alpha * acc_ref[br, st] + jnp.dot(vt_ref[kj], pr, preferred_element_type=F32)
        m_ref[br, st] = m_new

    def finish(br):
        m0 = m_ref[br, 0]
        m1 = m_ref[br, 1]
        m_all = jnp.maximum(m0, m1)
        acc = jnp.exp2(m0 - m_all) * acc_ref[br, 0] + jnp.exp2(m1 - m_all) * acc_ref[br, 1]
        return acc[0:HEAD_DIM] * (1.0 / acc[HEAD_DIM:HEAD_DIM + 1, :])

    def combine():
        o_win = finish(win)
        o_sel = finish(sel)
        o_cmp = ocmp_ref[...]
        gt_ref[...] = gate_ref[0].T
        base = kv_head * (N_BRANCH * GQA_GROUP)

        def gate(br, g):
            return gt_ref[pl.ds(base + br * GQA_GROUP + g, 1), :]

        combs = []
        for g in range(GQA_GROUP):
            sl = slice(g * tq, (g + 1) * tq)
            combs.append(gate(0, g) * o_cmp[:, sl] + gate(1, g) * o_sel[:, sl] + gate(2, g) * o_win[:, sl])
        outs = [jnp.concatenate(combs[p:p + 2], axis=0).T for p in range(0, GQA_GROUP, 2)]
        o_ref[0] = jnp.concatenate(outs, axis=1).astype(o_ref.dtype)

    @pl.when(qi >= 2)
    def _():
        sc_cmp = cmp_scores()
        sw0 = scores(qwin_ref, kw_ref, qi, WINDOW)
        cmp_finish(sc_cmp)
        sw1 = scores(qwin_ref, kw_ref, qi - 1, WINDOW - tq)
        consume(sw0, vwt_ref, qi, win, 0, True)
        sw2 = scores(qwin_ref, kw_ref, qi - 2, WINDOW - 2 * tq)
        consume(sw1, vwt_ref, qi - 1, win, 1, True)
        select_blocks()
        consume(sw2, vwt_ref, qi - 2, win, 0, False)

        ss0 = scores(qsel_ref, kp_ref, qi, WINDOW)
        ss1 = scores(qsel_ref, kp_ref, qi - 1, WINDOW - tq)
        consume(ss0, vst_ref, qi, sel, 0, True)
        sa_ref[...] = scores(qsel_ref, kp_ref, 0, None)
        consume(ss1, vst_ref, qi - 1, sel, 1, True)
        n_far = qi - 1

        def far_pair(pi):
            sb_ref[...] = scores(qsel_ref, kp_ref, 2 * pi + 1, None)
            consume(sa_ref[...], vst_ref, 2 * pi, sel, 1, False)
            sa_ref[...] = scores(qsel_ref, kp_ref, jnp.minimum(2 * pi + 2, n_far - 1), None)
            consume(sb_ref[...], vst_ref, 2 * pi + 1, sel, 0, False)

        n_pairs = n_far // 2

        def far_group(gi, carry):
            for u in range(FAR_UNROLL):
                far_pair(FAR_UNROLL * gi + u)
            return carry

        lax.fori_loop(0, n_pairs // FAR_UNROLL, far_group, 0)
        done = (n_pairs // FAR_UNROLL) * FAR_UNROLL
        left = n_pairs - done
        size = FAR_UNROLL // 2
        while size >= 1:
            @pl.when(left % (2 * size) >= size)
            def _(size=size, base=done):
                for u in range(size):
                    far_pair(base + u)
            done = done + jnp.where(left % (2 * size) >= size, size, 0)
            size //= 2

        @pl.when(n_far % 2 == 1)
        def _():
            consume(sa_ref[...], vst_ref, n_far - 1, sel, 1, False)

        combine()

    @pl.when(qi < 2)
    def _():
        cmp_finish(cmp_scores())
        select_blocks()
        sw0 = scores(qwin_ref, kw_ref, qi, WINDOW)
        ss0 = scores(qsel_ref, kp_ref, qi, WINDOW)
        consume(sw0, vwt_ref, qi, win, 0, True)
        consume(ss0, vst_ref, qi, sel, 0, True)
        for br in (win, sel):
            m_ref[br, 1] = jnp.full((1, rows), MASK_VALUE, F32)
            acc_ref[br, 1] = jnp.zeros((VT_ROWS, rows), F32)

        @pl.when(qi == 1)
        def _():
            sw1 = scores(qwin_ref, kw_ref, 0, WINDOW - tq)
            ss1 = scores(qsel_ref, kp_ref, 0, WINDOW - tq)
            consume(sw1, vwt_ref, 0, win, 1, False)
            consume(ss1, vst_ref, 0, sel, 1, False)

        combine()


def _attention(proj3, kvc, kvct, band, cmpb, gates3, *, tq):
    b, t_len, _ = proj3.shape
    assert WINDOW == 2 * tq
    nc = t_len // CMP_STRIDE
    rows = GQA_GROUP * tq
    qw = GQA_GROUP * HEAD_DIM
    slc_blk = COL_KV // KV_PAIR + N_KV_HEADS
    win_blk = COL_KV // KV_PAIR + 2 * N_KV_HEADS
    return pl.pallas_call(
        functools.partial(_attn_body, tq=tq, t_len=t_len),
        grid=(N_KV_HEADS, b, t_len // tq),
        in_specs=[
            pl.BlockSpec((1, tq, qw), lambda h, i, q: (i, q, h)),
            pl.BlockSpec((1, 1, nc, KV_PAIR), lambda h, i, q: (i, h, 0, 0)),
            pl.BlockSpec((1, 1, KV_PAIR, nc), lambda h, i, q: (i, h, 0, 0)),
            pl.BlockSpec((1, t_len, KV_PAIR), lambda h, i, q: (i, 0, slc_blk + h)),
            pl.BlockSpec((1, t_len, KV_PAIR), lambda h, i, q: (i, 0, win_blk + h)),
            pl.BlockSpec((1, WINDOW + tq, rows), lambda h, i, q: (h, 0, 0)),
            pl.BlockSpec((1, 2 * nc, rows), lambda h, i, q: (h, 0, 0)),
            pl.BlockSpec((1, tq, LANES), lambda h, i, q: (i, q, 0)),
        ],
        out_specs=pl.BlockSpec((1, tq, qw), lambda h, i, q: (i, q, h)),
        out_shape=jax.ShapeDtypeStruct((b, t_len, ATTN_WIDTH), BF16),
        scratch_shapes=[
            pltpu.VMEM((t_len, KV_PAIR), BF16),
            pltpu.VMEM((t_len // tq, VT_ROWS, tq), BF16),
            pltpu.VMEM((t_len // tq, VT_ROWS, tq), BF16),
            pltpu.VMEM((HEAD_DIM, tq), F32),
            pltpu.VMEM((2, 2, 1, rows), F32),
            pltpu.VMEM((2, 2, VT_ROWS, rows), F32),
            pltpu.VMEM((tq, rows), F32),
            pltpu.VMEM((tq, rows), F32),
            pltpu.VMEM((HEAD_DIM, rows), F32),
            pltpu.VMEM((KV_PAIR, rows), BF16),
            pltpu.VMEM((KV_PAIR, rows), BF16),
            pltpu.VMEM((LANES, tq), F32),
        ],
        compiler_params=_params("arbitrary", "arbitrary", "arbitrary"),
        name="nsa_attention",
    )(proj3, kvc, kvct, proj3, proj3, band, cmpb, gates3)


def _mix_out_body(a_ref, g_ref, ah_ref, gh_ref, w_ref, b_ref, lg_ref, lb_ref, attn_ref, x_ref, wa_ref, wc_ref,
                  o_ref, u_ref, y_ref, *, tt):
    ti = pl.program_id(1)
    acc_attn = jnp.dot(attn_ref[0], wa_ref[...], preferred_element_type=F32)
    u_ref[CONV_HALO:, :] = a_ref[0].astype(F32) * _sigmoid(g_ref[0].astype(F32))
    halo = ah_ref[0].astype(F32) * _sigmoid(gh_ref[0].astype(F32))
    u_ref[0:CONV_HALO, :] = jnp.where(ti > 0, halo, 0.0)
    base = CONV_HALO - (CONV_TAPS - 1)
    ext = CONV_ROWS + CONV_HALO
    for rb in range(tt // CONV_ROWS):
        rs = slice(rb * CONV_ROWS, (rb + 1) * CONV_ROWS)
        for cb in range(CONV_CHANNELS // LANES):
            cs = slice(cb * LANES, (cb + 1) * LANES)
            wblk = w_ref[:, cs]
            ublk = u_ref[rb * CONV_ROWS:rb * CONV_ROWS + ext, cs]
            acc = jnp.zeros((CONV_ROWS, LANES), F32)
            for r in range(SUBLANES):
                ur = ublk if r == 0 else pltpu.roll(ublk, ext - r, axis=0)
                for a in range(CONV_HALO // SUBLANES + 1):
                    k = SUBLANES * a + r - base
                    if 0 <= k < CONV_TAPS:
                        acc = acc + wblk[k:k + 1, :] * ur[SUBLANES * a:SUBLANES * a + CONV_ROWS]
            y_ref[rs, cs] = acc
        acc = y_ref[rs, :] + b_ref[...]
        mu = jnp.mean(acc, axis=-1, keepdims=True)
        xc = acc - mu
        var = jnp.mean(xc * xc, axis=-1, keepdims=True)
        y = xc * lax.rsqrt(var + NORM_EPS) * lg_ref[...] + lb_ref[...]
        conv_act = (y * _sigmoid(y)).astype(BF16)
        o_ref[0, rs, :] = (x_ref[0, rs, :] + acc_attn[rs]
                           + jnp.dot(conv_act, wc_ref[...], preferred_element_type=F32))


def _mix_out(proj3, w, b, lg, lb, attn, x3, w_out, *, layer, tt):
    bsz, t_len, _ = proj3.shape
    a_blk = COL_CONV_A // CONV_CHANNELS
    g_blk = COL_CONV_G // CONV_CHANNELS
    hpt = tt // CONV_HALO
    halo_idx = lambda i, t: jnp.maximum(t * hpt - 1, 0)
    vec = pl.BlockSpec((1, CONV_CHANNELS), lambda i, t: (0, 0))
    conv_rows = ATTN_WIDTH // CONV_CHANNELS
    return pl.pallas_call(
        functools.partial(_mix_out_body, tt=tt),
        grid=(bsz, t_len // tt),
        in_specs=[
            pl.BlockSpec((1, tt, CONV_CHANNELS), lambda i, t: (i, t, a_blk)),
            pl.BlockSpec((1, tt, CONV_CHANNELS), lambda i, t: (i, t, g_blk)),
            pl.BlockSpec((1, CONV_HALO, CONV_CHANNELS), lambda i, t: (i, halo_idx(i, t), a_blk)),
            pl.BlockSpec((1, CONV_HALO, CONV_CHANNELS), lambda i, t: (i, halo_idx(i, t), g_blk)),
            pl.BlockSpec((CONV_TAPS, CONV_CHANNELS), lambda i, t: (0, 0)),
            vec, vec, vec,
            pl.BlockSpec((1, tt, ATTN_WIDTH), lambda i, t: (i, t, 0)),
            pl.BlockSpec((1, tt, D_MODEL), lambda i, t: (i, t, 0)),
            pl.BlockSpec((None, ATTN_WIDTH, D_MODEL), lambda i, t: (layer, 0, 0)),
            pl.BlockSpec((None, CONV_CHANNELS, D_MODEL), lambda i, t: (layer, conv_rows, 0)),
        ],
        out_specs=pl.BlockSpec((1, tt, D_MODEL), lambda i, t: (i, t, 0)),
        out_shape=jax.ShapeDtypeStruct((bsz, t_len, D_MODEL), F32),
        scratch_shapes=[pltpu.VMEM((CONV_HALO + tt, CONV_CHANNELS), F32),
                        pltpu.VMEM((tt, CONV_CHANNELS), F32)],
        compiler_params=_params("arbitrary", "arbitrary"),
        name="mix_out",
    )(proj3, proj3, proj3, proj3, w, b, lg, lb, attn, x3, w_out, w_out)


def _ffn_up_body(x_ref, xh_ref, g_ref, wa_ref, wg_ref, cwa_ref, cwg_ref, cba_ref, cbg_ref, o_ref,
                 h_ref, ua_ref, ug_ref, *, tm, tiles_per_seq, rows):
    i = pl.program_id(0)

    @pl.when(pl.program_id(1) == 0)
    def _():
        h_ref[FFN_HALO:, :] = _rms_norm_rows(x_ref[...], g_ref[...]).astype(BF16)
        hh = _rms_norm_rows(xh_ref[...], g_ref[...])
        h_ref[0:FFN_HALO, :] = jnp.where(i % tiles_per_seq != 0, hh, 0.0).astype(BF16)

    units = [(u, c, slice(w, w + FFN_COLS)) for u, (c, w) in enumerate(
        (c, w) for c in range(tm // rows) for w in range(0, o_ref.shape[1], FFN_COLS))]

    def project(u, c, cs):
        hc = h_ref[c * rows:(c + 1) * rows + FFN_HALO, :]
        ug_ref[u] = jnp.dot(hc, wg_ref[:, cs], preferred_element_type=F32)
        ua_ref[u] = jnp.dot(hc, wa_ref[:, cs], preferred_element_type=F32)

    def conv(u_ref, u, cs, cw_ref, cb_ref):
        x = u_ref[u]
        y = cw_ref[FFN_CONV_TAPS - 1:FFN_CONV_TAPS, cs] * x[FFN_HALO:]
        for s in range(1, FFN_CONV_TAPS):
            k = FFN_CONV_TAPS - 1 - s
            y = y + cw_ref[k:k + 1, cs] * pltpu.roll(x, s, axis=0)[FFN_HALO:]
        return y + cb_ref[:, cs]

    def finish(u, c, cs):
        gate = conv(ug_ref, u, cs, cwg_ref, cbg_ref)
        act = gate * _sigmoid(gate)
        a = conv(ua_ref, u, cs, cwa_ref, cba_ref)
        o_ref[c * rows:(c + 1) * rows, cs] = (act * a).astype(o_ref.dtype)

    project(*units[0])
    for i_unit, unit in enumerate(units):
        if i_unit + 1 < len(units):
            project(*units[i_unit + 1])
        finish(*unit)


def _ffn_up(x2, g, w_up, cw, cb, *, layer, tm, tn, t_len):
    n = x2.shape[0]
    nj = D_FF // tn
    hpt = tm // FFN_HALO
    rows = min(FFN_ROWS, tm)
    return pl.pallas_call(
        functools.partial(_ffn_up_body, tm=tm, tiles_per_seq=t_len // tm, rows=rows),
        grid=(n // tm, nj),
        in_specs=[
            pl.BlockSpec((tm, D_MODEL), lambda i, j: (i, 0)),
            pl.BlockSpec((FFN_HALO, D_MODEL), lambda i, j: (jnp.maximum(i * hpt - 1, 0), 0)),
            pl.BlockSpec((1, D_MODEL), lambda i, j: (0, 0)),
            pl.BlockSpec((None, D_MODEL, tn), lambda i, j: (layer, 0, j)),
            pl.BlockSpec((None, D_MODEL, tn), lambda i, j: (layer, 0, j + nj)),
            pl.BlockSpec((FFN_CONV_TAPS, tn), lambda i, j: (0, j)),
            pl.BlockSpec((FFN_CONV_TAPS, tn), lambda i, j: (0, j + nj)),
            pl.BlockSpec((1, tn), lambda i, j: (0, j)),
            pl.BlockSpec((1, tn), lambda i, j: (0, j + nj)),
        ],
        out_specs=pl.BlockSpec((tm, tn), lambda i, j: (i, j)),
        out_shape=jax.ShapeDtypeStruct((n, D_FF), BF16),
        scratch_shapes=[
            pltpu.VMEM((FFN_HALO + tm, D_MODEL), BF16),
            pltpu.VMEM((tm // rows * (tn // FFN_COLS), FFN_HALO + rows, FFN_COLS), F32),
            pltpu.VMEM((tm // rows * (tn // FFN_COLS), FFN_HALO + rows, FFN_COLS), F32),
        ],
        compiler_params=_params("arbitrary", "arbitrary"),
        name="ffn_up",
    )(x2, x2, g, w_up, w_up, cw, cw, cb, cb)


def _ffn_down_body(act_ref, w_ref, x_ref, o_ref):
    o_ref[...] = x_ref[...] + jnp.dot(act_ref[...], w_ref[...], preferred_element_type=F32)


def _ffn_down(act, w_down, x2, *, layer, tm, tn):
    n = x2.shape[0]
    return pl.pallas_call(
        _ffn_down_body,
        grid=(n // tm, D_MODEL // tn),
        in_specs=[
            pl.BlockSpec((tm, D_FF), lambda i, j: (i, 0)),
            pl.BlockSpec((None, D_FF, tn), lambda i, j: (layer, 0, j)),
            pl.BlockSpec((tm, tn), lambda i, j: (i, j)),
        ],
        out_specs=pl.BlockSpec((tm, tn), lambda i, j: (i, j)),
        out_shape=jax.ShapeDtypeStruct((n, D_MODEL), F32),
        compiler_params=_params("arbitrary", "arbitrary"),
        name="ffn_down",
    )(act, w_down, x2)


def _final_norm_body(x_ref, g_ref, o_ref):
    o_ref[...] = _rms_norm_rows(x_ref[...], g_ref[...])


def _final_norm(x2, g, *, tm):
    n = x2.shape[0]
    return pl.pallas_call(
        _final_norm_body,
        grid=(n // tm,),
        in_specs=[pl.BlockSpec((tm, D_MODEL), lambda i: (i, 0)), pl.BlockSpec((1, D_MODEL), lambda i: (0, 0))],
        out_specs=pl.BlockSpec((tm, D_MODEL), lambda i: (i, 0)),
        out_shape=jax.ShapeDtypeStruct((n, D_MODEL), F32),
        compiler_params=_params("arbitrary"),
        name="final_norm",
    )(x2, g)


def _in_weight_layout(w_in):
    depth = w_in.shape[0]
    kv0 = ATTN_WIDTH
    gate0 = kv0 + KV_COLS
    conv0 = gate0 + GATE_COLS
    w_t = jnp.swapaxes(w_in, 1, 2)
    parts = [w_t[:, :ATTN_WIDTH], w_t[:, conv0:conv0 + 2 * CONV_CHANNELS]]
    for br in range(N_BRANCH):
        for h in range(N_KV_HEADS):
            for kv in range(2):
                c0 = kv0 + ((br * 2 + kv) * N_KV_HEADS + h) * HEAD_DIM
                parts.append(w_t[:, c0:c0 + HEAD_DIM])
    w_main = jnp.concatenate(parts, axis=1).astype(BF16)
    wg = w_in[..., gate0:gate0 + GATE_COLS].reshape(depth, D_MODEL, N_KV_HEADS, GQA_GROUP, N_BRANCH)
    wg = wg.transpose(0, 1, 2, 4, 3).reshape(depth, D_MODEL, N_KV_HEADS, N_BRANCH * GQA_GROUP)
    wg = wg.reshape(depth, D_MODEL, GATE_COLS)
    wg = jnp.pad(wg, ((0, 0), (0, 0), (0, LANES - GATE_COLS)))
    return w_main, wg.astype(BF16)


def _pick(n, prefs):
    for p in prefs:
        if n % p == 0:
            return p
    return n


def kernel(x, rel_bias, mix_norm_g, w_in, cmp_pos, cmp_w1, cmp_w2, conv_w, conv_b, conv_ln_g, conv_ln_b,
           w_out, ffn_norm_g, w_up, ffn_conv_w, ffn_conv_b, w_down, final_norm_g):
    bsz, t_len, _ = x.shape
    depth = w_in.shape[0]
    n = bsz * t_len
    assert t_len % 256 == 0 and t_len // SLC_LEN <= HEAD_DIM
    tq = 256
    tm = _pick(t_len, (1024, 512, 256))
    nc = t_len // CMP_STRIDE

    band, cmpb = _bias_tiles(rel_bias, t_len, tq)
    col_scale = jnp.concatenate([jnp.full((1, ATTN_WIDTH), Q_SCALE, F32),
                                 jnp.ones((1, PROJ_COLS - ATTN_WIDTH), F32)], axis=1)
    x2 = x.reshape(n, D_MODEL)
    w_main, wg = _in_weight_layout(w_in)
    w_out_b, w_up_b, w_down_b = w_out.astype(BF16), w_up.astype(BF16), w_down.astype(BF16)
    for l in range(depth):
        proj, gates = _in_projection(x2, mix_norm_g[l][None, :], w_main, wg, col_scale, layer=l, tm=tm, tn=1536)
        proj3 = proj.reshape(bsz, t_len, PROJ_COLS)
        gates3 = gates.reshape(bsz, t_len, LANES)

        posx, w1big, w2big = _compress_weight_layout(cmp_pos[l], cmp_w1[l], cmp_w2[l])
        tokl = proj3[:, :, COL_KV:COL_KV + CMP_WIDTH].reshape(bsz, nc, CMP_STRIDE, CMP_WIDTH).transpose(0, 2, 1, 3)
        kvc, kvct = _compress(tokl, posx, w1big, w2big)

        attn = _attention(proj3, kvc, kvct, band, cmpb, gates3, tq=tq)
        x2 = _mix_out(proj3, conv_w[l], conv_b[l][None, :], conv_ln_g[l][None, :], conv_ln_b[l][None, :],
                      attn, x2.reshape(bsz, t_len, D_MODEL), w_out_b, layer=l, tt=512).reshape(n, D_MODEL)

        act = _ffn_up(x2, ffn_norm_g[l][None, :], w_up_b, ffn_conv_w[l],
                      ffn_conv_b[l][None, :], layer=l, tm=tm, tn=512, t_len=t_len)
        x2 = _ffn_down(act, w_down_b, x2, layer=l, tm=tm, tn=512)
    x2 = _final_norm(x2, final_norm_g[None, :], tm=_pick(n, (1024, 512, 256)))
    return x2.reshape(bsz, t_len, D_MODEL)
```

```python
import functools
import math

import numpy as np
import jax
import jax.numpy as jnp
from jax import lax
from jax.experimental import pallas as pl
from jax.experimental.pallas import tpu as pltpu

F32 = jnp.float32
BF16 = jnp.bfloat16

D_MODEL = 2048
HEAD_DIM = 64
N_KV_HEADS = 4
GQA_GROUP = 4
N_Q_HEADS = N_KV_HEADS * GQA_GROUP
N_BRANCH = 3
ATTN_WIDTH = N_Q_HEADS * HEAD_DIM
CMP_LEN = 32
CMP_STRIDE = 16
SLC_LEN = 64
SLC_TOPK = 16
WINDOW = 512
CONV_CHANNELS = D_MODEL - ATTN_WIDTH
CONV_TAPS = 31
D_FF = 5632
FFN_CONV_TAPS = 3
N_BUCKETS = 32
MAX_DISTANCE = 128
NORM_EPS = 1e-6

KV_PAIR = 2 * HEAD_DIM
KV_COLS = N_BRANCH * N_KV_HEADS * KV_PAIR
GATE_COLS = N_BRANCH * N_Q_HEADS
COL_Q = 0
COL_CONV_A = ATTN_WIDTH
COL_CONV_G = COL_CONV_A + CONV_CHANNELS
COL_KV = COL_CONV_G + CONV_CHANNELS
PROJ_COLS = COL_KV + KV_COLS

MASK_VALUE = -1e30
BLOCK_PENALTY = -1e9
KEY_BIG = 1e30
KEY_TAKEN = -3e38
LOG2E = math.log2(math.e)
Q_SCALE = HEAD_DIM ** -0.5 * LOG2E

LANES = 128
SUBLANES = 8
CONV_ROWS = 128
FFN_ROWS = 1024
FFN_COLS = 512
VMEM_LIMIT = 56 * 1024 * 1024
CONV_HALO = 32
FFN_HALO = 16


def _t5_bucket_last_distance():
    n = np.arange(0, 4 * MAX_DISTANCE, dtype=np.int64)
    max_exact = N_BUCKETS // 2
    nf = np.maximum(n, 1).astype(np.float64)
    large = max_exact + np.floor(np.log(nf / max_exact) / math.log(MAX_DISTANCE / max_exact)
                                 * (N_BUCKETS - max_exact)).astype(np.int64)
    large = np.minimum(large, N_BUCKETS - 1)
    bucket = np.where(n < max_exact, n, large)
    last = []
    for b in range(N_BUCKETS - 1):
        idx = np.nonzero(bucket == b)[0]
        last.append(int(idx.max()) if idx.size else None)
    return last


_BUCKET_LAST = _t5_bucket_last_distance()


def _params(*sem):
    return pltpu.CompilerParams(dimension_semantics=sem, vmem_limit_bytes=VMEM_LIMIT)


def _sigmoid(x):
    return jax.nn.sigmoid(x)


def _rms_norm_rows(x, g):
    ms = jnp.mean(x * x, axis=-1, keepdims=True)
    return x * lax.rsqrt(ms + NORM_EPS) * g


def _bias_from_distance(dist, tab_ref, head):
    c_far = tab_ref[N_BUCKETS - 1, head]
    val = jnp.zeros(dist.shape, F32)
    for b in range(N_BUCKETS - 2, -1, -1):
        if _BUCKET_LAST[b] is None:
            continue
        val = jnp.where(dist <= _BUCKET_LAST[b], tab_ref[b, head] - c_far, val)
    return val


def _band_bias_body(tab_ref, o_ref, *, tq):
    head = pl.program_id(0) * GQA_GROUP + pl.program_id(1)
    shape = (WINDOW + tq, tq)
    dist = (lax.broadcasted_iota(jnp.int32, shape, 1) + WINDOW
            - lax.broadcasted_iota(jnp.int32, shape, 0))
    val = _bias_from_distance(dist, tab_ref, head) * LOG2E
    val = jnp.where(dist >= 0, jnp.where(dist < WINDOW, val, MASK_VALUE), MASK_VALUE)
    o_ref[0] = val


def _cmp_bias_body(tab_ref, o_ref, *, tq, nc):
    head = pl.program_id(0) * GQA_GROUP + pl.program_id(1)
    shape = (2 * nc, tq)
    r = lax.broadcasted_iota(jnp.int32, shape, 0)
    i = lax.broadcasted_iota(jnp.int32, shape, 1)
    dist = i - (r - nc) * CMP_STRIDE - (CMP_LEN - 1)
    val = _bias_from_distance(dist, tab_ref, head) * LOG2E
    o_ref[0] = jnp.where(dist >= 0, val, MASK_VALUE)


def _bias_tiles(rel_bias, t_len, tq):
    nc = t_len // CMP_STRIDE
    rows = GQA_GROUP * tq
    smem = pl.BlockSpec(memory_space=pltpu.SMEM)
    band = pl.pallas_call(
        functools.partial(_band_bias_body, tq=tq),
        grid=(N_KV_HEADS, GQA_GROUP),
        in_specs=[smem],
        out_specs=pl.BlockSpec((1, WINDOW + tq, tq), lambda h, g: (h, 0, g)),
        out_shape=jax.ShapeDtypeStruct((N_KV_HEADS, WINDOW + tq, rows), F32),
        compiler_params=_params("arbitrary", "arbitrary"),
        name="band_bias",
    )(rel_bias)
    cmpb = pl.pallas_call(
        functools.partial(_cmp_bias_body, tq=tq, nc=nc),
        grid=(N_KV_HEADS, GQA_GROUP),
        in_specs=[smem],
        out_specs=pl.BlockSpec((1, 2 * nc, tq), lambda h, g: (h, 0, g)),
        out_shape=jax.ShapeDtypeStruct((N_KV_HEADS, 2 * nc, rows), F32),
        compiler_params=_params("arbitrary", "arbitrary"),
        name="cmp_bias",
    )(rel_bias)
    return band, cmpb


def _inproj_body(x_ref, g_ref, w_ref, wg_ref, cs_ref, o_ref, gate_ref, h_ref):
    @pl.when(pl.program_id(1) == 0)
    def _():
        h = _rms_norm_rows(x_ref[...], g_ref[...]).astype(BF16)
        h_ref[...] = h
        gate_ref[...] = _sigmoid(jnp.dot(h, wg_ref[...], preferred_element_type=F32))

    acc = lax.dot_general(h_ref[...], w_ref[...], (((1,), (1,)), ((), ())),
                          preferred_element_type=F32)
    o_ref[...] = (acc * cs_ref[...]).astype(o_ref.dtype)


def _in_projection(x2, g, w, wg, col_scale, *, layer, tm, tn):
    n = x2.shape[0]
    gcols = wg.shape[-1]
    return pl.pallas_call(
        _inproj_body,
        grid=(n // tm, PROJ_COLS // tn),
        in_specs=[
            pl.BlockSpec((tm, D_MODEL), lambda i, j: (i, 0)),
            pl.BlockSpec((1, D_MODEL), lambda i, j: (0, 0)),
            pl.BlockSpec((None, tn, D_MODEL), lambda i, j: (layer, j, 0)),
            pl.BlockSpec((None, D_MODEL, gcols), lambda i, j: (layer, 0, 0)),
            pl.BlockSpec((1, tn), lambda i, j: (0, j)),
        ],
        out_specs=[
            pl.BlockSpec((tm, tn), lambda i, j: (i, j)),
            pl.BlockSpec((tm, gcols), lambda i, j: (i, 0)),
        ],
        out_shape=[
            jax.ShapeDtypeStruct((n, PROJ_COLS), BF16),
            jax.ShapeDtypeStruct((n, gcols), F32),
        ],
        scratch_shapes=[pltpu.VMEM((tm, D_MODEL), BF16)],
        compiler_params=_params("arbitrary", "arbitrary"),
        name="in_projection",
    )(x2, g, w, wg, col_scale)


CMP_WIDTH = N_KV_HEADS * KV_PAIR
CMP_TOKENS_PER_STEP = 8


def _compress_body(x_ref, pos_ref, w1_ref, w2_ref, o_ref, ot_ref, top_ref, bot_ref, *, nc):
    l = pl.program_id(1)

    @pl.when(l == 0)
    def _():
        top_ref[...] = jnp.zeros_like(top_ref)
        bot_ref[...] = jnp.zeros_like(bot_ref)

    top = top_ref[...]
    bot = bot_ref[...]
    for s in range(CMP_TOKENS_PER_STEP):
        x = x_ref[0, s].astype(F32)
        top = top + jnp.dot((x + pos_ref[s, 0]).astype(BF16), w1_ref[s, 0], preferred_element_type=F32)
        bot = bot + jnp.dot((x + pos_ref[s, 1]).astype(BF16), w1_ref[s, 1], preferred_element_type=F32)
    top_ref[...] = top
    bot_ref[...] = bot

    @pl.when(l == pl.num_programs(1) - 1)
    def _():
        pre = top_ref[...] + pltpu.roll(bot_ref[...], nc - 1, axis=0)
        act = pre * _sigmoid(pre)
        out = jnp.dot(act.astype(BF16), w2_ref[...], preferred_element_type=F32)
        for h in range(N_KV_HEADS):
            kv = out[:, h * KV_PAIR:(h + 1) * KV_PAIR]
            o_ref[0, h] = kv.astype(o_ref.dtype)
            ot_ref[0, h] = kv.T.astype(ot_ref.dtype)


def _compress(tokl, posx, w1big, w2big):
    b, _, nc, _ = tokl.shape
    return pl.pallas_call(
        functools.partial(_compress_body, nc=nc),
        grid=(b, CMP_STRIDE // CMP_TOKENS_PER_STEP),
        in_specs=[
            pl.BlockSpec((1, CMP_TOKENS_PER_STEP, nc, CMP_WIDTH), lambda i, l: (i, l, 0, 0)),
            pl.BlockSpec((CMP_TOKENS_PER_STEP, 2, 1, CMP_WIDTH), lambda i, l: (l, 0, 0, 0)),
            pl.BlockSpec((CMP_TOKENS_PER_STEP, 2, CMP_WIDTH, CMP_WIDTH), lambda i, l: (l, 0, 0, 0)),
            pl.BlockSpec((CMP_WIDTH, CMP_WIDTH), lambda i, l: (0, 0)),
        ],
        out_specs=[
            pl.BlockSpec((1, N_KV_HEADS, nc, KV_PAIR), lambda i, l: (i, 0, 0, 0)),
            pl.BlockSpec((1, N_KV_HEADS, KV_PAIR, nc), lambda i, l: (i, 0, 0, 0)),
        ],
        out_shape=[
            jax.ShapeDtypeStruct((b, N_KV_HEADS, nc, KV_PAIR), BF16),
            jax.ShapeDtypeStruct((b, N_KV_HEADS, KV_PAIR, nc), BF16),
        ],
        scratch_shapes=[pltpu.VMEM((nc, CMP_WIDTH), F32), pltpu.VMEM((nc, CMP_WIDTH), F32)],
        compiler_params=_params("arbitrary", "arbitrary"),
        name="compress",
    )(tokl, posx, w1big, w2big)


def _compress_weight_layout(pos, w1, w2):
    def block_diag(blocks):
        rows = []
        for h in range(N_KV_HEADS):
            for kv in range(2):
                off = (h * 2 + kv) * HEAD_DIM
                pad = [(0, 0)] * (blocks.ndim - 2) + [(off, CMP_WIDTH - HEAD_DIM - off)]
                rows.append(jnp.pad(blocks[kv], pad))
        return jnp.concatenate(rows, axis=-2).astype(BF16)

    w1r = w1.reshape(2, 2, CMP_STRIDE, HEAD_DIM, HEAD_DIM)
    w1big = block_diag(w1r.transpose(0, 2, 1, 3, 4))
    w2big = block_diag(w2)
    posr = pos.reshape(2, 2, CMP_STRIDE, HEAD_DIM).transpose(2, 1, 0, 3)
    posx = jnp.broadcast_to(posr[:, :, None], (CMP_STRIDE, 2, N_KV_HEADS, 2, HEAD_DIM))
    return posx.reshape(CMP_STRIDE, 2, 1, CMP_WIDTH), w1big, w2big


VT_ROWS = HEAD_DIM + 16
FAR_UNROLL = 4


def _attn_body(q_ref, kvc_ref, kvct_ref, kvs_ref, kvw_ref, bt_ref, bc_ref, gate_ref, o_ref,
               kp_ref, vst_ref, vwt_ref, key_ref, m_ref, acc_ref, sa_ref, sb_ref, ocmp_ref, qsel_ref, qwin_ref,
               gt_ref, *, tq, t_len):
    kv_head = pl.program_id(0)
    qi = pl.program_id(2)
    nc = t_len // CMP_STRIDE
    ns = t_len // SLC_LEN
    rows = GQA_GROUP * tq

    @pl.when(qi == 0)
    def _():
        k = kvs_ref[0][:, :HEAD_DIM]
        blk = lax.broadcasted_iota(jnp.int32, (t_len, HEAD_DIM), 0) // SLC_LEN
        col = lax.broadcasted_iota(jnp.int32, (t_len, HEAD_DIM), 1)
        onehot = jnp.where(blk == col, 1.0, 0.0).astype(BF16)
        kp_ref[...] = jnp.concatenate([k, onehot], axis=1)
        ones_rows = jnp.where(lax.broadcasted_iota(jnp.int32, (VT_ROWS - HEAD_DIM, tq), 0) == 0, 1.0, 0.0)
        for c in range(t_len // tq):
            st = kvs_ref[0, c * tq:(c + 1) * tq, :].astype(F32).T[HEAD_DIM:]
            vst_ref[c] = jnp.concatenate([st, ones_rows], axis=0).astype(BF16)
            wt = kvw_ref[0, c * tq:(c + 1) * tq, :].astype(F32).T[HEAD_DIM:]
            vwt_ref[c] = jnp.concatenate([wt, ones_rows], axis=0).astype(BF16)

    q_t = q_ref[0].astype(F32).T
    q4_t = jnp.concatenate([q_t[g * HEAD_DIM:(g + 1) * HEAD_DIM] for g in range(GQA_GROUP)],
                           axis=1).astype(BF16)
    qwin_ref[...] = jnp.concatenate([q4_t, jnp.zeros((HEAD_DIM, rows), BF16)], axis=0)
    qsel_ref[0:HEAD_DIM, :] = q4_t
    kw_ref = kvw_ref.at[0]
    win, sel = 0, 1

    def cmp_scores():
        bias_row = pl.multiple_of(nc - qi * (tq // CMP_STRIDE), tq // CMP_STRIDE)
        return (jnp.dot(kvc_ref[0, 0], qwin_ref[...], preferred_element_type=F32)
                + bc_ref[0, pl.ds(bias_row, nc), :])

    def cmp_finish(s):
        m = jnp.max(s, axis=0, keepdims=True)
        p = jnp.exp2(s - m)
        l = jnp.sum(p, axis=0, keepdims=True)
        pn = p * jnp.where(m > 0.5 * MASK_VALUE, 1.0 / l, 0.0)
        ocmp_ref[...] = jnp.dot(kvct_ref[0, 0, HEAD_DIM:, :], pn.astype(BF16),
                                preferred_element_type=F32)

        ps = pn[:, 0:tq] + pn[:, tq:2 * tq] + pn[:, 2 * tq:3 * tq] + pn[:, 3 * tq:4 * tq]
        sj = lax.broadcasted_iota(jnp.int32, (HEAD_DIM, nc), 0) * SLC_LEN
        ci = lax.broadcasted_iota(jnp.int32, (HEAD_DIM, nc), 1) * CMP_STRIDE
        overlap = jnp.where(ci < sj + SLC_LEN, jnp.where(ci + CMP_LEN > sj, 1.0, 0.0), 0.0).astype(BF16)
        p_hi = ps.astype(BF16)
        r_hi = ps - p_hi.astype(F32)
        p_md = r_hi.astype(BF16)
        p_lo = (r_hi - p_md.astype(F32)).astype(BF16)
        imp = (jnp.dot(overlap, p_hi, preferred_element_type=F32)
               + jnp.dot(overlap, p_md, preferred_element_type=F32)
               + jnp.dot(overlap, p_lo, preferred_element_type=F32))
        t = qi * tq + lax.broadcasted_iota(jnp.int32, (HEAD_DIM, tq), 1)
        blk = lax.broadcasted_iota(jnp.int32, (HEAD_DIM, tq), 0)
        cur = t // SLC_LEN
        key = jnp.where(blk == 0, KEY_BIG, jnp.where(blk == cur, KEY_BIG, jnp.where(blk == cur - 1, KEY_BIG, imp)))
        key_ref[...] = jnp.where(blk * SLC_LEN <= t, key, -KEY_BIG)

    def select_blocks():
        key = key_ref[...]
        blk = lax.broadcasted_iota(jnp.int32, (HEAD_DIM, tq), 0)
        pen = jnp.full((HEAD_DIM, tq), BLOCK_PENALTY, F32)
        for _ in range(min(SLC_TOPK, ns)):
            top = jnp.max(key, axis=0, keepdims=True)
            first = jnp.min(jnp.where(key == top, blk, HEAD_DIM), axis=0, keepdims=True)
            hit = blk == first
            pen = jnp.where(hit, 0.0, pen)
            key = jnp.where(hit, KEY_TAKEN, key)
        qsel_ref[HEAD_DIM:, :] = jnp.concatenate([pen.astype(BF16)] * GQA_GROUP, axis=1)

    def scores(qx_ref, k_ref, kj, bias_off):
        start = pl.multiple_of(kj * tq, tq)
        sc = jnp.dot(k_ref[pl.ds(start, tq), :], qx_ref[...], preferred_element_type=F32)
        if bias_off is not None:
            sc = sc + bt_ref[0, bias_off:bias_off + tq, :]
        return sc

    def consume(sc, vt_ref, kj, br, st, first):
        m_tile = jnp.max(sc, axis=0, keepdims=True)
        if first:
            m_new = m_tile
            pr = jnp.exp2(sc - m_new).astype(BF16)
            acc_ref[br, st] = jnp.dot(vt_ref[kj], pr, preferred_element_type=F32)
        else:
            m_old = m_ref[br, st]
            m_new = jnp.maximum(m_old, m_tile)
            alpha = jnp.exp2(m_old - m_new)
            pr = jnp.exp2(sc - m_new).astype(BF16)
            acc_ref[br, st] = alpha * acc_ref[br, st] + jnp.dot(vt_ref[kj], pr, preferred_element_type=F32)
        m_ref[br, st] = m_new

    def finish(br):
        m0 = m_ref[br, 0]
        m1 = m_ref[br, 1]
        m_all = jnp.maximum(m0, m1)
        acc = jnp.exp2(m0 - m_all) * acc_ref[br, 0] + jnp.exp2(m1 - m_all) * acc_ref[br, 1]
        return acc[0:HEAD_DIM] * (1.0 / acc[HEAD_DIM:HEAD_DIM + 1, :])

    def combine():
        o_win = finish(win)
        o_sel = finish(sel)
        o_cmp = ocmp_ref[...]
        gt_ref[...] = gate_ref[0].T
        base = kv_head * (N_BRANCH * GQA_GROUP)

        def gate(br, g):
            return gt_ref[pl.ds(base + br * GQA_GROUP + g, 1), :]

        combs = []
        for g in range(GQA_GROUP):
            sl = slice(g * tq, (g + 1) * tq)
            combs.append(gate(0, g) * o_cmp[:, sl] + gate(1, g) * o_sel[:, sl] + gate(2, g) * o_win[:, sl])
        outs = [jnp.concatenate(combs[p:p + 2], axis=0).T for p in range(0, GQA_GROUP, 2)]
        o_ref[0] = jnp.concatenate(outs, axis=1).astype(o_ref.dtype)

    @pl.when(qi >= 2)
    def _():
        sc_cmp = cmp_scores()
        sw0 = scores(qwin_ref, kw_ref, qi, WINDOW)
        cmp_finish(sc_cmp)
        sw1 = scores(qwin_ref, kw_ref, qi - 1, WINDOW - tq)
        consume(sw0, vwt_ref, qi, win, 0, True)
        sw2 = scores(qwin_ref, kw_ref, qi - 2, WINDOW - 2 * tq)
        consume(sw1, vwt_ref, qi - 1, win, 1, True)
        select_blocks()
        consume(sw2, vwt_ref, qi - 2, win, 0, False)

        ss0 = scores(qsel_ref, kp_ref, qi, WINDOW)
        ss1 = scores(qsel_ref, kp_ref, qi - 1, WINDOW - tq)
        consume(ss0, vst_ref, qi, sel, 0, True)
        sa_ref[...] = scores(qsel_ref, kp_ref, 0, None)
        consume(ss1, vst_ref, qi - 1, sel, 1, True)
        n_far = qi - 1

        def far_pair(pi):
            sb_ref[...] = scores(qsel_ref, kp_ref, 2 * pi + 1, None)
            consume(sa_ref[...], vst_ref, 2 * pi, sel, 1, False)
            sa_ref[...] = scores(qsel_ref, kp_ref, jnp.minimum(2 * pi + 2, n_far - 1), None)
            consume(sb_ref[...], vst_ref, 2 * pi + 1, sel, 0, False)

        n_pairs = n_far // 2

        def far_group(gi, carry):
            for u in range(FAR_UNROLL):
                far_pair(FAR_UNROLL * gi + u)
            return carry

        lax.fori_loop(0, n_pairs // FAR_UNROLL, far_group, 0)
        done = (n_pairs // FAR_UNROLL) * FAR_UNROLL
        left = n_pairs - done
        size = FAR_UNROLL // 2
        while size >= 1:
            @pl.when(left % (2 * size) >= size)
            def _(size=size, base=done):
                for u in range(size):
                    far_pair(base + u)
            done = done + jnp.where(left % (2 * size) >= size, size, 0)
            size //= 2

        @pl.when(n_far % 2 == 1)
        def _():
            consume(sa_ref[...], vst_ref, n_far - 1, sel, 1, False)

        combine()

    @pl.when(qi < 2)
    def _():
        cmp_finish(cmp_scores())
        select_blocks()
        sw0 = scores(qwin_ref, kw_ref, qi, WINDOW)
        ss0 = scores(qsel_ref, kp_ref, qi, WINDOW)
        consume(sw0, vwt_ref, qi, win, 0, True)
        consume(ss0, vst_ref, qi, sel, 0, True)
        for br in (win, sel):
            m_ref[br, 1] = jnp.full((1, rows), MASK_VALUE, F32)
            acc_ref[br, 1] = jnp.zeros((VT_ROWS, rows), F32)

        @pl.when(qi == 1)
        def _():
            sw1 = scores(qwin_ref, kw_ref, 0, WINDOW - tq)
            ss1 = scores(qsel_ref, kp_ref, 0, WINDOW - tq)
            consume(sw1, vwt_ref, 0, win, 1, False)
            consume(ss1, vst_ref, 0, sel, 1, False)

        combine()


def _attention(proj3, kvc, kvct, band, cmpb, gates3, *, tq):
    b, t_len, _ = proj3.shape
    assert WINDOW == 2 * tq
    nc = t_len // CMP_STRIDE
    rows = GQA_GROUP * tq
    qw = GQA_GROUP * HEAD_DIM
    slc_blk = COL_KV // KV_PAIR + N_KV_HEADS
    win_blk = COL_KV // KV_PAIR + 2 * N_KV_HEADS
    return pl.pallas_call(
        functools.partial(_attn_body, tq=tq, t_len=t_len),
        grid=(N_KV_HEADS, b, t_len // tq),
        in_specs=[
            pl.BlockSpec((1, tq, qw), lambda h, i, q: (i, q, h)),
            pl.BlockSpec((1, 1, nc, KV_PAIR), lambda h, i, q: (i, h, 0, 0)),
            pl.BlockSpec((1, 1, KV_PAIR, nc), lambda h, i, q: (i, h, 0, 0)),
            pl.BlockSpec((1, t_len, KV_PAIR), lambda h, i, q: (i, 0, slc_blk + h)),
            pl.BlockSpec((1, t_len, KV_PAIR), lambda h, i, q: (i, 0, win_blk + h)),
            pl.BlockSpec((1, WINDOW + tq, rows), lambda h, i, q: (h, 0, 0)),
            pl.BlockSpec((1, 2 * nc, rows), lambda h, i, q: (h, 0, 0)),
            pl.BlockSpec((1, tq, LANES), lambda h, i, q: (i, q, 0)),
        ],
        out_specs=pl.BlockSpec((1, tq, qw), lambda h, i, q: (i, q, h)),
        out_shape=jax.ShapeDtypeStruct((b, t_len, ATTN_WIDTH), BF16),
        scratch_shapes=[
            pltpu.VMEM((t_len, KV_PAIR), BF16),
            pltpu.VMEM((t_len // tq, VT_ROWS, tq), BF16),
            pltpu.VMEM((t_len // tq, VT_ROWS, tq), BF16),
            pltpu.VMEM((HEAD_DIM, tq), F32),
            pltpu.VMEM((2, 2, 1, rows), F32),
            pltpu.VMEM((2, 2, VT_ROWS, rows), F32),
            pltpu.VMEM((tq, rows), F32),
            pltpu.VMEM((tq, rows), F32),
            pltpu.VMEM((HEAD_DIM, rows), F32),
            pltpu.VMEM((KV_PAIR, rows), BF16),
            pltpu.VMEM((KV_PAIR, rows), BF16),
            pltpu.VMEM((LANES, tq), F32),
        ],
        compiler_params=_params("arbitrary", "arbitrary", "arbitrary"),
        name="nsa_attention",
    )(proj3, kvc, kvct, proj3, proj3, band, cmpb, gates3)


def _mix_out_body(a_ref, g_ref, ah_ref, gh_ref, w_ref, b_ref, lg_ref, lb_ref, attn_ref, x_ref, wa_ref, wc_ref,
                  o_ref, u_ref, y_ref, *, tt):
    ti = pl.program_id(1)
    u_ref[CONV_HALO:, :] = a_ref[0].astype(F32) * _sigmoid(g_ref[0].astype(F32))
    halo = ah_ref[0].astype(F32) * _sigmoid(gh_ref[0].astype(F32))
    u_ref[0:CONV_HALO, :] = jnp.where(ti > 0, halo, 0.0)
    base = CONV_HALO - (CONV_TAPS - 1)
    ext = CONV_ROWS + CONV_HALO
    for rb in range(tt // CONV_ROWS):
        rs = slice(rb * CONV_ROWS, (rb + 1) * CONV_ROWS)
        for cb in range(CONV_CHANNELS // LANES):
            cs = slice(cb * LANES, (cb + 1) * LANES)
            wblk = w_ref[:, cs]
            ublk = u_ref[rb * CONV_ROWS:rb * CONV_ROWS + ext, cs]
            acc = jnp.zeros((CONV_ROWS, LANES), F32)
            for r in range(SUBLANES):
                ur = ublk if r == 0 else pltpu.roll(ublk, ext - r, axis=0)
                for a in range(CONV_HALO // SUBLANES + 1):
                    k = SUBLANES * a + r - base
                    if 0 <= k < CONV_TAPS:
                        acc = acc + wblk[k:k + 1, :] * ur[SUBLANES * a:SUBLANES * a + CONV_ROWS]
            y_ref[rs, cs] = acc
        acc = y_ref[rs, :] + b_ref[...]
        mu = jnp.mean(acc, axis=-1, keepdims=True)
        xc = acc - mu
        var = jnp.mean(xc * xc, axis=-1, keepdims=True)
        y = xc * lax.rsqrt(var + NORM_EPS) * lg_ref[...] + lb_ref[...]
        conv_act = (y * _sigmoid(y)).astype(BF16)
        o_ref[0, rs, :] = (x_ref[0, rs, :]
                           + jnp.dot(attn_ref[0, rs, :], wa_ref[...], preferred_element_type=F32)
                           + jnp.dot(conv_act, wc_ref[...], preferred_element_type=F32))


def _mix_out(proj3, w, b, lg, lb, attn, x3, w_out, *, layer, tt):
    bsz, t_len, _ = proj3.shape
    a_blk = COL_CONV_A // CONV_CHANNELS
    g_blk = COL_CONV_G // CONV_CHANNELS
    hpt = tt // CONV_HALO
    halo_idx = lambda i, t: jnp.maximum(t * hpt - 1, 0)
    vec = pl.BlockSpec((1, CONV_CHANNELS), lambda i, t: (0, 0))
    conv_rows = ATTN_WIDTH // CONV_CHANNELS
    return pl.pallas_call(
        functools.partial(_mix_out_body, tt=tt),
        grid=(bsz, t_len // tt),
        in_specs=[
            pl.BlockSpec((1, tt, CONV_CHANNELS), lambda i, t: (i, t, a_blk)),
            pl.BlockSpec((1, tt, CONV_CHANNELS), lambda i, t: (i, t, g_blk)),
            pl.BlockSpec((1, CONV_HALO, CONV_CHANNELS), lambda i, t: (i, halo_idx(i, t), a_blk)),
            pl.BlockSpec((1, CONV_HALO, CONV_CHANNELS), lambda i, t: (i, halo_idx(i, t), g_blk)),
            pl.BlockSpec((CONV_TAPS, CONV_CHANNELS), lambda i, t: (0, 0)),
            vec, vec, vec,
            pl.BlockSpec((1, tt, ATTN_WIDTH), lambda i, t: (i, t, 0)),
            pl.BlockSpec((1, tt, D_MODEL), lambda i, t: (i, t, 0)),
            pl.BlockSpec((None, ATTN_WIDTH, D_MODEL), lambda i, t: (layer, 0, 0)),
            pl.BlockSpec((None, CONV_CHANNELS, D_MODEL), lambda i, t: (layer, conv_rows, 0)),
        ],
        out_specs=pl.BlockSpec((1, tt, D_MODEL), lambda i, t: (i, t, 0)),
        out_shape=jax.ShapeDtypeStruct((bsz, t_len, D_MODEL), F32),
        scratch_shapes=[pltpu.VMEM((CONV_HALO + tt, CONV_CHANNELS), F32),
                        pltpu.VMEM((tt, CONV_CHANNELS), F32)],
        compiler_params=_params("arbitrary", "arbitrary"),
        name="mix_out",
    )(proj3, proj3, proj3, proj3, w, b, lg, lb, attn, x3, w_out, w_out)


def _ffn_up_body(x_ref, xh_ref, g_ref, wa_ref, wg_ref, cwa_ref, cwg_ref, cba_ref, cbg_ref, o_ref,
                 h_ref, ua_ref, ug_ref, *, tm, tiles_per_seq, rows):
    i = pl.program_id(0)

    @pl.when(pl.program_id(1) == 0)
    def _():
        h_ref[FFN_HALO:, :] = _rms_norm_rows(x_ref[...], g_ref[...]).astype(BF16)
        hh = _rms_norm_rows(xh_ref[...], g_ref[...])
        h_ref[0:FFN_HALO, :] = jnp.where(i % tiles_per_seq != 0, hh, 0.0).astype(BF16)

    units = [(u, c, slice(w, w + FFN_COLS)) for u, (c, w) in enumerate(
        (c, w) for c in range(tm // rows) for w in range(0, o_ref.shape[1], FFN_COLS))]

    def project(u, c, cs):
        hc = h_ref[c * rows:(c + 1) * rows + FFN_HALO, :]
        ug_ref[u] = jnp.dot(hc, wg_ref[:, cs], preferred_element_type=F32)
        ua_ref[u] = jnp.dot(hc, wa_ref[:, cs], preferred_element_type=F32)

    def conv(u_ref, u, cs, cw_ref, cb_ref):
        x = u_ref[u]
        y = cw_ref[FFN_CONV_TAPS - 1:FFN_CONV_TAPS, cs] * x[FFN_HALO:]
        for s in range(1, FFN_CONV_TAPS):
            k = FFN_CONV_TAPS - 1 - s
            y = y + cw_ref[k:k + 1, cs] * pltpu.roll(x, s, axis=0)[FFN_HALO:]
        return y + cb_ref[:, cs]

    def finish(u, c, cs):
        gate = conv(ug_ref, u, cs, cwg_ref, cbg_ref)
        act = gate * _sigmoid(gate)
        a = conv(ua_ref, u, cs, cwa_ref, cba_ref)
        o_ref[c * rows:(c + 1) * rows, cs] = (act * a).astype(o_ref.dtype)

    project(*units[0])
    for i_unit, unit in enumerate(units):
        if i_unit + 1 < len(units):
            project(*units[i_unit + 1])
        finish(*unit)


def _ffn_up(x2, g, w_up, cw, cb, *, layer, tm, tn, t_len):
    n = x2.shape[0]
    nj = D_FF // tn
    hpt = tm // FFN_HALO
    rows = min(FFN_ROWS, tm)
    return pl.pallas_call(
        functools.partial(_ffn_up_body, tm=tm, tiles_per_seq=t_len // tm, rows=rows),
        grid=(n // tm, nj),
        in_specs=[
            pl.BlockSpec((tm, D_MODEL), lambda i, j: (i, 0)),
            pl.BlockSpec((FFN_HALO, D_MODEL), lambda i, j: (jnp.maximum(i * hpt - 1, 0), 0)),
            pl.BlockSpec((1, D_MODEL), lambda i, j: (0, 0)),
            pl.BlockSpec((None, D_MODEL, tn), lambda i, j: (layer, 0, j)),
            pl.BlockSpec((None, D_MODEL, tn), lambda i, j: (layer, 0, j + nj)),
            pl.BlockSpec((FFN_CONV_TAPS, tn), lambda i, j: (0, j)),
            pl.BlockSpec((FFN_CONV_TAPS, tn), lambda i, j: (0, j + nj)),
            pl.BlockSpec((1, tn), lambda i, j: (0, j)),
            pl.BlockSpec((1, tn), lambda i, j: (0, j + nj)),
        ],
        out_specs=pl.BlockSpec((tm, tn), lambda i, j: (i, j)),
        out_shape=jax.ShapeDtypeStruct((n, D_FF), BF16),
        scratch_shapes=[
            pltpu.VMEM((FFN_HALO + tm, D_MODEL), BF16),
            pltpu.VMEM((tm // rows * (tn // FFN_COLS), FFN_HALO + rows, FFN_COLS), F32),
            pltpu.VMEM((tm // rows * (tn // FFN_COLS), FFN_HALO + rows, FFN_COLS), F32),
        ],
        compiler_params=_params("arbitrary", "arbitrary"),
        name="ffn_up",
    )(x2, x2, g, w_up, w_up, cw, cw, cb, cb)


def _ffn_down_body(act_ref, w_ref, x_ref, o_ref):
    o_ref[...] = x_ref[...] + jnp.dot(act_ref[...], w_ref[...], preferred_element_type=F32)


def _ffn_down(act, w_down, x2, *, layer, tm, tn):
    n = x2.shape[0]
    return pl.pallas_call(
        _ffn_down_body,
        grid=(n // tm, D_MODEL // tn),
        in_specs=[
            pl.BlockSpec((tm, D_FF), lambda i, j: (i, 0)),
            pl.BlockSpec((None, D_FF, tn), lambda i, j: (layer, 0, j)),
            pl.BlockSpec((tm, tn), lambda i, j: (i, j)),
        ],
        out_specs=pl.BlockSpec((tm, tn), lambda i, j: (i, j)),
        out_shape=jax.ShapeDtypeStruct((n, D_MODEL), F32),
        compiler_params=_params("arbitrary", "arbitrary"),
        name="ffn_down",
    )(act, w_down, x2)


def _final_norm_body(x_ref, g_ref, o_ref):
    o_ref[...] = _rms_norm_rows(x_ref[...], g_ref[...])


def _final_norm(x2, g, *, tm):
    n = x2.shape[0]
    return pl.pallas_call(
        _final_norm_body,
        grid=(n // tm,),
        in_specs=[pl.BlockSpec((tm, D_MODEL), lambda i: (i, 0)), pl.BlockSpec((1, D_MODEL), lambda i: (0, 0))],
        out_specs=pl.BlockSpec((tm, D_MODEL), lambda i: (i, 0)),
        out_shape=jax.ShapeDtypeStruct((n, D_MODEL), F32),
        compiler_params=_params("arbitrary"),
        name="final_norm",
    )(x2, g)


def _in_weight_layout(w_in):
    depth = w_in.shape[0]
    kv0 = ATTN_WIDTH
    gate0 = kv0 + KV_COLS
    conv0 = gate0 + GATE_COLS
    w_t = jnp.swapaxes(w_in, 1, 2)
    parts = [w_t[:, :ATTN_WIDTH], w_t[:, conv0:conv0 + 2 * CONV_CHANNELS]]
    for br in range(N_BRANCH):
        for h in range(N_KV_HEADS):
            for kv in range(2):
                c0 = kv0 + ((br * 2 + kv) * N_KV_HEADS + h) * HEAD_DIM
                parts.append(w_t[:, c0:c0 + HEAD_DIM])
    w_main = jnp.concatenate(parts, axis=1).astype(BF16)
    wg = w_in[..., gate0:gate0 + GATE_COLS].reshape(depth, D_MODEL, N_KV_HEADS, GQA_GROUP, N_BRANCH)
    wg = wg.transpose(0, 1, 2, 4, 3).reshape(depth, D_MODEL, N_KV_HEADS, N_BRANCH * GQA_GROUP)
    wg = wg.reshape(depth, D_MODEL, GATE_COLS)
    wg = jnp.pad(wg, ((0, 0), (0, 0), (0, LANES - GATE_COLS)))
    return w_main, wg.astype(BF16)


def _pick(n, prefs):
    for p in prefs:
        if n % p == 0:
            return p
    return n


def kernel(x, rel_bias, mix_norm_g, w_in, cmp_pos, cmp_w1, cmp_w2, conv_w, conv_b, conv_ln_g, conv_ln_b,
           w_out, ffn_norm_g, w_up, ffn_conv_w, ffn_conv_b, w_down, final_norm_g):
    bsz, t_len, _ = x.shape
    depth = w_in.shape[0]
    n = bsz * t_len
    assert t_len % 256 == 0 and t_len // SLC_LEN <= HEAD_DIM
    tq = 256
    tm = _pick(t_len, (1024, 512, 256))
    nc = t_len // CMP_STRIDE

    band, cmpb = _bias_tiles(rel_bias, t_len, tq)
    col_scale = jnp.concatenate([jnp.full((1, ATTN_WIDTH), Q_SCALE, F32),
                                 jnp.ones((1, PROJ_COLS - ATTN_WIDTH), F32)], axis=1)
    x2 = x.reshape(n, D_MODEL)
    w_main, wg = _in_weight_layout(w_in)
    w_out_b, w_up_b, w_down_b = w_out.astype(BF16), w_up.astype(BF16), w_down.astype(BF16)
    for l in range(depth):
        proj, gates = _in_projection(x2, mix_norm_g[l][None, :], w_main, wg, col_scale, layer=l, tm=tm, tn=1536)
        proj3 = proj.reshape(bsz, t_len, PROJ_COLS)
        gates3 = gates.reshape(bsz, t_len, LANES)

        posx, w1big, w2big = _compress_weight_layout(cmp_pos[l], cmp_w1[l], cmp_w2[l])
        tokl = proj3[:, :, COL_KV:COL_KV + CMP_WIDTH].reshape(bsz, nc, CMP_STRIDE, CMP_WIDTH).transpose(0, 2, 1, 3)
        kvc, kvct = _compress(tokl, posx, w1big, w2big)

        attn = _attention(proj3, kvc, kvct, band, cmpb, gates3, tq=tq)
        x2 = _mix_out(proj3, conv_w[l], conv_b[l][None, :], conv_ln_g[l][None, :], conv_ln_b[l][None, :],
                      attn, x2.reshape(bsz, t_len, D_MODEL), w_out_b, layer=l, tt=512).reshape(n, D_MODEL)

        act = _ffn_up(x2, ffn_norm_g[l][None, :], w_up_b, ffn_conv_w[l],
                      ffn_conv_b[l][None, :], layer=l, tm=tm, tn=512, t_len=t_len)
        x2 = _ffn_down(act, w_down_b, x2, layer=l, tm=tm, tn=512)
    x2 = _final_norm(x2, final_norm_g[None, :], tm=_pick(n, (1024, 512, 256)))
    return x2.reshape(bsz, t_len, D_MODEL)
```

```python
import functools
import math

import numpy as np
import jax
import jax.numpy as jnp
from jax import lax
from jax.experimental import pallas as pl
from jax.experimental.pallas import tpu as pltpu

F32 = jnp.float32
BF16 = jnp.bfloat16

D_MODEL = 2048
HEAD_DIM = 64
N_KV_HEADS = 4
GQA_GROUP = 4
N_Q_HEADS = N_KV_HEADS * GQA_GROUP
N_BRANCH = 3
ATTN_WIDTH = N_Q_HEADS * HEAD_DIM
CMP_LEN = 32
CMP_STRIDE = 16
SLC_LEN = 64
SLC_TOPK = 16
WINDOW = 512
CONV_CHANNELS = D_MODEL - ATTN_WIDTH
CONV_TAPS = 31
D_FF = 5632
FFN_CONV_TAPS = 3
N_BUCKETS = 32
MAX_DISTANCE = 128
NORM_EPS = 1e-6

KV_PAIR = 2 * HEAD_DIM
KV_COLS = N_BRANCH * N_KV_HEADS * KV_PAIR
GATE_COLS = N_BRANCH * N_Q_HEADS
COL_Q = 0
COL_CONV_A = ATTN_WIDTH
COL_CONV_G = COL_CONV_A + CONV_CHANNELS
COL_KV = COL_CONV_G + CONV_CHANNELS
PROJ_COLS = COL_KV + KV_COLS

MASK_VALUE = -1e30
BLOCK_PENALTY = -1e9
KEY_BIG = 1e30
KEY_TAKEN = -3e38
LOG2E = math.log2(math.e)
Q_SCALE = HEAD_DIM ** -0.5 * LOG2E

LANES = 128
SUBLANES = 8
CONV_ROWS = 128
FFN_ROWS = 1024
FFN_COLS = 512
VMEM_LIMIT = 56 * 1024 * 1024
CONV_HALO = 32
FFN_HALO = 16


def _t5_bucket_last_distance():
    n = np.arange(0, 4 * MAX_DISTANCE, dtype=np.int64)
    max_exact = N_BUCKETS // 2
    nf = np.maximum(n, 1).astype(np.float64)
    large = max_exact + np.floor(np.log(nf / max_exact) / math.log(MAX_DISTANCE / max_exact)
                                 * (N_BUCKETS - max_exact)).astype(np.int64)
    large = np.minimum(large, N_BUCKETS - 1)
    bucket = np.where(n < max_exact, n, large)
    last = []
    for b in range(N_BUCKETS - 1):
        idx = np.nonzero(bucket == b)[0]
        last.append(int(idx.max()) if idx.size else None)
    return last


_BUCKET_LAST = _t5_bucket_last_distance()


def _params(*sem):
    return pltpu.CompilerParams(dimension_semantics=sem, vmem_limit_bytes=VMEM_LIMIT)


def _sigmoid(x):
    return jax.nn.sigmoid(x)


def _rms_norm_rows(x, g):
    ms = jnp.mean(x * x, axis=-1, keepdims=True)
    return x * lax.rsqrt(ms + NORM_EPS) * g


def _bias_from_distance(dist, tab_ref, head):
    c_far = tab_ref[N_BUCKETS - 1, head]
    val = jnp.zeros(dist.shape, F32)
    for b in range(N_BUCKETS - 2, -1, -1):
        if _BUCKET_LAST[b] is None:
            continue
        val = jnp.where(dist <= _BUCKET_LAST[b], tab_ref[b, head] - c_far, val)
    return val


def _band_bias_body(tab_ref, o_ref, *, tq):
    head = pl.program_id(0) * GQA_GROUP + pl.program_id(1)
    shape = (WINDOW + tq, tq)
    dist = (lax.broadcasted_iota(jnp.int32, shape, 1) + WINDOW
            - lax.broadcasted_iota(jnp.int32, shape, 0))
    val = _bias_from_distance(dist, tab_ref, head) * LOG2E
    val = jnp.where(dist >= 0, jnp.where(dist < WINDOW, val, MASK_VALUE), MASK_VALUE)
    o_ref[0] = val


def _cmp_bias_body(tab_ref, o_ref, *, tq, nc):
    head = pl.program_id(0) * GQA_GROUP + pl.program_id(1)
    shape = (2 * nc, tq)
    r = lax.broadcasted_iota(jnp.int32, shape, 0)
    i = lax.broadcasted_iota(jnp.int32, shape, 1)
    dist = i - (r - nc) * CMP_STRIDE - (CMP_LEN - 1)
    val = _bias_from_distance(dist, tab_ref, head) * LOG2E
    o_ref[0] = jnp.where(dist >= 0, val, MASK_VALUE)


def _bias_tiles(rel_bias, t_len, tq):
    nc = t_len // CMP_STRIDE
    rows = GQA_GROUP * tq
    smem = pl.BlockSpec(memory_space=pltpu.SMEM)
    band = pl.pallas_call(
        functools.partial(_band_bias_body, tq=tq),
        grid=(N_KV_HEADS, GQA_GROUP),
        in_specs=[smem],
        out_specs=pl.BlockSpec((1, WINDOW + tq, tq), lambda h, g: (h, 0, g)),
        out_shape=jax.ShapeDtypeStruct((N_KV_HEADS, WINDOW + tq, rows), F32),
        compiler_params=_params("arbitrary", "arbitrary"),
        name="band_bias",
    )(rel_bias)
    cmpb = pl.pallas_call(
        functools.partial(_cmp_bias_body, tq=tq, nc=nc),
        grid=(N_KV_HEADS, GQA_GROUP),
        in_specs=[smem],
        out_specs=pl.BlockSpec((1, 2 * nc, tq), lambda h, g: (h, 0, g)),
        out_shape=jax.ShapeDtypeStruct((N_KV_HEADS, 2 * nc, rows), F32),
        compiler_params=_params("arbitrary", "arbitrary"),
        name="cmp_bias",
    )(rel_bias)
    return band, cmpb


def _inproj_body(x_ref, g_ref, w_ref, wg_ref, cs_ref, o_ref, gate_ref, h_ref):
    @pl.when(pl.program_id(1) == 0)
    def _():
        h = _rms_norm_rows(x_ref[...], g_ref[...]).astype(BF16)
        h_ref[...] = h
        gate_ref[...] = _sigmoid(jnp.dot(h, wg_ref[...], preferred_element_type=F32))

    acc = lax.dot_general(h_ref[...], w_ref[...], (((1,), (1,)), ((), ())),
                          preferred_element_type=F32)
    o_ref[...] = (acc * cs_ref[...]).astype(o_ref.dtype)


def _in_projection(x2, g, w, wg, col_scale, *, layer, tm, tn):
    n = x2.shape[0]
    gcols = wg.shape[-1]
    return pl.pallas_call(
        _inproj_body,
        grid=(n // tm, PROJ_COLS // tn),
        in_specs=[
            pl.BlockSpec((tm, D_MODEL), lambda i, j: (i, 0)),
            pl.BlockSpec((1, D_MODEL), lambda i, j: (0, 0)),
            pl.BlockSpec((None, tn, D_MODEL), lambda i, j: (layer, j, 0)),
            pl.BlockSpec((None, D_MODEL, gcols), lambda i, j: (layer, 0, 0)),
            pl.BlockSpec((1, tn), lambda i, j: (0, j)),
        ],
        out_specs=[
            pl.BlockSpec((tm, tn), lambda i, j: (i, j)),
            pl.BlockSpec((tm, gcols), lambda i, j: (i, 0)),
        ],
        out_shape=[
            jax.ShapeDtypeStruct((n, PROJ_COLS), BF16),
            jax.ShapeDtypeStruct((n, gcols), F32),
        ],
        scratch_shapes=[pltpu.VMEM((tm, D_MODEL), BF16)],
        compiler_params=_params("arbitrary", "arbitrary"),
        name="in_projection",
    )(x2, g, w, wg, col_scale)


CMP_WIDTH = N_KV_HEADS * KV_PAIR
CMP_TOKENS_PER_STEP = 8


def _compress_body(x_ref, pos_ref, w1_ref, w2_ref, o_ref, ot_ref, top_ref, bot_ref, *, nc):
    l = pl.program_id(1)

    @pl.when(l == 0)
    def _():
        top_ref[...] = jnp.zeros_like(top_ref)
        bot_ref[...] = jnp.zeros_like(bot_ref)

    top = top_ref[...]
    bot = bot_ref[...]
    for s in range(CMP_TOKENS_PER_STEP):
        x = x_ref[0, s].astype(F32)
        top = top + jnp.dot((x + pos_ref[s, 0]).astype(BF16), w1_ref[s, 0], preferred_element_type=F32)
        bot = bot + jnp.dot((x + pos_ref[s, 1]).astype(BF16), w1_ref[s, 1], preferred_element_type=F32)
    top_ref[...] = top
    bot_ref[...] = bot

    @pl.when(l == pl.num_programs(1) - 1)
    def _():
        pre = top_ref[...] + pltpu.roll(bot_ref[...], nc - 1, axis=0)
        act = pre * _sigmoid(pre)
        out = jnp.dot(act.astype(BF16), w2_ref[...], preferred_element_type=F32)
        for h in range(N_KV_HEADS):
            kv = out[:, h * KV_PAIR:(h + 1) * KV_PAIR]
            o_ref[0, h] = kv.astype(o_ref.dtype)
            ot_ref[0, h] = kv.T.astype(ot_ref.dtype)


def _compress(tokl, posx, w1big, w2big):
    b, _, nc, _ = tokl.shape
    return pl.pallas_call(
        functools.partial(_compress_body, nc=nc),
        grid=(b, CMP_STRIDE // CMP_TOKENS_PER_STEP),
        in_specs=[
            pl.BlockSpec((1, CMP_TOKENS_PER_STEP, nc, CMP_WIDTH), lambda i, l: (i, l, 0, 0)),
            pl.BlockSpec((CMP_TOKENS_PER_STEP, 2, 1, CMP_WIDTH), lambda i, l: (l, 0, 0, 0)),
            pl.BlockSpec((CMP_TOKENS_PER_STEP, 2, CMP_WIDTH, CMP_WIDTH), lambda i, l: (l, 0, 0, 0)),
            pl.BlockSpec((CMP_WIDTH, CMP_WIDTH), lambda i, l: (0, 0)),
        ],
        out_specs=[
            pl.BlockSpec((1, N_KV_HEADS, nc, KV_PAIR), lambda i, l: (i, 0, 0, 0)),
            pl.BlockSpec((1, N_KV_HEADS, KV_PAIR, nc), lambda i, l: (i, 0, 0, 0)),
        ],
        out_shape=[
            jax.ShapeDtypeStruct((b, N_KV_HEADS, nc, KV_PAIR), BF16),
            jax.ShapeDtypeStruct((b, N_KV_HEADS, KV_PAIR, nc), BF16),
        ],
        scratch_shapes=[pltpu.VMEM((nc, CMP_WIDTH), F32), pltpu.VMEM((nc, CMP_WIDTH), F32)],
        compiler_params=_params("arbitrary", "arbitrary"),
        name="compress",
    )(tokl, posx, w1big, w2big)


def _compress_weight_layout(pos, w1, w2):
    def block_diag(blocks):
        rows = []
        for h in range(N_KV_HEADS):
            for kv in range(2):
                off = (h * 2 + kv) * HEAD_DIM
                pad = [(0, 0)] * (blocks.ndim - 2) + [(off, CMP_WIDTH - HEAD_DIM - off)]
                rows.append(jnp.pad(blocks[kv], pad))
        return jnp.concatenate(rows, axis=-2).astype(BF16)

    w1r = w1.reshape(2, 2, CMP_STRIDE, HEAD_DIM, HEAD_DIM)
    w1big = block_diag(w1r.transpose(0, 2, 1, 3, 4))
    w2big = block_diag(w2)
    posr = pos.reshape(2, 2, CMP_STRIDE, HEAD_DIM).transpose(2, 1, 0, 3)
    posx = jnp.broadcast_to(posr[:, :, None], (CMP_STRIDE, 2, N_KV_HEADS, 2, HEAD_DIM))
    return posx.reshape(CMP_STRIDE, 2, 1, CMP_WIDTH), w1big, w2big


VT_ROWS = HEAD_DIM + 16
FAR_UNROLL = 4


def _attn_body(q_ref, kvc_ref, kvct_ref, kvs_ref, kvw_ref, bt_ref, bc_ref, gate_ref, o_ref,
               kp_ref, vst_ref, vwt_ref, key_ref, m_ref, acc_ref, sa_ref, sb_ref, ocmp_ref, qsel_ref, qwin_ref,
               gt_ref, *, tq, t_len):
    kv_head = pl.program_id(0)
    qi = pl.program_id(2)
    nc = t_len // CMP_STRIDE
    ns = t_len // SLC_LEN
    rows = GQA_GROUP * tq

    @pl.when(qi == 0)
    def _():
        k = kvs_ref[0][:, :HEAD_DIM]
        blk = lax.broadcasted_iota(jnp.int32, (t_len, HEAD_DIM), 0) // SLC_LEN
        col = lax.broadcasted_iota(jnp.int32, (t_len, HEAD_DIM), 1)
        onehot = jnp.where(blk == col, 1.0, 0.0).astype(BF16)
        kp_ref[...] = jnp.concatenate([k, onehot], axis=1)
        ones_rows = jnp.where(lax.broadcasted_iota(jnp.int32, (VT_ROWS - HEAD_DIM, tq), 0) == 0, 1.0, 0.0)
        for c in range(t_len // tq):
            st = kvs_ref[0, c * tq:(c + 1) * tq, :].astype(F32).T[HEAD_DIM:]
            vst_ref[c] = jnp.concatenate([st, ones_rows], axis=0).astype(BF16)
            wt = kvw_ref[0, c * tq:(c + 1) * tq, :].astype(F32).T[HEAD_DIM:]
            vwt_ref[c] = jnp.concatenate([wt, ones_rows], axis=0).astype(BF16)

    q_t = q_ref[0].astype(F32).T
    q4_t = jnp.concatenate([q_t[g * HEAD_DIM:(g + 1) * HEAD_DIM] for g in range(GQA_GROUP)],
                           axis=1).astype(BF16)
    qwin_ref[...] = jnp.concatenate([q4_t, jnp.zeros((HEAD_DIM, rows), BF16)], axis=0)
    qsel_ref[0:HEAD_DIM, :] = q4_t
    kw_ref = kvw_ref.at[0]
    win, sel = 0, 1

    def cmp_scores():
        bias_row = pl.multiple_of(nc - qi * (tq // CMP_STRIDE), tq // CMP_STRIDE)
        return (jnp.dot(kvc_ref[0, 0], qwin_ref[...], preferred_element_type=F32)
                + bc_ref[0, pl.ds(bias_row, nc), :])

    def cmp_finish(s):
        m = jnp.max(s, axis=0, keepdims=True)
        p = jnp.exp2(s - m)
        l = jnp.sum(p, axis=0, keepdims=True)
        pn = p * jnp.where(m > 0.5 * MASK_VALUE, 1.0 / l, 0.0)
        ocmp_ref[...] = jnp.dot(kvct_ref[0, 0, HEAD_DIM:, :], pn.astype(BF16),
                                preferred_element_type=F32)

        ps = pn[:, 0:tq] + pn[:, tq:2 * tq] + pn[:, 2 * tq:3 * tq] + pn[:, 3 * tq:4 * tq]
        sj = lax.broadcasted_iota(jnp.int32, (HEAD_DIM, nc), 0) * SLC_LEN
        ci = lax.broadcasted_iota(jnp.int32, (HEAD_DIM, nc), 1) * CMP_STRIDE
        overlap = jnp.where(ci < sj + SLC_LEN, jnp.where(ci + CMP_LEN > sj, 1.0, 0.0), 0.0).astype(BF16)
        p_hi = ps.astype(BF16)
        r_hi = ps - p_hi.astype(F32)
        p_md = r_hi.astype(BF16)
        p_lo = (r_hi - p_md.astype(F32)).astype(BF16)
        imp = (jnp.dot(overlap, p_hi, preferred_element_type=F32)
               + jnp.dot(overlap, p_md, preferred_element_type=F32)
               + jnp.dot(overlap, p_lo, preferred_element_type=F32))
        t = qi * tq + lax.broadcasted_iota(jnp.int32, (HEAD_DIM, tq), 1)
        blk = lax.broadcasted_iota(jnp.int32, (HEAD_DIM, tq), 0)
        cur = t // SLC_LEN
        key = jnp.where(blk == 0, KEY_BIG, jnp.where(blk == cur, KEY_BIG, jnp.where(blk == cur - 1, KEY_BIG, imp)))
        key_ref[...] = jnp.where(blk * SLC_LEN <= t, key, -KEY_BIG)

    def select_blocks():
        key = key_ref[...]
        blk = lax.broadcasted_iota(jnp.int32, (HEAD_DIM, tq), 0)
        pen = jnp.full((HEAD_DIM, tq), BLOCK_PENALTY, F32)
        for _ in range(min(SLC_TOPK, ns)):
            top = jnp.max(key, axis=0, keepdims=True)
            first = jnp.min(jnp.where(key == top, blk, HEAD_DIM), axis=0, keepdims=True)
            hit = blk == first
            pen = jnp.where(hit, 0.0, pen)
            key = jnp.where(hit, KEY_TAKEN, key)
        qsel_ref[HEAD_DIM:, :] = jnp.concatenate([pen.astype(BF16)] * GQA_GROUP, axis=1)

    def scores(qx_ref, k_ref, kj, bias_off):
        start = pl.multiple_of(kj * tq, tq)
        sc = jnp.dot(k_ref[pl.ds(start, tq), :], qx_ref[...], preferred_element_type=F32)
        if bias_off is not None:
            sc = sc + bt_ref[0, bias_off:bias_off + tq, :]
        return sc

    def consume(sc, vt_ref, kj, br, st, first):
        m_tile = jnp.max(sc, axis=0, keepdims=True)
        if first:
            m_new = m_tile
            pr = jnp.exp2(sc - m_new).astype(BF16)
            acc_ref[br, st] = jnp.dot(vt_ref[kj], pr, preferred_element_type=F32)
        else:
            m_old = m_ref[br, st]
            m_new = jnp.maximum(m_old, m_tile)
            alpha = jnp.exp2(m_old - m_new)
            pr = jnp.exp2(sc - m_new).astype(BF16)
            acc_ref[br, st] = alpha * acc_ref[br, st] + jnp.dot(vt_ref[kj], pr, preferred_element_type=F32)
        m_ref[br, st] = m_new

    def finish(br):
        m0 = m_ref[br, 0]
        m1 = m_ref[br, 1]
        m_all = jnp.maximum(m0, m1)
        acc = jnp.exp2(m0 - m_all) * acc_ref[br, 0] + jnp.exp2(m1 - m_all) * acc_ref[br, 1]
        return acc[0:HEAD_DIM] * (1.0 / acc[HEAD_DIM:HEAD_DIM + 1, :])

    def combine():
        o_win = finish(win)
        o_sel = finish(sel)
        o_cmp = ocmp_ref[...]
        gt_ref[...] = gate_ref[0].T
        base = kv_head * (N_BRANCH * GQA_GROUP)

        def gate(br, g):
            return gt_ref[pl.ds(base + br * GQA_GROUP + g, 1), :]

        combs = []
        for g in range(GQA_GROUP):
            sl = slice(g * tq, (g + 1) * tq)
            combs.append(gate(0, g) * o_cmp[:, sl] + gate(1, g) * o_sel[:, sl] + gate(2, g) * o_win[:, sl])
        outs = [jnp.concatenate(combs[p:p + 2], axis=0).T for p in range(0, GQA_GROUP, 2)]
        o_ref[0] = jnp.concatenate(outs, axis=1).astype(o_ref.dtype)

    @pl.when(qi >= 2)
    def _():
        sc_cmp = cmp_scores()
        sw0 = scores(qwin_ref, kw_ref, qi, WINDOW)
        cmp_finish(sc_cmp)
        sw1 = scores(qwin_ref, kw_ref, qi - 1, WINDOW - tq)
        consume(sw0, vwt_ref, qi, win, 0, True)
        sw2 = scores(qwin_ref, kw_ref, qi - 2, WINDOW - 2 * tq)
        consume(sw1, vwt_ref, qi - 1, win, 1, True)
        select_blocks()
        consume(sw2, vwt_ref, qi - 2, win, 0, False)

        ss0 = scores(qsel_ref, kp_ref, qi, WINDOW)
        ss1 = scores(qsel_ref, kp_ref, qi - 1, WINDOW - tq)
        consume(ss0, vst_ref, qi, sel, 0, True)
        sa_ref[...] = scores(qsel_ref, kp_ref, 0, None)
        consume(ss1, vst_ref, qi - 1, sel, 1, True)
        n_far = qi - 1

        def far_pair(pi):
            sb_ref[...] = scores(qsel_ref, kp_ref, 2 * pi + 1, None)
            consume(sa_ref[...], vst_ref, 2 * pi, sel, 1, False)
            sa_ref[...] = scores(qsel_ref, kp_ref, jnp.minimum(2 * pi + 2, n_far - 1), None)
            consume(sb_ref[...], vst_ref, 2 * pi + 1, sel, 0, False)

        n_pairs = n_far // 2

        def far_group(gi, carry):
            for u in range(FAR_UNROLL):
                far_pair(FAR_UNROLL * gi + u)
            return carry

        lax.fori_loop(0, n_pairs // FAR_UNROLL, far_group, 0)
        done = (n_pairs // FAR_UNROLL) * FAR_UNROLL
        left = n_pairs - done
        size = FAR_UNROLL // 2
        while size >= 1:
            @pl.when(left % (2 * size) >= size)
            def _(size=size, base=done):
                for u in range(size):
                    far_pair(base + u)
            done = done + jnp.where(left % (2 * size) >= size, size, 0)
            size //= 2

        @pl.when(n_far % 2 == 1)
        def _():
            consume(sa_ref[...], vst_ref, n_far - 1, sel, 1, False)

        combine()

    @pl.when(qi < 2)
    def _():
        cmp_finish(cmp_scores())
        select_blocks()
        sw0 = scores(qwin_ref, kw_ref, qi, WINDOW)
        ss0 = scores(qsel_ref, kp_ref, qi, WINDOW)
        consume(sw0, vwt_ref, qi, win, 0, True)
        consume(ss0, vst_ref, qi, sel, 0, True)
        for br in (win, sel):
            m_ref[br, 1] = jnp.full((1, rows), MASK_VALUE, F32)
            acc_ref[br, 1] = jnp.zeros((VT_ROWS, rows), F32)

        @pl.when(qi == 1)
        def _():
            sw1 = scores(qwin_ref, kw_ref, 0, WINDOW - tq)
            ss1 = scores(qsel_ref, kp_ref, 0, WINDOW - tq)
            consume(sw1, vwt_ref, 0, win, 1, False)
            consume(ss1, vst_ref, 0, sel, 1, False)

        combine()


def _attention(proj3, kvc, kvct, band, cmpb, gates3, *, tq):
    b, t_len, _ = proj3.shape
    assert WINDOW == 2 * tq
    nc = t_len // CMP_STRIDE
    rows = GQA_GROUP * tq
    qw = GQA_GROUP * HEAD_DIM
    slc_blk = COL_KV // KV_PAIR + N_KV_HEADS
    win_blk = COL_KV // KV_PAIR + 2 * N_KV_HEADS
    return pl.pallas_call(
        functools.partial(_attn_body, tq=tq, t_len=t_len),
        grid=(N_KV_HEADS, b, t_len // tq),
        in_specs=[
            pl.BlockSpec((1, tq, qw), lambda h, i, q: (i, q, h)),
            pl.BlockSpec((1, 1, nc, KV_PAIR), lambda h, i, q: (i, h, 0, 0)),
            pl.BlockSpec((1, 1, KV_PAIR, nc), lambda h, i, q: (i, h, 0, 0)),
            pl.BlockSpec((1, t_len, KV_PAIR), lambda h, i, q: (i, 0, slc_blk + h)),
            pl.BlockSpec((1, t_len, KV_PAIR), lambda h, i, q: (i, 0, win_blk + h)),
            pl.BlockSpec((1, WINDOW + tq, rows), lambda h, i, q: (h, 0, 0)),
            pl.BlockSpec((1, 2 * nc, rows), lambda h, i, q: (h, 0, 0)),
            pl.BlockSpec((1, tq, LANES), lambda h, i, q: (i, q, 0)),
        ],
        out_specs=pl.BlockSpec((1, tq, qw), lambda h, i, q: (i, q, h)),
        out_shape=jax.ShapeDtypeStruct((b, t_len, ATTN_WIDTH), BF16),
        scratch_shapes=[
            pltpu.VMEM((t_len, KV_PAIR), BF16),
            pltpu.VMEM((t_len // tq, VT_ROWS, tq), BF16),
            pltpu.VMEM((t_len // tq, VT_ROWS, tq), BF16),
            pltpu.VMEM((HEAD_DIM, tq), F32),
            pltpu.VMEM((2, 2, 1, rows), F32),
            pltpu.VMEM((2, 2, VT_ROWS, rows), F32),
            pltpu.VMEM((tq, rows), F32),
            pltpu.VMEM((tq, rows), F32),
            pltpu.VMEM((HEAD_DIM, rows), F32),
            pltpu.VMEM((KV_PAIR, rows), BF16),
            pltpu.VMEM((KV_PAIR, rows), BF16),
            pltpu.VMEM((LANES, tq), F32),
        ],
        compiler_params=_params("arbitrary", "arbitrary", "arbitrary"),
        name="nsa_attention",
    )(proj3, kvc, kvct, proj3, proj3, band, cmpb, gates3)


def _mix_out_body(a_ref, g_ref, ah_ref, gh_ref, w_ref, b_ref, lg_ref, lb_ref, attn_ref, x_ref, wa_ref, wc_ref,
                  o_ref, u_ref, y_ref, *, tt):
    ti = pl.program_id(1)
    acc_attn = jnp.dot(attn_ref[0], wa_ref[...], preferred_element_type=F32)
    u_ref[CONV_HALO:, :] = a_ref[0].astype(F32) * _sigmoid(g_ref[0].astype(F32))
    halo = ah_ref[0].astype(F32) * _sigmoid(gh_ref[0].astype(F32))
    u_ref[0:CONV_HALO, :] = jnp.where(ti > 0, halo, 0.0)
    base = CONV_HALO - (CONV_TAPS - 1)
    ext = CONV_ROWS + CONV_HALO
    for rb in range(tt // CONV_ROWS):
        rs = slice(rb * CONV_ROWS, (rb + 1) * CONV_ROWS)
        for cb in range(CONV_CHANNELS // LANES):
            cs = slice(cb * LANES, (cb + 1) * LANES)
            wblk = w_ref[:, cs]
            ublk = u_ref[rb * CONV_ROWS:rb * CONV_ROWS + ext, cs]
            acc = jnp.zeros((CONV_ROWS, LANES), F32)
            for r in range(SUBLANES):
                ur = ublk if r == 0 else pltpu.roll(ublk, ext - r, axis=0)
                for a in range(CONV_HALO // SUBLANES + 1):
                    k = SUBLANES * a + r - base
                    if 0 <= k < CONV_TAPS:
                        acc = acc + wblk[k:k + 1, :] * ur[SUBLANES * a:SUBLANES * a + CONV_ROWS]
            y_ref[rs, cs] = acc
        acc = y_ref[rs, :] + b_ref[...]
        mu = jnp.mean(acc, axis=-1, keepdims=True)
        xc = acc - mu
        var = jnp.mean(xc * xc, axis=-1, keepdims=True)
        y = xc * lax.rsqrt(var + NORM_EPS) * lg_ref[...] + lb_ref[...]
        conv_act = (y * _sigmoid(y)).astype(BF16)
        o_ref[0, rs, :] = (x_ref[0, rs, :] + acc_attn[rs]
                           + jnp.dot(conv_act, wc_ref[...], preferred_element_type=F32))


def _mix_out(proj3, w, b, lg, lb, attn, x3, w_out, *, layer, tt):
    bsz, t_len, _ = proj3.shape
    a_blk = COL_CONV_A // CONV_CHANNELS
    g_blk = COL_CONV_G // CONV_CHANNELS
    hpt = tt // CONV_HALO
    halo_idx = lambda i, t: jnp.maximum(t * hpt - 1, 0)
    vec = pl.BlockSpec((1, CONV_CHANNELS), lambda i, t: (0, 0))
    conv_rows = ATTN_WIDTH // CONV_CHANNELS
    return pl.pallas_call(
        functools.partial(_mix_out_body, tt=tt),
        grid=(bsz, t_len // tt),
        in_specs=[
            pl.BlockSpec((1, tt, CONV_CHANNELS), lambda i, t: (i, t, a_blk)),
            pl.BlockSpec((1, tt, CONV_CHANNELS), lambda i, t: (i, t, g_blk)),
            pl.BlockSpec((1, CONV_HALO, CONV_CHANNELS), lambda i, t: (i, halo_idx(i, t), a_blk)),
            pl.BlockSpec((1, CONV_HALO, CONV_CHANNELS), lambda i, t: (i, halo_idx(i, t), g_blk)),
            pl.BlockSpec((CONV_TAPS, CONV_CHANNELS), lambda i, t: (0, 0)),
            vec, vec, vec,
            pl.BlockSpec((1, tt, ATTN_WIDTH), lambda i, t: (i, t, 0)),
            pl.BlockSpec((1, tt, D_MODEL), lambda i, t: (i, t, 0)),
            pl.BlockSpec((None, ATTN_WIDTH, D_MODEL), lambda i, t: (layer, 0, 0)),
            pl.BlockSpec((None, CONV_CHANNELS, D_MODEL), lambda i, t: (layer, conv_rows, 0)),
        ],
        out_specs=pl.BlockSpec((1, tt, D_MODEL), lambda i, t: (i, t, 0)),
        out_shape=jax.ShapeDtypeStruct((bsz, t_len, D_MODEL), F32),
        scratch_shapes=[pltpu.VMEM((CONV_HALO + tt, CONV_CHANNELS), F32),
                        pltpu.VMEM((tt, CONV_CHANNELS), F32)],
        compiler_params=_params("arbitrary", "arbitrary"),
        name="mix_out",
    )(proj3, proj3, proj3, proj3, w, b, lg, lb, attn, x3, w_out, w_out)


def _ffn_up_body(x_ref, xh_ref, g_ref, wa_ref, wg_ref, cwa_ref, cwg_ref, cba_ref, cbg_ref, o_ref,
                 h_ref, ua_ref, ug_ref, *, tm, tiles_per_seq, rows):
    i = pl.program_id(0)

    @pl.when(pl.program_id(1) == 0)
    def _():
        h_ref[FFN_HALO:, :] = _rms_norm_rows(x_ref[...], g_ref[...]).astype(BF16)
        hh = _rms_norm_rows(xh_ref[...], g_ref[...])
        h_ref[0:FFN_HALO, :] = jnp.where(i % tiles_per_seq != 0, hh, 0.0).astype(BF16)

    units = [(u, c, slice(w, w + FFN_COLS)) for u, (c, w) in enumerate(
        (c, w) for c in range(tm // rows) for w in range(0, o_ref.shape[1], FFN_COLS))]

    def project(u, c, cs):
        hc = h_ref[c * rows:(c + 1) * rows + FFN_HALO, :]
        ug_ref[u] = jnp.dot(hc, wg_ref[:, cs], preferred_element_type=F32)
        ua_ref[u] = jnp.dot(hc, wa_ref[:, cs], preferred_element_type=F32)

    def conv(u_ref, u, cs, cw_ref, cb_ref):
        x = u_ref[u]
        y = cw_ref[FFN_CONV_TAPS - 1:FFN_CONV_TAPS, cs] * x[FFN_HALO:]
        for s in range(1, FFN_CONV_TAPS):
            k = FFN_CONV_TAPS - 1 - s
            y = y + cw_ref[k:k + 1, cs] * pltpu.roll(x, s, axis=0)[FFN_HALO:]
        return y + cb_ref[:, cs]

    def finish(u, c, cs):
        gate = conv(ug_ref, u, cs, cwg_ref, cbg_ref)
        act = gate * _sigmoid(gate)
        a = conv(ua_ref, u, cs, cwa_ref, cba_ref)
        o_ref[c * rows:(c + 1) * rows, cs] = (act * a).astype(o_ref.dtype)

    project(*units[0])
    for i_unit, unit in enumerate(units):
        if i_unit + 1 < len(units):
            project(*units[i_unit + 1])
        finish(*unit)


def _ffn_up(x2, g, w_up, cw, cb, *, layer, tm, tn, t_len):
    n = x2.shape[0]
    nj = D_FF // tn
    hpt = tm // FFN_HALO
    rows = min(FFN_ROWS, tm)
    return pl.pallas_call(
        functools.partial(_ffn_up_body, tm=tm, tiles_per_seq=t_len // tm, rows=rows),
        grid=(n // tm, nj),
        in_specs=[
            pl.BlockSpec((tm, D_MODEL), lambda i, j: (i, 0)),
            pl.BlockSpec((FFN_HALO, D_MODEL), lambda i, j: (jnp.maximum(i * hpt - 1, 0), 0)),
            pl.BlockSpec((1, D_MODEL), lambda i, j: (0, 0)),
            pl.BlockSpec((None, D_MODEL, tn), lambda i, j: (layer, 0, j)),
            pl.BlockSpec((None, D_MODEL, tn), lambda i, j: (layer, 0, j + nj)),
            pl.BlockSpec((FFN_CONV_TAPS, tn), lambda i, j: (0, j)),
            pl.BlockSpec((FFN_CONV_TAPS, tn), lambda i, j: (0, j + nj)),
            pl.BlockSpec((1, tn), lambda i, j: (0, j)),
            pl.BlockSpec((1, tn), lambda i, j: (0, j + nj)),
        ],
        out_specs=pl.BlockSpec((tm, tn), lambda i, j: (i, j)),
        out_shape=jax.ShapeDtypeStruct((n, D_FF), BF16),
        scratch_shapes=[
            pltpu.VMEM((FFN_HALO + tm, D_MODEL), BF16),
            pltpu.VMEM((tm // rows * (tn // FFN_COLS), FFN_HALO + rows, FFN_COLS), F32),
            pltpu.VMEM((tm // rows * (tn // FFN_COLS), FFN_HALO + rows, FFN_COLS), F32),
        ],
        compiler_params=_params("arbitrary", "arbitrary"),
        name="ffn_up",
    )(x2, x2, g, w_up, w_up, cw, cw, cb, cb)


def _ffn_down_body(act_ref, w_ref, x_ref, o_ref):
    o_ref[...] = x_ref[...] + jnp.dot(act_ref[...], w_ref[...], preferred_element_type=F32)


def _ffn_down(act, w_down, x2, *, layer, tm, tn):
    n = x2.shape[0]
    return pl.pallas_call(
        _ffn_down_body,
        grid=(n // tm, D_MODEL // tn),
        in_specs=[
            pl.BlockSpec((tm, D_FF), lambda i, j: (i, 0)),
            pl.BlockSpec((None, D_FF, tn), lambda i, j: (layer, 0, j)),
            pl.BlockSpec((tm, tn), lambda i, j: (i, j)),
        ],
        out_specs=pl.BlockSpec((tm, tn), lambda i, j: (i, j)),
        out_shape=jax.ShapeDtypeStruct((n, D_MODEL), F32),
        compiler_params=_params("arbitrary", "arbitrary"),
        name="ffn_down",
    )(act, w_down, x2)


def _ffn_down_norm_body(act_ref, w_ref, x_ref, g_ref, o_ref):
    y = x_ref[...] + jnp.dot(act_ref[...], w_ref[...], preferred_element_type=F32)
    o_ref[...] = _rms_norm_rows(y, g_ref[...])


def _ffn_down_norm(act, w_down, x2, g, *, layer, tm):
    n = x2.shape[0]
    return pl.pallas_call(
        _ffn_down_norm_body,
        grid=(n // tm,),
        in_specs=[
            pl.BlockSpec((tm, D_FF), lambda i: (i, 0)),
            pl.BlockSpec((None, D_FF, D_MODEL), lambda i: (layer, 0, 0), pipeline_mode=pl.Buffered(1)),
            pl.BlockSpec((tm, D_MODEL), lambda i: (i, 0)),
            pl.BlockSpec((1, D_MODEL), lambda i: (0, 0)),
        ],
        out_specs=pl.BlockSpec((tm, D_MODEL), lambda i: (i, 0)),
        out_shape=jax.ShapeDtypeStruct((n, D_MODEL), F32),
        compiler_params=_params("arbitrary"),
        name="ffn_down_norm",
    )(act, w_down, x2, g)


def _final_norm_body(x_ref, g_ref, o_ref):
    o_ref[...] = _rms_norm_rows(x_ref[...], g_ref[...])


def _final_norm(x2, g, *, tm):
    n = x2.shape[0]
    return pl.pallas_call(
        _final_norm_body,
        grid=(n // tm,),
        in_specs=[pl.BlockSpec((tm, D_MODEL), lambda i: (i, 0)), pl.BlockSpec((1, D_MODEL), lambda i: (0, 0))],
        out_specs=pl.BlockSpec((tm, D_MODEL), lambda i: (i, 0)),
        out_shape=jax.ShapeDtypeStruct((n, D_MODEL), F32),
        compiler_params=_params("arbitrary"),
        name="final_norm",
    )(x2, g)


def _in_weight_layout(w_in):
    depth = w_in.shape[0]
    kv0 = ATTN_WIDTH
    gate0 = kv0 + KV_COLS
    conv0 = gate0 + GATE_COLS
    w_t = jnp.swapaxes(w_in, 1, 2)
    parts = [w_t[:, :ATTN_WIDTH], w_t[:, conv0:conv0 + 2 * CONV_CHANNELS]]
    for br in range(N_BRANCH):
        for h in range(N_KV_HEADS):
            for kv in range(2):
                c0 = kv0 + ((br * 2 + kv) * N_KV_HEADS + h) * HEAD_DIM
                parts.append(w_t[:, c0:c0 + HEAD_DIM])
    w_main = jnp.concatenate(parts, axis=1).astype(BF16)
    wg = w_in[..., gate0:gate0 + GATE_COLS].reshape(depth, D_MODEL, N_KV_HEADS, GQA_GROUP, N_BRANCH)
    wg = wg.transpose(0, 1, 2, 4, 3).reshape(depth, D_MODEL, N_KV_HEADS, N_BRANCH * GQA_GROUP)
    wg = wg.reshape(depth, D_MODEL, GATE_COLS)
    wg = jnp.pad(wg, ((0, 0), (0, 0), (0, LANES - GATE_COLS)))
    return w_main, wg.astype(BF16)


def _pick(n, prefs):
    for p in prefs:
        if n % p == 0:
            return p
    return n


def kernel(x, rel_bias, mix_norm_g, w_in, cmp_pos, cmp_w1, cmp_w2, conv_w, conv_b, conv_ln_g, conv_ln_b,
           w_out, ffn_norm_g, w_up, ffn_conv_w, ffn_conv_b, w_down, final_norm_g):
    bsz, t_len, _ = x.shape
    depth = w_in.shape[0]
    n = bsz * t_len
    assert t_len % 256 == 0 and t_len // SLC_LEN <= HEAD_DIM
    tq = 256
    tm = _pick(t_len, (1024, 512, 256))
    nc = t_len // CMP_STRIDE

    band, cmpb = _bias_tiles(rel_bias, t_len, tq)
    col_scale = jnp.concatenate([jnp.full((1, ATTN_WIDTH), Q_SCALE, F32),
                                 jnp.ones((1, PROJ_COLS - ATTN_WIDTH), F32)], axis=1)
    x2 = x.reshape(n, D_MODEL)
    w_main, wg = _in_weight_layout(w_in)
    w_out_b, w_up_b, w_down_b = w_out.astype(BF16), w_up.astype(BF16), w_down.astype(BF16)
    for l in range(depth):
        proj, gates = _in_projection(x2, mix_norm_g[l][None, :], w_main, wg, col_scale, layer=l, tm=tm, tn=1536)
        proj3 = proj.reshape(bsz, t_len, PROJ_COLS)
        gates3 = gates.reshape(bsz, t_len, LANES)

        posx, w1big, w2big = _compress_weight_layout(cmp_pos[l], cmp_w1[l], cmp_w2[l])
        tokl = proj3[:, :, COL_KV:COL_KV + CMP_WIDTH].reshape(bsz, nc, CMP_STRIDE, CMP_WIDTH).transpose(0, 2, 1, 3)
        kvc, kvct = _compress(tokl, posx, w1big, w2big)

        attn = _attention(proj3, kvc, kvct, band, cmpb, gates3, tq=tq)
        x2 = _mix_out(proj3, conv_w[l], conv_b[l][None, :], conv_ln_g[l][None, :], conv_ln_b[l][None, :],
                      attn, x2.reshape(bsz, t_len, D_MODEL), w_out_b, layer=l, tt=512).reshape(n, D_MODEL)

        act = _ffn_up(x2, ffn_norm_g[l][None, :], w_up_b, ffn_conv_w[l],
                      ffn_conv_b[l][None, :], layer=l, tm=tm, tn=512, t_len=t_len)
        if l + 1 < depth:
            x2 = _ffn_down(act, w_down_b, x2, layer=l, tm=tm, tn=512)
        else:
            x2 = _ffn_down_norm(act, w_down_b, x2, final_norm_g[None, :], layer=l, tm=256)
    return x2.reshape(bsz, t_len, D_MODEL)
```

```python
import functools
import math

import numpy as np
import jax
import jax.numpy as jnp
from jax import lax
from jax.experimental import pallas as pl
from jax.experimental.pallas import tpu as pltpu

F32 = jnp.float32
BF16 = jnp.bfloat16

D_MODEL = 2048
HEAD_DIM = 64
N_KV_HEADS = 4
GQA_GROUP = 4
N_Q_HEADS = N_KV_HEADS * GQA_GROUP
N_BRANCH = 3
ATTN_WIDTH = N_Q_HEADS * HEAD_DIM
CMP_LEN = 32
CMP_STRIDE = 16
SLC_LEN = 64
SLC_TOPK = 16
WINDOW = 512
CONV_CHANNELS = D_MODEL - ATTN_WIDTH
CONV_TAPS = 31
D_FF = 5632
FFN_CONV_TAPS = 3
N_BUCKETS = 32
MAX_DISTANCE = 128
NORM_EPS = 1e-6

KV_PAIR = 2 * HEAD_DIM
KV_COLS = N_BRANCH * N_KV_HEADS * KV_PAIR
GATE_COLS = N_BRANCH * N_Q_HEADS
COL_Q = 0
COL_CONV_A = ATTN_WIDTH
COL_CONV_G = COL_CONV_A + CONV_CHANNELS
COL_KV = COL_CONV_G + CONV_CHANNELS
PROJ_COLS = COL_KV + KV_COLS

MASK_VALUE = -1e30
BLOCK_PENALTY = -1e9
KEY_BIG = 1e30
KEY_TAKEN = -3e38
LOG2E = math.log2(math.e)
Q_SCALE = HEAD_DIM ** -0.5 * LOG2E

LANES = 128
SUBLANES = 8
CONV_ROWS = 128
FFN_ROWS = 1024
FFN_COLS = 512
VMEM_LIMIT = 56 * 1024 * 1024
CONV_HALO = 32
FFN_HALO = 16


def _t5_bucket_last_distance():
    n = np.arange(0, 4 * MAX_DISTANCE, dtype=np.int64)
    max_exact = N_BUCKETS // 2
    nf = np.maximum(n, 1).astype(np.float64)
    large = max_exact + np.floor(np.log(nf / max_exact) / math.log(MAX_DISTANCE / max_exact)
                                 * (N_BUCKETS - max_exact)).astype(np.int64)
    large = np.minimum(large, N_BUCKETS - 1)
    bucket = np.where(n < max_exact, n, large)
    last = []
    for b in range(N_BUCKETS - 1):
        idx = np.nonzero(bucket == b)[0]
        last.append(int(idx.max()) if idx.size else None)
    return last


_BUCKET_LAST = _t5_bucket_last_distance()


def _params(*sem):
    return pltpu.CompilerParams(dimension_semantics=sem, vmem_limit_bytes=VMEM_LIMIT)


def _sigmoid(x):
    return jax.nn.sigmoid(x)


def _rms_norm_rows(x, g):
    ms = jnp.mean(x * x, axis=-1, keepdims=True)
    return x * lax.rsqrt(ms + NORM_EPS) * g


def _bias_from_distance(dist, tab_ref, head):
    c_far = tab_ref[N_BUCKETS - 1, head]
    val = jnp.zeros(dist.shape, F32)
    for b in range(N_BUCKETS - 2, -1, -1):
        if _BUCKET_LAST[b] is None:
            continue
        val = jnp.where(dist <= _BUCKET_LAST[b], tab_ref[b, head] - c_far, val)
    return val


def _band_bias_body(tab_ref, o_ref, *, tq):
    head = pl.program_id(0) * GQA_GROUP + pl.program_id(1)
    shape = (WINDOW + tq, tq)
    dist = (lax.broadcasted_iota(jnp.int32, shape, 1) + WINDOW
            - lax.broadcasted_iota(jnp.int32, shape, 0))
    val = _bias_from_distance(dist, tab_ref, head) * LOG2E
    val = jnp.where(dist >= 0, jnp.where(dist < WINDOW, val, MASK_VALUE), MASK_VALUE)
    o_ref[0] = val


def _cmp_bias_body(tab_ref, o_ref, *, tq, nc):
    head = pl.program_id(0) * GQA_GROUP + pl.program_id(1)
    shape = (2 * nc, tq)
    r = lax.broadcasted_iota(jnp.int32, shape, 0)
    i = lax.broadcasted_iota(jnp.int32, shape, 1)
    dist = i - (r - nc) * CMP_STRIDE - (CMP_LEN - 1)
    val = _bias_from_distance(dist, tab_ref, head) * LOG2E
    o_ref[0] = jnp.where(dist >= 0, val, MASK_VALUE)


def _bias_tiles(rel_bias, t_len, tq):
    nc = t_len // CMP_STRIDE
    rows = GQA_GROUP * tq
    smem = pl.BlockSpec(memory_space=pltpu.SMEM)
    band = pl.pallas_call(
        functools.partial(_band_bias_body, tq=tq),
        grid=(N_KV_HEADS, GQA_GROUP),
        in_specs=[smem],
        out_specs=pl.BlockSpec((1, WINDOW + tq, tq), lambda h, g: (h, 0, g)),
        out_shape=jax.ShapeDtypeStruct((N_KV_HEADS, WINDOW + tq, rows), F32),
        compiler_params=_params("arbitrary", "arbitrary"),
        name="band_bias",
    )(rel_bias)
    cmpb = pl.pallas_call(
        functools.partial(_cmp_bias_body, tq=tq, nc=nc),
        grid=(N_KV_HEADS, GQA_GROUP),
        in_specs=[smem],
        out_specs=pl.BlockSpec((1, 2 * nc, tq), lambda h, g: (h, 0, g)),
        out_shape=jax.ShapeDtypeStruct((N_KV_HEADS, 2 * nc, rows), F32),
        compiler_params=_params("arbitrary", "arbitrary"),
        name="cmp_bias",
    )(rel_bias)
    return band, cmpb


def _inproj_body(x_ref, g_ref, w_ref, wg_ref, cs_ref, o_ref, gate_ref, h_ref):
    @pl.when(pl.program_id(1) == 0)
    def _():
        h = _rms_norm_rows(x_ref[...], g_ref[...]).astype(BF16)
        h_ref[...] = h
        gate_ref[...] = _sigmoid(jnp.dot(h, wg_ref[...], preferred_element_type=F32))

    acc = lax.dot_general(h_ref[...], w_ref[...], (((1,), (1,)), ((), ())),
                          preferred_element_type=F32)
    o_ref[...] = (acc * cs_ref[...]).astype(o_ref.dtype)


def _in_projection(x2, g, w, wg, col_scale, *, layer, tm, tn):
    n = x2.shape[0]
    gcols = wg.shape[-1]
    return pl.pallas_call(
        _inproj_body,
        grid=(n // tm, PROJ_COLS // tn),
        in_specs=[
            pl.BlockSpec((tm, D_MODEL), lambda i, j: (i, 0)),
            pl.BlockSpec((1, D_MODEL), lambda i, j: (0, 0)),
            pl.BlockSpec((None, tn, D_MODEL), lambda i, j: (layer, j, 0)),
            pl.BlockSpec((None, D_MODEL, gcols), lambda i, j: (layer, 0, 0)),
            pl.BlockSpec((1, tn), lambda i, j: (0, j)),
        ],
        out_specs=[
            pl.BlockSpec((tm, tn), lambda i, j: (i, j)),
            pl.BlockSpec((tm, gcols), lambda i, j: (i, 0)),
        ],
        out_shape=[
            jax.ShapeDtypeStruct((n, PROJ_COLS), BF16),
            jax.ShapeDtypeStruct((n, gcols), F32),
        ],
        scratch_shapes=[pltpu.VMEM((tm, D_MODEL), BF16)],
        compiler_params=_params("arbitrary", "arbitrary"),
        name="in_projection",
    )(x2, g, w, wg, col_scale)


CMP_WIDTH = N_KV_HEADS * KV_PAIR
CMP_TOKENS_PER_STEP = 8


def _compress_body(x_ref, pos_ref, w1_ref, w2_ref, o_ref, ot_ref, top_ref, bot_ref, *, nc):
    l = pl.program_id(1)

    @pl.when(l == 0)
    def _():
        top_ref[...] = jnp.zeros_like(top_ref)
        bot_ref[...] = jnp.zeros_like(bot_ref)

    top = top_ref[...]
    bot = bot_ref[...]
    for s in range(CMP_TOKENS_PER_STEP):
        x = x_ref[0, s].astype(F32)
        top = top + jnp.dot((x + pos_ref[s, 0]).astype(BF16), w1_ref[s, 0], preferred_element_type=F32)
        bot = bot + jnp.dot((x + pos_ref[s, 1]).astype(BF16), w1_ref[s, 1], preferred_element_type=F32)
    top_ref[...] = top
    bot_ref[...] = bot

    @pl.when(l == pl.num_programs(1) - 1)
    def _():
        pre = top_ref[...] + pltpu.roll(bot_ref[...], nc - 1, axis=0)
        act = pre * _sigmoid(pre)
        out = jnp.dot(act.astype(BF16), w2_ref[...], preferred_element_type=F32)
        for h in range(N_KV_HEADS):
            kv = out[:, h * KV_PAIR:(h + 1) * KV_PAIR]
            o_ref[0, h] = kv.astype(o_ref.dtype)
            ot_ref[0, h] = kv.T.astype(ot_ref.dtype)


def _compress(tokl, posx, w1big, w2big):
    b, _, nc, _ = tokl.shape
    return pl.pallas_call(
        functools.partial(_compress_body, nc=nc),
        grid=(b, CMP_STRIDE // CMP_TOKENS_PER_STEP),
        in_specs=[
            pl.BlockSpec((1, CMP_TOKENS_PER_STEP, nc, CMP_WIDTH), lambda i, l: (i, l, 0, 0)),
            pl.BlockSpec((CMP_TOKENS_PER_STEP, 2, 1, CMP_WIDTH), lambda i, l: (l, 0, 0, 0)),
            pl.BlockSpec((CMP_TOKENS_PER_STEP, 2, CMP_WIDTH, CMP_WIDTH), lambda i, l: (l, 0, 0, 0)),
            pl.BlockSpec((CMP_WIDTH, CMP_WIDTH), lambda i, l: (0, 0)),
        ],
        out_specs=[
            pl.BlockSpec((1, N_KV_HEADS, nc, KV_PAIR), lambda i, l: (i, 0, 0, 0)),
            pl.BlockSpec((1, N_KV_HEADS, KV_PAIR, nc), lambda i, l: (i, 0, 0, 0)),
        ],
        out_shape=[
            jax.ShapeDtypeStruct((b, N_KV_HEADS, nc, KV_PAIR), BF16),
            jax.ShapeDtypeStruct((b, N_KV_HEADS, KV_PAIR, nc), BF16),
        ],
        scratch_shapes=[pltpu.VMEM((nc, CMP_WIDTH), F32), pltpu.VMEM((nc, CMP_WIDTH), F32)],
        compiler_params=_params("arbitrary", "arbitrary"),
        name="compress",
    )(tokl, posx, w1big, w2big)


def _compress_weight_layout(pos, w1, w2):
    def block_diag(blocks):
        rows = []
        for h in range(N_KV_HEADS):
            for kv in range(2):
                off = (h * 2 + kv) * HEAD_DIM
                pad = [(0, 0)] * (blocks.ndim - 2) + [(off, CMP_WIDTH - HEAD_DIM - off)]
                rows.append(jnp.pad(blocks[kv], pad))
        return jnp.concatenate(rows, axis=-2).astype(BF16)

    w1r = w1.reshape(2, 2, CMP_STRIDE, HEAD_DIM, HEAD_DIM)
    w1big = block_diag(w1r.transpose(0, 2, 1, 3, 4))
    w2big = block_diag(w2)
    posr = pos.reshape(2, 2, CMP_STRIDE, HEAD_DIM).transpose(2, 1, 0, 3)
    posx = jnp.broadcast_to(posr[:, :, None], (CMP_STRIDE, 2, N_KV_HEADS, 2, HEAD_DIM))
    return posx.reshape(CMP_STRIDE, 2, 1, CMP_WIDTH), w1big, w2big


VT_ROWS = HEAD_DIM + 16
FAR_UNROLL = 4


def _attn_body(q_ref, kvc_ref, kvct_ref, kvs_ref, kvw_ref, bt_ref, bc_ref, gate_ref, o_ref,
               kp_ref, vst_ref, vwt_ref, key_ref, m_ref, acc_ref, sa_ref, sb_ref, ocmp_ref, qsel_ref, qwin_ref,
               gt_ref, *, tq, t_len):
    kv_head = pl.program_id(0)
    qi = pl.program_id(2)
    nc = t_len // CMP_STRIDE
    ns = t_len // SLC_LEN
    rows = GQA_GROUP * tq

    @pl.when(qi == 0)
    def _():
        k = kvs_ref[0][:, :HEAD_DIM]
        blk = lax.broadcasted_iota(jnp.int32, (t_len, HEAD_DIM), 0) // SLC_LEN
        col = lax.broadcasted_iota(jnp.int32, (t_len, HEAD_DIM), 1)
        onehot = jnp.where(blk == col, 1.0, 0.0).astype(BF16)
        kp_ref[...] = jnp.concatenate([k, onehot], axis=1)
        ones_rows = jnp.where(lax.broadcasted_iota(jnp.int32, (VT_ROWS - HEAD_DIM, tq), 0) == 0, 1.0, 0.0)
        for c in range(t_len // tq):
            st = kvs_ref[0, c * tq:(c + 1) * tq, :].astype(F32).T[HEAD_DIM:]
            vst_ref[c] = jnp.concatenate([st, ones_rows], axis=0).astype(BF16)
            wt = kvw_ref[0, c * tq:(c + 1) * tq, :].astype(F32).T[HEAD_DIM:]
            vwt_ref[c] = jnp.concatenate([wt, ones_rows], axis=0).astype(BF16)

    q_t = q_ref[0].astype(F32).T
    q4_t = jnp.concatenate([q_t[g * HEAD_DIM:(g + 1) * HEAD_DIM] for g in range(GQA_GROUP)],
                           axis=1).astype(BF16)
    qwin_ref[...] = jnp.concatenate([q4_t, jnp.zeros((HEAD_DIM, rows), BF16)], axis=0)
    qsel_ref[0:HEAD_DIM, :] = q4_t
    kw_ref = kvw_ref.at[0]
    win, sel = 0, 1

    def cmp_scores():
        bias_row = pl.multiple_of(nc - qi * (tq // CMP_STRIDE), tq // CMP_STRIDE)
        return (jnp.dot(kvc_ref[0, 0], qwin_ref[...], preferred_element_type=F32)
                + bc_ref[0, pl.ds(bias_row, nc), :])

    def cmp_finish(s):
        m = jnp.max(s, axis=0, keepdims=True)
        p = jnp.exp2(s - m)
        l = jnp.sum(p, axis=0, keepdims=True)
        pn = p * jnp.where(m > 0.5 * MASK_VALUE, 1.0 / l, 0.0)
        ocmp_ref[...] = jnp.dot(kvct_ref[0, 0, HEAD_DIM:, :], pn.astype(BF16),
                                preferred_element_type=F32)

        ps = pn[:, 0:tq] + pn[:, tq:2 * tq] + pn[:, 2 * tq:3 * tq] + pn[:, 3 * tq:4 * tq]
        sj = lax.broadcasted_iota(jnp.int32, (HEAD_DIM, nc), 0) * SLC_LEN
        ci = lax.broadcasted_iota(jnp.int32, (HEAD_DIM, nc), 1) * CMP_STRIDE
        overlap = jnp.where(ci < sj + SLC_LEN, jnp.where(ci + CMP_LEN > sj, 1.0, 0.0), 0.0).astype(BF16)
        p_hi = ps.astype(BF16)
        r_hi = ps - p_hi.astype(F32)
        p_md = r_hi.astype(BF16)
        p_lo = (r_hi - p_md.astype(F32)).astype(BF16)
        imp = (jnp.dot(overlap, p_hi, preferred_element_type=F32)
               + jnp.dot(overlap, p_md, preferred_element_type=F32)
               + jnp.dot(overlap, p_lo, preferred_element_type=F32))
        t = qi * tq + lax.broadcasted_iota(jnp.int32, (HEAD_DIM, tq), 1)
        blk = lax.broadcasted_iota(jnp.int32, (HEAD_DIM, tq), 0)
        cur = t // SLC_LEN
        key = jnp.where(blk == 0, KEY_BIG, jnp.where(blk == cur, KEY_BIG, jnp.where(blk == cur - 1, KEY_BIG, imp)))
        key_ref[...] = jnp.where(blk * SLC_LEN <= t, key, -KEY_BIG)

    def select_blocks():
        key = key_ref[...]
        blk = lax.broadcasted_iota(jnp.int32, (HEAD_DIM, tq), 0)
        pen = jnp.full((HEAD_DIM, tq), BLOCK_PENALTY, F32)
        for _ in range(min(SLC_TOPK, ns)):
            top = jnp.max(key, axis=0, keepdims=True)
            first = jnp.min(jnp.where(key == top, blk, HEAD_DIM), axis=0, keepdims=True)
            hit = blk == first
            pen = jnp.where(hit, 0.0, pen)
            key = jnp.where(hit, KEY_TAKEN, key)
        qsel_ref[HEAD_DIM:, :] = jnp.concatenate([pen.astype(BF16)] * GQA_GROUP, axis=1)

    def scores(qx_ref, k_ref, kj, bias_off):
        start = pl.multiple_of(kj * tq, tq)
        sc = jnp.dot(k_ref[pl.ds(start, tq), :], qx_ref[...], preferred_element_type=F32)
        if bias_off is not None:
            sc = sc + bt_ref[0, bias_off:bias_off + tq, :]
        return sc

    def consume(sc, vt_ref, kj, br, st, first):
        m_tile = jnp.max(sc, axis=0, keepdims=True)
        if first:
            m_new = m_tile
            pr = jnp.exp2(sc - m_new).astype(BF16)
            acc_ref[br, st] = jnp.dot(vt_ref[kj], pr, preferred_element_type=F32)
        else:
            m_old = m_ref[br, st]
            m_new = jnp.maximum(m_old, m_tile)
            alpha = jnp.exp2(m_old - m_new)
            pr = jnp.exp2(sc - m_new).astype(BF16)
            acc_ref[br, st] = alpha * acc_ref[br, st] + jnp.dot(vt_ref[kj], pr, preferred_element_type=F32)
        m_ref[br, st] = m_new

    def finish(br):
        m0 = m_ref[br, 0]
        m1 = m_ref[br, 1]
        m_all = jnp.maximum(m0, m1)
        acc = jnp.exp2(m0 - m_all) * acc_ref[br, 0] + jnp.exp2(m1 - m_all) * acc_ref[br, 1]
        return acc[0:HEAD_DIM] * (1.0 / acc[HEAD_DIM:HEAD_DIM + 1, :])

    def combine():
        o_win = finish(win)
        o_sel = finish(sel)
        o_cmp = ocmp_ref[...]
        gt_ref[...] = gate_ref[0].T
        base = kv_head * (N_BRANCH * GQA_GROUP)

        def gate(br, g):
            return gt_ref[pl.ds(base + br * GQA_GROUP + g, 1), :]

        combs = []
        for g in range(GQA_GROUP):
            sl = slice(g * tq, (g + 1) * tq)
            combs.append(gate(0, g) * o_cmp[:, sl] + gate(1, g) * o_sel[:, sl] + gate(2, g) * o_win[:, sl])
        outs = [jnp.concatenate(combs[p:p + 2], axis=0).T for p in range(0, GQA_GROUP, 2)]
        o_ref[0] = jnp.concatenate(outs, axis=1).astype(o_ref.dtype)

    @pl.when(qi >= 2)
    def _():
        sc_cmp = cmp_scores()
        sw0 = scores(qwin_ref, kw_ref, qi, WINDOW)
        cmp_finish(sc_cmp)
        sw1 = scores(qwin_ref, kw_ref, qi - 1, WINDOW - tq)
        consume(sw0, vwt_ref, qi, win, 0, True)
        sw2 = scores(qwin_ref, kw_ref, qi - 2, WINDOW - 2 * tq)
        consume(sw1, vwt_ref, qi - 1, win, 1, True)
        select_blocks()
        consume(sw2, vwt_ref, qi - 2, win, 0, False)

        ss0 = scores(qsel_ref, kp_ref, qi, WINDOW)
        ss1 = scores(qsel_ref, kp_ref, qi - 1, WINDOW - tq)
        consume(ss0, vst_ref, qi, sel, 0, True)
        sa_ref[...] = scores(qsel_ref, kp_ref, 0, None)
        consume(ss1, vst_ref, qi - 1, sel, 1, True)
        n_far = qi - 1

        def far_pair(pi):
            sb_ref[...] = scores(qsel_ref, kp_ref, 2 * pi + 1, None)
            consume(sa_ref[...], vst_ref, 2 * pi, sel, 1, False)
            sa_ref[...] = scores(qsel_ref, kp_ref, jnp.minimum(2 * pi + 2, n_far - 1), None)
            consume(sb_ref[...], vst_ref, 2 * pi + 1, sel, 0, False)

        n_pairs = n_far // 2

        def far_group(gi, carry):
            for u in range(FAR_UNROLL):
                far_pair(FAR_UNROLL * gi + u)
            return carry

        lax.fori_loop(0, n_pairs // FAR_UNROLL, far_group, 0)
        done = (n_pairs // FAR_UNROLL) * FAR_UNROLL
        left = n_pairs - done
        size = FAR_UNROLL // 2
        while size >= 1:
            @pl.when(left % (2 * size) >= size)
            def _(size=size, base=done):
                for u in range(size):
                    far_pair(base + u)
            done = done + jnp.where(left % (2 * size) >= size, size, 0)
            size //= 2

        @pl.when(n_far % 2 == 1)
        def _():
            consume(sa_ref[...], vst_ref, n_far - 1, sel, 1, False)

        combine()

    @pl.when(qi < 2)
    def _():
        cmp_finish(cmp_scores())
        select_blocks()
        sw0 = scores(qwin_ref, kw_ref, qi, WINDOW)
        ss0 = scores(qsel_ref, kp_ref, qi, WINDOW)
        consume(sw0, vwt_ref, qi, win, 0, True)
        consume(ss0, vst_ref, qi, sel, 0, True)
        for br in (win, sel):
            m_ref[br, 1] = jnp.full((1, rows), MASK_VALUE, F32)
            acc_ref[br, 1] = jnp.zeros((VT_ROWS, rows), F32)

        @pl.when(qi == 1)
        def _():
            sw1 = scores(qwin_ref, kw_ref, 0, WINDOW - tq)
            ss1 = scores(qsel_ref, kp_ref, 0, WINDOW - tq)
            consume(sw1, vwt_ref, 0, win, 1, False)
            consume(ss1, vst_ref, 0, sel, 1, False)

        combine()


def _attention(proj3, kvc, kvct, band, cmpb, gates3, *, tq):
    b, t_len, _ = proj3.shape
    assert WINDOW == 2 * tq
    nc = t_len // CMP_STRIDE
    rows = GQA_GROUP * tq
    qw = GQA_GROUP * HEAD_DIM
    slc_blk = COL_KV // KV_PAIR + N_KV_HEADS
    win_blk = COL_KV // KV_PAIR + 2 * N_KV_HEADS
    return pl.pallas_call(
        functools.partial(_attn_body, tq=tq, t_len=t_len),
        grid=(N_KV_HEADS, b, t_len // tq),
        in_specs=[
            pl.BlockSpec((1, tq, qw), lambda h, i, q: (i, q, h)),
            pl.BlockSpec((1, 1, nc, KV_PAIR), lambda h, i, q: (i, h, 0, 0)),
            pl.BlockSpec((1, 1, KV_PAIR, nc), lambda h, i, q: (i, h, 0, 0)),
            pl.BlockSpec((1, t_len, KV_PAIR), lambda h, i, q: (i, 0, slc_blk + h)),
            pl.BlockSpec((1, t_len, KV_PAIR), lambda h, i, q: (i, 0, win_blk + h)),
            pl.BlockSpec((1, WINDOW + tq, rows), lambda h, i, q: (h, 0, 0)),
            pl.BlockSpec((1, 2 * nc, rows), lambda h, i, q: (h, 0, 0)),
            pl.BlockSpec((1, tq, LANES), lambda h, i, q: (i, q, 0)),
        ],
        out_specs=pl.BlockSpec((1, tq, qw), lambda h, i, q: (i, q, h)),
        out_shape=jax.ShapeDtypeStruct((b, t_len, ATTN_WIDTH), BF16),
        scratch_shapes=[
            pltpu.VMEM((t_len, KV_PAIR), BF16),
            pltpu.VMEM((t_len // tq, VT_ROWS, tq), BF16),
            pltpu.VMEM((t_len // tq, VT_ROWS, tq), BF16),
            pltpu.VMEM((HEAD_DIM, tq), F32),
            pltpu.VMEM((2, 2, 1, rows), F32),
            pltpu.VMEM((2, 2, VT_ROWS, rows), F32),
            pltpu.VMEM((tq, rows), F32),
            pltpu.VMEM((tq, rows), F32),
            pltpu.VMEM((HEAD_DIM, rows), F32),
            pltpu.VMEM((KV_PAIR, rows), BF16),
            pltpu.VMEM((KV_PAIR, rows), BF16),
            pltpu.VMEM((LANES, tq), F32),
        ],
        compiler_params=_params("arbitrary", "arbitrary", "arbitrary"),
        name="nsa_attention",
    )(proj3, kvc, kvct, proj3, proj3, band, cmpb, gates3)


def _mix_out_body(a_ref, g_ref, ah_ref, gh_ref, w_ref, b_ref, lg_ref, lb_ref, attn_ref, x_ref, wa_ref, wc_ref,
                  o_ref, u_ref, y_ref, *, tt):
    ti = pl.program_id(1)
    u_ref[CONV_HALO:, :] = a_ref[0].astype(F32) * _sigmoid(g_ref[0].astype(F32))
    halo = ah_ref[0].astype(F32) * _sigmoid(gh_ref[0].astype(F32))
    u_ref[0:CONV_HALO, :] = jnp.where(ti > 0, halo, 0.0)
    base = CONV_HALO - (CONV_TAPS - 1)
    ext = CONV_ROWS + CONV_HALO
    for rb in range(tt // CONV_ROWS):
        rs = slice(rb * CONV_ROWS, (rb + 1) * CONV_ROWS)
        for cb in range(CONV_CHANNELS // LANES):
            cs = slice(cb * LANES, (cb + 1) * LANES)
            wblk = w_ref[:, cs]
            ublk = u_ref[rb * CONV_ROWS:rb * CONV_ROWS + ext, cs]
            acc = jnp.zeros((CONV_ROWS, LANES), F32)
            for r in range(SUBLANES):
                ur = ublk if r == 0 else pltpu.roll(ublk, ext - r, axis=0)
                for a in range(CONV_HALO // SUBLANES + 1):
                    k = SUBLANES * a + r - base
                    if 0 <= k < CONV_TAPS:
                        acc = acc + wblk[k:k + 1, :] * ur[SUBLANES * a:SUBLANES * a + CONV_ROWS]
            y_ref[rs, cs] = acc
        acc = y_ref[rs, :] + b_ref[...]
        mu = jnp.mean(acc, axis=-1, keepdims=True)
        xc = acc - mu
        var = jnp.mean(xc * xc, axis=-1, keepdims=True)
        y = xc * lax.rsqrt(var + NORM_EPS) * lg_ref[...] + lb_ref[...]
        conv_act = (y * _sigmoid(y)).astype(BF16)
        o_ref[0, rs, :] = (x_ref[0, rs, :]
                           + jnp.dot(attn_ref[0, rs, :], wa_ref[...], preferred_element_type=F32)
                           + jnp.dot(conv_act, wc_ref[...], preferred_element_type=F32))


def _mix_out(proj3, w, b, lg, lb, attn, x3, w_out, *, layer, tt):
    bsz, t_len, _ = proj3.shape
    a_blk = COL_CONV_A // CONV_CHANNELS
    g_blk = COL_CONV_G // CONV_CHANNELS
    hpt = tt // CONV_HALO
    halo_idx = lambda i, t: jnp.maximum(t * hpt - 1, 0)
    vec = pl.BlockSpec((1, CONV_CHANNELS), lambda i, t: (0, 0))
    conv_rows = ATTN_WIDTH // CONV_CHANNELS
    return pl.pallas_call(
        functools.partial(_mix_out_body, tt=tt),
        grid=(bsz, t_len // tt),
        in_specs=[
            pl.BlockSpec((1, tt, CONV_CHANNELS), lambda i, t: (i, t, a_blk)),
            pl.BlockSpec((1, tt, CONV_CHANNELS), lambda i, t: (i, t, g_blk)),
            pl.BlockSpec((1, CONV_HALO, CONV_CHANNELS), lambda i, t: (i, halo_idx(i, t), a_blk)),
            pl.BlockSpec((1, CONV_HALO, CONV_CHANNELS), lambda i, t: (i, halo_idx(i, t), g_blk)),
            pl.BlockSpec((CONV_TAPS, CONV_CHANNELS), lambda i, t: (0, 0)),
            vec, vec, vec,
            pl.BlockSpec((1, tt, ATTN_WIDTH), lambda i, t: (i, t, 0)),
            pl.BlockSpec((1, tt, D_MODEL), lambda i, t: (i, t, 0)),
            pl.BlockSpec((None, ATTN_WIDTH, D_MODEL), lambda i, t: (layer, 0, 0)),
            pl.BlockSpec((None, CONV_CHANNELS, D_MODEL), lambda i, t: (layer, conv_rows, 0)),
        ],
        out_specs=pl.BlockSpec((1, tt, D_MODEL), lambda i, t: (i, t, 0)),
        out_shape=jax.ShapeDtypeStruct((bsz, t_len, D_MODEL), F32),
        scratch_shapes=[pltpu.VMEM((CONV_HALO + tt, CONV_CHANNELS), F32),
                        pltpu.VMEM((tt, CONV_CHANNELS), F32)],
        compiler_params=_params("arbitrary", "arbitrary"),
        name="mix_out",
    )(proj3, proj3, proj3, proj3, w, b, lg, lb, attn, x3, w_out, w_out)


def _ffn_up_body(x_ref, xh_ref, g_ref, wa_ref, wg_ref, cwa_ref, cwg_ref, cba_ref, cbg_ref, o_ref,
                 h_ref, ua_ref, ug_ref, *, tm, tiles_per_seq, rows):
    i = pl.program_id(0)

    @pl.when(pl.program_id(1) == 0)
    def _():
        h_ref[FFN_HALO:, :] = _rms_norm_rows(x_ref[...], g_ref[...]).astype(BF16)
        hh = _rms_norm_rows(xh_ref[...], g_ref[...])
        h_ref[0:FFN_HALO, :] = jnp.where(i % tiles_per_seq != 0, hh, 0.0).astype(BF16)

    units = [(u, c, slice(w, w + FFN_COLS)) for u, (c, w) in enumerate(
        (c, w) for c in range(tm // rows) for w in range(0, o_ref.shape[1], FFN_COLS))]

    def project(u, c, cs):
        hc = h_ref[c * rows:(c + 1) * rows + FFN_HALO, :]
        ug_ref[u] = jnp.dot(hc, wg_ref[:, cs], preferred_element_type=F32)
        ua_ref[u] = jnp.dot(hc, wa_ref[:, cs], preferred_element_type=F32)

    def conv(u_ref, u, cs, cw_ref, cb_ref):
        x = u_ref[u]
        y = cw_ref[FFN_CONV_TAPS - 1:FFN_CONV_TAPS, cs] * x[FFN_HALO:]
        for s in range(1, FFN_CONV_TAPS):
            k = FFN_CONV_TAPS - 1 - s
            y = y + cw_ref[k:k + 1, cs] * pltpu.roll(x, s, axis=0)[FFN_HALO:]
        return y + cb_ref[:, cs]

    def finish(u, c, cs):
        gate = conv(ug_ref, u, cs, cwg_ref, cbg_ref)
        act = gate * _sigmoid(gate)
        a = conv(ua_ref, u, cs, cwa_ref, cba_ref)
        o_ref[c * rows:(c + 1) * rows, cs] = (act * a).astype(o_ref.dtype)

    project(*units[0])
    for i_unit, unit in enumerate(units):
        if i_unit + 1 < len(units):
            project(*units[i_unit + 1])
        finish(*unit)


def _ffn_up(x2, g, w_up, cw, cb, *, layer, tm, tn, t_len):
    n = x2.shape[0]
    nj = D_FF // tn
    hpt = tm // FFN_HALO
    rows = min(FFN_ROWS, tm)
    return pl.pallas_call(
        functools.partial(_ffn_up_body, tm=tm, tiles_per_seq=t_len // tm, rows=rows),
        grid=(n // tm, nj),
        in_specs=[
            pl.BlockSpec((tm, D_MODEL), lambda i, j: (i, 0)),
            pl.BlockSpec((FFN_HALO, D_MODEL), lambda i, j: (jnp.maximum(i * hpt - 1, 0), 0)),
            pl.BlockSpec((1, D_MODEL), lambda i, j: (0, 0)),
            pl.BlockSpec((None, D_MODEL, tn), lambda i, j: (layer, 0, j)),
            pl.BlockSpec((None, D_MODEL, tn), lambda i, j: (layer, 0, j + nj)),
            pl.BlockSpec((FFN_CONV_TAPS, tn), lambda i, j: (0, j)),
            pl.BlockSpec((FFN_CONV_TAPS, tn), lambda i, j: (0, j + nj)),
            pl.BlockSpec((1, tn), lambda i, j: (0, j)),
            pl.BlockSpec((1, tn), lambda i, j: (0, j + nj)),
        ],
        out_specs=pl.BlockSpec((tm, tn), lambda i, j: (i, j)),
        out_shape=jax.ShapeDtypeStruct((n, D_FF), BF16),
        scratch_shapes=[
            pltpu.VMEM((FFN_HALO + tm, D_MODEL), BF16),
            pltpu.VMEM((tm // rows * (tn // FFN_COLS), FFN_HALO + rows, FFN_COLS), F32),
            pltpu.VMEM((tm // rows * (tn // FFN_COLS), FFN_HALO + rows, FFN_COLS), F32),
        ],
        compiler_params=_params("arbitrary", "arbitrary"),
        name="ffn_up",
    )(x2, x2, g, w_up, w_up, cw, cw, cb, cb)


def _ffn_down_body(act_ref, w_ref, x_ref, o_ref):
    o_ref[...] = x_ref[...] + jnp.dot(act_ref[...], w_ref[...], preferred_element_type=F32)


def _ffn_down(act, w_down, x2, *, layer, tm, tn):
    n = x2.shape[0]
    return pl.pallas_call(
        _ffn_down_body,
        grid=(n // tm, D_MODEL // tn),
        in_specs=[
            pl.BlockSpec((tm, D_FF), lambda i, j: (i, 0)),
            pl.BlockSpec((None, D_FF, tn), lambda i, j: (layer, 0, j)),
            pl.BlockSpec((tm, tn), lambda i, j: (i, j)),
        ],
        out_specs=pl.BlockSpec((tm, tn), lambda i, j: (i, j)),
        out_shape=jax.ShapeDtypeStruct((n, D_MODEL), F32),
        compiler_params=_params("arbitrary", "arbitrary"),
        name="ffn_down",
    )(act, w_down, x2)


def _ffn_down_norm_body(act_ref, w_ref, x_ref, g_ref, o_ref):
    y = x_ref[...] + jnp.dot(act_ref[...], w_ref[...], preferred_element_type=F32)
    o_ref[...] = _rms_norm_rows(y, g_ref[...])


def _ffn_down_norm(act, w_down, x2, g, *, layer, tm):
    n = x2.shape[0]
    return pl.pallas_call(
        _ffn_down_norm_body,
        grid=(n // tm,),
        in_specs=[
            pl.BlockSpec((tm, D_FF), lambda i: (i, 0)),
            pl.BlockSpec((None, D_FF, D_MODEL), lambda i: (layer, 0, 0), pipeline_mode=pl.Buffered(1)),
            pl.BlockSpec((tm, D_MODEL), lambda i: (i, 0)),
            pl.BlockSpec((1, D_MODEL), lambda i: (0, 0)),
        ],
        out_specs=pl.BlockSpec((tm, D_MODEL), lambda i: (i, 0)),
        out_shape=jax.ShapeDtypeStruct((n, D_MODEL), F32),
        compiler_params=_params("arbitrary"),
        name="ffn_down_norm",
    )(act, w_down, x2, g)


def _in_weight_layout(w_in):
    depth = w_in.shape[0]
    kv0 = ATTN_WIDTH
    gate0 = kv0 + KV_COLS
    conv0 = gate0 + GATE_COLS
    w_t = jnp.swapaxes(w_in, 1, 2)
    parts = [w_t[:, :ATTN_WIDTH], w_t[:, conv0:conv0 + 2 * CONV_CHANNELS]]
    for br in range(N_BRANCH):
        for h in range(N_KV_HEADS):
            for kv in range(2):
                c0 = kv0 + ((br * 2 + kv) * N_KV_HEADS + h) * HEAD_DIM
                parts.append(w_t[:, c0:c0 + HEAD_DIM])
    w_main = jnp.concatenate(parts, axis=1).astype(BF16)
    wg = w_in[..., gate0:gate0 + GATE_COLS].reshape(depth, D_MODEL, N_KV_HEADS, GQA_GROUP, N_BRANCH)
    wg = wg.transpose(0, 1, 2, 4, 3).reshape(depth, D_MODEL, N_KV_HEADS, N_BRANCH * GQA_GROUP)
    wg = wg.reshape(depth, D_MODEL, GATE_COLS)
    wg = jnp.pad(wg, ((0, 0), (0, 0), (0, LANES - GATE_COLS)))
    return w_main, wg.astype(BF16)


def _pick(n, prefs):
    for p in prefs:
        if n % p == 0:
            return p
    return n


def kernel(x, rel_bias, mix_norm_g, w_in, cmp_pos, cmp_w1, cmp_w2, conv_w, conv_b, conv_ln_g, conv_ln_b,
           w_out, ffn_norm_g, w_up, ffn_conv_w, ffn_conv_b, w_down, final_norm_g):
    bsz, t_len, _ = x.shape
    depth = w_in.shape[0]
    n = bsz * t_len
    assert t_len % 256 == 0 and t_len // SLC_LEN <= HEAD_DIM
    tq = 256
    tm = _pick(t_len, (1024, 512, 256))
    nc = t_len // CMP_STRIDE

    band, cmpb = _bias_tiles(rel_bias, t_len, tq)
    col_scale = jnp.concatenate([jnp.full((1, ATTN_WIDTH), Q_SCALE, F32),
                                 jnp.ones((1, PROJ_COLS - ATTN_WIDTH), F32)], axis=1)
    x2 = x.reshape(n, D_MODEL)
    w_main, wg = _in_weight_layout(w_in)
    w_out_b, w_up_b, w_down_b = w_out.astype(BF16), w_up.astype(BF16), w_down.astype(BF16)
    for l in range(depth):
        proj, gates = _in_projection(x2, mix_norm_g[l][None, :], w_main, wg, col_scale, layer=l, tm=tm, tn=1536)
        proj3 = proj.reshape(bsz, t_len, PROJ_COLS)
        gates3 = gates.reshape(bsz, t_len, LANES)

        posx, w1big, w2big = _compress_weight_layout(cmp_pos[l], cmp_w1[l], cmp_w2[l])
        tokl = proj3[:, :, COL_KV:COL_KV + CMP_WIDTH].reshape(bsz, nc, CMP_STRIDE, CMP_WIDTH).transpose(0, 2, 1, 3)
        kvc, kvct = _compress(tokl, posx, w1big, w2big)

        attn = _attention(proj3, kvc, kvct, band, cmpb, gates3, tq=tq)
        x2 = _mix_out(proj3, conv_w[l], conv_b[l][None, :], conv_ln_g[l][None, :], conv_ln_b[l][None, :],
                      attn, x2.reshape(bsz, t_len, D_MODEL), w_out_b, layer=l, tt=512).reshape(n, D_MODEL)

        act = _ffn_up(x2, ffn_norm_g[l][None, :], w_up_b, ffn_conv_w[l],
                      ffn_conv_b[l][None, :], layer=l, tm=tm, tn=512, t_len=t_len)
        if l + 1 < depth:
            x2 = _ffn_down(act, w_down_b, x2, layer=l, tm=tm, tn=512)
        else:
            x2 = _ffn_down_norm(act, w_down_b, x2, final_norm_g[None, :], layer=l, tm=256)
    return x2.reshape(bsz, t_len, D_MODEL)
```
